```python
import math
import jax, jax.numpy as jnp
from jax import lax
import numpy as np

D_MODEL = 2048
BATCH = 4
SEQ = 2048
DEPTH = 1
DEC_BATCH = 128
DEC_SEQ = 8
PAST_LEN = 16384
PAGE_SIZE = 128

SSD_HEAD_DIM = 64
SSD_INNER = D_MODEL
SSD_HEADS = SSD_INNER // SSD_HEAD_DIM
SSD_GROUPS = 8
SSD_STATE = 128
SSD_CONV = 4
SSD_CHUNK = 128
SSD_BC = SSD_GROUPS * SSD_STATE
SSD_CONV_DIM = SSD_INNER + 2 * SSD_BC
GLA_HEADS = 8
GLA_KEY_DIM = D_MODEL // 2
GLA_VAL_DIM = D_MODEL
GLA_DK = GLA_KEY_DIM // GLA_HEADS
GLA_DV = GLA_VAL_DIM // GLA_HEADS
GLA_RANK = 16
GLA_TAU = 16.0
GLA_CHUNK = 64
MOE_GROUPS = 4
MOE_EPG = 8
MOE_EXPERTS = MOE_GROUPS * MOE_EPG
MOE_TOPK = 2
MOE_FF = D_MODEL // 4
DEEPNORM_ALPHA = (2.0 * DEPTH) ** 0.25
DEEPNORM_BETA = (8.0 * DEPTH) ** -0.25
EPS = 1e-5
IN_SPLIT_SIZES = (SSD_INNER, SSD_CONV_DIM, SSD_HEADS,
                  GLA_KEY_DIM, GLA_KEY_DIM, GLA_VAL_DIM, GLA_VAL_DIM, GLA_RANK,
                  D_MODEL, D_MODEL)
IN_COLS = sum(IN_SPLIT_SIZES)

kernel_name = "hybrid_ssd_gla_hmoe_deepnorm_step"


def _offsets(sizes):
    out, acc = [], 0
    for s in sizes[:-1]:
        acc += s
        out.append(acc)
    return out


def _pad_time(a, pad):
    if pad == 0:
        return a
    widths = [(0, 0)] * a.ndim
    widths[1] = (0, pad)
    return jnp.pad(a, widths)


def _layernorm(x, g, b):
    x = x.astype(jnp.float32)
    mu = jnp.mean(x, -1, keepdims=True)
    var = jnp.mean(jnp.square(x - mu), -1, keepdims=True)
    return (x - mu) * lax.rsqrt(var + EPS) * g + b


def _rms(x, w):
    return x * lax.rsqrt(jnp.mean(jnp.square(x), -1, keepdims=True) + EPS) * w


def _causal_conv(u, buf, w, bias):
    L = u.shape[1]
    up = jnp.concatenate([buf.astype(u.dtype), u], axis=1)
    out = bias + sum(w[i] * up[:, i:i + L] for i in range(SSD_CONV))
    return out, up[:, L:]


def _ssd_scan(xs, dt, A, B, C, h0):
    b, L, H, P = xs.shape
    N = B.shape[-1]
    Q = min(SSD_CHUNK, L)
    pad = (-L) % Q
    xs, dt, B, C = (_pad_time(t, pad) for t in (xs, dt, B, C))
    nc = (L + pad) // Q
    rep = H // SSD_GROUPS
    Bh = jnp.repeat(B, rep, axis=2).reshape(b, nc, Q, H, N)
    Ch = jnp.repeat(C, rep, axis=2).reshape(b, nc, Q, H, N)
    xc = xs.reshape(b, nc, Q, H, P)
    dtc = dt.reshape(b, nc, Q, H)
    acum = jnp.cumsum(dtc * A, axis=2)
    seg = acum[:, :, :, None, :] - acum[:, :, None, :, :]
    causal = jnp.tril(jnp.ones((Q, Q), bool))[None, None, :, :, None]
    decay = jnp.exp(jnp.where(causal, seg, -jnp.inf))
    scores = jnp.einsum('bcihn,bcjhn->bcijh', Ch, Bh) * decay
    y_diag = jnp.einsum('bcijh,bcjh,bcjhp->bcihp', scores, dtc, xc)
    decay_end = jnp.exp(acum[:, :, -1:, :] - acum)
    chunk_states = jnp.einsum('bcjhn,bcjh,bcjhp->bchpn', Bh, dtc * decay_end, xc)
    chunk_decay = jnp.exp(acum[:, :, -1, :])

    def step(h, inp):
        st, dec = inp
        return dec[:, :, None, None] * h + st, h

    h_last, h_prev = lax.scan(step, h0, (jnp.moveaxis(chunk_states, 1, 0),
                                         jnp.moveaxis(chunk_decay, 1, 0)))
    h_prev = jnp.moveaxis(h_prev, 0, 1)
    y_off = jnp.einsum('bcihn,bchpn,bcih->bcihp', Ch, h_prev, jnp.exp(acum))
    y = (y_diag + y_off).reshape(b, nc * Q, H, P)[:, :L]
    return y, h_last


def _gla_scan(q, k, v, gk, s0):
    b, L, H, DK = q.shape
    DV = v.shape[-1]
    Q = min(GLA_CHUNK, L)
    pad = (-L) % Q
    q, k, v, gk = (_pad_time(t, pad) for t in (q, k, v, gk))
    nc = (L + pad) // Q
    qc = q.reshape(b, nc, Q, H, DK)
    kc = k.reshape(b, nc, Q, H, DK)
    vc = v.reshape(b, nc, Q, H, DV)
    bcum = jnp.cumsum(gk.reshape(b, nc, Q, H, DK), axis=2)
    q_in = qc * jnp.exp(bcum)
    k_in = kc * jnp.exp(-bcum)
    causal = jnp.tril(jnp.ones((Q, Q), bool))[None, None, None]
    att = jnp.where(causal, jnp.einsum('bcihk,bcjhk->bchij', q_in, k_in), 0.0)
    o_intra = jnp.einsum('bchij,bcjhv->bcihv', att, vc)
    b_last = bcum[:, :, -1]
    k_end = kc * jnp.exp(b_last[:, :, None] - bcum)
    chunk_states = jnp.einsum('bcjhk,bcjhv->bchkv', k_end, vc)
    chunk_decay = jnp.exp(b_last)

    def step(s, inp):
        st, dec = inp
        return dec[..., None] * s + st, s

    s_last, s_prev = lax.scan(step, s0, (jnp.moveaxis(chunk_states, 1, 0),
                                         jnp.moveaxis(chunk_decay, 1, 0)))
    s_prev = jnp.moveaxis(s_prev, 0, 1)
    o_inter = jnp.einsum('bcihk,bchkv->bcihv', q_in, s_prev)
    o = (o_intra + o_inter).reshape(b, nc * Q, H, DV)[:, :L]
    return o, s_last


def _token_mixers(x, conv_buf, h_ssd, s_gla, w_in, conv_w, conv_b, dt_bias, a_log,
                  d_skip, ssd_norm_w, gla_w_a2, gla_b_a, gla_norm_w, w_out):
    b, L, _ = x.shape
    proj = x @ w_in
    z, xbc, dt, q, k, v, r, a_lo, g_ssd, g_gla = jnp.split(proj, _offsets(IN_SPLIT_SIZES), axis=-1)
    xbc_c, new_conv = _causal_conv(xbc, conv_buf, conv_w, conv_b)
    xbc_c = jax.nn.silu(xbc_c)
    xs, Bm, Cm = jnp.split(xbc_c, [SSD_INNER, SSD_INNER + SSD_BC], axis=-1)
    xs = xs.reshape(b, L, SSD_HEADS, SSD_HEAD_DIM)
    Bm = Bm.reshape(b, L, SSD_GROUPS, SSD_STATE)
    Cm = Cm.reshape(b, L, SSD_GROUPS, SSD_STATE)
    dt = jax.nn.softplus(dt + dt_bias)
    A = -jnp.exp(a_log)
    y, h_new = _ssd_scan(xs, dt, A, Bm, Cm, h_ssd)
    y = (y + d_skip[:, None] * xs).reshape(b, L, SSD_INNER)
    yg = (y * jax.nn.silu(z)).reshape(b, L, SSD_GROUPS, SSD_INNER // SSD_GROUPS)
    y_ssd = _rms(yg, 1.0).reshape(b, L, SSD_INNER) * ssd_norm_w
    qh = q.reshape(b, L, GLA_HEADS, GLA_DK) * (GLA_DK ** -0.5)
    kh = k.reshape(b, L, GLA_HEADS, GLA_DK)
    vh = v.reshape(b, L, GLA_HEADS, GLA_DV)
    gk = jax.nn.log_sigmoid(a_lo @ gla_w_a2 + gla_b_a) / GLA_TAU
    o, s_new = _gla_scan(qh, kh, vh, gk.reshape(b, L, GLA_HEADS, GLA_DK), s_gla)
    y_gla = _rms(o, gla_norm_w).reshape(b, L, GLA_VAL_DIM) * jax.nn.silu(r)
    merged = jax.nn.sigmoid(g_ssd) * y_ssd + jax.nn.sigmoid(g_gla) * y_gla
    return merged @ w_out, new_conv, h_new, s_new


def _hier_moe(u, w_rg, b_rg, w_re, b_re, w_gate, w_up, w_down):
    shp = u.shape
    t = u.reshape(-1, D_MODEL)
    glog = (t @ w_rg + b_rg).astype(jnp.float32)
    gprob = jax.nn.softmax(glog, -1)
    g_sel = jnp.argmax(glog, -1)
    p_g = jnp.max(gprob, -1, keepdims=True)
    elog = (t @ w_re + b_re).astype(jnp.float32).reshape(-1, MOE_GROUPS, MOE_EPG)
    elog_g = jnp.einsum('tge,tg->te', elog, jax.nn.one_hot(g_sel, MOE_GROUPS, dtype=jnp.float32))
    top_v, top_i = lax.top_k(elog_g, MOE_TOPK)
    w_k = jax.nn.softmax(top_v, -1) * p_g
    eid = g_sel[:, None] * MOE_EPG + top_i
    comb = jnp.sum(jax.nn.one_hot(eid, MOE_EXPERTS, dtype=jnp.float32) * w_k[..., None], axis=1)
    y = jnp.zeros(t.shape, jnp.float32)
    for e in range(MOE_EXPERTS):
        h = jax.nn.silu(t @ w_gate[e]) * (t @ w_up[e])
        y = y + comb[:, e:e + 1] * (h @ w_down[e])
    return y.reshape(shp)


def _layer(x, conv_buf, h_ssd, s_gla, w_in, conv_w, conv_b, dt_bias, a_log, d_skip,
           ssd_norm_w, gla_w_a2, gla_b_a, gla_norm_w, w_out, ln1_g, ln1_b,
           w_router_group, b_router_group, w_router_expert, b_router_expert,
           w_gate, w_up, w_down, ln2_g, ln2_b):
    x32 = x.astype(jnp.float32)
    mix, new_conv, h_new, s_new = _token_mixers(
        x32, conv_buf, h_ssd.astype(jnp.float32), s_gla.astype(jnp.float32), w_in, conv_w, conv_b,
        dt_bias, a_log, d_skip, ssd_norm_w, gla_w_a2, gla_b_a, gla_norm_w, w_out)
    h1 = _layernorm(DEEPNORM_ALPHA * x32 + mix, ln1_g, ln1_b)
    ff = _hier_moe(h1, w_router_group, b_router_group, w_router_expert, b_router_expert,
                   w_gate, w_up, w_down)
    h2 = _layernorm(DEEPNORM_ALPHA * h1 + ff, ln2_g, ln2_b)
    return h2.astype(x.dtype), new_conv, h_new, s_new


def setup_inputs(seed: int = 0) -> dict:
    key = jax.random.key(seed)
    ks = jax.random.split(key, 32)
    f32 = jnp.float32

    def nrm(k, shape, s):
        return jax.random.normal(k, shape, f32) * s

    dt0 = jnp.exp(jax.random.uniform(ks[8], (DEPTH, SSD_HEADS), f32, math.log(1e-3), math.log(1e-1)))
    return {
        "x_prompt": nrm(ks[0], (BATCH, SEQ, D_MODEL), 1.0),
        "x_sample": nrm(ks[1], (DEC_BATCH, DEC_SEQ, D_MODEL), 1.0),
        "state_ssd": nrm(ks[2], (DEPTH, DEC_BATCH, SSD_HEADS, SSD_HEAD_DIM, SSD_STATE), 0.5),
        "state_conv": nrm(ks[3], (DEPTH, DEC_BATCH, SSD_CONV - 1, SSD_CONV_DIM), 1.0),
        "state_gla": nrm(ks[4], (DEPTH, DEC_BATCH, GLA_HEADS, GLA_DK, GLA_DV), 0.5),
        "w_in": nrm(ks[5], (DEPTH, D_MODEL, IN_COLS), D_MODEL ** -0.5),
        "conv_w": nrm(ks[6], (DEPTH, SSD_CONV, SSD_CONV_DIM), 0.5),
        "conv_b": nrm(ks[7], (DEPTH, SSD_CONV_DIM), 0.02),
        "dt_bias": dt0 + jnp.log(-jnp.expm1(-dt0)),
        "a_log": jnp.log(jax.random.uniform(ks[9], (DEPTH, SSD_HEADS), f32, 1.0, 16.0)),
        "d_skip": 1.0 + nrm(ks[10], (DEPTH, SSD_HEADS), 0.1),
        "ssd_norm_w": 1.0 + nrm(ks[11], (DEPTH, SSD_INNER), 0.02),
        "gla_w_a2": nrm(ks[12], (DEPTH, GLA_RANK, GLA_KEY_DIM), GLA_RANK ** -0.5),
        "gla_b_a": nrm(ks[13], (DEPTH, GLA_KEY_DIM), 0.1),
        "gla_norm_w": 1.0 + nrm(ks[14], (DEPTH, GLA_DV), 0.02),
        "w_out": nrm(ks[15], (DEPTH, D_MODEL, D_MODEL), DEEPNORM_BETA * D_MODEL ** -0.5),
        "ln1_g": 1.0 + nrm(ks[16], (DEPTH, D_MODEL), 0.02),
        "ln1_b": nrm(ks[17], (DEPTH, D_MODEL), 0.02),
        "w_router_group": nrm(ks[18], (DEPTH, D_MODEL, MOE_GROUPS), D_MODEL ** -0.5),
        "b_router_group": nrm(ks[19], (DEPTH, MOE_GROUPS), 0.01),
        "w_router_expert": nrm(ks[20], (DEPTH, D_MODEL, MOE_EXPERTS), D_MODEL ** -0.5),
        "b_router_expert": nrm(ks[21], (DEPTH, MOE_EXPERTS), 0.01),
        "w_gate": nrm(ks[22], (DEPTH, MOE_EXPERTS, D_MODEL, MOE_FF), D_MODEL ** -0.5),
        "w_up": nrm(ks[23], (DEPTH, MOE_EXPERTS, D_MODEL, MOE_FF), D_MODEL ** -0.5),
        "w_down": nrm(ks[24], (DEPTH, MOE_EXPERTS, MOE_FF, D_MODEL), DEEPNORM_BETA * MOE_FF ** -0.5),
        "ln2_g": 1.0 + nrm(ks[25], (DEPTH, D_MODEL), 0.02),
        "ln2_b": nrm(ks[26], (DEPTH, D_MODEL), 0.02),
    }


def reference(x_prompt, x_sample, state_ssd, state_conv, state_gla, w_in, conv_w, conv_b,
              dt_bias, a_log, d_skip, ssd_norm_w, gla_w_a2, gla_b_a, gla_norm_w, w_out,
              ln1_g, ln1_b, w_router_group, b_router_group, w_router_expert, b_router_expert,
              w_gate, w_up, w_down, ln2_g, ln2_b):
    f32 = jnp.float32
    hp, hs = x_prompt, x_sample
    ssd_p, conv_p, gla_p, ssd_s, conv_s, gla_s = [], [], [], [], [], []
    for l in range(DEPTH):
        lp = [w[l] for w in (w_in, conv_w, conv_b, dt_bias, a_log, d_skip, ssd_norm_w,
                             gla_w_a2, gla_b_a, gla_norm_w, w_out, ln1_g, ln1_b,
                             w_router_group, b_router_group, w_router_expert, b_router_expert,
                             w_gate, w_up, w_down, ln2_g, ln2_b)]
        hp, cp, sp, gp = _layer(
            hp, jnp.zeros((BATCH, SSD_CONV - 1, SSD_CONV_DIM), f32),
            jnp.zeros((BATCH, SSD_HEADS, SSD_HEAD_DIM, SSD_STATE), f32),
            jnp.zeros((BATCH, GLA_HEADS, GLA_DK, GLA_DV), f32), *lp)
        hs, cs, ss, gs = _layer(hs, state_conv[l], state_ssd[l], state_gla[l], *lp)
        ssd_p.append(sp); conv_p.append(cp); gla_p.append(gp)
        ssd_s.append(ss); conv_s.append(cs); gla_s.append(gs)
    return (hp, hs, jnp.stack(ssd_p), jnp.stack(conv_p), jnp.stack(gla_p),
            jnp.stack(ssd_s), jnp.stack(conv_s), jnp.stack(gla_s))
```

```python
import functools
import math

import jax
import jax.numpy as jnp
from jax import lax
from jax.experimental import pallas as pl
from jax.experimental.pallas import tpu as pltpu

F32 = jnp.float32
BF16 = jnp.bfloat16
HIGHEST = lax.Precision.HIGHEST

D_MODEL = 2048
SSD_HEADS = 32
SSD_HEAD_DIM = 64
SSD_GROUPS = 8
SSD_STATE = 128
SSD_CONV = 4
SSD_CHUNK = 128
SSD_BC = SSD_GROUPS * SSD_STATE
SSD_CONV_DIM = D_MODEL + 2 * SSD_BC
SSD_GROUP_COLS = D_MODEL // SSD_GROUPS
HEADS_PER_GROUP = SSD_HEADS // SSD_GROUPS
GLA_HEADS = 8
GLA_DK = 128
GLA_DV = 256
GLA_KEY_DIM = GLA_HEADS * GLA_DK
GLA_RANK = 16
GLA_TAU = 16.0
GLA_CHUNK = 64
MOE_GROUPS = 4
MOE_EPG = 8
MOE_EXPERTS = 32
MOE_FF = 512
DEPTH = 1
DEEPNORM_ALPHA = (2.0 * DEPTH) ** 0.25
EPS = 1e-5
IN_SPLIT_SIZES = (D_MODEL, SSD_CONV_DIM, SSD_HEADS, GLA_KEY_DIM, GLA_KEY_DIM, D_MODEL, D_MODEL,
                  GLA_RANK, D_MODEL, D_MODEL)

LANES = 128
HALO_ROWS = 8
COL_Z, COL_XS, COL_BC, COL_QK, COL_V, COL_R, COL_GS, COL_GG = range(8)
TAIL_DT, TAIL_ALO = 0, 1
ROUTER_LANES = 128
VMEM_LIMIT = 56 * 1024 * 1024


def _cparams(n_axes):
    return pltpu.CompilerParams(dimension_semantics=("arbitrary",) * n_axes,
                                vmem_limit_bytes=VMEM_LIMIT)


def _silu(x):
    return x * jax.nn.sigmoid(x)


def _softplus(x):
    return jnp.maximum(x, 0.0) + jnp.log1p(jnp.exp(-jnp.abs(x)))


def _tril(n):
    r = lax.broadcasted_iota(jnp.int32, (n, n), 0)
    c = lax.broadcasted_iota(jnp.int32, (n, n), 1)
    return r >= c


def _expand_group(p, g, lane_head):
    h0 = g * HEADS_PER_GROUP
    out = p[:, h0 + HEADS_PER_GROUP - 1:h0 + HEADS_PER_GROUP]
    for hh in range(HEADS_PER_GROUP - 2, -1, -1):
        out = jnp.where(lane_head == hh, p[:, h0 + hh:h0 + hh + 1], out)
    return out


def _split3(x):
    hi = x.astype(BF16)
    r = x - hi.astype(F32)
    mid = r.astype(BF16)
    lo = (r - mid.astype(F32)).astype(BF16)
    return hi, mid, lo


def _dot_sel(sel, x):
    hi, mid, lo = _split3(x)
    return _dot(sel, lo) + _dot(sel, mid) + _dot(sel, hi)


def _col_bcast(row):
    return jnp.broadcast_to(row, (LANES, LANES)).T


def _dot(a, b, **kw):
    return jnp.dot(a, b, preferred_element_type=F32, **kw)


def _dot_nt(a, b, **kw):
    return lax.dot_general(a, b, (((1,), (1,)), ((), ())), preferred_element_type=F32, **kw)


def _dot_tn(a, b, **kw):
    return lax.dot_general(a, b, (((0,), (0,)), ((), ())), preferred_element_type=F32, **kw)


def _matmul_kernel(x_ref, w_ref, o_ref):
    o_ref[...] = _dot(x_ref[...], w_ref[...])


def _matmul(x, w, tm, tn):
    m, k = x.shape
    n = w.shape[1]
    return pl.pallas_call(
        _matmul_kernel,
        grid=(n // tn, m // tm),
        in_specs=[pl.BlockSpec((tm, k), lambda j, i: (i, 0)),
                  pl.BlockSpec((k, tn), lambda j, i: (0, j))],
        out_specs=pl.BlockSpec((tm, tn), lambda j, i: (i, j)),
        out_shape=jax.ShapeDtypeStruct((m, n), F32),
        compiler_params=_cparams(2),
        name="in_proj",
    )(x, w)


def _conv_silu(ubuf, cw_ref, cb_ref, lo, width, rows):
    acc = cb_ref[:, lo:lo + width]
    for i in range(SSD_CONV):
        r0 = HALO_ROWS - (SSD_CONV - 1) + i
        acc = acc + cw_ref[i:i + 1, lo:lo + width] * ubuf[r0:r0 + rows, lo:lo + width]
    return _silu(acc)


def _gated_group_norm(y_g, z_g, nw_g):
    yg = y_g * _silu(z_g)
    ms = jnp.mean(yg * yg, axis=-1, keepdims=True)
    return yg * lax.rsqrt(ms + EPS) * nw_g


def _ssd_prompt_kernel(z_ref, xs_ref, bc_ref, tail_ref, cw_ref, cb_ref, dtb_ref, alog_ref,
                       dskip_ref, nw_ref, y_ref, hout_ref, convout_ref, ubuf, ht):
    q = SSD_CHUNK
    c = pl.program_id(1)

    @pl.when(c == 0)
    def _():
        ubuf[0:HALO_ROWS, :] = jnp.zeros((HALO_ROWS, SSD_CONV_DIM), F32)
        ht[...] = jnp.zeros(ht.shape, F32)

    ubuf[HALO_ROWS:HALO_ROWS + q, 0:D_MODEL] = xs_ref[...]
    ubuf[HALO_ROWS:HALO_ROWS + q, D_MODEL:SSD_CONV_DIM] = bc_ref[...]

    dt = _softplus(tail_ref[...] + dtb_ref[...])
    da = dt * (-jnp.exp(alog_ref[...]))
    causal = _tril(q)
    acum = _dot_sel(causal.astype(BF16), da)
    acum_t = acum.T
    dt_t = dt.T
    a_last = acum[q - 1:q, :]
    decay_in = jnp.exp(acum)
    w_end = dt * jnp.exp(a_last - acum)
    chunk_decay = jnp.exp(a_last)
    lane_head = lax.broadcasted_iota(jnp.int32, (q, SSD_GROUP_COLS), 1) >> 6
    assert SSD_HEAD_DIM == 1 << 6

    for g in range(SSD_GROUPS):
        lo = g * SSD_GROUP_COLS
        cols = slice(lo, lo + SSD_GROUP_COLS)
        xs_g = _conv_silu(ubuf, cw_ref, cb_ref, lo, SSD_GROUP_COLS, q)
        b_g = _conv_silu(ubuf, cw_ref, cb_ref, D_MODEL + g * SSD_STATE, SSD_STATE, q).astype(BF16)
        c_g = _conv_silu(ubuf, cw_ref, cb_ref, D_MODEL + SSD_BC + g * SSD_STATE, SSD_STATE,
                         q).astype(BF16)
        scores = _dot_nt(c_g, b_g)
        h_g = ht[g]
        y_g = (_dot(c_g, h_g.astype(BF16)) * _expand_group(decay_in, g, lane_head)
               + dskip_ref[:, cols] * xs_g)
        for hh in range(HEADS_PER_GROUP):
            h = g * HEADS_PER_GROUP + hh
            seg = acum[:, h:h + 1] - acum_t[h:h + 1, :]
            decay = jnp.exp(jnp.where(causal, seg, -jnp.inf))
            m = (scores * decay * dt_t[h:h + 1, :]).astype(BF16)
            x_h = jnp.where(lane_head == hh, xs_g, 0.0).astype(BF16)
            y_g = y_g + _dot(m, x_h)
        y_ref[:, cols] = _gated_group_norm(y_g, z_ref[:, cols], nw_ref[:, cols])
        xw = (xs_g * _expand_group(w_end, g, lane_head)).astype(BF16)
        ht[g] = h_g * _expand_group(chunk_decay, g, lane_head[0:1]) + _dot_tn(b_g, xw)

    last = HALO_ROWS + q - (SSD_CONV - 1)
    tail_rows = ubuf[last:last + SSD_CONV - 1, :]
    ubuf[HALO_ROWS - (SSD_CONV - 1):HALO_ROWS, :] = tail_rows

    @pl.when(c == pl.num_programs(1) - 1)
    def _():
        convout_ref[0] = tail_rows
        for g in range(SSD_GROUPS):
            h_t = ht[g].T
            for hh in range(HEADS_PER_GROUP):
                hout_ref[0, g * HEADS_PER_GROUP + hh] = h_t[hh * SSD_HEAD_DIM:(hh + 1) * SSD_HEAD_DIM]


def _row_params(p):
    return pl.BlockSpec(p.shape, lambda *_: (0,) * p.ndim)


def _ssd_prompt(proj, tail, n_seq, seq_len, cw, cb, dtb, alog, dskip_e, nw):
    q = SSD_CHUNK
    nc = seq_len // q

    def col(blk):
        return pl.BlockSpec((q, D_MODEL), lambda b, c: (b * nc + c, blk))

    return pl.pallas_call(
        _ssd_prompt_kernel,
        grid=(n_seq, nc),
        in_specs=[col(COL_Z), col(COL_XS), col(COL_BC),
                  pl.BlockSpec((q, LANES), lambda b, c: (b * nc + c, TAIL_DT)),
                  _row_params(cw), _row_params(cb), _row_params(dtb), _row_params(alog),
                  _row_params(dskip_e), _row_params(nw)],
        out_specs=[pl.BlockSpec((q, D_MODEL), lambda b, c: (b * nc + c, 0)),
                   pl.BlockSpec((1, SSD_HEADS, SSD_HEAD_DIM, SSD_STATE), lambda b, c: (b, 0, 0, 0)),
                   pl.BlockSpec((1, SSD_CONV - 1, SSD_CONV_DIM), lambda b, c: (b, 0, 0))],
        out_shape=[jax.ShapeDtypeStruct((n_seq * seq_len, D_MODEL), F32),
                   jax.ShapeDtypeStruct((n_seq, SSD_HEADS, SSD_HEAD_DIM, SSD_STATE), F32),
                   jax.ShapeDtypeStruct((n_seq, SSD_CONV - 1, SSD_CONV_DIM), F32)],
        scratch_shapes=[pltpu.VMEM((HALO_ROWS + q, SSD_CONV_DIM), F32),
                        pltpu.VMEM((SSD_GROUPS, SSD_STATE, SSD_GROUP_COLS), F32)],
        compiler_params=_cparams(2),
        name="ssd_prompt",
    )(proj, proj, proj, tail, cw, cb, dtb, alog, dskip_e, nw)


def _ssd_sample_kernel(z_ref, xs_ref, bc_ref, tail_ref, h_ref, conv_ref, cw_ref, cb_ref, dtb_ref,
                       alog_ref, dskip_ref, nw_ref, y_ref, hout_ref, convout_ref, ubuf, decay_rows):
    q = xs_ref.shape[0]
    ubuf[0:HALO_ROWS - (SSD_CONV - 1), :] = jnp.zeros((HALO_ROWS - (SSD_CONV - 1), SSD_CONV_DIM), F32)
    ubuf[HALO_ROWS - (SSD_CONV - 1):HALO_ROWS, :] = conv_ref[0]
    ubuf[HALO_ROWS:HALO_ROWS + q, 0:D_MODEL] = xs_ref[...]
    ubuf[HALO_ROWS:HALO_ROWS + q, D_MODEL:SSD_CONV_DIM] = bc_ref[...]
    convout_ref[0] = ubuf[HALO_ROWS + q - (SSD_CONV - 1):HALO_ROWS + q, :]

    dt = _softplus(tail_ref[...] + dtb_ref[...])
    da = dt * (-jnp.exp(alog_ref[...]))
    acum = _dot_sel(_tril(q).astype(BF16), da)
    a_last = acum[q - 1:q, :]
    decay_in = jnp.exp(acum)
    w_end = dt * jnp.exp(a_last - acum)
    decay_rows[...] = _col_bcast(jnp.exp(a_last))[:SSD_HEADS]
    row = lax.broadcasted_iota(jnp.int32, (q, SSD_GROUP_COLS), 0)
    lane_head = lax.broadcasted_iota(jnp.int32, (q, SSD_GROUP_COLS), 1) >> 6

    for g in range(SSD_GROUPS):
        lo = g * SSD_GROUP_COLS
        cols = slice(lo, lo + SSD_GROUP_COLS)
        xs_g = _conv_silu(ubuf, cw_ref, cb_ref, lo, SSD_GROUP_COLS, q)
        b_g = _conv_silu(ubuf, cw_ref, cb_ref, D_MODEL + g * SSD_STATE, SSD_STATE, q)
        c_g = _conv_silu(ubuf, cw_ref, cb_ref, D_MODEL + SSD_BC + g * SSD_STATE, SSD_STATE, q)
        h_g = h_ref[0, g * HEADS_PER_GROUP:(g + 1) * HEADS_PER_GROUP].reshape(
            SSD_GROUP_COLS, SSD_STATE)
        y_g = (_dot_nt(c_g.astype(BF16), h_g.astype(BF16)) * _expand_group(decay_in, g, lane_head)
               + dskip_ref[:, cols] * xs_g)
        acum_g = _expand_group(acum, g, lane_head)
        dt_g = _expand_group(dt, g, lane_head)
        for j in range(q):
            s_j = jnp.sum(c_g * b_g[j:j + 1, :], axis=-1, keepdims=True)
            coef = s_j * jnp.exp(acum_g - acum_g[j:j + 1, :]) * dt_g[j:j + 1, :]
            y_g = y_g + jnp.where(row >= j, coef, 0.0) * xs_g[j:j + 1, :]
        y_ref[:, cols] = _gated_group_norm(y_g, z_ref[:, cols], nw_ref[:, cols])
        upd = _dot_tn(xs_g * _expand_group(w_end, g, lane_head), b_g,
                      precision=HIGHEST)
        for hh in range(HEADS_PER_GROUP):
            h = g * HEADS_PER_GROUP + hh
            rows = slice(hh * SSD_HEAD_DIM, (hh + 1) * SSD_HEAD_DIM)
            hout_ref[0, h] = h_g[rows] * decay_rows[h:h + 1, :] + upd[rows]


def _ssd_sample(proj, tail, row0, n_seq, seq_len, h0, conv0, cw, cb, dtb, alog, dskip_e, nw):
    q = seq_len
    blk0 = row0 // q

    def col(blk):
        return pl.BlockSpec((q, D_MODEL), lambda b: (blk0 + b, blk))

    state_spec = pl.BlockSpec((1, SSD_HEADS, SSD_HEAD_DIM, SSD_STATE), lambda b: (b, 0, 0, 0))
    conv_spec = pl.BlockSpec((1, SSD_CONV - 1, SSD_CONV_DIM), lambda b: (b, 0, 0))
    return pl.pallas_call(
        _ssd_sample_kernel,
        grid=(n_seq,),
        in_specs=[col(COL_Z), col(COL_XS), col(COL_BC),
                  pl.BlockSpec((q, LANES), lambda b: (blk0 + b, TAIL_DT)),
                  state_spec, conv_spec,
                  _row_params(cw), _row_params(cb), _row_params(dtb), _row_params(alog),
                  _row_params(dskip_e), _row_params(nw)],
        out_specs=[pl.BlockSpec((q, D_MODEL), lambda b: (b, 0)), state_spec, conv_spec],
        out_shape=[jax.ShapeDtypeStruct((n_seq * seq_len, D_MODEL), F32),
                   jax.ShapeDtypeStruct(h0.shape, F32),
                   jax.ShapeDtypeStruct(conv0.shape, F32)],
        scratch_shapes=[pltpu.VMEM((HALO_ROWS + q, SSD_CONV_DIM), F32),
                        pltpu.VMEM((SSD_HEADS, SSD_STATE), F32)],
        compiler_params=_cparams(1),
        name="ssd_sample",
    )(proj, proj, proj, tail, h0, conv0, cw, cb, dtb, alog, dskip_e, nw)


def _gla_chunk(qk_ref, v_ref, r_ref, alo_ref, wa_ref, ba_ref, nw_ref, y_ref, get_state, put_state):
    q = v_ref.shape[0]
    a_lo = alo_ref[...][:, :GLA_RANK].astype(BF16)
    gk = -_softplus(-(_dot(a_lo, wa_ref[...]) + ba_ref[...])) / GLA_TAU
    causal = _tril(q)
    bcum = _dot_sel(causal.astype(BF16), gk)
    for h in range(GLA_HEADS):
        kc = slice(h * GLA_DK, (h + 1) * GLA_DK)
        vc = slice(h * GLA_DV, (h + 1) * GLA_DV)
        b_h = bcum[:, kc]
        b_last = b_h[q - 1:q, :]
        q_h = qk_ref[:, kc] * (GLA_DK ** -0.5)
        k_h = qk_ref[:, GLA_KEY_DIM + h * GLA_DK:GLA_KEY_DIM + (h + 1) * GLA_DK]
        v_h = v_ref[:, vc].astype(BF16)
        q_in = (q_h * jnp.exp(b_h)).astype(BF16)
        k_in = (k_h * jnp.exp(-b_h)).astype(BF16)
        att = jnp.where(causal, _dot_nt(q_in, k_in), 0.0).astype(BF16)
        s_h = get_state(h)
        o = _dot(att, v_h) + _dot(q_in, s_h.astype(BF16))
        k_end = (k_h * jnp.exp(b_last - b_h)).astype(BF16)
        d_col = _col_bcast(jnp.exp(b_last))
        put_state(h, s_h * jnp.concatenate([d_col, d_col], axis=1) + _dot_tn(k_end, v_h))
        ms = jnp.mean(o * o, axis=-1, keepdims=True)
        y_ref[:, vc] = o * lax.rsqrt(ms + EPS) * nw_ref[...] * _silu(r_ref[:, vc])


def _gla_prompt_kernel(qk_ref, v_ref, r_ref, alo_ref, wa_ref, ba_ref, nw_ref, y_ref, sout_ref, st):
    c = pl.program_id(1)

    @pl.when(c == 0)
    def _():
        st[...] = jnp.zeros(st.shape, F32)

    def put(h, val):
        st[h] = val

    _gla_chunk(qk_ref, v_ref, r_ref, alo_ref, wa_ref, ba_ref, nw_ref, y_ref, lambda h: st[h], put)

    @pl.when(c == pl.num_programs(1) - 1)
    def _():
        sout_ref[0] = st[...]


def _gla_prompt(proj, tail, n_seq, seq_len, wa, ba, nw):
    q = GLA_CHUNK
    nc = seq_len // q

    def col(blk):
        return pl.BlockSpec((q, D_MODEL), lambda b, c: (b * nc + c, blk))

    return pl.pallas_call(
        _gla_prompt_kernel,
        grid=(n_seq, nc),
        in_specs=[col(COL_QK), col(COL_V), col(COL_R),
                  pl.BlockSpec((q, LANES), lambda b, c: (b * nc + c, TAIL_ALO)),
                  _row_params(wa), _row_params(ba), _row_params(nw)],
        out_specs=[pl.BlockSpec((q, D_MODEL), lambda b, c: (b * nc + c, 0)),
                   pl.BlockSpec((1, GLA_HEADS, GLA_DK, GLA_DV), lambda b, c: (b, 0, 0, 0))],
        out_shape=[jax.ShapeDtypeStruct((n_seq * seq_len, D_MODEL), F32),
                   jax.ShapeDtypeStruct((n_seq, GLA_HEADS, GLA_DK, GLA_DV), F32)],
        scratch_shapes=[pltpu.VMEM((GLA_HEADS, GLA_DK, GLA_DV), F32)],
        compiler_params=_cparams(2),
        name="gla_prompt",
    )(proj, proj, proj, tail, wa, ba, nw)


def _gla_sample_kernel(qk_ref, v_ref, r_ref, alo_ref, s_ref, wa_ref, ba_ref, nw_ref, y_ref, sout_ref):
    def put(h, val):
        sout_ref[0, h] = val

    _gla_chunk(qk_ref, v_ref, r_ref, alo_ref, wa_ref, ba_ref, nw_ref, y_ref,
               lambda h: s_ref[0, h], put)


def _gla_sample(proj, tail, row0, n_seq, seq_len, s0, wa, ba, nw):
    q = seq_len
    blk0 = row0 // q

    def col(blk):
        return pl.BlockSpec((q, D_MODEL), lambda b: (blk0 + b, blk))

    state_spec = pl.BlockSpec((1, GLA_HEADS, GLA_DK, GLA_DV), lambda b: (b, 0, 0, 0))
    return pl.pallas_call(
        _gla_sample_kernel,
        grid=(n_seq,),
        in_specs=[col(COL_QK), col(COL_V), col(COL_R),
                  pl.BlockSpec((q, LANES), lambda b: (blk0 + b, TAIL_ALO)),
                  state_spec, _row_params(wa), _row_params(ba), _row_params(nw)],
        out_specs=[pl.BlockSpec((q, D_MODEL), lambda b: (b, 0)), state_spec],
        out_shape=[jax.ShapeDtypeStruct((n_seq * seq_len, D_MODEL), F32),
                   jax.ShapeDtypeStruct(s0.shape, F32)],
        compiler_params=_cparams(1),
        name="gla_sample",
    )(proj, proj, proj, tail, s0, wa, ba, nw)


def _layernorm(x, g, b):
    mu = jnp.mean(x, axis=-1, keepdims=True)
    xc = x - mu
    var = jnp.mean(xc * xc, axis=-1, keepdims=True)
    return xc * lax.rsqrt(var + EPS) * g + b


def _route(logits):
    lane = lax.broadcasted_iota(jnp.int32, logits.shape, 1)
    neg = -jnp.inf
    big = ROUTER_LANES
    glog = jnp.where(lane < MOE_GROUPS, logits, neg)
    gmax = jnp.max(glog, axis=-1, keepdims=True)
    g_sel = jnp.min(jnp.where(glog == gmax, lane, big), axis=-1, keepdims=True)
    p_g = 1.0 / jnp.sum(jnp.exp(glog - gmax), axis=-1, keepdims=True)
    e_lane = lane - MOE_GROUPS
    in_group = (e_lane >= 0) & (e_lane < MOE_EXPERTS) & (e_lane // MOE_EPG == g_sel)
    el = jnp.where(in_group, logits, neg)
    v1 = jnp.max(el, axis=-1, keepdims=True)
    i1 = jnp.min(jnp.where(el == v1, lane, big), axis=-1, keepdims=True)
    el2 = jnp.where(lane == i1, neg, el)
    v2 = jnp.max(el2, axis=-1, keepdims=True)
    i2 = jnp.min(jnp.where(el2 == v2, lane, big), axis=-1, keepdims=True)
    e2 = jnp.exp(v2 - v1)
    w1 = p_g / (1.0 + e2)
    w2 = p_g * e2 / (1.0 + e2)
    return jnp.where(lane == i1, w1, 0.0) + jnp.where(lane == i2, w2, 0.0)


def _merge_kernel(ys_ref, yg_ref, gs_ref, gg_ref, x_ref, wo_ref, g_ref, b_ref, wrh_ref, wrl_ref,
                  br_ref, h1_ref, h1b_ref, comb_ref):
    merged = jax.nn.sigmoid(gs_ref[...]) * ys_ref[...] + jax.nn.sigmoid(gg_ref[...]) * yg_ref[...]
    mix = _dot(merged.astype(BF16), wo_ref[...])
    h1 = _layernorm(DEEPNORM_ALPHA * x_ref[...] + mix, g_ref[...], b_ref[...])
    h1_ref[...] = h1
    h_hi = h1.astype(BF16)
    h1b_ref[...] = h_hi
    h_lo = (h1 - h_hi.astype(F32)).astype(BF16)
    logits = (_dot(h_hi, wrh_ref[...]) + _dot(h_lo, wrh_ref[...]) + _dot(h_hi, wrl_ref[...])
              + br_ref[...])
    comb_ref[...] = _route(logits)


def _merge(y_ssd, y_gla, proj, row0, x2d, wo, g, b, wr_hi, wr_lo, br, tm):
    m = x2d.shape[0]
    blk0 = row0 // tm

    def rows(i):
        return (i, 0)

    full = pl.BlockSpec((tm, D_MODEL), rows)
    return pl.pallas_call(
        _merge_kernel,
        grid=(m // tm,),
        in_specs=[full, full,
                  pl.BlockSpec((tm, D_MODEL), lambda i: (blk0 + i, COL_GS)),
                  pl.BlockSpec((tm, D_MODEL), lambda i: (blk0 + i, COL_GG)),
                  full, _row_params(wo), _row_params(g), _row_params(b),
                  _row_params(wr_hi), _row_params(wr_lo), _row_params(br)],
        out_specs=[full, full, pl.BlockSpec((tm, ROUTER_LANES), rows)],
        out_shape=[jax.ShapeDtypeStruct((m, D_MODEL), F32),
                   jax.ShapeDtypeStruct((m, D_MODEL), BF16),
                   jax.ShapeDtypeStruct((m, ROUTER_LANES), F32)],
        compiler_params=_cparams(1),
        name="merge_outproj_ln_router",
    )(y_ssd, y_gla, proj, proj, x2d, wo, g, b, wr_hi, wr_lo, br)


def _moe_kernel(hb_ref, comb_ref, wg_ref, wu_ref, wd_ref, h1_ref, g_ref, b_ref, y_ref, acc):
    e = pl.program_id(1)

    @pl.when(e == 0)
    def _():
        acc[...] = jnp.zeros(acc.shape, F32)

    x = hb_ref[...]
    hid = _silu(_dot(x, wg_ref[0])) * _dot(x, wu_ref[0])
    comb = comb_ref[...]
    lane = lax.broadcasted_iota(jnp.int32, comb.shape, 1)
    w_e = jnp.sum(jnp.where(lane == e + MOE_GROUPS, comb, 0.0), axis=-1, keepdims=True)
    acc[...] += _dot((hid * w_e).astype(BF16), wd_ref[0])

    @pl.when(e == pl.num_programs(1) - 1)
    def _():
        y_ref[...] = _layernorm(DEEPNORM_ALPHA * h1_ref[...] + acc[...], g_ref[...], b_ref[...])


def _moe(h1, h1b, comb, wg, wu, wd, g, b, tm):
    m = h1.shape[0]
    rows = lambda i, e: (i, 0)
    return pl.pallas_call(
        _moe_kernel,
        grid=(m // tm, MOE_EXPERTS),
        in_specs=[pl.BlockSpec((tm, D_MODEL), rows),
                  pl.BlockSpec((tm, ROUTER_LANES), rows),
                  pl.BlockSpec((1, D_MODEL, MOE_FF), lambda i, e: (e, 0, 0)),
                  pl.BlockSpec((1, D_MODEL, MOE_FF), lambda i, e: (e, 0, 0)),
                  pl.BlockSpec((1, MOE_FF, D_MODEL), lambda i, e: (e, 0, 0)),
                  pl.BlockSpec((tm, D_MODEL), rows),
                  _row_params(g), _row_params(b)],
        out_specs=pl.BlockSpec((tm, D_MODEL), rows),
        out_shape=jax.ShapeDtypeStruct((m, D_MODEL), F32),
        scratch_shapes=[pltpu.VMEM((tm, D_MODEL), F32)],
        compiler_params=_cparams(2),
        name="experts_ln",
    )(h1b, comb, wg, wu, wd, h1, g, b)


def _pad_lanes(v, width=LANES):
    v = v.reshape(1, -1)
    return jnp.pad(v, ((0, 0), (0, width - v.shape[1])))


def kernel(x_prompt, x_sample, state_ssd, state_conv, state_gla, w_in, conv_w, conv_b, dt_bias, a_log, d_skip, ssd_norm_w, gla_w_a2, gla_b_a, gla_norm_w, w_out, ln1_g, ln1_b, w_router_group, b_router_group, w_router_expert, b_router_expert, w_gate, w_up, w_down, ln2_g, ln2_b):
    assert w_in.shape[0] == DEPTH == 1
    n_p, len_p, _ = x_prompt.shape
    n_s, len_s, _ = x_sample.shape
    rows_p, rows_s = n_p * len_p, n_s * len_s
    xp = x_prompt.reshape(rows_p, D_MODEL)
    xs = x_sample.reshape(rows_s, D_MODEL)

    offs = [0]
    for s in IN_SPLIT_SIZES:
        offs.append(offs[-1] + s)
    seg = [w_in[0][:, offs[i]:offs[i + 1]] for i in range(len(IN_SPLIT_SIZES))]
    wz, wxbc, wdt, wq, wk, wv, wr, walo, wgs, wgg = seg
    w_main = jnp.concatenate([wz, wxbc, wq, wk, wv, wr, wgs, wgg], axis=1).astype(BF16)
    w_tail = jnp.concatenate([jnp.pad(wdt, ((0, 0), (0, LANES - SSD_HEADS))),
                              jnp.pad(walo, ((0, 0), (0, LANES - GLA_RANK)))], axis=1).astype(BF16)
    x_all = jnp.concatenate([xp, xs], axis=0).astype(BF16)

    proj = _matmul(x_all, w_main, 512, D_MODEL)
    tail = _matmul(x_all, w_tail, 512, 2 * LANES)

    cw, cb = conv_w[0], conv_b[0].reshape(1, -1)
    dtb, alog = _pad_lanes(dt_bias[0]), _pad_lanes(a_log[0])
    dskip_e = jnp.repeat(d_skip[0], SSD_HEAD_DIM).reshape(1, -1)
    nw_ssd = ssd_norm_w[0].reshape(1, -1)
    wa, ba, nw_gla = gla_w_a2[0].astype(BF16), gla_b_a[0].reshape(1, -1), gla_norm_w[0].reshape(1, -1)

    ys_p, ssd_p, conv_p = _ssd_prompt(proj, tail, n_p, len_p, cw, cb, dtb, alog, dskip_e, nw_ssd)
    ys_s, ssd_s, conv_s = _ssd_sample(proj, tail, rows_p, n_s, len_s, state_ssd[0], state_conv[0],
                                      cw, cb, dtb, alog, dskip_e, nw_ssd)
    yg_p, gla_p = _gla_prompt(proj, tail, n_p, len_p, wa, ba, nw_gla)
    yg_s, gla_s = _gla_sample(proj, tail, rows_p, n_s, len_s, state_gla[0], wa, ba, nw_gla)

    wo = w_out[0].astype(BF16)
    w_r = jnp.pad(jnp.concatenate([w_router_group[0], w_router_expert[0]], axis=1),
                  ((0, 0), (0, ROUTER_LANES - MOE_GROUPS - MOE_EXPERTS)))
    wr_hi = w_r.astype(BF16)
    wr_lo = (w_r - wr_hi.astype(F32)).astype(BF16)
    b_r = _pad_lanes(jnp.concatenate([b_router_group[0], b_router_expert[0]]), ROUTER_LANES)
    g1, b1 = ln1_g[0].reshape(1, -1), ln1_b[0].reshape(1, -1)
    g2, b2 = ln2_g[0].reshape(1, -1), ln2_b[0].reshape(1, -1)
    wg, wu, wd = w_gate[0].astype(BF16), w_up[0].astype(BF16), w_down[0].astype(BF16)

    outs = []
    for ys_, yg_, row0, x2d in ((ys_p, yg_p, 0, xp), (ys_s, yg_s, rows_p, xs)):
        h1, h1b, comb = _merge(ys_, yg_, proj, row0, x2d, wo, g1, b1, wr_hi, wr_lo, b_r, 256)
        outs.append(_moe(h1, h1b, comb, wg, wu, wd, g2, b2, 512))
    y_p = outs[0].reshape(x_prompt.shape)
    y_s = outs[1].reshape(x_sample.shape)
    return (y_p, y_s, ssd_p[None], conv_p[None], gla_p[None], ssd_s[None], conv_s[None], gla_s[None])
```

```python
import functools
import math

import jax
import jax.numpy as jnp
from jax import lax
from jax.experimental import pallas as pl
from jax.experimental.pallas import tpu as pltpu

F32 = jnp.float32
BF16 = jnp.bfloat16
HIGHEST = lax.Precision.HIGHEST

D_MODEL = 2048
SSD_HEADS = 32
SSD_HEAD_DIM = 64
SSD_GROUPS = 8
SSD_STATE = 128
SSD_CONV = 4
SSD_CHUNK = 128
SSD_BC = SSD_GROUPS * SSD_STATE
SSD_CONV_DIM = D_MODEL + 2 * SSD_BC
SSD_GROUP_COLS = D_MODEL // SSD_GROUPS
HEADS_PER_GROUP = SSD_HEADS // SSD_GROUPS
GLA_HEADS = 8
GLA_DK = 128
GLA_DV = 256
GLA_KEY_DIM = GLA_HEADS * GLA_DK
GLA_RANK = 16
GLA_TAU = 16.0
GLA_CHUNK = 64
MOE_GROUPS = 4
MOE_EPG = 8
MOE_EXPERTS = 32
MOE_FF = 512
DEPTH = 1
DEEPNORM_ALPHA = (2.0 * DEPTH) ** 0.25
EPS = 1e-5
IN_SPLIT_SIZES = (D_MODEL, SSD_CONV_DIM, SSD_HEADS, GLA_KEY_DIM, GLA_KEY_DIM, D_MODEL, D_MODEL,
                  GLA_RANK, D_MODEL, D_MODEL)

LANES = 128
HALO_ROWS = 8
COL_Z, COL_XS, COL_BC, COL_QK, COL_V, COL_R, COL_GS, COL_GG = range(8)
TAIL_DT, TAIL_ALO = 0, 1
ROUTER_LANES = 128
ROUTE_E1, ROUTE_E2, ROUTE_W1, ROUTE_W2 = 0, 1, 2, 3
MERGE_ROWS = 256
EXPERT_ROWS = 256
GATHER_WINDOW = 512
VMEM_LIMIT = 56 * 1024 * 1024


def _cparams(n_axes):
    return pltpu.CompilerParams(dimension_semantics=("arbitrary",) * n_axes,
                                vmem_limit_bytes=VMEM_LIMIT)


def _silu(x):
    return x * jax.nn.sigmoid(x)


def _softplus(x):
    return jnp.maximum(x, 0.0) + jnp.log1p(jnp.exp(-jnp.abs(x)))


def _tril(n):
    r = lax.broadcasted_iota(jnp.int32, (n, n), 0)
    c = lax.broadcasted_iota(jnp.int32, (n, n), 1)
    return r >= c


def _expand_group(p, g, lane_head):
    h0 = g * HEADS_PER_GROUP
    out = p[:, h0 + HEADS_PER_GROUP - 1:h0 + HEADS_PER_GROUP]
    for hh in range(HEADS_PER_GROUP - 2, -1, -1):
        out = jnp.where(lane_head == hh, p[:, h0 + hh:h0 + hh + 1], out)
    return out


def _split3(x):
    hi = x.astype(BF16)
    r = x - hi.astype(F32)
    mid = r.astype(BF16)
    lo = (r - mid.astype(F32)).astype(BF16)
    return hi, mid, lo


def _dot_sel(sel, x):
    hi, mid, lo = _split3(x)
    return _dot(sel, lo) + _dot(sel, mid) + _dot(sel, hi)


def _col_bcast(row):
    return jnp.broadcast_to(row, (LANES, LANES)).T


def _dot(a, b, **kw):
    return jnp.dot(a, b, preferred_element_type=F32, **kw)


def _dot_nt(a, b, **kw):
    return lax.dot_general(a, b, (((1,), (1,)), ((), ())), preferred_element_type=F32, **kw)


def _dot_tn(a, b, **kw):
    return lax.dot_general(a, b, (((0,), (0,)), ((), ())), preferred_element_type=F32, **kw)


def _matmul_kernel(x_ref, w_ref, o_ref):
    o_ref[...] = _dot(x_ref[...], w_ref[...])


def _matmul(x, w, tm, tn):
    m, k = x.shape
    n = w.shape[1]
    return pl.pallas_call(
        _matmul_kernel,
        grid=(n // tn, m // tm),
        in_specs=[pl.BlockSpec((tm, k), lambda j, i: (i, 0)),
                  pl.BlockSpec((k, tn), lambda j, i: (0, j))],
        out_specs=pl.BlockSpec((tm, tn), lambda j, i: (i, j)),
        out_shape=jax.ShapeDtypeStruct((m, n), F32),
        compiler_params=_cparams(2),
        name="in_proj",
    )(x, w)


def _conv_silu(ubuf, cw_ref, cb_ref, lo, width, rows):
    acc = cb_ref[:, lo:lo + width]
    for i in range(SSD_CONV):
        r0 = HALO_ROWS - (SSD_CONV - 1) + i
        acc = acc + cw_ref[i:i + 1, lo:lo + width] * ubuf[r0:r0 + rows, lo:lo + width]
    return _silu(acc)


def _gated_group_norm(y_g, z_g, nw_g):
    yg = y_g * _silu(z_g)
    ms = jnp.mean(yg * yg, axis=-1, keepdims=True)
    return yg * lax.rsqrt(ms + EPS) * nw_g


def _ssd_prompt_kernel(z_ref, xs_ref, bc_ref, tail_ref, cw_ref, cb_ref, dtb_ref, alog_ref,
                       dskip_ref, nw_ref, y_ref, hout_ref, convout_ref, ubuf, ht):
    q = SSD_CHUNK
    c = pl.program_id(1)

    @pl.when(c == 0)
    def _():
        ubuf[0:HALO_ROWS, :] = jnp.zeros((HALO_ROWS, SSD_CONV_DIM), F32)
        ht[...] = jnp.zeros(ht.shape, F32)

    ubuf[HALO_ROWS:HALO_ROWS + q, 0:D_MODEL] = xs_ref[...]
    ubuf[HALO_ROWS:HALO_ROWS + q, D_MODEL:SSD_CONV_DIM] = bc_ref[...]

    dt = _softplus(tail_ref[...] + dtb_ref[...])
    da = dt * (-jnp.exp(alog_ref[...]))
    causal = _tril(q)
    acum = _dot_sel(causal.astype(BF16), da)
    acum_t = acum.T
    dt_t = dt.T
    a_last = acum[q - 1:q, :]
    decay_in = jnp.exp(acum)
    w_end = dt * jnp.exp(a_last - acum)
    chunk_decay = jnp.exp(a_last)
    lane_head = lax.broadcasted_iota(jnp.int32, (q, SSD_GROUP_COLS), 1) >> 6
    assert SSD_HEAD_DIM == 1 << 6

    for g in range(SSD_GROUPS):
        lo = g * SSD_GROUP_COLS
        cols = slice(lo, lo + SSD_GROUP_COLS)
        xs_g = _conv_silu(ubuf, cw_ref, cb_ref, lo, SSD_GROUP_COLS, q)
        b_g = _conv_silu(ubuf, cw_ref, cb_ref, D_MODEL + g * SSD_STATE, SSD_STATE, q).astype(BF16)
        c_g = _conv_silu(ubuf, cw_ref, cb_ref, D_MODEL + SSD_BC + g * SSD_STATE, SSD_STATE,
                         q).astype(BF16)
        scores = _dot_nt(c_g, b_g)
        h_g = ht[g]
        y_g = (_dot(c_g, h_g.astype(BF16)) * _expand_group(decay_in, g, lane_head)
               + dskip_ref[:, cols] * xs_g)
        for hh in range(HEADS_PER_GROUP):
            h = g * HEADS_PER_GROUP + hh
            seg = acum[:, h:h + 1] - acum_t[h:h + 1, :]
            decay = jnp.exp(jnp.where(causal, seg, -jnp.inf))
            m = (scores * decay * dt_t[h:h + 1, :]).astype(BF16)
            x_h = jnp.where(lane_head == hh, xs_g, 0.0).astype(BF16)
            y_g = y_g + _dot(m, x_h)
        y_ref[:, cols] = _gated_group_norm(y_g, z_ref[:, cols], nw_ref[:, cols])
        xw = (xs_g * _expand_group(w_end, g, lane_head)).astype(BF16)
        ht[g] = h_g * _expand_group(chunk_decay, g, lane_head[0:1]) + _dot_tn(b_g, xw)

    last = HALO_ROWS + q - (SSD_CONV - 1)
    tail_rows = ubuf[last:last + SSD_CONV - 1, :]
    ubuf[HALO_ROWS - (SSD_CONV - 1):HALO_ROWS, :] = tail_rows

    @pl.when(c == pl.num_programs(1) - 1)
    def _():
        convout_ref[0] = tail_rows
        for g in range(SSD_GROUPS):
            h_t = ht[g].T
            for hh in range(HEADS_PER_GROUP):
                hout_ref[0, g * HEADS_PER_GROUP + hh] = h_t[hh * SSD_HEAD_DIM:(hh + 1) * SSD_HEAD_DIM]


def _row_params(p):
    return pl.BlockSpec(p.shape, lambda *_: (0,) * p.ndim)


def _ssd_prompt(proj, tail, n_seq, seq_len, cw, cb, dtb, alog, dskip_e, nw):
    q = SSD_CHUNK
    nc = seq_len // q

    def col(blk):
        return pl.BlockSpec((q, D_MODEL), lambda b, c: (b * nc + c, blk))

    return pl.pallas_call(
        _ssd_prompt_kernel,
        grid=(n_seq, nc),
        in_specs=[col(COL_Z), col(COL_XS), col(COL_BC),
                  pl.BlockSpec((q, LANES), lambda b, c: (b * nc + c, TAIL_DT)),
                  _row_params(cw), _row_params(cb), _row_params(dtb), _row_params(alog),
                  _row_params(dskip_e), _row_params(nw)],
        out_specs=[pl.BlockSpec((q, D_MODEL), lambda b, c: (b * nc + c, 0)),
                   pl.BlockSpec((1, SSD_HEADS, SSD_HEAD_DIM, SSD_STATE), lambda b, c: (b, 0, 0, 0)),
                   pl.BlockSpec((1, SSD_CONV - 1, SSD_CONV_DIM), lambda b, c: (b, 0, 0))],
        out_shape=[jax.ShapeDtypeStruct((n_seq * seq_len, D_MODEL), F32),
                   jax.ShapeDtypeStruct((n_seq, SSD_HEADS, SSD_HEAD_DIM, SSD_STATE), F32),
                   jax.ShapeDtypeStruct((n_seq, SSD_CONV - 1, SSD_CONV_DIM), F32)],
        scratch_shapes=[pltpu.VMEM((HALO_ROWS + q, SSD_CONV_DIM), F32),
                        pltpu.VMEM((SSD_GROUPS, SSD_STATE, SSD_GROUP_COLS), F32)],
        compiler_params=_cparams(2),
        name="ssd_prompt",
    )(proj, proj, proj, tail, cw, cb, dtb, alog, dskip_e, nw)


def _ssd_sample_kernel(z_ref, xs_ref, bc_ref, tail_ref, h_ref, conv_ref, cw_ref, cb_ref, dtb_ref,
                       alog_ref, dskip_ref, nw_ref, y_ref, hout_ref, convout_ref, ubuf, decay_rows):
    q = xs_ref.shape[0]
    ubuf[0:HALO_ROWS - (SSD_CONV - 1), :] = jnp.zeros((HALO_ROWS - (SSD_CONV - 1), SSD_CONV_DIM), F32)
    ubuf[HALO_ROWS - (SSD_CONV - 1):HALO_ROWS, :] = conv_ref[0]
    ubuf[HALO_ROWS:HALO_ROWS + q, 0:D_MODEL] = xs_ref[...]
    ubuf[HALO_ROWS:HALO_ROWS + q, D_MODEL:SSD_CONV_DIM] = bc_ref[...]
    convout_ref[0] = ubuf[HALO_ROWS + q - (SSD_CONV - 1):HALO_ROWS + q, :]

    dt = _softplus(tail_ref[...] + dtb_ref[...])
    da = dt * (-jnp.exp(alog_ref[...]))
    acum = _dot_sel(_tril(q).astype(BF16), da)
    a_last = acum[q - 1:q, :]
    decay_in = jnp.exp(acum)
    w_end = dt * jnp.exp(a_last - acum)
    decay_rows[...] = _col_bcast(jnp.exp(a_last))[:SSD_HEADS]
    row = lax.broadcasted_iota(jnp.int32, (q, SSD_GROUP_COLS), 0)
    lane_head = lax.broadcasted_iota(jnp.int32, (q, SSD_GROUP_COLS), 1) >> 6

    for g in range(SSD_GROUPS):
        lo = g * SSD_GROUP_COLS
        cols = slice(lo, lo + SSD_GROUP_COLS)
        xs_g = _conv_silu(ubuf, cw_ref, cb_ref, lo, SSD_GROUP_COLS, q)
        b_g = _conv_silu(ubuf, cw_ref, cb_ref, D_MODEL + g * SSD_STATE, SSD_STATE, q)
        c_g = _conv_silu(ubuf, cw_ref, cb_ref, D_MODEL + SSD_BC + g * SSD_STATE, SSD_STATE, q)
        h_g = h_ref[0, g * HEADS_PER_GROUP:(g + 1) * HEADS_PER_GROUP].reshape(
            SSD_GROUP_COLS, SSD_STATE)
        y_g = (_dot_nt(c_g.astype(BF16), h_g.astype(BF16)) * _expand_group(decay_in, g, lane_head)
               + dskip_ref[:, cols] * xs_g)
        acum_g = _expand_group(acum, g, lane_head)
        dt_g = _expand_group(dt, g, lane_head)
        for j in range(q):
            s_j = jnp.sum(c_g * b_g[j:j + 1, :], axis=-1, keepdims=True)
            coef = s_j * jnp.exp(acum_g - acum_g[j:j + 1, :]) * dt_g[j:j + 1, :]
            y_g = y_g + jnp.where(row >= j, coef, 0.0) * xs_g[j:j + 1, :]
        y_ref[:, cols] = _gated_group_norm(y_g, z_ref[:, cols], nw_ref[:, cols])
        upd = _dot_tn(xs_g * _expand_group(w_end, g, lane_head), b_g,
                      precision=HIGHEST)
        for hh in range(HEADS_PER_GROUP):
            h = g * HEADS_PER_GROUP + hh
            rows = slice(hh * SSD_HEAD_DIM, (hh + 1) * SSD_HEAD_DIM)
            hout_ref[0, h] = h_g[rows] * decay_rows[h:h + 1, :] + upd[rows]


def _ssd_sample(proj, tail, row0, n_seq, seq_len, h0, conv0, cw, cb, dtb, alog, dskip_e, nw):
    q = seq_len
    blk0 = row0 // q

    def col(blk):
        return pl.BlockSpec((q, D_MODEL), lambda b: (blk0 + b, blk))

    state_spec = pl.BlockSpec((1, SSD_HEADS, SSD_HEAD_DIM, SSD_STATE), lambda b: (b, 0, 0, 0))
    conv_spec = pl.BlockSpec((1, SSD_CONV - 1, SSD_CONV_DIM), lambda b: (b, 0, 0))
    return pl.pallas_call(
        _ssd_sample_kernel,
        grid=(n_seq,),
        in_specs=[col(COL_Z), col(COL_XS), col(COL_BC),
                  pl.BlockSpec((q, LANES), lambda b: (blk0 + b, TAIL_DT)),
                  state_spec, conv_spec,
                  _row_params(cw), _row_params(cb), _row_params(dtb), _row_params(alog),
                  _row_params(dskip_e), _row_params(nw)],
        out_specs=[pl.BlockSpec((q, D_MODEL), lambda b: (b, 0)), state_spec, conv_spec],
        out_shape=[jax.ShapeDtypeStruct((n_seq * seq_len, D_MODEL), F32),
                   jax.ShapeDtypeStruct(h0.shape, F32),
                   jax.ShapeDtypeStruct(conv0.shape, F32)],
        scratch_shapes=[pltpu.VMEM((HALO_ROWS + q, SSD_CONV_DIM), F32),
                        pltpu.VMEM((SSD_HEADS, SSD_STATE), F32)],
        compiler_params=_cparams(1),
        name="ssd_sample",
    )(proj, proj, proj, tail, h0, conv0, cw, cb, dtb, alog, dskip_e, nw)


def _gla_chunk(qk_ref, v_ref, r_ref, alo_ref, wa_ref, ba_ref, nw_ref, y_ref, get_state, put_state):
    q = v_ref.shape[0]
    a_lo = alo_ref[...][:, :GLA_RANK].astype(BF16)
    gk = -_softplus(-(_dot(a_lo, wa_ref[...]) + ba_ref[...])) / GLA_TAU
    causal = _tril(q)
    bcum = _dot_sel(causal.astype(BF16), gk)
    for h in range(GLA_HEADS):
        kc = slice(h * GLA_DK, (h + 1) * GLA_DK)
        vc = slice(h * GLA_DV, (h + 1) * GLA_DV)
        b_h = bcum[:, kc]
        b_last = b_h[q - 1:q, :]
        q_h = qk_ref[:, kc] * (GLA_DK ** -0.5)
        k_h = qk_ref[:, GLA_KEY_DIM + h * GLA_DK:GLA_KEY_DIM + (h + 1) * GLA_DK]
        v_h = v_ref[:, vc].astype(BF16)
        q_in = (q_h * jnp.exp(b_h)).astype(BF16)
        k_in = (k_h * jnp.exp(-b_h)).astype(BF16)
        att = jnp.where(causal, _dot_nt(q_in, k_in), 0.0).astype(BF16)
        s_h = get_state(h)
        o = _dot(att, v_h) + _dot(q_in, s_h.astype(BF16))
        k_end = (k_h * jnp.exp(b_last - b_h)).astype(BF16)
        d_col = _col_bcast(jnp.exp(b_last))
        put_state(h, s_h * jnp.concatenate([d_col, d_col], axis=1) + _dot_tn(k_end, v_h))
        ms = jnp.mean(o * o, axis=-1, keepdims=True)
        y_ref[:, vc] = o * lax.rsqrt(ms + EPS) * nw_ref[...] * _silu(r_ref[:, vc])


def _gla_prompt_kernel(qk_ref, v_ref, r_ref, alo_ref, wa_ref, ba_ref, nw_ref, y_ref, sout_ref, st):
    c = pl.program_id(1)

    @pl.when(c == 0)
    def _():
        st[...] = jnp.zeros(st.shape, F32)

    def put(h, val):
        st[h] = val

    _gla_chunk(qk_ref, v_ref, r_ref, alo_ref, wa_ref, ba_ref, nw_ref, y_ref, lambda h: st[h], put)

    @pl.when(c == pl.num_programs(1) - 1)
    def _():
        sout_ref[0] = st[...]


def _gla_prompt(proj, tail, n_seq, seq_len, wa, ba, nw):
    q = GLA_CHUNK
    nc = seq_len // q

    def col(blk):
        return pl.BlockSpec((q, D_MODEL), lambda b, c: (b * nc + c, blk))

    return pl.pallas_call(
        _gla_prompt_kernel,
        grid=(n_seq, nc),
        in_specs=[col(COL_QK), col(COL_V), col(COL_R),
                  pl.BlockSpec((q, LANES), lambda b, c: (b * nc + c, TAIL_ALO)),
                  _row_params(wa), _row_params(ba), _row_params(nw)],
        out_specs=[pl.BlockSpec((q, D_MODEL), lambda b, c: (b * nc + c, 0)),
                   pl.BlockSpec((1, GLA_HEADS, GLA_DK, GLA_DV), lambda b, c: (b, 0, 0, 0))],
        out_shape=[jax.ShapeDtypeStruct((n_seq * seq_len, D_MODEL), F32),
                   jax.ShapeDtypeStruct((n_seq, GLA_HEADS, GLA_DK, GLA_DV), F32)],
        scratch_shapes=[pltpu.VMEM((GLA_HEADS, GLA_DK, GLA_DV), F32)],
        compiler_params=_cparams(2),
        name="gla_prompt",
    )(proj, proj, proj, tail, wa, ba, nw)


def _gla_sample_kernel(qk_ref, v_ref, r_ref, alo_ref, s_ref, wa_ref, ba_ref, nw_ref, y_ref, sout_ref):
    def put(h, val):
        sout_ref[0, h] = val

    _gla_chunk(qk_ref, v_ref, r_ref, alo_ref, wa_ref, ba_ref, nw_ref, y_ref,
               lambda h: s_ref[0, h], put)


def _gla_sample(proj, tail, row0, n_seq, seq_len, s0, wa, ba, nw):
    q = seq_len
    blk0 = row0 // q

    def col(blk):
        return pl.BlockSpec((q, D_MODEL), lambda b: (blk0 + b, blk))

    state_spec = pl.BlockSpec((1, GLA_HEADS, GLA_DK, GLA_DV), lambda b: (b, 0, 0, 0))
    return pl.pallas_call(
        _gla_sample_kernel,
        grid=(n_seq,),
        in_specs=[col(COL_QK), col(COL_V), col(COL_R),
                  pl.BlockSpec((q, LANES), lambda b: (blk0 + b, TAIL_ALO)),
                  state_spec, _row_params(wa), _row_params(ba), _row_params(nw)],
        out_specs=[pl.BlockSpec((q, D_MODEL), lambda b: (b, 0)), state_spec],
        out_shape=[jax.ShapeDtypeStruct((n_seq * seq_len, D_MODEL), F32),
                   jax.ShapeDtypeStruct(s0.shape, F32)],
        compiler_params=_cparams(1),
        name="gla_sample",
    )(proj, proj, proj, tail, s0, wa, ba, nw)


def _layernorm(x, g, b):
    mu = jnp.mean(x, axis=-1, keepdims=True)
    xc = x - mu
    var = jnp.mean(xc * xc, axis=-1, keepdims=True)
    return xc * lax.rsqrt(var + EPS) * g + b


def _route(logits):
    lane = lax.broadcasted_iota(jnp.int32, logits.shape, 1)
    neg = -jnp.inf
    big = ROUTER_LANES
    glog = jnp.where(lane < MOE_GROUPS, logits, neg)
    gmax = jnp.max(glog, axis=-1, keepdims=True)
    g_sel = jnp.min(jnp.where(glog == gmax, lane, big), axis=-1, keepdims=True)
    p_g = 1.0 / jnp.sum(jnp.exp(glog - gmax), axis=-1, keepdims=True)
    e_lane = lane - MOE_GROUPS
    in_group = (e_lane >= 0) & (e_lane < MOE_EXPERTS) & (e_lane // MOE_EPG == g_sel)
    el = jnp.where(in_group, logits, neg)
    v1 = jnp.max(el, axis=-1, keepdims=True)
    i1 = jnp.min(jnp.where(el == v1, lane, big), axis=-1, keepdims=True)
    el2 = jnp.where(lane == i1, neg, el)
    v2 = jnp.max(el2, axis=-1, keepdims=True)
    i2 = jnp.min(jnp.where(el2 == v2, lane, big), axis=-1, keepdims=True)
    e2 = jnp.exp(v2 - v1)
    w1 = p_g / (1.0 + e2)
    w2 = p_g * e2 / (1.0 + e2)
    first = (i1 - MOE_GROUPS).astype(F32)
    second = (i2 - MOE_GROUPS).astype(F32)
    return jnp.where(lane == ROUTE_E1, first,
                     jnp.where(lane == ROUTE_E2, second,
                               jnp.where(lane == ROUTE_W1, w1,
                                         jnp.where(lane == ROUTE_W2, w2, 0.0))))


ROW_CHUNKS = D_MODEL // LANES


def _store_chunked(ref, val):
    rows = val.shape[0]
    for c in range(ROW_CHUNKS):
        ref[pl.ds(c, rows, stride=ROW_CHUNKS), :] = val[:, c * LANES:(c + 1) * LANES]


def _load_chunked(ref, c):
    return ref[pl.ds(c, ref.shape[0] // ROW_CHUNKS, stride=ROW_CHUNKS), :]


def _merge_kernel(ys_ref, yg_ref, gs_ref, gg_ref, x_ref, wo_ref, g_ref, b_ref, wrh_ref, wrl_ref,
                  br_ref, h1_ref, route_ref):
    merged = jax.nn.sigmoid(gs_ref[...]) * ys_ref[...] + jax.nn.sigmoid(gg_ref[...]) * yg_ref[...]
    mix = _dot(merged.astype(BF16), wo_ref[...])
    h1 = _layernorm(DEEPNORM_ALPHA * x_ref[...] + mix, g_ref[...], b_ref[...])
    _store_chunked(h1_ref, h1)
    h_hi = h1.astype(BF16)
    h_lo = (h1 - h_hi.astype(F32)).astype(BF16)
    logits = (_dot(h_hi, wrh_ref[...]) + _dot(h_lo, wrh_ref[...]) + _dot(h_hi, wrl_ref[...])
              + br_ref[...])
    route_ref[...] = _route(logits)


def _merge(y_ssd, y_gla, proj, row0, x2d, wo, g, b, wr_hi, wr_lo, br, tm):
    m = x2d.shape[0]
    blk0 = row0 // tm

    def rows(i):
        return (i, 0)

    full = pl.BlockSpec((tm, D_MODEL), rows)
    return pl.pallas_call(
        _merge_kernel,
        grid=(m // tm,),
        in_specs=[full, full,
                  pl.BlockSpec((tm, D_MODEL), lambda i: (blk0 + i, COL_GS)),
                  pl.BlockSpec((tm, D_MODEL), lambda i: (blk0 + i, COL_GG)),
                  full, _row_params(wo), _row_params(g), _row_params(b),
                  _row_params(wr_hi), _row_params(wr_lo), _row_params(br)],
        out_specs=[pl.BlockSpec((tm * ROW_CHUNKS, LANES), rows),
                   pl.BlockSpec((tm, ROUTER_LANES), rows)],
        out_shape=[jax.ShapeDtypeStruct((m * ROW_CHUNKS, LANES), F32),
                   jax.ShapeDtypeStruct((m, ROUTER_LANES), F32)],
        compiler_params=_cparams(1),
        name="merge_outproj_ln_router",
    )(y_ssd, y_gla, proj, proj, x2d, wo, g, b, wr_hi, wr_lo, br)


def _gather_kernel(idx_ref, table_ref, o_ref, sem):
    n = o_ref.shape[0] // ROW_CHUNKS

    def issue(r, carry):
        src = pl.multiple_of(idx_ref[0, r] * ROW_CHUNKS, ROW_CHUNKS)
        dst = pl.multiple_of(r * ROW_CHUNKS, ROW_CHUNKS)
        pltpu.make_async_copy(table_ref.at[pl.ds(src, ROW_CHUNKS)], o_ref.at[pl.ds(dst, ROW_CHUNKS)],
                              sem).start()
        return carry

    lax.fori_loop(0, n, issue, 0, unroll=8)
    pltpu.make_async_copy(table_ref.at[pl.ds(0, n * ROW_CHUNKS)], o_ref, sem).wait()


def _gather_rows(table, idx, window):
    n = idx.shape[0]
    steps = n // window
    return pl.pallas_call(
        _gather_kernel,
        grid=(steps,),
        in_specs=[pl.BlockSpec((None, 1, window), lambda i: (i, 0, 0), memory_space=pltpu.SMEM),
                  pl.BlockSpec(memory_space=pl.ANY)],
        out_specs=pl.BlockSpec((window * ROW_CHUNKS, LANES), lambda i: (i, 0)),
        out_shape=jax.ShapeDtypeStruct((n * ROW_CHUNKS, LANES), table.dtype),
        scratch_shapes=[pltpu.SemaphoreType.DMA],
        compiler_params=_cparams(1),
        name="gather_rows",
    )(idx.reshape(steps, 1, window), table)


def _expert_plan(route, tm):
    t = route.shape[0]
    e = jnp.concatenate([route[:, ROUTE_E1], route[:, ROUTE_E2]]).astype(jnp.int32)
    onehot = (e[:, None] == jnp.arange(MOE_EXPERTS, dtype=jnp.int32)[None, :]).astype(jnp.int32)
    csum = jnp.cumsum(onehot, axis=0)
    rank = jnp.sum(onehot * csum, axis=1) - 1
    tiles_e = (csum[-1] + tm - 1) // tm
    tile_end = jnp.cumsum(tiles_e)
    slot = (tile_end - tiles_e)[e] * tm + rank
    n_tiles = (2 * t) // tm + MOE_EXPERTS
    tid = jnp.arange(n_tiles, dtype=jnp.int32)
    used = tid < tile_end[-1]
    tile_e = jnp.sum(tile_end[None, :] <= jnp.minimum(tid, tile_end[-1] - 1)[:, None], axis=1)
    tok = jnp.tile(jnp.arange(t, dtype=jnp.int32), 2)
    token_of_slot = jnp.zeros((n_tiles * tm,), jnp.int32).at[slot].set(tok)
    return token_of_slot, slot, tile_e.astype(jnp.int32), used.astype(jnp.int32)


def _experts_kernel(te_ref, used_ref, x_ref, wg_ref, wu_ref, wd_ref, y_ref, wg_b, wu_b, wd_b, x_b):
    i = pl.program_id(0)
    e = te_ref[i]
    e_prev = te_ref[jnp.maximum(i - 1, 0)]

    @pl.when((i == 0) | (e != e_prev))
    def _():
        wg_b[...] = wg_ref[0].astype(BF16)
        wu_b[...] = wu_ref[0].astype(BF16)
        wd_b[...] = wd_ref[0].astype(BF16)

    @pl.when(used_ref[i] == 1)
    def _():
        for c in range(ROW_CHUNKS):
            x_b[:, c * LANES:(c + 1) * LANES] = _load_chunked(x_ref, c).astype(BF16)
        x = x_b[...]
        hid = (_silu(_dot(x, wg_b[...])) * _dot(x, wu_b[...])).astype(BF16)
        _store_chunked(y_ref, _dot(hid, wd_b[...]))

    @pl.when(used_ref[i] == 0)
    def _():
        y_ref[...] = jnp.zeros(y_ref.shape, y_ref.dtype)


def _experts(xs, tile_e, used, wg, wu, wd, tm):
    n_tiles = xs.shape[0] // (tm * ROW_CHUNKS)
    grid_spec = pltpu.PrefetchScalarGridSpec(
        num_scalar_prefetch=2,
        grid=(n_tiles,),
        in_specs=[pl.BlockSpec((tm * ROW_CHUNKS, LANES), lambda i, te, us: (i, 0)),
                  pl.BlockSpec((1, D_MODEL, MOE_FF), lambda i, te, us: (te[i], 0, 0)),
                  pl.BlockSpec((1, D_MODEL, MOE_FF), lambda i, te, us: (te[i], 0, 0)),
                  pl.BlockSpec((1, MOE_FF, D_MODEL), lambda i, te, us: (te[i], 0, 0))],
        out_specs=pl.BlockSpec((tm * ROW_CHUNKS, LANES), lambda i, te, us: (i, 0)),
        scratch_shapes=[pltpu.VMEM((D_MODEL, MOE_FF), BF16), pltpu.VMEM((D_MODEL, MOE_FF), BF16),
                        pltpu.VMEM((MOE_FF, D_MODEL), BF16), pltpu.VMEM((tm, D_MODEL), BF16)])
    return pl.pallas_call(
        _experts_kernel,
        grid_spec=grid_spec,
        out_shape=jax.ShapeDtypeStruct(xs.shape, F32),
        compiler_params=_cparams(1),
        name="experts",
    )(tile_e, used, xs, wg, wu, wd)


def _combine_kernel(g1_ref, g2_ref, route_ref, h1_ref, g_ref, b_ref, y_ref):
    route = route_ref[...]
    w1 = route[:, ROUTE_W1:ROUTE_W1 + 1]
    w2 = route[:, ROUTE_W2:ROUTE_W2 + 1]
    pre = jnp.concatenate(
        [DEEPNORM_ALPHA * _load_chunked(h1_ref, c)
         + w1 * _load_chunked(g1_ref, c) + w2 * _load_chunked(g2_ref, c)
         for c in range(ROW_CHUNKS)], axis=1)
    y_ref[...] = _layernorm(pre, g_ref[...], b_ref[...])


def _combine(gathered, route, h1, row0, n_tok, m, g, b, tm):
    blk0 = row0 // tm
    blk1 = (n_tok + row0) // tm
    chunked = lambda blk: pl.BlockSpec((tm * ROW_CHUNKS, LANES), lambda i: (blk + i, 0))
    return pl.pallas_call(
        _combine_kernel,
        grid=(m // tm,),
        in_specs=[chunked(blk0), chunked(blk1),
                  pl.BlockSpec((tm, ROUTER_LANES), lambda i: (blk0 + i, 0)),
                  chunked(blk0),
                  _row_params(g), _row_params(b)],
        out_specs=pl.BlockSpec((tm, D_MODEL), lambda i: (i, 0)),
        out_shape=jax.ShapeDtypeStruct((m, D_MODEL), F32),
        compiler_params=_cparams(1),
        name="combine_ln",
    )(gathered, gathered, route, h1, g, b)


def _pad_lanes(v, width=LANES):
    v = v.reshape(1, -1)
    return jnp.pad(v, ((0, 0), (0, width - v.shape[1])))


def kernel(x_prompt, x_sample, state_ssd, state_conv, state_gla, w_in, conv_w, conv_b, dt_bias, a_log, d_skip, ssd_norm_w, gla_w_a2, gla_b_a, gla_norm_w, w_out, ln1_g, ln1_b, w_router_group, b_router_group, w_router_expert, b_router_expert, w_gate, w_up, w_down, ln2_g, ln2_b):
    assert w_in.shape[0] == DEPTH == 1
    n_p, len_p, _ = x_prompt.shape
    n_s, len_s, _ = x_sample.shape
    rows_p, rows_s = n_p * len_p, n_s * len_s
    xp = x_prompt.reshape(rows_p, D_MODEL)
    xs = x_sample.reshape(rows_s, D_MODEL)

    offs = [0]
    for s in IN_SPLIT_SIZES:
        offs.append(offs[-1] + s)
    seg = [w_in[0][:, offs[i]:offs[i + 1]] for i in range(len(IN_SPLIT_SIZES))]
    wz, wxbc, wdt, wq, wk, wv, wr, walo, wgs, wgg = seg
    w_main = jnp.concatenate([wz, wxbc, wq, wk, wv, wr, wgs, wgg], axis=1).astype(BF16)
    w_tail = jnp.concatenate([jnp.pad(wdt, ((0, 0), (0, LANES - SSD_HEADS))),
                              jnp.pad(walo, ((0, 0), (0, LANES - GLA_RANK)))], axis=1).astype(BF16)
    x_all = jnp.concatenate([xp, xs], axis=0).astype(BF16)

    proj = _matmul(x_all, w_main, 512, D_MODEL)
    tail = _matmul(x_all, w_tail, 512, 2 * LANES)

    cw, cb = conv_w[0], conv_b[0].reshape(1, -1)
    dtb, alog = _pad_lanes(dt_bias[0]), _pad_lanes(a_log[0])
    dskip_e = jnp.repeat(d_skip[0], SSD_HEAD_DIM).reshape(1, -1)
    nw_ssd = ssd_norm_w[0].reshape(1, -1)
    wa, ba, nw_gla = gla_w_a2[0].astype(BF16), gla_b_a[0].reshape(1, -1), gla_norm_w[0].reshape(1, -1)

    ys_p, ssd_p, conv_p = _ssd_prompt(proj, tail, n_p, len_p, cw, cb, dtb, alog, dskip_e, nw_ssd)
    ys_s, ssd_s, conv_s = _ssd_sample(proj, tail, rows_p, n_s, len_s, state_ssd[0], state_conv[0],
                                      cw, cb, dtb, alog, dskip_e, nw_ssd)
    yg_p, gla_p = _gla_prompt(proj, tail, n_p, len_p, wa, ba, nw_gla)
    yg_s, gla_s = _gla_sample(proj, tail, rows_p, n_s, len_s, state_gla[0], wa, ba, nw_gla)

    wo = w_out[0].astype(BF16)
    w_r = jnp.pad(jnp.concatenate([w_router_group[0], w_router_expert[0]], axis=1),
                  ((0, 0), (0, ROUTER_LANES - MOE_GROUPS - MOE_EXPERTS)))
    wr_hi = w_r.astype(BF16)
    wr_lo = (w_r - wr_hi.astype(F32)).astype(BF16)
    b_r = _pad_lanes(jnp.concatenate([b_router_group[0], b_router_expert[0]]), ROUTER_LANES)
    g1, b1 = ln1_g[0].reshape(1, -1), ln1_b[0].reshape(1, -1)
    g2, b2 = ln2_g[0].reshape(1, -1), ln2_b[0].reshape(1, -1)

    h1_p, route_p = _merge(ys_p, yg_p, proj, 0, xp, wo, g1, b1, wr_hi, wr_lo, b_r, MERGE_ROWS)
    h1_s, route_s = _merge(ys_s, yg_s, proj, rows_p, xs, wo, g1, b1, wr_hi, wr_lo, b_r, MERGE_ROWS)
    h1 = jnp.concatenate([h1_p, h1_s], axis=0)
    route = jnp.concatenate([route_p, route_s], axis=0)
    n_tok = rows_p + rows_s
    token_of_slot, slot, tile_e, used = _expert_plan(route, EXPERT_ROWS)
    x_sorted = _gather_rows(h1, token_of_slot, GATHER_WINDOW)
    y_sorted = _experts(x_sorted, tile_e, used, w_gate[0], w_up[0], w_down[0], EXPERT_ROWS)
    y_tok = _gather_rows(y_sorted, slot, GATHER_WINDOW)
    outs = [_combine(y_tok, route, h1, 0, n_tok, rows_p, g2, b2, MERGE_ROWS),
            _combine(y_tok, route, h1, rows_p, n_tok, rows_s, g2, b2, MERGE_ROWS)]
    y_p = outs[0].reshape(x_prompt.shape)
    y_s = outs[1].reshape(x_sample.shape)
    return (y_p, y_s, ssd_p[None], conv_p[None], gla_p[None], ssd_s[None], conv_s[None], gla_s[None])
```

```python
import functools
import math

import jax
import jax.numpy as jnp
from jax import lax
from jax.experimental import pallas as pl
from jax.experimental.pallas import tpu as pltpu

F32 = jnp.float32
BF16 = jnp.bfloat16
HIGHEST = lax.Precision.HIGHEST

D_MODEL = 2048
SSD_HEADS = 32
SSD_HEAD_DIM = 64
SSD_GROUPS = 8
SSD_STATE = 128
SSD_CONV = 4
SSD_CHUNK = 128
SSD_BC = SSD_GROUPS * SSD_STATE
SSD_CONV_DIM = D_MODEL + 2 * SSD_BC
SSD_GROUP_COLS = D_MODEL // SSD_GROUPS
HEADS_PER_GROUP = SSD_HEADS // SSD_GROUPS
GLA_HEADS = 8
GLA_DK = 128
GLA_DV = 256
GLA_KEY_DIM = GLA_HEADS * GLA_DK
GLA_RANK = 16
GLA_TAU = 16.0
GLA_CHUNK = 64
MOE_GROUPS = 4
MOE_EPG = 8
MOE_EXPERTS = 32
MOE_FF = 512
DEPTH = 1
DEEPNORM_ALPHA = (2.0 * DEPTH) ** 0.25
EPS = 1e-5
IN_SPLIT_SIZES = (D_MODEL, SSD_CONV_DIM, SSD_HEADS, GLA_KEY_DIM, GLA_KEY_DIM, D_MODEL, D_MODEL,
                  GLA_RANK, D_MODEL, D_MODEL)

LANES = 128
HALO_ROWS = 8
COL_Z, COL_XS, COL_BC = range(3)
COL_QK, COL_V, COL_R = range(3)
COL_GS, COL_GG = range(2)
TAIL_DT, TAIL_ALO = 0, 1
ROUTER_LANES = 128
ROUTE_E1, ROUTE_E2, ROUTE_W1, ROUTE_W2 = 0, 1, 2, 3
MERGE_ROWS = 256
EXPERT_ROWS = 256
GATHER_WINDOW = 512
VMEM_LIMIT = 56 * 1024 * 1024


def _cparams(n_axes):
    return pltpu.CompilerParams(dimension_semantics=("arbitrary",) * n_axes,
                                vmem_limit_bytes=VMEM_LIMIT)


def _silu(x):
    return x * jax.nn.sigmoid(x)


def _softplus(x):
    return jnp.maximum(x, 0.0) + jnp.log1p(jnp.exp(-jnp.abs(x)))


def _tril(n):
    r = lax.broadcasted_iota(jnp.int32, (n, n), 0)
    c = lax.broadcasted_iota(jnp.int32, (n, n), 1)
    return r >= c


def _expand_group(p, g, lane_head):
    h0 = g * HEADS_PER_GROUP
    out = p[:, h0 + HEADS_PER_GROUP - 1:h0 + HEADS_PER_GROUP]
    for hh in range(HEADS_PER_GROUP - 2, -1, -1):
        out = jnp.where(lane_head == hh, p[:, h0 + hh:h0 + hh + 1], out)
    return out


def _split3(x):
    hi = x.astype(BF16)
    r = x - hi.astype(F32)
    mid = r.astype(BF16)
    lo = (r - mid.astype(F32)).astype(BF16)
    return hi, mid, lo


def _dot_sel(sel, x):
    hi, mid, lo = _split3(x)
    return _dot(sel, lo) + _dot(sel, mid) + _dot(sel, hi)


def _col_bcast(row):
    return jnp.broadcast_to(row, (LANES, LANES)).T


def _dot(a, b, **kw):
    return jnp.dot(a, b, preferred_element_type=F32, **kw)


def _dot_nt(a, b, **kw):
    return lax.dot_general(a, b, (((1,), (1,)), ((), ())), preferred_element_type=F32, **kw)


def _dot_tn(a, b, **kw):
    return lax.dot_general(a, b, (((0,), (0,)), ((), ())), preferred_element_type=F32, **kw)


def _matmul_kernel(x_ref, w_ref, o_ref):
    o_ref[...] = _dot(x_ref[...], w_ref[...])


def _matmul(x, w, tm, tn):
    m, k = x.shape
    n = w.shape[1]
    return pl.pallas_call(
        _matmul_kernel,
        grid=(n // tn, m // tm),
        in_specs=[pl.BlockSpec((tm, k), lambda j, i: (i, 0)),
                  pl.BlockSpec((k, tn), lambda j, i: (0, j))],
        out_specs=pl.BlockSpec((tm, tn), lambda j, i: (i, j)),
        out_shape=jax.ShapeDtypeStruct((m, n), F32),
        compiler_params=_cparams(2),
        name="in_proj",
    )(x, w)


def _matmul_f32w_kernel(x_ref, w_ref, o_ref, wb):
    @pl.when(pl.program_id(1) == 0)
    def _():
        wb[...] = w_ref[...].astype(BF16)

    o_ref[...] = _dot(x_ref[...], wb[...])


def _matmul_f32w(x, w, n, tm, tn):
    m, k = x.shape
    return pl.pallas_call(
        _matmul_f32w_kernel,
        grid=(n // tn, m // tm),
        in_specs=[pl.BlockSpec((tm, k), lambda j, i: (i, 0)),
                  pl.BlockSpec((k, tn), lambda j, i: (0, j))],
        out_specs=pl.BlockSpec((tm, tn), lambda j, i: (i, j)),
        out_shape=jax.ShapeDtypeStruct((m, n), F32),
        scratch_shapes=[pltpu.VMEM((k, tn), BF16)],
        compiler_params=_cparams(2),
        name="in_proj_f32w",
    )(x, w)


def _conv_silu(ubuf, cw_ref, cb_ref, lo, width, rows):
    acc = cb_ref[:, lo:lo + width]
    for i in range(SSD_CONV):
        r0 = HALO_ROWS - (SSD_CONV - 1) + i
        acc = acc + cw_ref[i:i + 1, lo:lo + width] * ubuf[r0:r0 + rows, lo:lo + width]
    return _silu(acc)


def _gated_group_norm(y_g, z_g, nw_g):
    yg = y_g * _silu(z_g)
    ms = jnp.mean(yg * yg, axis=-1, keepdims=True)
    return yg * lax.rsqrt(ms + EPS) * nw_g


def _ssd_prompt_kernel(z_ref, xs_ref, bc_ref, tail_ref, cw_ref, cb_ref, dtb_ref, alog_ref,
                       dskip_ref, nw_ref, y_ref, hout_ref, convout_ref, ubuf, ht):
    q = SSD_CHUNK
    c = pl.program_id(1)

    @pl.when(c == 0)
    def _():
        ubuf[0:HALO_ROWS, :] = jnp.zeros((HALO_ROWS, SSD_CONV_DIM), F32)
        ht[...] = jnp.zeros(ht.shape, F32)

    ubuf[HALO_ROWS:HALO_ROWS + q, 0:D_MODEL] = xs_ref[...]
    ubuf[HALO_ROWS:HALO_ROWS + q, D_MODEL:SSD_CONV_DIM] = bc_ref[...]

    dt = _softplus(tail_ref[...] + dtb_ref[...])
    da = dt * (-jnp.exp(alog_ref[...]))
    causal = _tril(q)
    acum = _dot_sel(causal.astype(BF16), da)
    acum_t = acum.T
    dt_t = dt.T
    a_last = acum[q - 1:q, :]
    decay_in = jnp.exp(acum)
    w_end = dt * jnp.exp(a_last - acum)
    chunk_decay = jnp.exp(a_last)
    lane_head = lax.broadcasted_iota(jnp.int32, (q, SSD_GROUP_COLS), 1) >> 6
    assert SSD_HEAD_DIM == 1 << 6

    for g in range(SSD_GROUPS):
        lo = g * SSD_GROUP_COLS
        cols = slice(lo, lo + SSD_GROUP_COLS)
        xs_g = _conv_silu(ubuf, cw_ref, cb_ref, lo, SSD_GROUP_COLS, q)
        b_g = _conv_silu(ubuf, cw_ref, cb_ref, D_MODEL + g * SSD_STATE, SSD_STATE, q).astype(BF16)
        c_g = _conv_silu(ubuf, cw_ref, cb_ref, D_MODEL + SSD_BC + g * SSD_STATE, SSD_STATE,
                         q).astype(BF16)
        scores = _dot_nt(c_g, b_g)
        h_g = ht[g]
        xs_b = xs_g.astype(BF16)
        m_heads, x_heads = [], []
        for hh in range(HEADS_PER_GROUP):
            h = g * HEADS_PER_GROUP + hh
            seg = acum[:, h:h + 1] - acum_t[h:h + 1, :]
            decay = jnp.exp(jnp.where(causal, seg, -jnp.inf))
            m_heads.append((scores * decay * dt_t[h:h + 1, :]).astype(BF16))
            x_heads.append(jnp.where(lane_head == hh, xs_b, jnp.zeros_like(xs_b)))
        y_g = (_dot(jnp.concatenate(m_heads, axis=1), jnp.concatenate(x_heads, axis=0))
               + _dot(c_g, h_g.astype(BF16)) * _expand_group(decay_in, g, lane_head)
               + dskip_ref[:, cols] * xs_g)
        y_ref[:, cols] = _gated_group_norm(y_g, z_ref[:, cols], nw_ref[:, cols])
        xw = (xs_g * _expand_group(w_end, g, lane_head)).astype(BF16)
        ht[g] = h_g * _expand_group(chunk_decay, g, lane_head[0:1]) + _dot_tn(b_g, xw)

    last = HALO_ROWS + q - (SSD_CONV - 1)
    tail_rows = ubuf[last:last + SSD_CONV - 1, :]
    ubuf[HALO_ROWS - (SSD_CONV - 1):HALO_ROWS, :] = tail_rows

    @pl.when(c == pl.num_programs(1) - 1)
    def _():
        convout_ref[0] = tail_rows
        for g in range(SSD_GROUPS):
            h_t = ht[g].T
            for hh in range(HEADS_PER_GROUP):
                hout_ref[0, g * HEADS_PER_GROUP + hh] = h_t[hh * SSD_HEAD_DIM:(hh + 1) * SSD_HEAD_DIM]


def _row_params(p):
    return pl.BlockSpec(p.shape, lambda *_: (0,) * p.ndim)


def _ssd_prompt(proj, tail, n_seq, seq_len, cw, cb, dtb, alog, dskip_e, nw):
    q = SSD_CHUNK
    nc = seq_len // q

    def col(blk):
        return pl.BlockSpec((q, D_MODEL), lambda b, c: (b * nc + c, blk))

    return pl.pallas_call(
        _ssd_prompt_kernel,
        grid=(n_seq, nc),
        in_specs=[col(COL_Z), col(COL_XS), col(COL_BC),
                  pl.BlockSpec((q, LANES), lambda b, c: (b * nc + c, TAIL_DT)),
                  _row_params(cw), _row_params(cb), _row_params(dtb), _row_params(alog),
                  _row_params(dskip_e), _row_params(nw)],
        out_specs=[pl.BlockSpec((q, D_MODEL), lambda b, c: (b * nc + c, 0)),
                   pl.BlockSpec((1, SSD_HEADS, SSD_HEAD_DIM, SSD_STATE), lambda b, c: (b, 0, 0, 0)),
                   pl.BlockSpec((1, SSD_CONV - 1, SSD_CONV_DIM), lambda b, c: (b, 0, 0))],
        out_shape=[jax.ShapeDtypeStruct((n_seq * seq_len, D_MODEL), F32),
                   jax.ShapeDtypeStruct((n_seq, SSD_HEADS, SSD_HEAD_DIM, SSD_STATE), F32),
                   jax.ShapeDtypeStruct((n_seq, SSD_CONV - 1, SSD_CONV_DIM), F32)],
        scratch_shapes=[pltpu.VMEM((HALO_ROWS + q, SSD_CONV_DIM), F32),
                        pltpu.VMEM((SSD_GROUPS, SSD_STATE, SSD_GROUP_COLS), F32)],
        compiler_params=_cparams(2),
        name="ssd_prompt",
    )(proj, proj, proj, tail, cw, cb, dtb, alog, dskip_e, nw)


def _ssd_sample_kernel(z_ref, xs_ref, bc_ref, tail_ref, h_ref, conv_ref, cw_ref, cb_ref, dtb_ref,
                       alog_ref, dskip_ref, nw_ref, y_ref, hout_ref, convout_ref, ubuf, decay_rows):
    q = xs_ref.shape[0]
    ubuf[0:HALO_ROWS - (SSD_CONV - 1), :] = jnp.zeros((HALO_ROWS - (SSD_CONV - 1), SSD_CONV_DIM), F32)
    ubuf[HALO_ROWS - (SSD_CONV - 1):HALO_ROWS, :] = conv_ref[0]
    ubuf[HALO_ROWS:HALO_ROWS + q, 0:D_MODEL] = xs_ref[...]
    ubuf[HALO_ROWS:HALO_ROWS + q, D_MODEL:SSD_CONV_DIM] = bc_ref[...]
    convout_ref[0] = ubuf[HALO_ROWS + q - (SSD_CONV - 1):HALO_ROWS + q, :]

    dt = _softplus(tail_ref[...] + dtb_ref[...])
    da = dt * (-jnp.exp(alog_ref[...]))
    acum = _dot_sel(_tril(q).astype(BF16), da)
    a_last = acum[q - 1:q, :]
    decay_in = jnp.exp(acum)
    w_end = dt * jnp.exp(a_last - acum)
    decay_rows[...] = _col_bcast(jnp.exp(a_last))[:SSD_HEADS]
    row = lax.broadcasted_iota(jnp.int32, (q, SSD_GROUP_COLS), 0)
    lane_head = lax.broadcasted_iota(jnp.int32, (q, SSD_GROUP_COLS), 1) >> 6

    for g in range(SSD_GROUPS):
        lo = g * SSD_GROUP_COLS
        cols = slice(lo, lo + SSD_GROUP_COLS)
        xs_g = _conv_silu(ubuf, cw_ref, cb_ref, lo, SSD_GROUP_COLS, q)
        b_g = _conv_silu(ubuf, cw_ref, cb_ref, D_MODEL + g * SSD_STATE, SSD_STATE, q)
        c_g = _conv_silu(ubuf, cw_ref, cb_ref, D_MODEL + SSD_BC + g * SSD_STATE, SSD_STATE, q)
        h_g = h_ref[0, g * HEADS_PER_GROUP:(g + 1) * HEADS_PER_GROUP].reshape(
            SSD_GROUP_COLS, SSD_STATE)
        y_g = (_dot_nt(c_g.astype(BF16), h_g.astype(BF16)) * _expand_group(decay_in, g, lane_head)
               + dskip_ref[:, cols] * xs_g)
        acum_g = _expand_group(acum, g, lane_head)
        dt_g = _expand_group(dt, g, lane_head)
        for j in range(q):
            s_j = jnp.sum(c_g * b_g[j:j + 1, :], axis=-1, keepdims=True)
            coef = s_j * jnp.exp(acum_g - acum_g[j:j + 1, :]) * dt_g[j:j + 1, :]
            y_g = y_g + jnp.where(row >= j, coef, 0.0) * xs_g[j:j + 1, :]
        y_ref[:, cols] = _gated_group_norm(y_g, z_ref[:, cols], nw_ref[:, cols])
        upd = _dot_tn(xs_g * _expand_group(w_end, g, lane_head), b_g,
                      precision=HIGHEST)
        for hh in range(HEADS_PER_GROUP):
            h = g * HEADS_PER_GROUP + hh
            rows = slice(hh * SSD_HEAD_DIM, (hh + 1) * SSD_HEAD_DIM)
            hout_ref[0, h] = h_g[rows] * decay_rows[h:h + 1, :] + upd[rows]


def _ssd_sample(proj, tail, row0, n_seq, seq_len, h0, conv0, cw, cb, dtb, alog, dskip_e, nw):
    q = seq_len
    blk0 = row0 // q

    def col(blk):
        return pl.BlockSpec((q, D_MODEL), lambda b: (blk0 + b, blk))

    state_spec = pl.BlockSpec((1, SSD_HEADS, SSD_HEAD_DIM, SSD_STATE), lambda b: (b, 0, 0, 0))
    conv_spec = pl.BlockSpec((1, SSD_CONV - 1, SSD_CONV_DIM), lambda b: (b, 0, 0))
    return pl.pallas_call(
        _ssd_sample_kernel,
        grid=(n_seq,),
        in_specs=[col(COL_Z), col(COL_XS), col(COL_BC),
                  pl.BlockSpec((q, LANES), lambda b: (blk0 + b, TAIL_DT)),
                  state_spec, conv_spec,
                  _row_params(cw), _row_params(cb), _row_params(dtb), _row_params(alog),
                  _row_params(dskip_e), _row_params(nw)],
        out_specs=[pl.BlockSpec((q, D_MODEL), lambda b: (b, 0)), state_spec, conv_spec],
        out_shape=[jax.ShapeDtypeStruct((n_seq * seq_len, D_MODEL), F32),
                   jax.ShapeDtypeStruct(h0.shape, F32),
                   jax.ShapeDtypeStruct(conv0.shape, F32)],
        scratch_shapes=[pltpu.VMEM((HALO_ROWS + q, SSD_CONV_DIM), F32),
                        pltpu.VMEM((SSD_HEADS, SSD_STATE), F32)],
        compiler_params=_cparams(1),
        name="ssd_sample",
    )(proj, proj, proj, tail, h0, conv0, cw, cb, dtb, alog, dskip_e, nw)


def _gla_chunk(qk_ref, v_ref, r_ref, alo_ref, wa_ref, ba_ref, nw_ref, y_ref, get_state, put_state):
    q = v_ref.shape[0]
    a_lo = alo_ref[...][:, :GLA_RANK].astype(BF16)
    gk = -_softplus(-(_dot(a_lo, wa_ref[...]) + ba_ref[...])) / GLA_TAU
    causal = _tril(q)
    bcum = _dot_sel(causal.astype(BF16), gk)
    for h in range(GLA_HEADS):
        kc = slice(h * GLA_DK, (h + 1) * GLA_DK)
        vc = slice(h * GLA_DV, (h + 1) * GLA_DV)
        b_h = bcum[:, kc]
        b_last = b_h[q - 1:q, :]
        q_h = qk_ref[:, kc] * (GLA_DK ** -0.5)
        k_h = qk_ref[:, GLA_KEY_DIM + h * GLA_DK:GLA_KEY_DIM + (h + 1) * GLA_DK]
        v_h = v_ref[:, vc].astype(BF16)
        q_in = (q_h * jnp.exp(b_h)).astype(BF16)
        k_in = (k_h * jnp.exp(-b_h)).astype(BF16)
        att = jnp.where(causal, _dot_nt(q_in, k_in), 0.0).astype(BF16)
        s_h = get_state(h)
        o = _dot(att, v_h) + _dot(q_in, s_h.astype(BF16))
        k_end = (k_h * jnp.exp(b_last - b_h)).astype(BF16)
        d_col = _col_bcast(jnp.exp(b_last))
        put_state(h, s_h * jnp.concatenate([d_col, d_col], axis=1) + _dot_tn(k_end, v_h))
        ms = jnp.mean(o * o, axis=-1, keepdims=True)
        y_ref[:, vc] = o * lax.rsqrt(ms + EPS) * nw_ref[...] * _silu(r_ref[:, vc])


def _gla_prompt_kernel(qk_ref, v_ref, r_ref, alo_ref, wa_ref, ba_ref, nw_ref, y_ref, sout_ref, st):
    c = pl.program_id(1)

    @pl.when(c == 0)
    def _():
        st[...] = jnp.zeros(st.shape, F32)

    def put(h, val):
        st[h] = val

    _gla_chunk(qk_ref, v_ref, r_ref, alo_ref, wa_ref, ba_ref, nw_ref, y_ref, lambda h: st[h], put)

    @pl.when(c == pl.num_programs(1) - 1)
    def _():
        sout_ref[0] = st[...]


def _gla_prompt(proj, tail, n_seq, seq_len, wa, ba, nw):
    q = GLA_CHUNK
    nc = seq_len // q

    def col(blk):
        return pl.BlockSpec((q, D_MODEL), lambda b, c: (b * nc + c, blk))

    return pl.pallas_call(
        _gla_prompt_kernel,
        grid=(n_seq, nc),
        in_specs=[col(COL_QK), col(COL_V), col(COL_R),
                  pl.BlockSpec((q, LANES), lambda b, c: (b * nc + c, TAIL_ALO)),
                  _row_params(wa), _row_params(ba), _row_params(nw)],
        out_specs=[pl.BlockSpec((q, D_MODEL), lambda b, c: (b * nc + c, 0)),
                   pl.BlockSpec((1, GLA_HEADS, GLA_DK, GLA_DV), lambda b, c: (b, 0, 0, 0))],
        out_shape=[jax.ShapeDtypeStruct((n_seq * seq_len, D_MODEL), F32),
                   jax.ShapeDtypeStruct((n_seq, GLA_HEADS, GLA_DK, GLA_DV), F32)],
        scratch_shapes=[pltpu.VMEM((GLA_HEADS, GLA_DK, GLA_DV), F32)],
        compiler_params=_cparams(2),
        name="gla_prompt",
    )(proj, proj, proj, tail, wa, ba, nw)


def _gla_sample_kernel(qk_ref, v_ref, r_ref, alo_ref, s_ref, wa_ref, ba_ref, nw_ref, y_ref, sout_ref):
    def put(h, val):
        sout_ref[0, h] = val

    _gla_chunk(qk_ref, v_ref, r_ref, alo_ref, wa_ref, ba_ref, nw_ref, y_ref,
               lambda h: s_ref[0, h], put)


def _gla_sample(proj, tail, row0, n_seq, seq_len, s0, wa, ba, nw):
    q = seq_len
    blk0 = row0 // q

    def col(blk):
        return pl.BlockSpec((q, D_MODEL), lambda b: (blk0 + b, blk))

    state_spec = pl.BlockSpec((1, GLA_HEADS, GLA_DK, GLA_DV), lambda b: (b, 0, 0, 0))
    return pl.pallas_call(
        _gla_sample_kernel,
        grid=(n_seq,),
        in_specs=[col(COL_QK), col(COL_V), col(COL_R),
                  pl.BlockSpec((q, LANES), lambda b: (blk0 + b, TAIL_ALO)),
                  state_spec, _row_params(wa), _row_params(ba), _row_params(nw)],
        out_specs=[pl.BlockSpec((q, D_MODEL), lambda b: (b, 0)), state_spec],
        out_shape=[jax.ShapeDtypeStruct((n_seq * seq_len, D_MODEL), F32),
                   jax.ShapeDtypeStruct(s0.shape, F32)],
        compiler_params=_cparams(1),
        name="gla_sample",
    )(proj, proj, proj, tail, s0, wa, ba, nw)


def _layernorm(x, g, b):
    mu = jnp.mean(x, axis=-1, keepdims=True)
    xc = x - mu
    var = jnp.mean(xc * xc, axis=-1, keepdims=True)
    return xc * lax.rsqrt(var + EPS) * g + b


def _route(logits):
    lane = lax.broadcasted_iota(jnp.int32, logits.shape, 1)
    neg = -jnp.inf
    big = ROUTER_LANES
    glog = jnp.where(lane < MOE_GROUPS, logits, neg)
    gmax = jnp.max(glog, axis=-1, keepdims=True)
    g_sel = jnp.min(jnp.where(glog == gmax, lane, big), axis=-1, keepdims=True)
    p_g = 1.0 / jnp.sum(jnp.exp(glog - gmax), axis=-1, keepdims=True)
    e_lane = lane - MOE_GROUPS
    in_group = (e_lane >= 0) & (e_lane < MOE_EXPERTS) & (e_lane // MOE_EPG == g_sel)
    el = jnp.where(in_group, logits, neg)
    v1 = jnp.max(el, axis=-1, keepdims=True)
    i1 = jnp.min(jnp.where(el == v1, lane, big), axis=-1, keepdims=True)
    el2 = jnp.where(lane == i1, neg, el)
    v2 = jnp.max(el2, axis=-1, keepdims=True)
    i2 = jnp.min(jnp.where(el2 == v2, lane, big), axis=-1, keepdims=True)
    e2 = jnp.exp(v2 - v1)
    w1 = p_g / (1.0 + e2)
    w2 = p_g * e2 / (1.0 + e2)
    first = (i1 - MOE_GROUPS).astype(F32)
    second = (i2 - MOE_GROUPS).astype(F32)
    return jnp.where(lane == ROUTE_E1, first,
                     jnp.where(lane == ROUTE_E2, second,
                               jnp.where(lane == ROUTE_W1, w1,
                                         jnp.where(lane == ROUTE_W2, w2, 0.0))))


ROW_CHUNKS = D_MODEL // LANES


def _store_chunked(ref, val):
    rows = val.shape[0]
    for c in range(ROW_CHUNKS):
        ref[pl.ds(c, rows, stride=ROW_CHUNKS), :] = val[:, c * LANES:(c + 1) * LANES]


def _load_chunked(ref, c):
    return ref[pl.ds(c, ref.shape[0] // ROW_CHUNKS, stride=ROW_CHUNKS), :]


def _merge_kernel(n_first, ysa_ref, ysb_ref, yga_ref, ygb_ref, gs_ref, gg_ref, xa_ref, xb_ref, wo_ref,
                  g_ref, b_ref, wrh_ref, wrl_ref, br_ref, h1_ref, route_ref):
    first = pl.program_id(0) < n_first
    ys = jnp.where(first, ysa_ref[...], ysb_ref[...])
    yg = jnp.where(first, yga_ref[...], ygb_ref[...])
    x = jnp.where(first, xa_ref[...], xb_ref[...])
    merged = jax.nn.sigmoid(gs_ref[...]) * ys + jax.nn.sigmoid(gg_ref[...]) * yg
    mix = _dot(merged.astype(BF16), wo_ref[...])
    h1 = _layernorm(DEEPNORM_ALPHA * x + mix, g_ref[...], b_ref[...])
    _store_chunked(h1_ref, h1)
    h_hi = h1.astype(BF16)
    h_lo = (h1 - h_hi.astype(F32)).astype(BF16)
    logits = (_dot(h_hi, wrh_ref[...]) + _dot(h_lo, wrh_ref[...]) + _dot(h_hi, wrl_ref[...])
              + br_ref[...])
    route_ref[...] = _route(logits)


def _merge(ys, yg, gates, xs, wo, g, b, wr_hi, wr_lo, br, tm):
    n_a, n_b = xs[0].shape[0] // tm, xs[1].shape[0] // tm
    m = (n_a + n_b) * tm

    def rows(i):
        return (i, 0)

    pair = [pl.BlockSpec((tm, D_MODEL), lambda i: (jnp.minimum(i, n_a - 1), 0)),
            pl.BlockSpec((tm, D_MODEL), lambda i: (jnp.maximum(i - n_a, 0), 0))]
    return pl.pallas_call(
        functools.partial(_merge_kernel, n_a),
        grid=(m // tm,),
        in_specs=[*pair, *pair,
                  pl.BlockSpec((tm, D_MODEL), lambda i: (i, COL_GS)),
                  pl.BlockSpec((tm, D_MODEL), lambda i: (i, COL_GG)),
                  *pair,
                  pl.BlockSpec(wo.shape, lambda i: (0, 0), pipeline_mode=pl.Buffered(1)),
                  _row_params(g), _row_params(b),
                  _row_params(wr_hi), _row_params(wr_lo), _row_params(br)],
        out_specs=[pl.BlockSpec((tm * ROW_CHUNKS, LANES), rows),
                   pl.BlockSpec((tm, ROUTER_LANES), rows)],
        out_shape=[jax.ShapeDtypeStruct((m * ROW_CHUNKS, LANES), F32),
                   jax.ShapeDtypeStruct((m, ROUTER_LANES), F32)],
        compiler_params=_cparams(1),
        name="merge_outproj_ln_router",
    )(*ys, *yg, gates, gates, *xs, wo, g, b, wr_hi, wr_lo, br)


def _gather_kernel(idx_ref, table_ref, o_ref, sem):
    n = o_ref.shape[0] // ROW_CHUNKS

    def issue(r, carry):
        src = pl.multiple_of(idx_ref[0, r] * ROW_CHUNKS, ROW_CHUNKS)
        dst = pl.multiple_of(r * ROW_CHUNKS, ROW_CHUNKS)
        pltpu.make_async_copy(table_ref.at[pl.ds(src, ROW_CHUNKS)], o_ref.at[pl.ds(dst, ROW_CHUNKS)],
                              sem).start()
        return carry

    lax.fori_loop(0, n, issue, 0, unroll=8)
    pltpu.make_async_copy(table_ref.at[pl.ds(0, n * ROW_CHUNKS)], o_ref, sem).wait()


def _gather_rows(table, idx, window):
    n = idx.shape[0]
    steps = n // window
    return pl.pallas_call(
        _gather_kernel,
        grid=(steps,),
        in_specs=[pl.BlockSpec((None, 1, window), lambda i: (i, 0, 0), memory_space=pltpu.SMEM),
                  pl.BlockSpec(memory_space=pl.ANY)],
        out_specs=pl.BlockSpec((window * ROW_CHUNKS, LANES), lambda i: (i, 0)),
        out_shape=jax.ShapeDtypeStruct((n * ROW_CHUNKS, LANES), table.dtype),
        scratch_shapes=[pltpu.SemaphoreType.DMA],
        compiler_params=_cparams(1),
        name="gather_rows",
    )(idx.reshape(steps, 1, window), table)


def _dispatch_kernel(slot_ref, h_ref, init_hbm, x_hbm, sem):
    n = h_ref.shape[0] // ROW_CHUNKS

    def issue(r, carry):
        src = h_ref.at[pl.ds(pl.multiple_of(r * ROW_CHUNKS, ROW_CHUNKS), ROW_CHUNKS)]
        for k in range(2):
            dst = pl.multiple_of(slot_ref[k, r] * ROW_CHUNKS, ROW_CHUNKS)
            pltpu.make_async_copy(src, x_hbm.at[pl.ds(dst, ROW_CHUNKS)], sem).start()
        return carry

    lax.fori_loop(0, n, issue, 0, unroll=4)
    for k in range(2):
        pltpu.make_async_copy(h_ref, x_hbm.at[pl.ds(0, n * ROW_CHUNKS)], sem).wait()


def _dispatch(h1, slot, n_slots, window):
    t = slot.shape[1]
    steps = t // window
    slot3 = slot.reshape(2, steps, window).transpose(1, 0, 2)
    init = jnp.zeros((n_slots * ROW_CHUNKS, LANES), h1.dtype)
    return pl.pallas_call(
        _dispatch_kernel,
        grid=(steps,),
        in_specs=[pl.BlockSpec((None, 2, window), lambda i: (i, 0, 0), memory_space=pltpu.SMEM),
                  pl.BlockSpec((window * ROW_CHUNKS, LANES), lambda i: (i, 0)),
                  pl.BlockSpec(memory_space=pl.ANY)],
        out_specs=pl.BlockSpec(memory_space=pl.ANY),
        out_shape=jax.ShapeDtypeStruct(init.shape, h1.dtype),
        input_output_aliases={2: 0},
        scratch_shapes=[pltpu.SemaphoreType.DMA],
        compiler_params=_cparams(1),
        name="dispatch_rows",
    )(slot3, h1, init)


def _expert_plan(route, tm):
    t = route.shape[0]
    ids = jnp.arange(MOE_EXPERTS, dtype=jnp.int32)
    e = jnp.concatenate([route[:, ROUTE_E1], route[:, ROUTE_E2]]).astype(jnp.int32)
    onehot = (e[:, None] == ids[None, :]).astype(jnp.int32)
    csum = jnp.cumsum(onehot, axis=0)
    rank = jnp.sum(onehot * csum, axis=1) - 1
    cnt = csum[-1]
    tiles_e = (cnt + tm - 1) // tm
    tile_end = jnp.cumsum(tiles_e)
    tile_start = tile_end - tiles_e
    slot = tile_start[e] * tm + rank
    n_tiles = (2 * t) // tm + MOE_EXPERTS
    tid = jnp.arange(n_tiles, dtype=jnp.int32)
    used = tid < tile_end[-1]
    tile_e = jnp.sum(tile_end[None, :] <= jnp.minimum(tid, tile_end[-1] - 1)[:, None], axis=1)
    nvalid = jnp.where(used, jnp.clip(cnt[tile_e] - (tid - tile_start[tile_e]) * tm, 0, tm), 0)
    first = used & (tid == tile_start[tile_e])
    has = tiles_e > 0
    ordinal = jnp.cumsum(has.astype(jnp.int32)) - 1
    later = jnp.where((ids[None, :] > ids[:, None]) & has[None, :], ids[None, :], MOE_EXPERTS)
    nxt_e = jnp.min(later, axis=1)
    nxt_e = jnp.where(nxt_e == MOE_EXPERTS, -1, nxt_e)
    i32 = lambda a: a.astype(jnp.int32)
    return (slot.reshape(2, t), n_tiles * tm, i32(tile_e), i32(nvalid), i32(first),
            i32(nxt_e[tile_e]), i32(ordinal[tile_e] % 2))


def _experts_kernel(te_ref, nv_ref, first_ref, nxt_ref, par_ref, x_ref, wg_hbm, wu_hbm, wd_hbm, y_ref,
                    wg_f, wu_f, wd_f, wg_b, wu_b, wd_b, x_b, sem):
    i = pl.program_id(0)
    s = par_ref[i]

    def weight_copies(e, slot):
        return (pltpu.make_async_copy(wg_hbm.at[e], wg_f.at[slot], sem.at[slot]),
                pltpu.make_async_copy(wu_hbm.at[e], wu_f.at[slot], sem.at[slot]),
                pltpu.make_async_copy(wd_hbm.at[e], wd_f.at[slot], sem.at[slot]))

    @pl.when(i == 0)
    def _():
        for cp in weight_copies(te_ref[0], par_ref[0]):
            cp.start()

    @pl.when(first_ref[i] == 1)
    def _():
        @pl.when(nxt_ref[i] >= 0)
        def _():
            for cp in weight_copies(nxt_ref[i], 1 - s):
                cp.start()

        for cp in weight_copies(te_ref[i], s):
            cp.wait()
        wg_b[...] = wg_f[s].astype(BF16)
        wu_b[...] = wu_f[s].astype(BF16)
        wd_b[...] = wd_f[s].astype(BF16)

    nv = nv_ref[i]

    @pl.when(nv > 0)
    def _():
        for c in range(ROW_CHUNKS):
            x_b[:, c * LANES:(c + 1) * LANES] = _load_chunked(x_ref, c).astype(BF16)
        x = x_b[...]
        hid = (_silu(_dot(x, wg_b[...])) * _dot(x, wu_b[...])).astype(BF16)
        _store_chunked(y_ref, _dot(hid, wd_b[...]))

    @pl.when(nv == 0)
    def _():
        y_ref[...] = jnp.zeros(y_ref.shape, y_ref.dtype)


def _experts(xs, tile_e, nvalid, first, nxt, par, wg, wu, wd, tm):
    n_tiles = xs.shape[0] // (tm * ROW_CHUNKS)
    rows = pl.BlockSpec((tm * ROW_CHUNKS, LANES), lambda i, *_: (i, 0))
    hbm = pl.BlockSpec(memory_space=pl.ANY)
    grid_spec = pltpu.PrefetchScalarGridSpec(
        num_scalar_prefetch=5,
        grid=(n_tiles,),
        in_specs=[rows, hbm, hbm, hbm],
        out_specs=rows,
        scratch_shapes=[pltpu.VMEM((2, D_MODEL, MOE_FF), F32), pltpu.VMEM((2, D_MODEL, MOE_FF), F32),
                        pltpu.VMEM((2, MOE_FF, D_MODEL), F32),
                        pltpu.VMEM((D_MODEL, MOE_FF), BF16), pltpu.VMEM((D_MODEL, MOE_FF), BF16),
                        pltpu.VMEM((MOE_FF, D_MODEL), BF16), pltpu.VMEM((tm, D_MODEL), BF16),
                        pltpu.SemaphoreType.DMA((2,))])
    return pl.pallas_call(
        _experts_kernel,
        grid_spec=grid_spec,
        out_shape=jax.ShapeDtypeStruct(xs.shape, F32),
        compiler_params=_cparams(1),
        name="experts",
    )(tile_e, nvalid, first, nxt, par, xs, wg, wu, wd)


def _combine_kernel(g1_ref, g2_ref, route_ref, h1_ref, g_ref, b_ref, y_ref):
    route = route_ref[...]
    w1 = route[:, ROUTE_W1:ROUTE_W1 + 1]
    w2 = route[:, ROUTE_W2:ROUTE_W2 + 1]
    pre = jnp.concatenate(
        [DEEPNORM_ALPHA * _load_chunked(h1_ref, c)
         + w1 * _load_chunked(g1_ref, c) + w2 * _load_chunked(g2_ref, c)
         for c in range(ROW_CHUNKS)], axis=1)
    y_ref[...] = _layernorm(pre, g_ref[...], b_ref[...])


def _combine(gathered, route, h1, row0, n_tok, m, g, b, tm):
    blk0 = row0 // tm
    blk1 = (n_tok + row0) // tm
    chunked = lambda blk: pl.BlockSpec((tm * ROW_CHUNKS, LANES), lambda i: (blk + i, 0))
    return pl.pallas_call(
        _combine_kernel,
        grid=(m // tm,),
        in_specs=[chunked(blk0), chunked(blk1),
                  pl.BlockSpec((tm, ROUTER_LANES), lambda i: (blk0 + i, 0)),
                  chunked(blk0),
                  _row_params(g), _row_params(b)],
        out_specs=pl.BlockSpec((tm, D_MODEL), lambda i: (i, 0)),
        out_shape=jax.ShapeDtypeStruct((m, D_MODEL), F32),
        compiler_params=_cparams(1),
        name="combine_ln",
    )(gathered, gathered, route, h1, g, b)


def _pad_lanes(v, width=LANES):
    v = v.reshape(1, -1)
    return jnp.pad(v, ((0, 0), (0, width - v.shape[1])))


def kernel(x_prompt, x_sample, state_ssd, state_conv, state_gla, w_in, conv_w, conv_b, dt_bias, a_log, d_skip, ssd_norm_w, gla_w_a2, gla_b_a, gla_norm_w, w_out, ln1_g, ln1_b, w_router_group, b_router_group, w_router_expert, b_router_expert, w_gate, w_up, w_down, ln2_g, ln2_b):
    assert w_in.shape[0] == DEPTH == 1
    n_p, len_p, _ = x_prompt.shape
    n_s, len_s, _ = x_sample.shape
    rows_p, rows_s = n_p * len_p, n_s * len_s
    xp = x_prompt.reshape(rows_p, D_MODEL)
    xs = x_sample.reshape(rows_s, D_MODEL)

    offs = [0]
    for s in IN_SPLIT_SIZES:
        offs.append(offs[-1] + s)
    w_in0 = w_in[0]
    n_ssd = offs[2]
    w_gla = w_in0[:, offs[3]:offs[7]].astype(BF16)
    w_gates = w_in0[:, offs[8]:offs[10]].astype(BF16)
    w_tail = jnp.concatenate(
        [jnp.pad(w_in0[:, offs[2]:offs[3]], ((0, 0), (0, LANES - SSD_HEADS))),
         jnp.pad(w_in0[:, offs[7]:offs[8]], ((0, 0), (0, LANES - GLA_RANK)))], axis=1).astype(BF16)
    x_all = jnp.concatenate([xp, xs], axis=0).astype(BF16)
    total = rows_p + rows_s

    proj_ssd = _matmul_f32w(x_all, w_in0, n_ssd, 512, 1024)
    proj_gla = _matmul(x_all, w_gla, 512, D_MODEL)
    proj_gate = _matmul(x_all, w_gates, 512, D_MODEL)
    tail = _matmul(x_all, w_tail, 512, 2 * LANES)

    cw, cb = conv_w[0], conv_b[0].reshape(1, -1)
    dtb, alog = _pad_lanes(dt_bias[0]), _pad_lanes(a_log[0])
    dskip_e = jnp.repeat(d_skip[0], SSD_HEAD_DIM).reshape(1, -1)
    nw_ssd = ssd_norm_w[0].reshape(1, -1)
    wa, ba, nw_gla = gla_w_a2[0].astype(BF16), gla_b_a[0].reshape(1, -1), gla_norm_w[0].reshape(1, -1)

    ys_p, ssd_p, conv_p = _ssd_prompt(proj_ssd, tail, n_p, len_p, cw, cb, dtb, alog, dskip_e, nw_ssd)
    ys_s, ssd_s, conv_s = _ssd_sample(proj_ssd, tail, rows_p, n_s, len_s, state_ssd[0], state_conv[0],
                                      cw, cb, dtb, alog, dskip_e, nw_ssd)
    yg_p, gla_p = _gla_prompt(proj_gla, tail, n_p, len_p, wa, ba, nw_gla)
    yg_s, gla_s = _gla_sample(proj_gla, tail, rows_p, n_s, len_s, state_gla[0], wa, ba, nw_gla)

    wo = w_out[0].astype(BF16)
    w_r = jnp.pad(jnp.concatenate([w_router_group[0], w_router_expert[0]], axis=1),
                  ((0, 0), (0, ROUTER_LANES - MOE_GROUPS - MOE_EXPERTS)))
    wr_hi = w_r.astype(BF16)
    wr_lo = (w_r - wr_hi.astype(F32)).astype(BF16)
    b_r = _pad_lanes(jnp.concatenate([b_router_group[0], b_router_expert[0]]), ROUTER_LANES)
    g1, b1 = ln1_g[0].reshape(1, -1), ln1_b[0].reshape(1, -1)
    g2, b2 = ln2_g[0].reshape(1, -1), ln2_b[0].reshape(1, -1)

    h1, route = _merge((ys_p, ys_s), (yg_p, yg_s), proj_gate, (xp, xs), wo, g1, b1, wr_hi, wr_lo, b_r,
                       MERGE_ROWS)
    slot, n_slots, tile_e, nvalid, first, nxt, par = _expert_plan(route, EXPERT_ROWS)
    x_sorted = _dispatch(h1, slot, n_slots, GATHER_WINDOW)
    y_sorted = _experts(x_sorted, tile_e, nvalid, first, nxt, par, w_gate[0], w_up[0], w_down[0],
                        EXPERT_ROWS)
    y_tok = _gather_rows(y_sorted, slot.reshape(-1), GATHER_WINDOW)
    outs = [_combine(y_tok, route, h1, 0, total, rows_p, g2, b2, MERGE_ROWS),
            _combine(y_tok, route, h1, rows_p, total, rows_s, g2, b2, MERGE_ROWS)]
    y_p = outs[0].reshape(x_prompt.shape)
    y_s = outs[1].reshape(x_sample.shape)
    return (y_p, y_s, ssd_p[None], conv_p[None], gla_p[None], ssd_s[None], conv_s[None], gla_s[None])
```

```python
import functools
import math

import jax
import jax.numpy as jnp
from jax import lax
from jax.experimental import pallas as pl
from jax.experimental.pallas import tpu as pltpu

F32 = jnp.float32
BF16 = jnp.bfloat16
HIGHEST = lax.Precision.HIGHEST

D_MODEL = 2048
SSD_HEADS = 32
SSD_HEAD_DIM = 64
SSD_GROUPS = 8
SSD_STATE = 128
SSD_CONV = 4
SSD_CHUNK = 128
SSD_BC = SSD_GROUPS * SSD_STATE
SSD_CONV_DIM = D_MODEL + 2 * SSD_BC
SSD_GROUP_COLS = D_MODEL // SSD_GROUPS
HEADS_PER_GROUP = SSD_HEADS // SSD_GROUPS
GLA_HEADS = 8
GLA_DK = 128
GLA_DV = 256
GLA_KEY_DIM = GLA_HEADS * GLA_DK
GLA_RANK = 16
GLA_TAU = 16.0
GLA_CHUNK = 64
MOE_GROUPS = 4
MOE_EPG = 8
MOE_EXPERTS = 32
MOE_FF = 512
DEPTH = 1
DEEPNORM_ALPHA = (2.0 * DEPTH) ** 0.25
EPS = 1e-5
IN_SPLIT_SIZES = (D_MODEL, SSD_CONV_DIM, SSD_HEADS, GLA_KEY_DIM, GLA_KEY_DIM, D_MODEL, D_MODEL,
                  GLA_RANK, D_MODEL, D_MODEL)

LANES = 128
HALO_ROWS = 8
COL_Z, COL_XS, COL_BC, COL_QK, COL_V, COL_R, COL_GS, COL_GG = range(8)
TAIL_DT, TAIL_ALO = 0, 1
ROUTER_LANES = 128
ROUTE_E1, ROUTE_E2, ROUTE_W1, ROUTE_W2 = 0, 1, 2, 3
MERGE_ROWS = 256
EXPERT_ROWS = 256
GATHER_WINDOW = 512
VMEM_LIMIT = 56 * 1024 * 1024


def _cparams(n_axes):
    return pltpu.CompilerParams(dimension_semantics=("arbitrary",) * n_axes,
                                vmem_limit_bytes=VMEM_LIMIT)


def _silu(x):
    return x * jax.nn.sigmoid(x)


def _softplus(x):
    return jnp.maximum(x, 0.0) + jnp.log1p(jnp.exp(-jnp.abs(x)))


def _tril(n):
    r = lax.broadcasted_iota(jnp.int32, (n, n), 0)
    c = lax.broadcasted_iota(jnp.int32, (n, n), 1)
    return r >= c


def _expand_group(p, g, lane_head):
    h0 = g * HEADS_PER_GROUP
    out = p[:, h0 + HEADS_PER_GROUP - 1:h0 + HEADS_PER_GROUP]
    for hh in range(HEADS_PER_GROUP - 2, -1, -1):
        out = jnp.where(lane_head == hh, p[:, h0 + hh:h0 + hh + 1], out)
    return out


def _split3(x):
    hi = x.astype(BF16)
    r = x - hi.astype(F32)
    mid = r.astype(BF16)
    lo = (r - mid.astype(F32)).astype(BF16)
    return hi, mid, lo


def _dot_sel(sel, x):
    hi, mid, lo = _split3(x)
    return _dot(sel, lo) + _dot(sel, mid) + _dot(sel, hi)


def _col_bcast(row):
    return jnp.broadcast_to(row, (LANES, LANES)).T


def _dot(a, b, **kw):
    return jnp.dot(a, b, preferred_element_type=F32, **kw)


def _dot_nt(a, b, **kw):
    return lax.dot_general(a, b, (((1,), (1,)), ((), ())), preferred_element_type=F32, **kw)


def _dot_tn(a, b, **kw):
    return lax.dot_general(a, b, (((0,), (0,)), ((), ())), preferred_element_type=F32, **kw)


def _matmul_kernel(n_first, xa_ref, xb_ref, wt_ref, o_ref):
    x = jnp.where(pl.program_id(0) < n_first, xa_ref[...], xb_ref[...])
    o_ref[...] = _dot_nt(x, wt_ref[...])


def _matmul_nt(xs, wt, tm):
    n, k = wt.shape
    n_a, n_b = xs[0].shape[0] // tm, xs[1].shape[0] // tm
    return pl.pallas_call(
        functools.partial(_matmul_kernel, n_a),
        grid=(n_a + n_b,),
        in_specs=[pl.BlockSpec((tm, k), lambda i: (jnp.minimum(i, n_a - 1), 0)),
                  pl.BlockSpec((tm, k), lambda i: (jnp.maximum(i - n_a, 0), 0)),
                  pl.BlockSpec((n, k), lambda i: (0, 0))],
        out_specs=pl.BlockSpec((tm, n), lambda i: (i, 0)),
        out_shape=jax.ShapeDtypeStruct(((n_a + n_b) * tm, n), F32),
        compiler_params=_cparams(1),
        name="in_proj_tail",
    )(*xs, wt)


IN_PROJ_COLS = 1024
IN_PROJ_ROWS = 512
CAST_ROWS = 256
SUBLANES = 8


def _in_proj_kernel(plan, n_first, xa_ref, xb_ref, wt_hbm, o_ref, wf, wb, sem):
    j = pl.program_id(0)
    i = pl.program_id(1)
    tn = wb.shape[0]

    def block_copy(jj, slot):
        shift = 0
        for lo, hi, sh in plan:
            shift = jnp.where((jj >= lo) & (jj < hi), sh, shift)
        src = pl.multiple_of(jj * tn + shift, SUBLANES)
        return pltpu.make_async_copy(wt_hbm.at[pl.ds(src, tn)], wf.at[slot], sem.at[slot])

    @pl.when(i == 0)
    def _():
        slot = j % 2

        @pl.when(j == 0)
        def _():
            block_copy(j, slot).start()

        @pl.when(j + 1 < pl.num_programs(0))
        def _():
            block_copy(j + 1, 1 - slot).start()

        block_copy(j, slot).wait()
        for r in range(0, tn, CAST_ROWS):
            wb[r:r + CAST_ROWS, :] = wf[slot, r:r + CAST_ROWS, :].astype(BF16)

    x = jnp.where(i < n_first, xa_ref[...], xb_ref[...])
    o_ref[...] = _dot_nt(x, wb[...])


def _in_proj(xs, wt, segments):
    tm, tn = IN_PROJ_ROWS, IN_PROJ_COLS
    k = wt.shape[1]
    n_a, n_b = xs[0].shape[0] // tm, xs[1].shape[0] // tm
    plan, dest = [], 0
    for src, count in segments:
        assert count % tn == 0 and dest % tn == 0 and (src - dest) % SUBLANES == 0
        plan.append((dest // tn, (dest + count) // tn, src - dest))
        dest += count
    return pl.pallas_call(
        functools.partial(_in_proj_kernel, tuple(plan), n_a),
        grid=(dest // tn, n_a + n_b),
        in_specs=[pl.BlockSpec((tm, k), lambda j, i: (jnp.minimum(i, n_a - 1), 0)),
                  pl.BlockSpec((tm, k), lambda j, i: (jnp.maximum(i - n_a, 0), 0)),
                  pl.BlockSpec(memory_space=pl.ANY)],
        out_specs=pl.BlockSpec((tm, tn), lambda j, i: (i, j)),
        out_shape=jax.ShapeDtypeStruct(((n_a + n_b) * tm, dest), F32),
        scratch_shapes=[pltpu.VMEM((2, tn, k), F32), pltpu.VMEM((tn, k), BF16),
                        pltpu.SemaphoreType.DMA((2,))],
        compiler_params=_cparams(2),
        name="in_proj_main",
    )(*xs, wt)


def _conv_silu(ubuf, cw_ref, cb_ref, lo, width, rows):
    acc = cb_ref[:, lo:lo + width]
    for i in range(SSD_CONV):
        r0 = HALO_ROWS - (SSD_CONV - 1) + i
        acc = acc + cw_ref[i:i + 1, lo:lo + width] * ubuf[r0:r0 + rows, lo:lo + width]
    return _silu(acc)


def _gated_group_norm(y_g, z_g, nw_g):
    yg = y_g * _silu(z_g)
    ms = jnp.mean(yg * yg, axis=-1, keepdims=True)
    return yg * lax.rsqrt(ms + EPS) * nw_g


def _ssd_prompt_kernel(z_ref, xs_ref, bc_ref, tail_ref, cw_ref, cb_ref, dtb_ref, alog_ref,
                       dskip_ref, nw_ref, y_ref, hout_ref, convout_ref, ubuf, ht):
    q = SSD_CHUNK
    c = pl.program_id(1)

    @pl.when(c == 0)
    def _():
        ubuf[0:HALO_ROWS, :] = jnp.zeros((HALO_ROWS, SSD_CONV_DIM), F32)
        ht[...] = jnp.zeros(ht.shape, F32)

    ubuf[HALO_ROWS:HALO_ROWS + q, 0:D_MODEL] = xs_ref[...]
    ubuf[HALO_ROWS:HALO_ROWS + q, D_MODEL:SSD_CONV_DIM] = bc_ref[...]

    dt = _softplus(tail_ref[...] + dtb_ref[...])
    da = dt * (-jnp.exp(alog_ref[...]))
    causal = _tril(q)
    acum = _dot_sel(causal.astype(BF16), da)
    acum_t = acum.T
    dt_t = dt.T
    a_last = acum[q - 1:q, :]
    decay_in = jnp.exp(acum)
    w_end = dt * jnp.exp(a_last - acum)
    chunk_decay = jnp.exp(a_last)
    lane_head = lax.broadcasted_iota(jnp.int32, (q, SSD_GROUP_COLS), 1) >> 6
    assert SSD_HEAD_DIM == 1 << 6

    for g in range(SSD_GROUPS):
        lo = g * SSD_GROUP_COLS
        cols = slice(lo, lo + SSD_GROUP_COLS)
        xs_g = _conv_silu(ubuf, cw_ref, cb_ref, lo, SSD_GROUP_COLS, q)
        b_g = _conv_silu(ubuf, cw_ref, cb_ref, D_MODEL + g * SSD_STATE, SSD_STATE, q).astype(BF16)
        c_g = _conv_silu(ubuf, cw_ref, cb_ref, D_MODEL + SSD_BC + g * SSD_STATE, SSD_STATE,
                         q).astype(BF16)
        scores = _dot_nt(c_g, b_g)
        h_g = ht[g]
        xs_b = xs_g.astype(BF16)
        m_heads, x_heads = [], []
        for hh in range(HEADS_PER_GROUP):
            h = g * HEADS_PER_GROUP + hh
            seg = acum[:, h:h + 1] - acum_t[h:h + 1, :]
            decay = jnp.exp(jnp.where(causal, seg, -jnp.inf))
            m_heads.append((scores * decay * dt_t[h:h + 1, :]).astype(BF16))
            x_heads.append(jnp.where(lane_head == hh, xs_b, jnp.zeros_like(xs_b)))
        y_g = (_dot(jnp.concatenate(m_heads, axis=1), jnp.concatenate(x_heads, axis=0))
               + _dot(c_g, h_g.astype(BF16)) * _expand_group(decay_in, g, lane_head)
               + dskip_ref[:, cols] * xs_g)
        y_ref[:, cols] = _gated_group_norm(y_g, z_ref[:, cols], nw_ref[:, cols])
        xw = (xs_g * _expand_group(w_end, g, lane_head)).astype(BF16)
        ht[g] = h_g * _expand_group(chunk_decay, g, lane_head[0:1]) + _dot_tn(b_g, xw)

    last = HALO_ROWS + q - (SSD_CONV - 1)
    tail_rows = ubuf[last:last + SSD_CONV - 1, :]
    ubuf[HALO_ROWS - (SSD_CONV - 1):HALO_ROWS, :] = tail_rows

    @pl.when(c == pl.num_programs(1) - 1)
    def _():
        convout_ref[0] = tail_rows
        for g in range(SSD_GROUPS):
            h_t = ht[g].T
            for hh in range(HEADS_PER_GROUP):
                hout_ref[0, g * HEADS_PER_GROUP + hh] = h_t[hh * SSD_HEAD_DIM:(hh + 1) * SSD_HEAD_DIM]


def _row_params(p):
    return pl.BlockSpec(p.shape, lambda *_: (0,) * p.ndim)


def _ssd_prompt(proj, tail, n_seq, seq_len, cw, cb, dtb, alog, dskip_e, nw):
    q = SSD_CHUNK
    nc = seq_len // q

    def col(blk):
        return pl.BlockSpec((q, D_MODEL), lambda b, c: (b * nc + c, blk))

    return pl.pallas_call(
        _ssd_prompt_kernel,
        grid=(n_seq, nc),
        in_specs=[col(COL_Z), col(COL_XS), col(COL_BC),
                  pl.BlockSpec((q, LANES), lambda b, c: (b * nc + c, TAIL_DT)),
                  _row_params(cw), _row_params(cb), _row_params(dtb), _row_params(alog),
                  _row_params(dskip_e), _row_params(nw)],
        out_specs=[pl.BlockSpec((q, D_MODEL), lambda b, c: (b * nc + c, 0)),
                   pl.BlockSpec((1, SSD_HEADS, SSD_HEAD_DIM, SSD_STATE), lambda b, c: (b, 0, 0, 0)),
                   pl.BlockSpec((1, SSD_CONV - 1, SSD_CONV_DIM), lambda b, c: (b, 0, 0))],
        out_shape=[jax.ShapeDtypeStruct((n_seq * seq_len, D_MODEL), F32),
                   jax.ShapeDtypeStruct((n_seq, SSD_HEADS, SSD_HEAD_DIM, SSD_STATE), F32),
                   jax.ShapeDtypeStruct((n_seq, SSD_CONV - 1, SSD_CONV_DIM), F32)],
        scratch_shapes=[pltpu.VMEM((HALO_ROWS + q, SSD_CONV_DIM), F32),
                        pltpu.VMEM((SSD_GROUPS, SSD_STATE, SSD_GROUP_COLS), F32)],
        compiler_params=_cparams(2),
        name="ssd_prompt",
    )(proj, proj, proj, tail, cw, cb, dtb, alog, dskip_e, nw)


def _ssd_sample_kernel(z_ref, xs_ref, bc_ref, tail_ref, h_ref, conv_ref, cw_ref, cb_ref, dtb_ref,
                       alog_ref, dskip_ref, nw_ref, y_ref, hout_ref, convout_ref, ubuf, decay_rows):
    q = xs_ref.shape[0]
    ubuf[0:HALO_ROWS - (SSD_CONV - 1), :] = jnp.zeros((HALO_ROWS - (SSD_CONV - 1), SSD_CONV_DIM), F32)
    ubuf[HALO_ROWS - (SSD_CONV - 1):HALO_ROWS, :] = conv_ref[0]
    ubuf[HALO_ROWS:HALO_ROWS + q, 0:D_MODEL] = xs_ref[...]
    ubuf[HALO_ROWS:HALO_ROWS + q, D_MODEL:SSD_CONV_DIM] = bc_ref[...]
    convout_ref[0] = ubuf[HALO_ROWS + q - (SSD_CONV - 1):HALO_ROWS + q, :]

    dt = _softplus(tail_ref[...] + dtb_ref[...])
    da = dt * (-jnp.exp(alog_ref[...]))
    acum = _dot_sel(_tril(q).astype(BF16), da)
    a_last = acum[q - 1:q, :]
    decay_in = jnp.exp(acum)
    w_end = dt * jnp.exp(a_last - acum)
    decay_rows[...] = _col_bcast(jnp.exp(a_last))[:SSD_HEADS]
    row = lax.broadcasted_iota(jnp.int32, (q, SSD_GROUP_COLS), 0)
    lane_head = lax.broadcasted_iota(jnp.int32, (q, SSD_GROUP_COLS), 1) >> 6

    for g in range(SSD_GROUPS):
        lo = g * SSD_GROUP_COLS
        cols = slice(lo, lo + SSD_GROUP_COLS)
        xs_g = _conv_silu(ubuf, cw_ref, cb_ref, lo, SSD_GROUP_COLS, q)
        b_g = _conv_silu(ubuf, cw_ref, cb_ref, D_MODEL + g * SSD_STATE, SSD_STATE, q)
        c_g = _conv_silu(ubuf, cw_ref, cb_ref, D_MODEL + SSD_BC + g * SSD_STATE, SSD_STATE, q)
        h_g = h_ref[0, g * HEADS_PER_GROUP:(g + 1) * HEADS_PER_GROUP].reshape(
            SSD_GROUP_COLS, SSD_STATE)
        y_g = (_dot_nt(c_g.astype(BF16), h_g.astype(BF16)) * _expand_group(decay_in, g, lane_head)
               + dskip_ref[:, cols] * xs_g)
        acum_g = _expand_group(acum, g, lane_head)
        dt_g = _expand_group(dt, g, lane_head)
        for j in range(q):
            s_j = jnp.sum(c_g * b_g[j:j + 1, :], axis=-1, keepdims=True)
            coef = s_j * jnp.exp(acum_g - acum_g[j:j + 1, :]) * dt_g[j:j + 1, :]
            y_g = y_g + jnp.where(row >= j, coef, 0.0) * xs_g[j:j + 1, :]
        y_ref[:, cols] = _gated_group_norm(y_g, z_ref[:, cols], nw_ref[:, cols])
        upd = _dot_tn(xs_g * _expand_group(w_end, g, lane_head), b_g,
                      precision=HIGHEST)
        for hh in range(HEADS_PER_GROUP):
            h = g * HEADS_PER_GROUP + hh
            rows = slice(hh * SSD_HEAD_DIM, (hh + 1) * SSD_HEAD_DIM)
            hout_ref[0, h] = h_g[rows] * decay_rows[h:h + 1, :] + upd[rows]


def _ssd_sample(proj, tail, row0, n_seq, seq_len, h0, conv0, cw, cb, dtb, alog, dskip_e, nw):
    q = seq_len
    blk0 = row0 // q

    def col(blk):
        return pl.BlockSpec((q, D_MODEL), lambda b: (blk0 + b, blk))

    state_spec = pl.BlockSpec((1, SSD_HEADS, SSD_HEAD_DIM, SSD_STATE), lambda b: (b, 0, 0, 0))
    conv_spec = pl.BlockSpec((1, SSD_CONV - 1, SSD_CONV_DIM), lambda b: (b, 0, 0))
    return pl.pallas_call(
        _ssd_sample_kernel,
        grid=(n_seq,),
        in_specs=[col(COL_Z), col(COL_XS), col(COL_BC),
                  pl.BlockSpec((q, LANES), lambda b: (blk0 + b, TAIL_DT)),
                  state_spec, conv_spec,
                  _row_params(cw), _row_params(cb), _row_params(dtb), _row_params(alog),
                  _row_params(dskip_e), _row_params(nw)],
        out_specs=[pl.BlockSpec((q, D_MODEL), lambda b: (b, 0)), state_spec, conv_spec],
        out_shape=[jax.ShapeDtypeStruct((n_seq * seq_len, D_MODEL), F32),
                   jax.ShapeDtypeStruct(h0.shape, F32),
                   jax.ShapeDtypeStruct(conv0.shape, F32)],
        scratch_shapes=[pltpu.VMEM((HALO_ROWS + q, SSD_CONV_DIM), F32),
                        pltpu.VMEM((SSD_HEADS, SSD_STATE), F32)],
        compiler_params=_cparams(1),
        name="ssd_sample",
    )(proj, proj, proj, tail, h0, conv0, cw, cb, dtb, alog, dskip_e, nw)


def _gla_chunk(qk_ref, v_ref, r_ref, alo_ref, wa_ref, ba_ref, nw_ref, y_ref, get_state, put_state):
    q = v_ref.shape[0]
    a_lo = alo_ref[...][:, :GLA_RANK].astype(BF16)
    gk = -_softplus(-(_dot(a_lo, wa_ref[...]) + ba_ref[...])) / GLA_TAU
    causal = _tril(q)
    bcum = _dot_sel(causal.astype(BF16), gk)
    for h in range(GLA_HEADS):
        kc = slice(h * GLA_DK, (h + 1) * GLA_DK)
        vc = slice(h * GLA_DV, (h + 1) * GLA_DV)
        b_h = bcum[:, kc]
        b_last = b_h[q - 1:q, :]
        q_h = qk_ref[:, kc] * (GLA_DK ** -0.5)
        k_h = qk_ref[:, GLA_KEY_DIM + h * GLA_DK:GLA_KEY_DIM + (h + 1) * GLA_DK]
        v_h = v_ref[:, vc].astype(BF16)
        q_in = (q_h * jnp.exp(b_h)).astype(BF16)
        k_in = (k_h * jnp.exp(-b_h)).astype(BF16)
        att = jnp.where(causal, _dot_nt(q_in, k_in), 0.0).astype(BF16)
        s_h = get_state(h)
        o = _dot(att, v_h) + _dot(q_in, s_h.astype(BF16))
        k_end = (k_h * jnp.exp(b_last - b_h)).astype(BF16)
        d_col = _col_bcast(jnp.exp(b_last))
        put_state(h, s_h * jnp.concatenate([d_col, d_col], axis=1) + _dot_tn(k_end, v_h))
        ms = jnp.mean(o * o, axis=-1, keepdims=True)
        y_ref[:, vc] = o * lax.rsqrt(ms + EPS) * nw_ref[...] * _silu(r_ref[:, vc])


def _gla_prompt_kernel(qk_ref, v_ref, r_ref, alo_ref, wa_ref, ba_ref, nw_ref, y_ref, sout_ref, st):
    c = pl.program_id(1)

    @pl.when(c == 0)
    def _():
        st[...] = jnp.zeros(st.shape, F32)

    def put(h, val):
        st[h] = val

    _gla_chunk(qk_ref, v_ref, r_ref, alo_ref, wa_ref, ba_ref, nw_ref, y_ref, lambda h: st[h], put)

    @pl.when(c == pl.num_programs(1) - 1)
    def _():
        sout_ref[0] = st[...]


def _gla_prompt(proj, tail, n_seq, seq_len, wa, ba, nw):
    q = GLA_CHUNK
    nc = seq_len // q

    def col(blk):
        return pl.BlockSpec((q, D_MODEL), lambda b, c: (b * nc + c, blk))

    return pl.pallas_call(
        _gla_prompt_kernel,
        grid=(n_seq, nc),
        in_specs=[col(COL_QK), col(COL_V), col(COL_R),
                  pl.BlockSpec((q, LANES), lambda b, c: (b * nc + c, TAIL_ALO)),
                  _row_params(wa), _row_params(ba), _row_params(nw)],
        out_specs=[pl.BlockSpec((q, D_MODEL), lambda b, c: (b * nc + c, 0)),
                   pl.BlockSpec((1, GLA_HEADS, GLA_DK, GLA_DV), lambda b, c: (b, 0, 0, 0))],
        out_shape=[jax.ShapeDtypeStruct((n_seq * seq_len, D_MODEL), F32),
                   jax.ShapeDtypeStruct((n_seq, GLA_HEADS, GLA_DK, GLA_DV), F32)],
        scratch_shapes=[pltpu.VMEM((GLA_HEADS, GLA_DK, GLA_DV), F32)],
        compiler_params=_cparams(2),
        name="gla_prompt",
    )(proj, proj, proj, tail, wa, ba, nw)


def _gla_sample_kernel(qk_ref, v_ref, r_ref, alo_ref, s_ref, wa_ref, ba_ref, nw_ref, y_ref, sout_ref):
    def put(h, val):
        sout_ref[0, h] = val

    _gla_chunk(qk_ref, v_ref, r_ref, alo_ref, wa_ref, ba_ref, nw_ref, y_ref,
               lambda h: s_ref[0, h], put)


def _gla_sample(proj, tail, row0, n_seq, seq_len, s0, wa, ba, nw):
    q = seq_len
    blk0 = row0 // q

    def col(blk):
        return pl.BlockSpec((q, D_MODEL), lambda b: (blk0 + b, blk))

    state_spec = pl.BlockSpec((1, GLA_HEADS, GLA_DK, GLA_DV), lambda b: (b, 0, 0, 0))
    return pl.pallas_call(
        _gla_sample_kernel,
        grid=(n_seq,),
        in_specs=[col(COL_QK), col(COL_V), col(COL_R),
                  pl.BlockSpec((q, LANES), lambda b: (blk0 + b, TAIL_ALO)),
                  state_spec, _row_params(wa), _row_params(ba), _row_params(nw)],
        out_specs=[pl.BlockSpec((q, D_MODEL), lambda b: (b, 0)), state_spec],
        out_shape=[jax.ShapeDtypeStruct((n_seq * seq_len, D_MODEL), F32),
                   jax.ShapeDtypeStruct(s0.shape, F32)],
        compiler_params=_cparams(1),
        name="gla_sample",
    )(proj, proj, proj, tail, s0, wa, ba, nw)


def _layernorm(x, g, b):
    mu = jnp.mean(x, axis=-1, keepdims=True)
    xc = x - mu
    var = jnp.mean(xc * xc, axis=-1, keepdims=True)
    return xc * lax.rsqrt(var + EPS) * g + b


def _route(logits):
    lane = lax.broadcasted_iota(jnp.int32, logits.shape, 1)
    neg = -jnp.inf
    big = ROUTER_LANES
    glog = jnp.where(lane < MOE_GROUPS, logits, neg)
    gmax = jnp.max(glog, axis=-1, keepdims=True)
    g_sel = jnp.min(jnp.where(glog == gmax, lane, big), axis=-1, keepdims=True)
    p_g = 1.0 / jnp.sum(jnp.exp(glog - gmax), axis=-1, keepdims=True)
    e_lane = lane - MOE_GROUPS
    in_group = (e_lane >= 0) & (e_lane < MOE_EXPERTS) & (e_lane // MOE_EPG == g_sel)
    el = jnp.where(in_group, logits, neg)
    v1 = jnp.max(el, axis=-1, keepdims=True)
    i1 = jnp.min(jnp.where(el == v1, lane, big), axis=-1, keepdims=True)
    el2 = jnp.where(lane == i1, neg, el)
    v2 = jnp.max(el2, axis=-1, keepdims=True)
    i2 = jnp.min(jnp.where(el2 == v2, lane, big), axis=-1, keepdims=True)
    e2 = jnp.exp(v2 - v1)
    w1 = p_g / (1.0 + e2)
    w2 = p_g * e2 / (1.0 + e2)
    first = (i1 - MOE_GROUPS).astype(F32)
    second = (i2 - MOE_GROUPS).astype(F32)
    return jnp.where(lane == ROUTE_E1, first,
                     jnp.where(lane == ROUTE_E2, second,
                               jnp.where(lane == ROUTE_W1, w1,
                                         jnp.where(lane == ROUTE_W2, w2, 0.0))))


ROW_CHUNKS = D_MODEL // LANES


def _store_chunked(ref, val):
    rows = val.shape[0]
    for c in range(ROW_CHUNKS):
        ref[pl.ds(c, rows, stride=ROW_CHUNKS), :] = val[:, c * LANES:(c + 1) * LANES]


def _load_chunked(ref, c):
    return ref[pl.ds(c, ref.shape[0] // ROW_CHUNKS, stride=ROW_CHUNKS), :]


def _merge_kernel(n_first, ysa_ref, ysb_ref, yga_ref, ygb_ref, gs_ref, gg_ref, xa_ref, xb_ref, wo_ref,
                  g_ref, b_ref, wrh_ref, wrl_ref, br_ref, h1_ref, route_ref):
    first = pl.program_id(0) < n_first
    ys = jnp.where(first, ysa_ref[...], ysb_ref[...])
    yg = jnp.where(first, yga_ref[...], ygb_ref[...])
    x = jnp.where(first, xa_ref[...], xb_ref[...])
    merged = jax.nn.sigmoid(gs_ref[...]) * ys + jax.nn.sigmoid(gg_ref[...]) * yg
    mix = _dot(merged.astype(BF16), wo_ref[...])
    h1 = _layernorm(DEEPNORM_ALPHA * x + mix, g_ref[...], b_ref[...])
    _store_chunked(h1_ref, h1)
    h_hi = h1.astype(BF16)
    h_lo = (h1 - h_hi.astype(F32)).astype(BF16)
    logits = (_dot(h_hi, wrh_ref[...]) + _dot(h_lo, wrh_ref[...]) + _dot(h_hi, wrl_ref[...])
              + br_ref[...])
    route_ref[...] = _route(logits)


def _merge(ys, yg, gates, xs, wo, g, b, wr_hi, wr_lo, br, tm):
    n_a, n_b = xs[0].shape[0] // tm, xs[1].shape[0] // tm
    m = (n_a + n_b) * tm

    def rows(i):
        return (i, 0)

    pair = [pl.BlockSpec((tm, D_MODEL), lambda i: (jnp.minimum(i, n_a - 1), 0)),
            pl.BlockSpec((tm, D_MODEL), lambda i: (jnp.maximum(i - n_a, 0), 0))]
    return pl.pallas_call(
        functools.partial(_merge_kernel, n_a),
        grid=(m // tm,),
        in_specs=[*pair, *pair,
                  pl.BlockSpec((tm, D_MODEL), lambda i: (i, COL_GS)),
                  pl.BlockSpec((tm, D_MODEL), lambda i: (i, COL_GG)),
                  *pair,
                  pl.BlockSpec(wo.shape, lambda i: (0, 0), pipeline_mode=pl.Buffered(1)),
                  _row_params(g), _row_params(b),
                  _row_params(wr_hi), _row_params(wr_lo), _row_params(br)],
        out_specs=[pl.BlockSpec((tm * ROW_CHUNKS, LANES), rows),
                   pl.BlockSpec((tm, ROUTER_LANES), rows)],
        out_shape=[jax.ShapeDtypeStruct((m * ROW_CHUNKS, LANES), F32),
                   jax.ShapeDtypeStruct((m, ROUTER_LANES), F32)],
        compiler_params=_cparams(1),
        name="merge_outproj_ln_router",
    )(*ys, *yg, gates, gates, *xs, wo, g, b, wr_hi, wr_lo, br)


def _gather_kernel(idx_ref, table_ref, o_ref, sem):
    n = o_ref.shape[0] // ROW_CHUNKS

    def issue(r, carry):
        src = pl.multiple_of(idx_ref[0, r] * ROW_CHUNKS, ROW_CHUNKS)
        dst = pl.multiple_of(r * ROW_CHUNKS, ROW_CHUNKS)
        pltpu.make_async_copy(table_ref.at[pl.ds(src, ROW_CHUNKS)], o_ref.at[pl.ds(dst, ROW_CHUNKS)],
                              sem).start()
        return carry

    lax.fori_loop(0, n, issue, 0, unroll=8)
    pltpu.make_async_copy(table_ref.at[pl.ds(0, n * ROW_CHUNKS)], o_ref, sem).wait()


def _gather_rows(table, idx, window):
    n = idx.shape[0]
    steps = n // window
    return pl.pallas_call(
        _gather_kernel,
        grid=(steps,),
        in_specs=[pl.BlockSpec((None, 1, window), lambda i: (i, 0, 0), memory_space=pltpu.SMEM),
                  pl.BlockSpec(memory_space=pl.ANY)],
        out_specs=pl.BlockSpec((window * ROW_CHUNKS, LANES), lambda i: (i, 0)),
        out_shape=jax.ShapeDtypeStruct((n * ROW_CHUNKS, LANES), table.dtype),
        scratch_shapes=[pltpu.SemaphoreType.DMA],
        compiler_params=_cparams(1),
        name="gather_rows",
    )(idx.reshape(steps, 1, window), table)


def _zero_runs(pads_ref, zeros, x_hbm, sem, action):
    tm = zeros.shape[0] // ROW_CHUNKS

    def piece(pos, rows):
        dst = pl.multiple_of(pos * ROW_CHUNKS, ROW_CHUNKS)
        return pltpu.make_async_copy(zeros.at[pl.ds(0, rows * ROW_CHUNKS)],
                                     x_hbm.at[pl.ds(dst, rows * ROW_CHUNKS)], sem)

    def run(e, carry):
        pos, length = pads_ref[0, e], pads_ref[1, e]

        def whole(q, c):
            getattr(piece(pos + q * tm, tm), action)()
            return c

        lax.fori_loop(0, length // tm, whole, 0)
        pos = pos + (length // tm) * tm
        rows = tm // 2
        while rows >= 1:
            @pl.when((length & rows) != 0)
            def _(pos=pos, rows=rows):
                getattr(piece(pos, rows), action)()

            pos = pos + (length & rows)
            rows //= 2
        return carry

    lax.fori_loop(0, pads_ref.shape[1], run, 0)


def _dispatch_kernel(slot_ref, pads_ref, h_ref, x_hbm, zeros, sem, zsem):
    n = h_ref.shape[0] // ROW_CHUNKS

    def issue(r, carry):
        src = h_ref.at[pl.ds(pl.multiple_of(r * ROW_CHUNKS, ROW_CHUNKS), ROW_CHUNKS)]
        for k in range(2):
            dst = pl.multiple_of(slot_ref[k, r] * ROW_CHUNKS, ROW_CHUNKS)
            pltpu.make_async_copy(src, x_hbm.at[pl.ds(dst, ROW_CHUNKS)], sem).start()
        return carry

    lax.fori_loop(0, n, issue, 0, unroll=4)

    @pl.when(pl.program_id(0) == pl.num_programs(0) - 1)
    def _():
        zeros[...] = jnp.zeros(zeros.shape, zeros.dtype)
        _zero_runs(pads_ref, zeros, x_hbm, zsem, "start")
        _zero_runs(pads_ref, zeros, x_hbm, zsem, "wait")

    for k in range(2):
        pltpu.make_async_copy(h_ref, x_hbm.at[pl.ds(0, n * ROW_CHUNKS)], sem).wait()


def _dispatch(h1, slot, pads, n_slots, tm, window):
    t = slot.shape[1]
    steps = t // window
    slot3 = slot.reshape(2, steps, window).transpose(1, 0, 2)
    return pl.pallas_call(
        _dispatch_kernel,
        grid=(steps,),
        in_specs=[pl.BlockSpec((None, 2, window), lambda i: (i, 0, 0), memory_space=pltpu.SMEM),
                  pl.BlockSpec(pads.shape, lambda i: (0, 0), memory_space=pltpu.SMEM),
                  pl.BlockSpec((window * ROW_CHUNKS, LANES), lambda i: (i, 0))],
        out_specs=pl.BlockSpec(memory_space=pl.ANY),
        out_shape=jax.ShapeDtypeStruct((n_slots * ROW_CHUNKS, LANES), h1.dtype),
        scratch_shapes=[pltpu.VMEM((tm * ROW_CHUNKS, LANES), h1.dtype), pltpu.SemaphoreType.DMA,
                        pltpu.SemaphoreType.DMA],
        compiler_params=_cparams(1),
        name="dispatch_rows",
    )(slot3, pads, h1)


def _expert_plan(route, tm):
    t = route.shape[0]
    ids = jnp.arange(MOE_EXPERTS, dtype=jnp.int32)
    e = jnp.concatenate([route[:, ROUTE_E1], route[:, ROUTE_E2]]).astype(jnp.int32)
    onehot = (e[:, None] == ids[None, :]).astype(jnp.int32)
    csum = jnp.cumsum(onehot, axis=0)
    rank = jnp.sum(onehot * csum, axis=1) - 1
    cnt = csum[-1]
    tiles_e = (cnt + tm - 1) // tm
    tile_end = jnp.cumsum(tiles_e)
    tile_start = tile_end - tiles_e
    slot = tile_start[e] * tm + rank
    n_tiles = (2 * t) // tm + MOE_EXPERTS
    tid = jnp.arange(n_tiles, dtype=jnp.int32)
    used = tid < tile_end[-1]
    tile_e = jnp.sum(tile_end[None, :] <= jnp.minimum(tid, tile_end[-1] - 1)[:, None], axis=1)
    nvalid = jnp.where(used, jnp.clip(cnt[tile_e] - (tid - tile_start[tile_e]) * tm, 0, tm), 0)
    first = used & (tid == tile_start[tile_e])
    has = tiles_e > 0
    ordinal = jnp.cumsum(has.astype(jnp.int32)) - 1
    later = jnp.where((ids[None, :] > ids[:, None]) & has[None, :], ids[None, :], MOE_EXPERTS)
    nxt_e = jnp.min(later, axis=1)
    nxt_e = jnp.where(nxt_e == MOE_EXPERTS, -1, nxt_e)
    i32 = lambda a: a.astype(jnp.int32)
    pads = jnp.stack([jnp.append(tile_start * tm + cnt, tile_end[-1] * tm),
                      jnp.append(tiles_e * tm - cnt, (n_tiles - tile_end[-1]) * tm)])
    return (slot.reshape(2, t), i32(pads), n_tiles * tm, i32(tile_e), i32(nvalid), i32(first),
            i32(nxt_e[tile_e]), i32(ordinal[tile_e] % 2))


def _experts_kernel(te_ref, nv_ref, first_ref, nxt_ref, par_ref, x_ref, wg_hbm, wu_hbm, wd_hbm, y_ref,
                    wg_f, wu_f, wd_f, wg_b, wu_b, wd_b, x_b, sem):
    i = pl.program_id(0)
    s = par_ref[i]

    def weight_copies(e, slot):
        return (pltpu.make_async_copy(wg_hbm.at[e], wg_f.at[slot], sem.at[slot]),
                pltpu.make_async_copy(wu_hbm.at[e], wu_f.at[slot], sem.at[slot]),
                pltpu.make_async_copy(wd_hbm.at[e], wd_f.at[slot], sem.at[slot]))

    @pl.when(i == 0)
    def _():
        for cp in weight_copies(te_ref[0], par_ref[0]):
            cp.start()

    @pl.when(first_ref[i] == 1)
    def _():
        @pl.when(nxt_ref[i] >= 0)
        def _():
            for cp in weight_copies(nxt_ref[i], 1 - s):
                cp.start()

        for cp in weight_copies(te_ref[i], s):
            cp.wait()
        wg_b[...] = wg_f[s].astype(BF16)
        wu_b[...] = wu_f[s].astype(BF16)
        wd_b[...] = wd_f[s].astype(BF16)

    nv = nv_ref[i]

    @pl.when(nv > 0)
    def _():
        for c in range(ROW_CHUNKS):
            x_b[:, c * LANES:(c + 1) * LANES] = _load_chunked(x_ref, c).astype(BF16)
        x = x_b[...]
        hid = (_silu(_dot(x, wg_b[...])) * _dot(x, wu_b[...])).astype(BF16)
        _store_chunked(y_ref, _dot(hid, wd_b[...]))

    @pl.when(nv == 0)
    def _():
        y_ref[...] = jnp.zeros(y_ref.shape, y_ref.dtype)


def _experts(xs, tile_e, nvalid, first, nxt, par, wg, wu, wd, tm):
    n_tiles = xs.shape[0] // (tm * ROW_CHUNKS)
    rows = pl.BlockSpec((tm * ROW_CHUNKS, LANES), lambda i, *_: (i, 0))
    hbm = pl.BlockSpec(memory_space=pl.ANY)
    grid_spec = pltpu.PrefetchScalarGridSpec(
        num_scalar_prefetch=5,
        grid=(n_tiles,),
        in_specs=[rows, hbm, hbm, hbm],
        out_specs=rows,
        scratch_shapes=[pltpu.VMEM((2, D_MODEL, MOE_FF), F32), pltpu.VMEM((2, D_MODEL, MOE_FF), F32),
                        pltpu.VMEM((2, MOE_FF, D_MODEL), F32),
                        pltpu.VMEM((D_MODEL, MOE_FF), BF16), pltpu.VMEM((D_MODEL, MOE_FF), BF16),
                        pltpu.VMEM((MOE_FF, D_MODEL), BF16), pltpu.VMEM((tm, D_MODEL), BF16),
                        pltpu.SemaphoreType.DMA((2,))])
    return pl.pallas_call(
        _experts_kernel,
        grid_spec=grid_spec,
        out_shape=jax.ShapeDtypeStruct(xs.shape, F32),
        compiler_params=_cparams(1),
        name="experts",
    )(tile_e, nvalid, first, nxt, par, xs, wg, wu, wd)


def _combine_kernel(g1_ref, g2_ref, route_ref, h1_ref, g_ref, b_ref, y_ref):
    route = route_ref[...]
    w1 = route[:, ROUTE_W1:ROUTE_W1 + 1]
    w2 = route[:, ROUTE_W2:ROUTE_W2 + 1]
    pre = jnp.concatenate(
        [DEEPNORM_ALPHA * _load_chunked(h1_ref, c)
         + w1 * _load_chunked(g1_ref, c) + w2 * _load_chunked(g2_ref, c)
         for c in range(ROW_CHUNKS)], axis=1)
    y_ref[...] = _layernorm(pre, g_ref[...], b_ref[...])


def _combine(gathered, route, h1, row0, n_tok, m, g, b, tm):
    blk0 = row0 // tm
    blk1 = (n_tok + row0) // tm
    chunked = lambda blk: pl.BlockSpec((tm * ROW_CHUNKS, LANES), lambda i: (blk + i, 0))
    return pl.pallas_call(
        _combine_kernel,
        grid=(m // tm,),
        in_specs=[chunked(blk0), chunked(blk1),
                  pl.BlockSpec((tm, ROUTER_LANES), lambda i: (blk0 + i, 0)),
                  chunked(blk0),
                  _row_params(g), _row_params(b)],
        out_specs=pl.BlockSpec((tm, D_MODEL), lambda i: (i, 0)),
        out_shape=jax.ShapeDtypeStruct((m, D_MODEL), F32),
        compiler_params=_cparams(1),
        name="combine_ln",
    )(gathered, gathered, route, h1, g, b)


def _pad_lanes(v, width=LANES):
    v = v.reshape(1, -1)
    return jnp.pad(v, ((0, 0), (0, width - v.shape[1])))


def kernel(x_prompt, x_sample, state_ssd, state_conv, state_gla, w_in, conv_w, conv_b, dt_bias, a_log, d_skip, ssd_norm_w, gla_w_a2, gla_b_a, gla_norm_w, w_out, ln1_g, ln1_b, w_router_group, b_router_group, w_router_expert, b_router_expert, w_gate, w_up, w_down, ln2_g, ln2_b):
    assert w_in.shape[0] == DEPTH == 1
    n_p, len_p, _ = x_prompt.shape
    n_s, len_s, _ = x_sample.shape
    rows_p, rows_s = n_p * len_p, n_s * len_s
    xp = x_prompt.reshape(rows_p, D_MODEL)
    xs = x_sample.reshape(rows_s, D_MODEL)

    offs = [0]
    for s in IN_SPLIT_SIZES:
        offs.append(offs[-1] + s)
    w_t = w_in[0].T
    w_tail_t = jnp.concatenate(
        [jnp.pad(w_t[offs[2]:offs[3]], ((0, LANES - SSD_HEADS), (0, 0))),
         jnp.pad(w_t[offs[7]:offs[8]], ((0, LANES - GLA_RANK), (0, 0)))], axis=0).astype(BF16)
    x_b = (xp.astype(BF16), xs.astype(BF16))
    total = rows_p + rows_s

    proj = _in_proj(x_b, w_t, ((0, offs[2]), (offs[3], offs[7] - offs[3]),
                               (offs[8], offs[10] - offs[8])))
    tail = _matmul_nt(x_b, w_tail_t, IN_PROJ_ROWS)

    cw, cb = conv_w[0], conv_b[0].reshape(1, -1)
    dtb, alog = _pad_lanes(dt_bias[0]), _pad_lanes(a_log[0])
    dskip_e = jnp.repeat(d_skip[0], SSD_HEAD_DIM).reshape(1, -1)
    nw_ssd = ssd_norm_w[0].reshape(1, -1)
    wa, ba, nw_gla = gla_w_a2[0].astype(BF16), gla_b_a[0].reshape(1, -1), gla_norm_w[0].reshape(1, -1)

    ys_p, ssd_p, conv_p = _ssd_prompt(proj, tail, n_p, len_p, cw, cb, dtb, alog, dskip_e, nw_ssd)
    ys_s, ssd_s, conv_s = _ssd_sample(proj, tail, rows_p, n_s, len_s, state_ssd[0], state_conv[0],
                                      cw, cb, dtb, alog, dskip_e, nw_ssd)
    yg_p, gla_p = _gla_prompt(proj, tail, n_p, len_p, wa, ba, nw_gla)
    yg_s, gla_s = _gla_sample(proj, tail, rows_p, n_s, len_s, state_gla[0], wa, ba, nw_gla)

    wo = w_out[0].astype(BF16)
    w_r = jnp.pad(jnp.concatenate([w_router_group[0], w_router_expert[0]], axis=1),
                  ((0, 0), (0, ROUTER_LANES - MOE_GROUPS - MOE_EXPERTS)))
    wr_hi = w_r.astype(BF16)
    wr_lo = (w_r - wr_hi.astype(F32)).astype(BF16)
    b_r = _pad_lanes(jnp.concatenate([b_router_group[0], b_router_expert[0]]), ROUTER_LANES)
    g1, b1 = ln1_g[0].reshape(1, -1), ln1_b[0].reshape(1, -1)
    g2, b2 = ln2_g[0].reshape(1, -1), ln2_b[0].reshape(1, -1)

    h1, route = _merge((ys_p, ys_s), (yg_p, yg_s), proj, (xp, xs), wo, g1, b1, wr_hi, wr_lo, b_r,
                       MERGE_ROWS)
    slot, pads, n_slots, tile_e, nvalid, first, nxt, par = _expert_plan(route, EXPERT_ROWS)
    x_sorted = _dispatch(h1, slot, pads, n_slots, EXPERT_ROWS, GATHER_WINDOW)
    y_sorted = _experts(x_sorted, tile_e, nvalid, first, nxt, par, w_gate[0], w_up[0], w_down[0],
                        EXPERT_ROWS)
    y_tok = _gather_rows(y_sorted, slot.reshape(-1), GATHER_WINDOW)
    outs = [_combine(y_tok, route, h1, 0, total, rows_p, g2, b2, MERGE_ROWS),
            _combine(y_tok, route, h1, rows_p, total, rows_s, g2, b2, MERGE_ROWS)]
    y_p = outs[0].reshape(x_prompt.shape)
    y_s = outs[1].reshape(x_sample.shape)
    return (y_p, y_s, ssd_p[None], conv_p[None], gla_p[None], ssd_s[None], conv_s[None], gla_s[None])
```

```python
import functools
import math

import jax
import jax.numpy as jnp
from jax import lax
from jax.experimental import pallas as pl
from jax.experimental.pallas import tpu as pltpu

F32 = jnp.float32
BF16 = jnp.bfloat16
HIGHEST = lax.Precision.HIGHEST

D_MODEL = 2048
SSD_HEADS = 32
SSD_HEAD_DIM = 64
SSD_GROUPS = 8
SSD_STATE = 128
SSD_CONV = 4
SSD_CHUNK = 128
SSD_BC = SSD_GROUPS * SSD_STATE
SSD_CONV_DIM = D_MODEL + 2 * SSD_BC
SSD_GROUP_COLS = D_MODEL // SSD_GROUPS
HEADS_PER_GROUP = SSD_HEADS // SSD_GROUPS
GLA_HEADS = 8
GLA_DK = 128
GLA_DV = 256
GLA_KEY_DIM = GLA_HEADS * GLA_DK
GLA_RANK = 16
GLA_TAU = 16.0
GLA_CHUNK = 64
MOE_GROUPS = 4
MOE_EPG = 8
MOE_EXPERTS = 32
MOE_FF = 512
DEPTH = 1
DEEPNORM_ALPHA = (2.0 * DEPTH) ** 0.25
EPS = 1e-5
IN_SPLIT_SIZES = (D_MODEL, SSD_CONV_DIM, SSD_HEADS, GLA_KEY_DIM, GLA_KEY_DIM, D_MODEL, D_MODEL,
                  GLA_RANK, D_MODEL, D_MODEL)

LANES = 128
HALO_ROWS = 8
COL_Z, COL_XS, COL_BC, COL_QK, COL_V, COL_R, COL_GS, COL_GG = range(8)
TAIL_DT, TAIL_ALO = 0, 1
ROUTER_LANES = 128
ROUTE_E1, ROUTE_E2, ROUTE_W1, ROUTE_W2 = 0, 1, 2, 3
MERGE_ROWS = 256
EXPERT_ROWS = 256
GATHER_WINDOW = 512
DMA_QUEUES = 2
VMEM_LIMIT = 56 * 1024 * 1024


def _cparams(n_axes):
    return pltpu.CompilerParams(dimension_semantics=("arbitrary",) * n_axes,
                                vmem_limit_bytes=VMEM_LIMIT)


def _silu(x):
    return x * jax.nn.sigmoid(x)


def _softplus(x):
    return jnp.maximum(x, 0.0) + jnp.log1p(jnp.exp(-jnp.abs(x)))


def _tril(n):
    r = lax.broadcasted_iota(jnp.int32, (n, n), 0)
    c = lax.broadcasted_iota(jnp.int32, (n, n), 1)
    return r >= c


def _expand_group(p, g, lane_head):
    h0 = g * HEADS_PER_GROUP
    out = p[:, h0 + HEADS_PER_GROUP - 1:h0 + HEADS_PER_GROUP]
    for hh in range(HEADS_PER_GROUP - 2, -1, -1):
        out = jnp.where(lane_head == hh, p[:, h0 + hh:h0 + hh + 1], out)
    return out


def _split3(x):
    hi = x.astype(BF16)
    r = x - hi.astype(F32)
    mid = r.astype(BF16)
    lo = (r - mid.astype(F32)).astype(BF16)
    return hi, mid, lo


def _dot_sel(sel, x):
    hi, mid, lo = _split3(x)
    return _dot(sel, lo) + _dot(sel, mid) + _dot(sel, hi)


def _col_bcast(row):
    return jnp.broadcast_to(row, (LANES, LANES)).T


def _dot(a, b, **kw):
    return jnp.dot(a, b, preferred_element_type=F32, **kw)


def _dot_nt(a, b, **kw):
    return lax.dot_general(a, b, (((1,), (1,)), ((), ())), preferred_element_type=F32, **kw)


def _dot_tn(a, b, **kw):
    return lax.dot_general(a, b, (((0,), (0,)), ((), ())), preferred_element_type=F32, **kw)


def _matmul_kernel(n_first, xa_ref, xb_ref, wt_ref, o_ref):
    x = jnp.where(pl.program_id(0) < n_first, xa_ref[...], xb_ref[...])
    o_ref[...] = _dot_nt(x, wt_ref[...].astype(BF16))


def _matmul_nt(xs, wt, tm):
    n, k = wt.shape
    n_a, n_b = xs[0].shape[0] // tm, xs[1].shape[0] // tm
    return pl.pallas_call(
        functools.partial(_matmul_kernel, n_a),
        grid=(n_a + n_b,),
        in_specs=[pl.BlockSpec((tm, k), lambda i: (jnp.minimum(i, n_a - 1), 0)),
                  pl.BlockSpec((tm, k), lambda i: (jnp.maximum(i - n_a, 0), 0)),
                  pl.BlockSpec((n, k), lambda i: (0, 0))],
        out_specs=pl.BlockSpec((tm, n), lambda i: (i, 0)),
        out_shape=jax.ShapeDtypeStruct(((n_a + n_b) * tm, n), F32),
        compiler_params=_cparams(1),
        name="in_proj_tail",
    )(*xs, wt)


IN_PROJ_COLS = 1024
IN_PROJ_ROWS = 1024
CAST_ROWS = 256
SUBLANES = 8


def _in_proj_kernel(plan, n_first, xa_ref, xb_ref, wt_hbm, o_ref, wf, wb, sem):
    j = pl.program_id(0)
    i = pl.program_id(1)
    tn = wb.shape[0]

    def block_copy(jj, slot):
        shift = 0
        for lo, hi, sh in plan:
            shift = jnp.where((jj >= lo) & (jj < hi), sh, shift)
        src = pl.multiple_of(jj * tn + shift, SUBLANES)
        return pltpu.make_async_copy(wt_hbm.at[pl.ds(src, tn)], wf.at[slot], sem.at[slot])

    @pl.when(i == 0)
    def _():
        slot = j % 2

        @pl.when(j == 0)
        def _():
            block_copy(j, slot).start()

        @pl.when(j + 1 < pl.num_programs(0))
        def _():
            block_copy(j + 1, 1 - slot).start()

        block_copy(j, slot).wait()
        for r in range(0, tn, CAST_ROWS):
            wb[r:r + CAST_ROWS, :] = wf[slot, r:r + CAST_ROWS, :].astype(BF16)

    x = jnp.where(i < n_first, xa_ref[...], xb_ref[...])
    o_ref[...] = _dot_nt(x, wb[...])


def _in_proj(xs, wt, segments):
    tm, tn = IN_PROJ_ROWS, IN_PROJ_COLS
    k = wt.shape[1]
    n_a, n_b = xs[0].shape[0] // tm, xs[1].shape[0] // tm
    plan, dest = [], 0
    for src, count in segments:
        assert count % tn == 0 and dest % tn == 0 and (src - dest) % SUBLANES == 0
        plan.append((dest // tn, (dest + count) // tn, src - dest))
        dest += count
    return pl.pallas_call(
        functools.partial(_in_proj_kernel, tuple(plan), n_a),
        grid=(dest // tn, n_a + n_b),
        in_specs=[pl.BlockSpec((tm, k), lambda j, i: (jnp.minimum(i, n_a - 1), 0)),
                  pl.BlockSpec((tm, k), lambda j, i: (jnp.maximum(i - n_a, 0), 0)),
                  pl.BlockSpec(memory_space=pl.ANY)],
        out_specs=pl.BlockSpec((tm, tn), lambda j, i: (i, j)),
        out_shape=jax.ShapeDtypeStruct(((n_a + n_b) * tm, dest), F32),
        scratch_shapes=[pltpu.VMEM((2, tn, k), F32), pltpu.VMEM((tn, k), BF16),
                        pltpu.SemaphoreType.DMA((2,))],
        compiler_params=_cparams(2),
        name="in_proj_main",
    )(*xs, wt)


def _conv_silu(ubuf, cw_ref, cb_ref, lo, width, rows):
    acc = cb_ref[:, lo:lo + width]
    for i in range(SSD_CONV):
        r0 = HALO_ROWS - (SSD_CONV - 1) + i
        acc = acc + cw_ref[i:i + 1, lo:lo + width] * ubuf[r0:r0 + rows, lo:lo + width]
    return _silu(acc)


def _gated_group_norm(y_g, z_g, nw_g):
    yg = y_g * _silu(z_g)
    ms = jnp.mean(yg * yg, axis=-1, keepdims=True)
    return yg * lax.rsqrt(ms + EPS) * nw_g


def _ssd_prompt_kernel(z_ref, xs_ref, bc_ref, tail_ref, cw_ref, cb_ref, dtb_ref, alog_ref,
                       dskip_ref, nw_ref, y_ref, hout_ref, convout_ref, ubuf, ht):
    q = SSD_CHUNK
    c = pl.program_id(1)

    @pl.when(c == 0)
    def _():
        ubuf[0:HALO_ROWS, :] = jnp.zeros((HALO_ROWS, SSD_CONV_DIM), F32)
        ht[...] = jnp.zeros(ht.shape, F32)

    ubuf[HALO_ROWS:HALO_ROWS + q, 0:D_MODEL] = xs_ref[...]
    ubuf[HALO_ROWS:HALO_ROWS + q, D_MODEL:SSD_CONV_DIM] = bc_ref[...]

    dt = _softplus(tail_ref[...] + dtb_ref[...])
    da = dt * (-jnp.exp(alog_ref[...]))
    causal = _tril(q)
    acum = _dot_sel(causal.astype(BF16), da)
    acum_t = acum.T
    dt_t = dt.T
    a_last = acum[q - 1:q, :]
    decay_in = jnp.exp(acum)
    w_end = dt * jnp.exp(a_last - acum)
    chunk_decay = jnp.exp(a_last)
    lane_head = lax.broadcasted_iota(jnp.int32, (q, SSD_GROUP_COLS), 1) >> 6
    assert SSD_HEAD_DIM == 1 << 6

    for g in range(SSD_GROUPS):
        lo = g * SSD_GROUP_COLS
        cols = slice(lo, lo + SSD_GROUP_COLS)
        xs_g = _conv_silu(ubuf, cw_ref, cb_ref, lo, SSD_GROUP_COLS, q)
        b_g = _conv_silu(ubuf, cw_ref, cb_ref, D_MODEL + g * SSD_STATE, SSD_STATE, q).astype(BF16)
        c_g = _conv_silu(ubuf, cw_ref, cb_ref, D_MODEL + SSD_BC + g * SSD_STATE, SSD_STATE,
                         q).astype(BF16)
        scores = _dot_nt(c_g, b_g)
        h_g = ht[g]
        xs_b = xs_g.astype(BF16)
        m_heads, x_heads = [], []
        for hh in range(HEADS_PER_GROUP):
            h = g * HEADS_PER_GROUP + hh
            seg = acum[:, h:h + 1] - acum_t[h:h + 1, :]
            decay = jnp.exp(jnp.where(causal, seg, -jnp.inf))
            m_heads.append((scores * decay * dt_t[h:h + 1, :]).astype(BF16))
            x_heads.append(jnp.where(lane_head == hh, xs_b, jnp.zeros_like(xs_b)))
        y_g = (_dot(jnp.concatenate(m_heads, axis=1), jnp.concatenate(x_heads, axis=0))
               + _dot(c_g, h_g.astype(BF16)) * _expand_group(decay_in, g, lane_head)
               + dskip_ref[:, cols] * xs_g)
        y_ref[:, cols] = _gated_group_norm(y_g, z_ref[:, cols], nw_ref[:, cols])
        xw = (xs_g * _expand_group(w_end, g, lane_head)).astype(BF16)
        ht[g] = h_g * _expand_group(chunk_decay, g, lane_head[0:1]) + _dot_tn(b_g, xw)

    last = HALO_ROWS + q - (SSD_CONV - 1)
    tail_rows = ubuf[last:last + SSD_CONV - 1, :]
    ubuf[HALO_ROWS - (SSD_CONV - 1):HALO_ROWS, :] = tail_rows

    @pl.when(c == pl.num_programs(1) - 1)
    def _():
        convout_ref[0] = tail_rows
        for g in range(SSD_GROUPS):
            h_t = ht[g].T
            for hh in range(HEADS_PER_GROUP):
                hout_ref[0, g * HEADS_PER_GROUP + hh] = h_t[hh * SSD_HEAD_DIM:(hh + 1) * SSD_HEAD_DIM]


def _row_params(p):
    return pl.BlockSpec(p.shape, lambda *_: (0,) * p.ndim)


def _ssd_prompt(proj, tail, n_seq, seq_len, cw, cb, dtb, alog, dskip_e, nw):
    q = SSD_CHUNK
    nc = seq_len // q

    def col(blk):
        return pl.BlockSpec((q, D_MODEL), lambda b, c: (b * nc + c, blk))

    return pl.pallas_call(
        _ssd_prompt_kernel,
        grid=(n_seq, nc),
        in_specs=[col(COL_Z), col(COL_XS), col(COL_BC),
                  pl.BlockSpec((q, LANES), lambda b, c: (b * nc + c, TAIL_DT)),
                  _row_params(cw), _row_params(cb), _row_params(dtb), _row_params(alog),
                  _row_params(dskip_e), _row_params(nw)],
        out_specs=[pl.BlockSpec((q, D_MODEL), lambda b, c: (b * nc + c, 0)),
                   pl.BlockSpec((1, SSD_HEADS, SSD_HEAD_DIM, SSD_STATE), lambda b, c: (b, 0, 0, 0)),
                   pl.BlockSpec((1, SSD_CONV - 1, SSD_CONV_DIM), lambda b, c: (b, 0, 0))],
        out_shape=[jax.ShapeDtypeStruct((n_seq * seq_len, D_MODEL), F32),
                   jax.ShapeDtypeStruct((n_seq, SSD_HEADS, SSD_HEAD_DIM, SSD_STATE), F32),
                   jax.ShapeDtypeStruct((n_seq, SSD_CONV - 1, SSD_CONV_DIM), F32)],
        scratch_shapes=[pltpu.VMEM((HALO_ROWS + q, SSD_CONV_DIM), F32),
                        pltpu.VMEM((SSD_GROUPS, SSD_STATE, SSD_GROUP_COLS), F32)],
        compiler_params=_cparams(2),
        name="ssd_prompt",
    )(proj, proj, proj, tail, cw, cb, dtb, alog, dskip_e, nw)


SAMPLE_SEQS = 4


def _dot_tn_split(a, b):
    a_hi = a.astype(BF16)
    a_lo = (a - a_hi.astype(F32)).astype(BF16)
    b_hi = b.astype(BF16)
    b_lo = (b - b_hi.astype(F32)).astype(BF16)
    return _dot_tn(a_lo, b_hi) + _dot_tn(a_hi, b_lo) + _dot_tn(a_hi, b_hi)


def _ssd_sample_seq(z_ref, xs_ref, bc_ref, tail_ref, h_ref, conv_ref, cw_ref, cb_ref, dtb_ref,
                    alog_ref, dskip_ref, nw_ref, y_ref, hout_ref, convout_ref, ubuf, decay_rows):
    q = xs_ref.shape[0]
    ubuf[0:HALO_ROWS - (SSD_CONV - 1), :] = jnp.zeros((HALO_ROWS - (SSD_CONV - 1), SSD_CONV_DIM), F32)
    ubuf[HALO_ROWS - (SSD_CONV - 1):HALO_ROWS, :] = conv_ref[...]
    ubuf[HALO_ROWS:HALO_ROWS + q, 0:D_MODEL] = xs_ref[...]
    ubuf[HALO_ROWS:HALO_ROWS + q, D_MODEL:SSD_CONV_DIM] = bc_ref[...]
    convout_ref[...] = ubuf[HALO_ROWS + q - (SSD_CONV - 1):HALO_ROWS + q, :]

    dt = _softplus(tail_ref[...] + dtb_ref[...])
    da = dt * (-jnp.exp(alog_ref[...]))
    acum = _dot_sel(_tril(q).astype(BF16), da)
    a_last = acum[q - 1:q, :]
    decay_in = jnp.exp(acum)
    w_end = dt * jnp.exp(a_last - acum)
    decay_rows[...] = _col_bcast(jnp.exp(a_last))[:SSD_HEADS]
    row = lax.broadcasted_iota(jnp.int32, (q, SSD_GROUP_COLS), 0)
    lane_head = lax.broadcasted_iota(jnp.int32, (q, SSD_GROUP_COLS), 1) >> 6

    for g in range(SSD_GROUPS):
        lo = g * SSD_GROUP_COLS
        cols = slice(lo, lo + SSD_GROUP_COLS)
        xs_g = _conv_silu(ubuf, cw_ref, cb_ref, lo, SSD_GROUP_COLS, q)
        b_g = _conv_silu(ubuf, cw_ref, cb_ref, D_MODEL + g * SSD_STATE, SSD_STATE, q)
        c_g = _conv_silu(ubuf, cw_ref, cb_ref, D_MODEL + SSD_BC + g * SSD_STATE, SSD_STATE, q)
        h_g = h_ref[g * HEADS_PER_GROUP:(g + 1) * HEADS_PER_GROUP].reshape(
            SSD_GROUP_COLS, SSD_STATE)
        y_g = (_dot_nt(c_g.astype(BF16), h_g.astype(BF16)) * _expand_group(decay_in, g, lane_head)
               + dskip_ref[:, cols] * xs_g)
        acum_g = _expand_group(acum, g, lane_head)
        dt_g = _expand_group(dt, g, lane_head)
        for j in range(q):
            s_j = jnp.sum(c_g * b_g[j:j + 1, :], axis=-1, keepdims=True)
            coef = s_j * jnp.exp(acum_g - acum_g[j:j + 1, :]) * dt_g[j:j + 1, :]
            y_g = y_g + jnp.where(row >= j, coef, 0.0) * xs_g[j:j + 1, :]
        y_ref[:, cols] = _gated_group_norm(y_g, z_ref[:, cols], nw_ref[:, cols])
        upd = _dot_tn_split(xs_g * _expand_group(w_end, g, lane_head), b_g)
        for hh in range(HEADS_PER_GROUP):
            h = g * HEADS_PER_GROUP + hh
            rows = slice(hh * SSD_HEAD_DIM, (hh + 1) * SSD_HEAD_DIM)
            hout_ref[h] = h_g[rows] * decay_rows[h:h + 1, :] + upd[rows]


def _ssd_sample_kernel(z_ref, xs_ref, bc_ref, tail_ref, h_ref, conv_ref, cw_ref, cb_ref, dtb_ref,
                       alog_ref, dskip_ref, nw_ref, y_ref, hout_ref, convout_ref, ubuf, decay_rows):
    n = h_ref.shape[0]
    q = xs_ref.shape[0] // n
    for s in range(n):
        rows = pl.ds(s * q, q)
        _ssd_sample_seq(z_ref.at[rows], xs_ref.at[rows], bc_ref.at[rows], tail_ref.at[rows],
                        h_ref.at[s], conv_ref.at[s], cw_ref, cb_ref, dtb_ref, alog_ref, dskip_ref,
                        nw_ref, y_ref.at[rows], hout_ref.at[s], convout_ref.at[s], ubuf.at[s],
                        decay_rows.at[s])


def _ssd_sample(proj, tail, row0, n_seq, seq_len, h0, conv0, cw, cb, dtb, alog, dskip_e, nw):
    nb = SAMPLE_SEQS
    q = seq_len
    blk0 = row0 // (nb * q)

    def col(blk):
        return pl.BlockSpec((nb * q, D_MODEL), lambda b: (blk0 + b, blk))

    state_spec = pl.BlockSpec((nb, SSD_HEADS, SSD_HEAD_DIM, SSD_STATE), lambda b: (b, 0, 0, 0))
    conv_spec = pl.BlockSpec((nb, SSD_CONV - 1, SSD_CONV_DIM), lambda b: (b, 0, 0))
    return pl.pallas_call(
        _ssd_sample_kernel,
        grid=(n_seq // nb,),
        in_specs=[col(COL_Z), col(COL_XS), col(COL_BC),
                  pl.BlockSpec((nb * q, LANES), lambda b: (blk0 + b, TAIL_DT)),
                  state_spec, conv_spec,
                  _row_params(cw), _row_params(cb), _row_params(dtb), _row_params(alog),
                  _row_params(dskip_e), _row_params(nw)],
        out_specs=[pl.BlockSpec((nb * q, D_MODEL), lambda b: (b, 0)), state_spec, conv_spec],
        out_shape=[jax.ShapeDtypeStruct((n_seq * seq_len, D_MODEL), F32),
                   jax.ShapeDtypeStruct(h0.shape, F32),
                   jax.ShapeDtypeStruct(conv0.shape, F32)],
        scratch_shapes=[pltpu.VMEM((nb, HALO_ROWS + q, SSD_CONV_DIM), F32),
                        pltpu.VMEM((nb, SSD_HEADS, SSD_STATE), F32)],
        compiler_params=_cparams(1),
        name="ssd_sample",
    )(proj, proj, proj, tail, h0, conv0, cw, cb, dtb, alog, dskip_e, nw)


def _gla_chunk(qk_ref, v_ref, r_ref, alo_ref, wa_ref, ba_ref, nw_ref, y_ref, get_state, put_state):
    q = v_ref.shape[0]
    a_lo = alo_ref[...][:, :GLA_RANK].astype(BF16)
    gk = -_softplus(-(_dot(a_lo, wa_ref[...]) + ba_ref[...])) / GLA_TAU
    causal = _tril(q)
    bcum = _dot_sel(causal.astype(BF16), gk)
    for h in range(GLA_HEADS):
        kc = slice(h * GLA_DK, (h + 1) * GLA_DK)
        vc = slice(h * GLA_DV, (h + 1) * GLA_DV)
        b_h = bcum[:, kc]
        b_last = b_h[q - 1:q, :]
        q_h = qk_ref[:, kc] * (GLA_DK ** -0.5)
        k_h = qk_ref[:, GLA_KEY_DIM + h * GLA_DK:GLA_KEY_DIM + (h + 1) * GLA_DK]
        v_h = v_ref[:, vc].astype(BF16)
        q_in = (q_h * jnp.exp(b_h)).astype(BF16)
        k_in = (k_h * jnp.exp(-b_h)).astype(BF16)
        att = jnp.where(causal, _dot_nt(q_in, k_in), 0.0).astype(BF16)
        s_h = get_state(h)
        o = _dot(att, v_h) + _dot(q_in, s_h.astype(BF16))
        k_end = (k_h * jnp.exp(b_last - b_h)).astype(BF16)
        d_col = _col_bcast(jnp.exp(b_last))
        put_state(h, s_h * jnp.concatenate([d_col, d_col], axis=1) + _dot_tn(k_end, v_h))
        ms = jnp.mean(o * o, axis=-1, keepdims=True)
        y_ref[:, vc] = o * lax.rsqrt(ms + EPS) * nw_ref[...] * _silu(r_ref[:, vc])


def _gla_prompt_kernel(qk_ref, v_ref, r_ref, alo_ref, wa_ref, ba_ref, nw_ref, y_ref, sout_ref, st):
    c = pl.program_id(1)

    @pl.when(c == 0)
    def _():
        st[...] = jnp.zeros(st.shape, F32)

    def put(h, val):
        st[h] = val

    _gla_chunk(qk_ref, v_ref, r_ref, alo_ref, wa_ref, ba_ref, nw_ref, y_ref, lambda h: st[h], put)

    @pl.when(c == pl.num_programs(1) - 1)
    def _():
        sout_ref[0] = st[...]


def _gla_prompt(proj, tail, n_seq, seq_len, wa, ba, nw):
    q = GLA_CHUNK
    nc = seq_len // q

    def col(blk):
        return pl.BlockSpec((q, D_MODEL), lambda b, c: (b * nc + c, blk))

    return pl.pallas_call(
        _gla_prompt_kernel,
        grid=(n_seq, nc),
        in_specs=[col(COL_QK), col(COL_V), col(COL_R),
                  pl.BlockSpec((q, LANES), lambda b, c: (b * nc + c, TAIL_ALO)),
                  _row_params(wa), _row_params(ba), _row_params(nw)],
        out_specs=[pl.BlockSpec((q, D_MODEL), lambda b, c: (b * nc + c, 0)),
                   pl.BlockSpec((1, GLA_HEADS, GLA_DK, GLA_DV), lambda b, c: (b, 0, 0, 0))],
        out_shape=[jax.ShapeDtypeStruct((n_seq * seq_len, D_MODEL), F32),
                   jax.ShapeDtypeStruct((n_seq, GLA_HEADS, GLA_DK, GLA_DV), F32)],
        scratch_shapes=[pltpu.VMEM((GLA_HEADS, GLA_DK, GLA_DV), F32)],
        compiler_params=_cparams(2),
        name="gla_prompt",
    )(proj, proj, proj, tail, wa, ba, nw)


def _gla_sample_kernel(qk_ref, v_ref, r_ref, alo_ref, s_ref, wa_ref, ba_ref, nw_ref, y_ref, sout_ref):
    n = s_ref.shape[0]
    q = v_ref.shape[0] // n
    for s in range(n):
        rows = pl.ds(s * q, q)

        def put(h, val, s=s):
            sout_ref[s, h] = val

        _gla_chunk(qk_ref.at[rows], v_ref.at[rows], r_ref.at[rows], alo_ref.at[rows], wa_ref, ba_ref,
                   nw_ref, y_ref.at[rows], lambda h, s=s: s_ref[s, h], put)


def _gla_sample(proj, tail, row0, n_seq, seq_len, s0, wa, ba, nw):
    nb = SAMPLE_SEQS
    q = seq_len
    blk0 = row0 // (nb * q)

    def col(blk):
        return pl.BlockSpec((nb * q, D_MODEL), lambda b: (blk0 + b, blk))

    state_spec = pl.BlockSpec((nb, GLA_HEADS, GLA_DK, GLA_DV), lambda b: (b, 0, 0, 0))
    return pl.pallas_call(
        _gla_sample_kernel,
        grid=(n_seq // nb,),
        in_specs=[col(COL_QK), col(COL_V), col(COL_R),
                  pl.BlockSpec((nb * q, LANES), lambda b: (blk0 + b, TAIL_ALO)),
                  state_spec, _row_params(wa), _row_params(ba), _row_params(nw)],
        out_specs=[pl.BlockSpec((nb * q, D_MODEL), lambda b: (b, 0)), state_spec],
        out_shape=[jax.ShapeDtypeStruct((n_seq * seq_len, D_MODEL), F32),
                   jax.ShapeDtypeStruct(s0.shape, F32)],
        compiler_params=_cparams(1),
        name="gla_sample",
    )(proj, proj, proj, tail, s0, wa, ba, nw)


def _layernorm(x, g, b):
    mu = jnp.mean(x, axis=-1, keepdims=True)
    xc = x - mu
    var = jnp.mean(xc * xc, axis=-1, keepdims=True)
    return xc * lax.rsqrt(var + EPS) * g + b


def _route(logits):
    lane = lax.broadcasted_iota(jnp.int32, logits.shape, 1)
    neg = -jnp.inf
    big = ROUTER_LANES
    glog = jnp.where(lane < MOE_GROUPS, logits, neg)
    gmax = jnp.max(glog, axis=-1, keepdims=True)
    g_sel = jnp.min(jnp.where(glog == gmax, lane, big), axis=-1, keepdims=True)
    p_g = 1.0 / jnp.sum(jnp.exp(glog - gmax), axis=-1, keepdims=True)
    e_lane = lane - MOE_GROUPS
    in_group = (e_lane >= 0) & (e_lane < MOE_EXPERTS) & (e_lane // MOE_EPG == g_sel)
    el = jnp.where(in_group, logits, neg)
    v1 = jnp.max(el, axis=-1, keepdims=True)
    i1 = jnp.min(jnp.where(el == v1, lane, big), axis=-1, keepdims=True)
    el2 = jnp.where(lane == i1, neg, el)
    v2 = jnp.max(el2, axis=-1, keepdims=True)
    i2 = jnp.min(jnp.where(el2 == v2, lane, big), axis=-1, keepdims=True)
    e2 = jnp.exp(v2 - v1)
    w1 = p_g / (1.0 + e2)
    w2 = p_g * e2 / (1.0 + e2)
    first = (i1 - MOE_GROUPS).astype(F32)
    second = (i2 - MOE_GROUPS).astype(F32)
    return jnp.where(lane == ROUTE_E1, first,
                     jnp.where(lane == ROUTE_E2, second,
                               jnp.where(lane == ROUTE_W1, w1,
                                         jnp.where(lane == ROUTE_W2, w2, 0.0))))


ROW_CHUNKS = D_MODEL // LANES


def _store_chunked(ref, val):
    rows = val.shape[0]
    for c in range(ROW_CHUNKS):
        ref[pl.ds(c, rows, stride=ROW_CHUNKS), :] = val[:, c * LANES:(c + 1) * LANES]


def _load_chunked(ref, c):
    return ref[pl.ds(c, ref.shape[0] // ROW_CHUNKS, stride=ROW_CHUNKS), :]


def _merge_kernel(n_first, ysa_ref, ysb_ref, yga_ref, ygb_ref, gs_ref, gg_ref, xa_ref, xb_ref, wo_ref,
                  g_ref, b_ref, wrh_ref, wrl_ref, br_ref, h1_ref, route_ref):
    def tile(ys_ref, yg_ref, x_ref):
        merged = (jax.nn.sigmoid(gs_ref[...]) * ys_ref[...]
                  + jax.nn.sigmoid(gg_ref[...]) * yg_ref[...])
        mix = _dot(merged.astype(BF16), wo_ref[...])
        h1 = _layernorm(DEEPNORM_ALPHA * x_ref[...] + mix, g_ref[...], b_ref[...])
        _store_chunked(h1_ref, h1)
        h_hi = h1.astype(BF16)
        h_lo = (h1 - h_hi.astype(F32)).astype(BF16)
        logits = (_dot(h_hi, wrh_ref[...]) + _dot(h_lo, wrh_ref[...]) + _dot(h_hi, wrl_ref[...])
                  + br_ref[...])
        route_ref[...] = _route(logits)

    first = pl.program_id(0) < n_first
    pl.when(first)(lambda: tile(ysa_ref, yga_ref, xa_ref))
    pl.when(jnp.logical_not(first))(lambda: tile(ysb_ref, ygb_ref, xb_ref))


def _merge(ys, yg, gates, xs, wo, g, b, wr_hi, wr_lo, br, tm):
    n_a, n_b = xs[0].shape[0] // tm, xs[1].shape[0] // tm
    m = (n_a + n_b) * tm

    def rows(i):
        return (i, 0)

    pair = [pl.BlockSpec((tm, D_MODEL), lambda i: (jnp.minimum(i, n_a - 1), 0)),
            pl.BlockSpec((tm, D_MODEL), lambda i: (jnp.maximum(i - n_a, 0), 0))]
    return pl.pallas_call(
        functools.partial(_merge_kernel, n_a),
        grid=(m // tm,),
        in_specs=[*pair, *pair,
                  pl.BlockSpec((tm, D_MODEL), lambda i: (i, COL_GS)),
                  pl.BlockSpec((tm, D_MODEL), lambda i: (i, COL_GG)),
                  *pair,
                  pl.BlockSpec(wo.shape, lambda i: (0, 0), pipeline_mode=pl.Buffered(1)),
                  _row_params(g), _row_params(b),
                  _row_params(wr_hi), _row_params(wr_lo), _row_params(br)],
        out_specs=[pl.BlockSpec((tm * ROW_CHUNKS, LANES), rows),
                   pl.BlockSpec((tm, ROUTER_LANES), rows)],
        out_shape=[jax.ShapeDtypeStruct((m * ROW_CHUNKS, LANES), F32),
                   jax.ShapeDtypeStruct((m, ROUTER_LANES), F32)],
        compiler_params=_cparams(1),
        name="merge_outproj_ln_router",
    )(*ys, *yg, gates, gates, *xs, wo, g, b, wr_hi, wr_lo, br)


def _gather_kernel(idx_ref, table_ref, o_ref, sem):
    n = o_ref.shape[0] // ROW_CHUNKS

    def issue(r2, carry):
        for queue in range(DMA_QUEUES):
            r = r2 * DMA_QUEUES + queue
            src = pl.multiple_of(idx_ref[0, r] * ROW_CHUNKS, ROW_CHUNKS)
            dst = pl.multiple_of(r * ROW_CHUNKS, ROW_CHUNKS)
            pltpu.make_async_copy(table_ref.at[pl.ds(src, ROW_CHUNKS)],
                                  o_ref.at[pl.ds(dst, ROW_CHUNKS)], sem).start(priority=queue)
        return carry

    lax.fori_loop(0, n // DMA_QUEUES, issue, 0, unroll=4)
    pltpu.make_async_copy(table_ref.at[pl.ds(0, n * ROW_CHUNKS)], o_ref, sem).wait()


def _gather_rows(table, idx, window):
    n = idx.shape[0]
    steps = n // window
    return pl.pallas_call(
        _gather_kernel,
        grid=(steps,),
        in_specs=[pl.BlockSpec((None, 1, window), lambda i: (i, 0, 0), memory_space=pltpu.SMEM),
                  pl.BlockSpec(memory_space=pl.ANY)],
        out_specs=pl.BlockSpec((window * ROW_CHUNKS, LANES), lambda i: (i, 0)),
        out_shape=jax.ShapeDtypeStruct((n * ROW_CHUNKS, LANES), table.dtype),
        scratch_shapes=[pltpu.SemaphoreType.DMA],
        compiler_params=_cparams(1),
        name="gather_rows",
    )(idx.reshape(steps, 1, window), table)


def _zero_runs(pads_ref, zeros, x_hbm, sem, action):
    tm = zeros.shape[0] // ROW_CHUNKS

    def piece(pos, rows):
        dst = pl.multiple_of(pos * ROW_CHUNKS, ROW_CHUNKS)
        return pltpu.make_async_copy(zeros.at[pl.ds(0, rows * ROW_CHUNKS)],
                                     x_hbm.at[pl.ds(dst, rows * ROW_CHUNKS)], sem)

    def run(e, carry):
        pos, length = pads_ref[0, e], pads_ref[1, e]

        def whole(q, c):
            getattr(piece(pos + q * tm, tm), action)()
            return c

        lax.fori_loop(0, length // tm, whole, 0)
        pos = pos + (length // tm) * tm
        rows = tm // 2
        while rows >= 1:
            @pl.when((length & rows) != 0)
            def _(pos=pos, rows=rows):
                getattr(piece(pos, rows), action)()

            pos = pos + (length & rows)
            rows //= 2
        return carry

    lax.fori_loop(0, pads_ref.shape[1], run, 0)


def _dispatch_kernel(slot_ref, pads_ref, h_ref, x_hbm, zeros, sem, zsem):
    n = h_ref.shape[0] // ROW_CHUNKS

    def issue(r, carry):
        src = h_ref.at[pl.ds(pl.multiple_of(r * ROW_CHUNKS, ROW_CHUNKS), ROW_CHUNKS)]
        for k in range(2):
            dst = pl.multiple_of(slot_ref[k, r] * ROW_CHUNKS, ROW_CHUNKS)
            pltpu.make_async_copy(src, x_hbm.at[pl.ds(dst, ROW_CHUNKS)], sem).start(
                priority=k % DMA_QUEUES)
        return carry

    lax.fori_loop(0, n, issue, 0, unroll=4)

    @pl.when(pl.program_id(0) == pl.num_programs(0) - 1)
    def _():
        zeros[...] = jnp.zeros(zeros.shape, zeros.dtype)
        _zero_runs(pads_ref, zeros, x_hbm, zsem, "start")
        _zero_runs(pads_ref, zeros, x_hbm, zsem, "wait")

    for k in range(2):
        pltpu.make_async_copy(h_ref, x_hbm.at[pl.ds(0, n * ROW_CHUNKS)], sem).wait()


def _dispatch(h1, slot, pads, n_slots, tm, window):
    t = slot.shape[1]
    steps = t // window
    slot3 = slot.reshape(2, steps, window).transpose(1, 0, 2)
    return pl.pallas_call(
        _dispatch_kernel,
        grid=(steps,),
        in_specs=[pl.BlockSpec((None, 2, window), lambda i: (i, 0, 0), memory_space=pltpu.SMEM),
                  pl.BlockSpec(pads.shape, lambda i: (0, 0), memory_space=pltpu.SMEM),
                  pl.BlockSpec((window * ROW_CHUNKS, LANES), lambda i: (i, 0))],
        out_specs=pl.BlockSpec(memory_space=pl.ANY),
        out_shape=jax.ShapeDtypeStruct((n_slots * ROW_CHUNKS, LANES), h1.dtype),
        scratch_shapes=[pltpu.VMEM((tm * ROW_CHUNKS, LANES), h1.dtype), pltpu.SemaphoreType.DMA,
                        pltpu.SemaphoreType.DMA],
        compiler_params=_cparams(1),
        name="dispatch_rows",
    )(slot3, pads, h1)


def _expert_plan(route, tm):
    t = route.shape[0]
    ids = jnp.arange(MOE_EXPERTS, dtype=jnp.int32)
    e = jnp.concatenate([route[:, ROUTE_E1], route[:, ROUTE_E2]]).astype(jnp.int32)
    onehot = (e[:, None] == ids[None, :]).astype(jnp.int32)
    csum = jnp.cumsum(onehot, axis=0)
    rank = jnp.sum(onehot * csum, axis=1) - 1
    cnt = csum[-1]
    tiles_e = (cnt + tm - 1) // tm
    tile_end = jnp.cumsum(tiles_e)
    tile_start = tile_end - tiles_e
    slot = tile_start[e] * tm + rank
    n_tiles = (2 * t) // tm + MOE_EXPERTS
    tid = jnp.arange(n_tiles, dtype=jnp.int32)
    used = tid < tile_end[-1]
    tile_e = jnp.sum(tile_end[None, :] <= jnp.minimum(tid, tile_end[-1] - 1)[:, None], axis=1)
    nvalid = jnp.where(used, jnp.clip(cnt[tile_e] - (tid - tile_start[tile_e]) * tm, 0, tm), 0)
    first = used & (tid == tile_start[tile_e])
    has = tiles_e > 0
    ordinal = jnp.cumsum(has.astype(jnp.int32)) - 1
    later = jnp.where((ids[None, :] > ids[:, None]) & has[None, :], ids[None, :], MOE_EXPERTS)
    nxt_e = jnp.min(later, axis=1)
    nxt_e = jnp.where(nxt_e == MOE_EXPERTS, -1, nxt_e)
    i32 = lambda a: a.astype(jnp.int32)
    pads = jnp.stack([jnp.append(tile_start * tm + cnt, tile_end[-1] * tm),
                      jnp.append(tiles_e * tm - cnt, (n_tiles - tile_end[-1]) * tm)])
    return (slot.reshape(2, t), i32(pads), n_tiles * tm, i32(tile_e), i32(nvalid), i32(first),
            i32(nxt_e[tile_e]), i32(ordinal[tile_e] % 2))


def _experts_kernel(te_ref, nv_ref, first_ref, nxt_ref, par_ref, x_ref, wg_hbm, wu_hbm, wd_hbm, y_ref,
                    wg_f, wu_f, wd_f, wg_b, wu_b, wd_b, x_b, sem):
    i = pl.program_id(0)
    s = par_ref[i]

    def weight_copies(e, slot):
        return (pltpu.make_async_copy(wg_hbm.at[e], wg_f.at[slot], sem.at[slot]),
                pltpu.make_async_copy(wu_hbm.at[e], wu_f.at[slot], sem.at[slot]),
                pltpu.make_async_copy(wd_hbm.at[e], wd_f.at[slot], sem.at[slot]))

    @pl.when(i == 0)
    def _():
        for cp in weight_copies(te_ref[0], par_ref[0]):
            cp.start()

    @pl.when(first_ref[i] == 1)
    def _():
        @pl.when(nxt_ref[i] >= 0)
        def _():
            for cp in weight_copies(nxt_ref[i], 1 - s):
                cp.start()

        for cp in weight_copies(te_ref[i], s):
            cp.wait()
        wg_b[...] = wg_f[s].astype(BF16)
        wu_b[...] = wu_f[s].astype(BF16)
        wd_b[...] = wd_f[s].astype(BF16)

    nv = nv_ref[i]

    @pl.when(nv > 0)
    def _():
        for c in range(ROW_CHUNKS):
            x_b[:, c * LANES:(c + 1) * LANES] = _load_chunked(x_ref, c).astype(BF16)
        x = x_b[...]
        hid = (_silu(_dot(x, wg_b[...])) * _dot(x, wu_b[...])).astype(BF16)
        _store_chunked(y_ref, _dot(hid, wd_b[...]))

    @pl.when(nv == 0)
    def _():
        y_ref[...] = jnp.zeros(y_ref.shape, y_ref.dtype)


def _experts(xs, tile_e, nvalid, first, nxt, par, wg, wu, wd, tm):
    n_tiles = xs.shape[0] // (tm * ROW_CHUNKS)
    rows = pl.BlockSpec((tm * ROW_CHUNKS, LANES), lambda i, *_: (i, 0))
    hbm = pl.BlockSpec(memory_space=pl.ANY)
    grid_spec = pltpu.PrefetchScalarGridSpec(
        num_scalar_prefetch=5,
        grid=(n_tiles,),
        in_specs=[rows, hbm, hbm, hbm],
        out_specs=rows,
        scratch_shapes=[pltpu.VMEM((2, D_MODEL, MOE_FF), F32), pltpu.VMEM((2, D_MODEL, MOE_FF), F32),
                        pltpu.VMEM((2, MOE_FF, D_MODEL), F32),
                        pltpu.VMEM((D_MODEL, MOE_FF), BF16), pltpu.VMEM((D_MODEL, MOE_FF), BF16),
                        pltpu.VMEM((MOE_FF, D_MODEL), BF16), pltpu.VMEM((tm, D_MODEL), BF16),
                        pltpu.SemaphoreType.DMA((2,))])
    return pl.pallas_call(
        _experts_kernel,
        grid_spec=grid_spec,
        out_shape=jax.ShapeDtypeStruct(xs.shape, F32),
        compiler_params=_cparams(1),
        name="experts",
    )(tile_e, nvalid, first, nxt, par, xs, wg, wu, wd)


def _combine_kernel(g1_ref, g2_ref, route_ref, h1_ref, g_ref, b_ref, y_ref):
    route = route_ref[...]
    w1 = route[:, ROUTE_W1:ROUTE_W1 + 1]
    w2 = route[:, ROUTE_W2:ROUTE_W2 + 1]
    pre = jnp.concatenate(
        [DEEPNORM_ALPHA * _load_chunked(h1_ref, c)
         + w1 * _load_chunked(g1_ref, c) + w2 * _load_chunked(g2_ref, c)
         for c in range(ROW_CHUNKS)], axis=1)
    y_ref[...] = _layernorm(pre, g_ref[...], b_ref[...])


def _combine(gathered, route, h1, row0, n_tok, m, g, b, tm):
    blk0 = row0 // tm
    blk1 = (n_tok + row0) // tm
    chunked = lambda blk: pl.BlockSpec((tm * ROW_CHUNKS, LANES), lambda i: (blk + i, 0))
    return pl.pallas_call(
        _combine_kernel,
        grid=(m // tm,),
        in_specs=[chunked(blk0), chunked(blk1),
                  pl.BlockSpec((tm, ROUTER_LANES), lambda i: (blk0 + i, 0)),
                  chunked(blk0),
                  _row_params(g), _row_params(b)],
        out_specs=pl.BlockSpec((tm, D_MODEL), lambda i: (i, 0)),
        out_shape=jax.ShapeDtypeStruct((m, D_MODEL), F32),
        compiler_params=_cparams(1),
        name="combine_ln",
    )(gathered, gathered, route, h1, g, b)


def _pad_lanes(v, width=LANES):
    v = v.reshape(1, -1)
    return jnp.pad(v, ((0, 0), (0, width - v.shape[1])))


def kernel(x_prompt, x_sample, state_ssd, state_conv, state_gla, w_in, conv_w, conv_b, dt_bias, a_log, d_skip, ssd_norm_w, gla_w_a2, gla_b_a, gla_norm_w, w_out, ln1_g, ln1_b, w_router_group, b_router_group, w_router_expert, b_router_expert, w_gate, w_up, w_down, ln2_g, ln2_b):
    assert w_in.shape[0] == DEPTH == 1
    n_p, len_p, _ = x_prompt.shape
    n_s, len_s, _ = x_sample.shape
    rows_p, rows_s = n_p * len_p, n_s * len_s
    xp = x_prompt.reshape(rows_p, D_MODEL)
    xs = x_sample.reshape(rows_s, D_MODEL)

    offs = [0]
    for s in IN_SPLIT_SIZES:
        offs.append(offs[-1] + s)
    w_t = w_in[0].T
    w_tail_t = jnp.concatenate(
        [jnp.pad(w_t[offs[2]:offs[3]], ((0, LANES - SSD_HEADS), (0, 0))),
         jnp.pad(w_t[offs[7]:offs[8]], ((0, LANES - GLA_RANK), (0, 0)))], axis=0)
    x_b = (xp.astype(BF16), xs.astype(BF16))
    total = rows_p + rows_s

    proj = _in_proj(x_b, w_t, ((0, offs[2]), (offs[3], offs[7] - offs[3]),
                               (offs[8], offs[10] - offs[8])))
    tail = _matmul_nt(x_b, w_tail_t, IN_PROJ_ROWS)

    cw, cb = conv_w[0], conv_b[0].reshape(1, -1)
    dtb, alog = _pad_lanes(dt_bias[0]), _pad_lanes(a_log[0])
    dskip_e = jnp.repeat(d_skip[0], SSD_HEAD_DIM).reshape(1, -1)
    nw_ssd = ssd_norm_w[0].reshape(1, -1)
    wa, ba, nw_gla = gla_w_a2[0].astype(BF16), gla_b_a[0].reshape(1, -1), gla_norm_w[0].reshape(1, -1)

    ys_p, ssd_p, conv_p = _ssd_prompt(proj, tail, n_p, len_p, cw, cb, dtb, alog, dskip_e, nw_ssd)
    ys_s, ssd_s, conv_s = _ssd_sample(proj, tail, rows_p, n_s, len_s, state_ssd[0], state_conv[0],
                                      cw, cb, dtb, alog, dskip_e, nw_ssd)
    yg_p, gla_p = _gla_prompt(proj, tail, n_p, len_p, wa, ba, nw_gla)
    yg_s, gla_s = _gla_sample(proj, tail, rows_p, n_s, len_s, state_gla[0], wa, ba, nw_gla)

    wo = w_out[0].astype(BF16)
    w_r = jnp.pad(jnp.concatenate([w_router_group[0], w_router_expert[0]], axis=1),
                  ((0, 0), (0, ROUTER_LANES - MOE_GROUPS - MOE_EXPERTS)))
    wr_hi = w_r.astype(BF16)
    wr_lo = (w_r - wr_hi.astype(F32)).astype(BF16)
    b_r = _pad_lanes(jnp.concatenate([b_router_group[0], b_router_expert[0]]), ROUTER_LANES)
    g1, b1 = ln1_g[0].reshape(1, -1), ln1_b[0].reshape(1, -1)
    g2, b2 = ln2_g[0].reshape(1, -1), ln2_b[0].reshape(1, -1)

    h1, route = _merge((ys_p, ys_s), (yg_p, yg_s), proj, (xp, xs), wo, g1, b1, wr_hi, wr_lo, b_r,
                       MERGE_ROWS)
    slot, pads, n_slots, tile_e, nvalid, first, nxt, par = _expert_plan(route, EXPERT_ROWS)
    x_sorted = _dispatch(h1, slot, pads, n_slots, EXPERT_ROWS, GATHER_WINDOW)
    y_sorted = _experts(x_sorted, tile_e, nvalid, first, nxt, par, w_gate[0], w_up[0], w_down[0],
                        EXPERT_ROWS)
    y_tok = _gather_rows(y_sorted, slot.reshape(-1), GATHER_WINDOW)
    outs = [_combine(y_tok, route, h1, 0, total, rows_p, g2, b2, MERGE_ROWS),
            _combine(y_tok, route, h1, rows_p, total, rows_s, g2, b2, MERGE_ROWS)]
    y_p = outs[0].reshape(x_prompt.shape)
    y_s = outs[1].reshape(x_sample.shape)
    return (y_p, y_s, ssd_p[None], conv_p[None], gla_p[None], ssd_s[None], conv_s[None], gla_s[None])
```

```python
import functools
import math

import jax
import jax.numpy as jnp
from jax import lax
from jax.experimental import pallas as pl
from jax.experimental.pallas import tpu as pltpu

F32 = jnp.float32
BF16 = jnp.bfloat16
HIGHEST = lax.Precision.HIGHEST

D_MODEL = 2048
SSD_HEADS = 32
SSD_HEAD_DIM = 64
SSD_GROUPS = 8
SSD_STATE = 128
SSD_CONV = 4
SSD_CHUNK = 128
SSD_BC = SSD_GROUPS * SSD_STATE
SSD_CONV_DIM = D_MODEL + 2 * SSD_BC
SSD_GROUP_COLS = D_MODEL // SSD_GROUPS
HEADS_PER_GROUP = SSD_HEADS // SSD_GROUPS
GLA_HEADS = 8
GLA_DK = 128
GLA_DV = 256
GLA_KEY_DIM = GLA_HEADS * GLA_DK
GLA_RANK = 16
GLA_TAU = 16.0
GLA_CHUNK = 64
MOE_GROUPS = 4
MOE_EPG = 8
MOE_EXPERTS = 32
MOE_FF = 512
DEPTH = 1
DEEPNORM_ALPHA = (2.0 * DEPTH) ** 0.25
EPS = 1e-5
IN_SPLIT_SIZES = (D_MODEL, SSD_CONV_DIM, SSD_HEADS, GLA_KEY_DIM, GLA_KEY_DIM, D_MODEL, D_MODEL,
                  GLA_RANK, D_MODEL, D_MODEL)

LANES = 128
HALO_ROWS = 8
COL_Z, COL_XS, COL_BC, COL_QK, COL_V, COL_R, COL_GS, COL_GG = range(8)
TAIL_DT, TAIL_ALO = 0, 1
ROUTER_LANES = 128
ROUTE_E1, ROUTE_E2, ROUTE_W1, ROUTE_W2 = 0, 1, 2, 3
MERGE_ROWS = 256
EXPERT_ROWS = 256
DISPATCH_WINDOW = 512
DMA_QUEUES = 2
VMEM_LIMIT = 56 * 1024 * 1024


def _cparams(n_axes):
    return pltpu.CompilerParams(dimension_semantics=("arbitrary",) * n_axes,
                                vmem_limit_bytes=VMEM_LIMIT)


def _silu(x):
    return x * jax.nn.sigmoid(x)


def _softplus(x):
    return jnp.maximum(x, 0.0) + jnp.log1p(jnp.exp(-jnp.abs(x)))


def _tril(n):
    r = lax.broadcasted_iota(jnp.int32, (n, n), 0)
    c = lax.broadcasted_iota(jnp.int32, (n, n), 1)
    return r >= c


def _expand_group(p, g, lane_head):
    h0 = g * HEADS_PER_GROUP
    out = p[:, h0 + HEADS_PER_GROUP - 1:h0 + HEADS_PER_GROUP]
    for hh in range(HEADS_PER_GROUP - 2, -1, -1):
        out = jnp.where(lane_head == hh, p[:, h0 + hh:h0 + hh + 1], out)
    return out


def _split3(x):
    hi = x.astype(BF16)
    r = x - hi.astype(F32)
    mid = r.astype(BF16)
    lo = (r - mid.astype(F32)).astype(BF16)
    return hi, mid, lo


def _dot_sel(sel, x):
    hi, mid, lo = _split3(x)
    return _dot(sel, lo) + _dot(sel, mid) + _dot(sel, hi)


def _col_bcast(row):
    return jnp.broadcast_to(row, (LANES, LANES)).T


def _dot(a, b, **kw):
    return jnp.dot(a, b, preferred_element_type=F32, **kw)


def _dot_nt(a, b, **kw):
    return lax.dot_general(a, b, (((1,), (1,)), ((), ())), preferred_element_type=F32, **kw)


def _dot_tn(a, b, **kw):
    return lax.dot_general(a, b, (((0,), (0,)), ((), ())), preferred_element_type=F32, **kw)


def _matmul_kernel(n_first, xa_ref, xb_ref, wt_ref, o_ref):
    x = jnp.where(pl.program_id(0) < n_first, xa_ref[...], xb_ref[...])
    o_ref[...] = _dot_nt(x, wt_ref[...].astype(BF16))


def _matmul_nt(xs, wt, tm):
    n, k = wt.shape
    n_a, n_b = xs[0].shape[0] // tm, xs[1].shape[0] // tm
    return pl.pallas_call(
        functools.partial(_matmul_kernel, n_a),
        grid=(n_a + n_b,),
        in_specs=[pl.BlockSpec((tm, k), lambda i: (jnp.minimum(i, n_a - 1), 0)),
                  pl.BlockSpec((tm, k), lambda i: (jnp.maximum(i - n_a, 0), 0)),
                  pl.BlockSpec((n, k), lambda i: (0, 0))],
        out_specs=pl.BlockSpec((tm, n), lambda i: (i, 0)),
        out_shape=jax.ShapeDtypeStruct(((n_a + n_b) * tm, n), F32),
        compiler_params=_cparams(1),
        name="in_proj_tail",
    )(*xs, wt)


IN_PROJ_COLS = 1024
IN_PROJ_ROWS = 1024
CAST_ROWS = 256
SUBLANES = 8


def _in_proj_kernel(plan, n_first, xa_ref, xb_ref, wt_hbm, o_ref, wf, wb, sem):
    j = pl.program_id(0)
    i = pl.program_id(1)
    tn = wb.shape[0]

    def block_copy(jj, slot):
        shift = 0
        for lo, hi, sh in plan:
            shift = jnp.where((jj >= lo) & (jj < hi), sh, shift)
        src = pl.multiple_of(jj * tn + shift, SUBLANES)
        return pltpu.make_async_copy(wt_hbm.at[pl.ds(src, tn)], wf.at[slot], sem.at[slot])

    @pl.when(i == 0)
    def _():
        slot = j % 2

        @pl.when(j == 0)
        def _():
            block_copy(j, slot).start()

        @pl.when(j + 1 < pl.num_programs(0))
        def _():
            block_copy(j + 1, 1 - slot).start()

        block_copy(j, slot).wait()
        for r in range(0, tn, CAST_ROWS):
            wb[r:r + CAST_ROWS, :] = wf[slot, r:r + CAST_ROWS, :].astype(BF16)

    x = jnp.where(i < n_first, xa_ref[...], xb_ref[...])
    o_ref[...] = _dot_nt(x, wb[...])


def _in_proj(xs, wt, segments):
    tm, tn = IN_PROJ_ROWS, IN_PROJ_COLS
    k = wt.shape[1]
    n_a, n_b = xs[0].shape[0] // tm, xs[1].shape[0] // tm
    plan, dest = [], 0
    for src, count in segments:
        assert count % tn == 0 and dest % tn == 0 and (src - dest) % SUBLANES == 0
        plan.append((dest // tn, (dest + count) // tn, src - dest))
        dest += count
    return pl.pallas_call(
        functools.partial(_in_proj_kernel, tuple(plan), n_a),
        grid=(dest // tn, n_a + n_b),
        in_specs=[pl.BlockSpec((tm, k), lambda j, i: (jnp.minimum(i, n_a - 1), 0)),
                  pl.BlockSpec((tm, k), lambda j, i: (jnp.maximum(i - n_a, 0), 0)),
                  pl.BlockSpec(memory_space=pl.ANY)],
        out_specs=pl.BlockSpec((tm, tn), lambda j, i: (i, j)),
        out_shape=jax.ShapeDtypeStruct(((n_a + n_b) * tm, dest), F32),
        scratch_shapes=[pltpu.VMEM((2, tn, k), F32), pltpu.VMEM((tn, k), BF16),
                        pltpu.SemaphoreType.DMA((2,))],
        compiler_params=_cparams(2),
        name="in_proj_main",
    )(*xs, wt)


def _conv_silu(ubuf, cw_ref, cb_ref, lo, width, rows):
    acc = cb_ref[:, lo:lo + width]
    for i in range(SSD_CONV):
        r0 = HALO_ROWS - (SSD_CONV - 1) + i
        acc = acc + cw_ref[i:i + 1, lo:lo + width] * ubuf[r0:r0 + rows, lo:lo + width]
    return _silu(acc)


def _gated_group_norm(y_g, z_g, nw_g):
    yg = y_g * _silu(z_g)
    ms = jnp.mean(yg * yg, axis=-1, keepdims=True)
    return yg * lax.rsqrt(ms + EPS) * nw_g


def _ssd_prompt_kernel(z_ref, xs_ref, bc_ref, tail_ref, cw_ref, cb_ref, dtb_ref, alog_ref,
                       dskip_ref, nw_ref, y_ref, hout_ref, convout_ref, ubuf, ht):
    q = SSD_CHUNK
    c = pl.program_id(1)

    @pl.when(c == 0)
    def _():
        ubuf[0:HALO_ROWS, :] = jnp.zeros((HALO_ROWS, SSD_CONV_DIM), F32)
        ht[...] = jnp.zeros(ht.shape, F32)

    ubuf[HALO_ROWS:HALO_ROWS + q, 0:D_MODEL] = xs_ref[...]
    ubuf[HALO_ROWS:HALO_ROWS + q, D_MODEL:SSD_CONV_DIM] = bc_ref[...]

    dt = _softplus(tail_ref[...] + dtb_ref[...])
    da = dt * (-jnp.exp(alog_ref[...]))
    causal = _tril(q)
    acum = _dot_sel(causal.astype(BF16), da)
    acum_t = acum.T
    dt_t = dt.T
    a_last = acum[q - 1:q, :]
    decay_in = jnp.exp(acum)
    w_end = dt * jnp.exp(a_last - acum)
    chunk_decay = jnp.exp(a_last)
    lane_head = lax.broadcasted_iota(jnp.int32, (q, SSD_GROUP_COLS), 1) >> 6
    assert SSD_HEAD_DIM == 1 << 6

    for g in range(SSD_GROUPS):
        lo = g * SSD_GROUP_COLS
        cols = slice(lo, lo + SSD_GROUP_COLS)
        xs_g = _conv_silu(ubuf, cw_ref, cb_ref, lo, SSD_GROUP_COLS, q)
        b_g = _conv_silu(ubuf, cw_ref, cb_ref, D_MODEL + g * SSD_STATE, SSD_STATE, q).astype(BF16)
        c_g = _conv_silu(ubuf, cw_ref, cb_ref, D_MODEL + SSD_BC + g * SSD_STATE, SSD_STATE,
                         q).astype(BF16)
        scores = _dot_nt(c_g, b_g)
        h_g = ht[g]
        xs_b = xs_g.astype(BF16)
        m_heads, x_heads = [], []
        for hh in range(HEADS_PER_GROUP):
            h = g * HEADS_PER_GROUP + hh
            seg = acum[:, h:h + 1] - acum_t[h:h + 1, :]
            decay = jnp.exp(jnp.where(causal, seg, -jnp.inf))
            m_heads.append((scores * decay * dt_t[h:h + 1, :]).astype(BF16))
            x_heads.append(jnp.where(lane_head == hh, xs_b, jnp.zeros_like(xs_b)))
        y_g = (_dot(jnp.concatenate(m_heads, axis=1), jnp.concatenate(x_heads, axis=0))
               + _dot(c_g, h_g.astype(BF16)) * _expand_group(decay_in, g, lane_head)
               + dskip_ref[:, cols] * xs_g)
        y_ref[:, cols] = _gated_group_norm(y_g, z_ref[:, cols], nw_ref[:, cols])
        xw = (xs_g * _expand_group(w_end, g, lane_head)).astype(BF16)
        ht[g] = h_g * _expand_group(chunk_decay, g, lane_head[0:1]) + _dot_tn(b_g, xw)

    last = HALO_ROWS + q - (SSD_CONV - 1)
    tail_rows = ubuf[last:last + SSD_CONV - 1, :]
    ubuf[HALO_ROWS - (SSD_CONV - 1):HALO_ROWS, :] = tail_rows

    @pl.when(c == pl.num_programs(1) - 1)
    def _():
        convout_ref[0] = tail_rows
        for g in range(SSD_GROUPS):
            h_t = ht[g].T
            for hh in range(HEADS_PER_GROUP):
                hout_ref[0, g * HEADS_PER_GROUP + hh] = h_t[hh * SSD_HEAD_DIM:(hh + 1) * SSD_HEAD_DIM]


def _row_params(p):
    return pl.BlockSpec(p.shape, lambda *_: (0,) * p.ndim)


def _ssd_prompt(proj, tail, n_seq, seq_len, cw, cb, dtb, alog, dskip_e, nw):
    q = SSD_CHUNK
    nc = seq_len // q

    def col(blk):
        return pl.BlockSpec((q, D_MODEL), lambda b, c: (b * nc + c, blk))

    return pl.pallas_call(
        _ssd_prompt_kernel,
        grid=(n_seq, nc),
        in_specs=[col(COL_Z), col(COL_XS), col(COL_BC),
                  pl.BlockSpec((q, LANES), lambda b, c: (b * nc + c, TAIL_DT)),
                  _row_params(cw), _row_params(cb), _row_params(dtb), _row_params(alog),
                  _row_params(dskip_e), _row_params(nw)],
        out_specs=[pl.BlockSpec((q, D_MODEL), lambda b, c: (b * nc + c, 0)),
                   pl.BlockSpec((1, SSD_HEADS, SSD_HEAD_DIM, SSD_STATE), lambda b, c: (b, 0, 0, 0)),
                   pl.BlockSpec((1, SSD_CONV - 1, SSD_CONV_DIM), lambda b, c: (b, 0, 0))],
        out_shape=[jax.ShapeDtypeStruct((n_seq * seq_len, D_MODEL), F32),
                   jax.ShapeDtypeStruct((n_seq, SSD_HEADS, SSD_HEAD_DIM, SSD_STATE), F32),
                   jax.ShapeDtypeStruct((n_seq, SSD_CONV - 1, SSD_CONV_DIM), F32)],
        scratch_shapes=[pltpu.VMEM((HALO_ROWS + q, SSD_CONV_DIM), F32),
                        pltpu.VMEM((SSD_GROUPS, SSD_STATE, SSD_GROUP_COLS), F32)],
        compiler_params=_cparams(2),
        name="ssd_prompt",
    )(proj, proj, proj, tail, cw, cb, dtb, alog, dskip_e, nw)


SAMPLE_SEQS = 4


def _dot_tn_split(a, b):
    a_hi = a.astype(BF16)
    a_lo = (a - a_hi.astype(F32)).astype(BF16)
    b_hi = b.astype(BF16)
    b_lo = (b - b_hi.astype(F32)).astype(BF16)
    return _dot_tn(a_lo, b_hi) + _dot_tn(a_hi, b_lo) + _dot_tn(a_hi, b_hi)


def _ssd_sample_seq(z_ref, xs_ref, bc_ref, tail_ref, h_ref, conv_ref, cw_ref, cb_ref, dtb_ref,
                    alog_ref, dskip_ref, nw_ref, y_ref, hout_ref, convout_ref, ubuf, decay_rows):
    q = xs_ref.shape[0]
    ubuf[0:HALO_ROWS - (SSD_CONV - 1), :] = jnp.zeros((HALO_ROWS - (SSD_CONV - 1), SSD_CONV_DIM), F32)
    ubuf[HALO_ROWS - (SSD_CONV - 1):HALO_ROWS, :] = conv_ref[...]
    ubuf[HALO_ROWS:HALO_ROWS + q, 0:D_MODEL] = xs_ref[...]
    ubuf[HALO_ROWS:HALO_ROWS + q, D_MODEL:SSD_CONV_DIM] = bc_ref[...]
    convout_ref[...] = ubuf[HALO_ROWS + q - (SSD_CONV - 1):HALO_ROWS + q, :]

    dt = _softplus(tail_ref[...] + dtb_ref[...])
    da = dt * (-jnp.exp(alog_ref[...]))
    acum = _dot_sel(_tril(q).astype(BF16), da)
    a_last = acum[q - 1:q, :]
    decay_in = jnp.exp(acum)
    w_end = dt * jnp.exp(a_last - acum)
    decay_rows[...] = _col_bcast(jnp.exp(a_last))[:SSD_HEADS]
    row = lax.broadcasted_iota(jnp.int32, (q, SSD_GROUP_COLS), 0)
    lane_head = lax.broadcasted_iota(jnp.int32, (q, SSD_GROUP_COLS), 1) >> 6

    for g in range(SSD_GROUPS):
        lo = g * SSD_GROUP_COLS
        cols = slice(lo, lo + SSD_GROUP_COLS)
        xs_g = _conv_silu(ubuf, cw_ref, cb_ref, lo, SSD_GROUP_COLS, q)
        b_g = _conv_silu(ubuf, cw_ref, cb_ref, D_MODEL + g * SSD_STATE, SSD_STATE, q)
        c_g = _conv_silu(ubuf, cw_ref, cb_ref, D_MODEL + SSD_BC + g * SSD_STATE, SSD_STATE, q)
        h_g = h_ref[g * HEADS_PER_GROUP:(g + 1) * HEADS_PER_GROUP].reshape(
            SSD_GROUP_COLS, SSD_STATE)
        y_g = (_dot_nt(c_g.astype(BF16), h_g.astype(BF16)) * _expand_group(decay_in, g, lane_head)
               + dskip_ref[:, cols] * xs_g)
        acum_g = _expand_group(acum, g, lane_head)
        dt_g = _expand_group(dt, g, lane_head)
        for j in range(q):
            s_j = jnp.sum(c_g * b_g[j:j + 1, :], axis=-1, keepdims=True)
            coef = s_j * jnp.exp(acum_g - acum_g[j:j + 1, :]) * dt_g[j:j + 1, :]
            y_g = y_g + jnp.where(row >= j, coef, 0.0) * xs_g[j:j + 1, :]
        y_ref[:, cols] = _gated_group_norm(y_g, z_ref[:, cols], nw_ref[:, cols])
        upd = _dot_tn_split(xs_g * _expand_group(w_end, g, lane_head), b_g)
        for hh in range(HEADS_PER_GROUP):
            h = g * HEADS_PER_GROUP + hh
            rows = slice(hh * SSD_HEAD_DIM, (hh + 1) * SSD_HEAD_DIM)
            hout_ref[h] = h_g[rows] * decay_rows[h:h + 1, :] + upd[rows]


def _ssd_sample_kernel(z_ref, xs_ref, bc_ref, tail_ref, h_ref, conv_ref, cw_ref, cb_ref, dtb_ref,
                       alog_ref, dskip_ref, nw_ref, y_ref, hout_ref, convout_ref, ubuf, decay_rows):
    n = h_ref.shape[0]
    q = xs_ref.shape[0] // n
    for s in range(n):
        rows = pl.ds(s * q, q)
        _ssd_sample_seq(z_ref.at[rows], xs_ref.at[rows], bc_ref.at[rows], tail_ref.at[rows],
                        h_ref.at[s], conv_ref.at[s], cw_ref, cb_ref, dtb_ref, alog_ref, dskip_ref,
                        nw_ref, y_ref.at[rows], hout_ref.at[s], convout_ref.at[s], ubuf.at[s],
                        decay_rows.at[s])


def _ssd_sample(proj, tail, row0, n_seq, seq_len, h0, conv0, cw, cb, dtb, alog, dskip_e, nw):
    nb = SAMPLE_SEQS
    q = seq_len
    blk0 = row0 // (nb * q)

    def col(blk):
        return pl.BlockSpec((nb * q, D_MODEL), lambda b: (blk0 + b, blk))

    state_spec = pl.BlockSpec((nb, SSD_HEADS, SSD_HEAD_DIM, SSD_STATE), lambda b: (b, 0, 0, 0))
    conv_spec = pl.BlockSpec((nb, SSD_CONV - 1, SSD_CONV_DIM), lambda b: (b, 0, 0))
    return pl.pallas_call(
        _ssd_sample_kernel,
        grid=(n_seq // nb,),
        in_specs=[col(COL_Z), col(COL_XS), col(COL_BC),
                  pl.BlockSpec((nb * q, LANES), lambda b: (blk0 + b, TAIL_DT)),
                  state_spec, conv_spec,
                  _row_params(cw), _row_params(cb), _row_params(dtb), _row_params(alog),
                  _row_params(dskip_e), _row_params(nw)],
        out_specs=[pl.BlockSpec((nb * q, D_MODEL), lambda b: (b, 0)), state_spec, conv_spec],
        out_shape=[jax.ShapeDtypeStruct((n_seq * seq_len, D_MODEL), F32),
                   jax.ShapeDtypeStruct(h0.shape, F32),
                   jax.ShapeDtypeStruct(conv0.shape, F32)],
        scratch_shapes=[pltpu.VMEM((nb, HALO_ROWS + q, SSD_CONV_DIM), F32),
                        pltpu.VMEM((nb, SSD_HEADS, SSD_STATE), F32)],
        compiler_params=_cparams(1),
        name="ssd_sample",
    )(proj, proj, proj, tail, h0, conv0, cw, cb, dtb, alog, dskip_e, nw)


def _gla_chunk(qk_ref, v_ref, r_ref, alo_ref, wa_ref, ba_ref, nw_ref, y_ref, get_state, put_state):
    q = v_ref.shape[0]
    a_lo = alo_ref[...][:, :GLA_RANK].astype(BF16)
    gk = -_softplus(-(_dot(a_lo, wa_ref[...]) + ba_ref[...])) / GLA_TAU
    causal = _tril(q)
    bcum = _dot_sel(causal.astype(BF16), gk)
    for h in range(GLA_HEADS):
        kc = slice(h * GLA_DK, (h + 1) * GLA_DK)
        vc = slice(h * GLA_DV, (h + 1) * GLA_DV)
        b_h = bcum[:, kc]
        b_last = b_h[q - 1:q, :]
        q_h = qk_ref[:, kc] * (GLA_DK ** -0.5)
        k_h = qk_ref[:, GLA_KEY_DIM + h * GLA_DK:GLA_KEY_DIM + (h + 1) * GLA_DK]
        v_h = v_ref[:, vc].astype(BF16)
        q_in = (q_h * jnp.exp(b_h)).astype(BF16)
        k_in = (k_h * jnp.exp(-b_h)).astype(BF16)
        att = jnp.where(causal, _dot_nt(q_in, k_in), 0.0).astype(BF16)
        s_h = get_state(h)
        o = _dot(att, v_h) + _dot(q_in, s_h.astype(BF16))
        k_end = (k_h * jnp.exp(b_last - b_h)).astype(BF16)
        d_col = _col_bcast(jnp.exp(b_last))
        put_state(h, s_h * jnp.concatenate([d_col, d_col], axis=1) + _dot_tn(k_end, v_h))
        ms = jnp.mean(o * o, axis=-1, keepdims=True)
        y_ref[:, vc] = o * lax.rsqrt(ms + EPS) * nw_ref[...] * _silu(r_ref[:, vc])


def _gla_prompt_kernel(qk_ref, v_ref, r_ref, alo_ref, wa_ref, ba_ref, nw_ref, y_ref, sout_ref, st):
    c = pl.program_id(1)

    @pl.when(c == 0)
    def _():
        st[...] = jnp.zeros(st.shape, F32)

    def put(h, val):
        st[h] = val

    _gla_chunk(qk_ref, v_ref, r_ref, alo_ref, wa_ref, ba_ref, nw_ref, y_ref, lambda h: st[h], put)

    @pl.when(c == pl.num_programs(1) - 1)
    def _():
        sout_ref[0] = st[...]


def _gla_prompt(proj, tail, n_seq, seq_len, wa, ba, nw):
    q = GLA_CHUNK
    nc = seq_len // q

    def col(blk):
        return pl.BlockSpec((q, D_MODEL), lambda b, c: (b * nc + c, blk))

    return pl.pallas_call(
        _gla_prompt_kernel,
        grid=(n_seq, nc),
        in_specs=[col(COL_QK), col(COL_V), col(COL_R),
                  pl.BlockSpec((q, LANES), lambda b, c: (b * nc + c, TAIL_ALO)),
                  _row_params(wa), _row_params(ba), _row_params(nw)],
        out_specs=[pl.BlockSpec((q, D_MODEL), lambda b, c: (b * nc + c, 0)),
                   pl.BlockSpec((1, GLA_HEADS, GLA_DK, GLA_DV), lambda b, c: (b, 0, 0, 0))],
        out_shape=[jax.ShapeDtypeStruct((n_seq * seq_len, D_MODEL), F32),
                   jax.ShapeDtypeStruct((n_seq, GLA_HEADS, GLA_DK, GLA_DV), F32)],
        scratch_shapes=[pltpu.VMEM((GLA_HEADS, GLA_DK, GLA_DV), F32)],
        compiler_params=_cparams(2),
        name="gla_prompt",
    )(proj, proj, proj, tail, wa, ba, nw)


def _gla_sample_kernel(qk_ref, v_ref, r_ref, alo_ref, s_ref, wa_ref, ba_ref, nw_ref, y_ref, sout_ref):
    n = s_ref.shape[0]
    q = v_ref.shape[0] // n
    for s in range(n):
        rows = pl.ds(s * q, q)

        def put(h, val, s=s):
            sout_ref[s, h] = val

        _gla_chunk(qk_ref.at[rows], v_ref.at[rows], r_ref.at[rows], alo_ref.at[rows], wa_ref, ba_ref,
                   nw_ref, y_ref.at[rows], lambda h, s=s: s_ref[s, h], put)


def _gla_sample(proj, tail, row0, n_seq, seq_len, s0, wa, ba, nw):
    nb = SAMPLE_SEQS
    q = seq_len
    blk0 = row0 // (nb * q)

    def col(blk):
        return pl.BlockSpec((nb * q, D_MODEL), lambda b: (blk0 + b, blk))

    state_spec = pl.BlockSpec((nb, GLA_HEADS, GLA_DK, GLA_DV), lambda b: (b, 0, 0, 0))
    return pl.pallas_call(
        _gla_sample_kernel,
        grid=(n_seq // nb,),
        in_specs=[col(COL_QK), col(COL_V), col(COL_R),
                  pl.BlockSpec((nb * q, LANES), lambda b: (blk0 + b, TAIL_ALO)),
                  state_spec, _row_params(wa), _row_params(ba), _row_params(nw)],
        out_specs=[pl.BlockSpec((nb * q, D_MODEL), lambda b: (b, 0)), state_spec],
        out_shape=[jax.ShapeDtypeStruct((n_seq * seq_len, D_MODEL), F32),
                   jax.ShapeDtypeStruct(s0.shape, F32)],
        compiler_params=_cparams(1),
        name="gla_sample",
    )(proj, proj, proj, tail, s0, wa, ba, nw)


def _layernorm(x, g, b):
    mu = jnp.mean(x, axis=-1, keepdims=True)
    xc = x - mu
    var = jnp.mean(xc * xc, axis=-1, keepdims=True)
    return xc * lax.rsqrt(var + EPS) * g + b


def _route(logits):
    lane = lax.broadcasted_iota(jnp.int32, logits.shape, 1)
    neg = -jnp.inf
    big = ROUTER_LANES
    glog = jnp.where(lane < MOE_GROUPS, logits, neg)
    gmax = jnp.max(glog, axis=-1, keepdims=True)
    g_sel = jnp.min(jnp.where(glog == gmax, lane, big), axis=-1, keepdims=True)
    p_g = 1.0 / jnp.sum(jnp.exp(glog - gmax), axis=-1, keepdims=True)
    e_lane = lane - MOE_GROUPS
    in_group = (e_lane >= 0) & (e_lane < MOE_EXPERTS) & (e_lane // MOE_EPG == g_sel)
    el = jnp.where(in_group, logits, neg)
    v1 = jnp.max(el, axis=-1, keepdims=True)
    i1 = jnp.min(jnp.where(el == v1, lane, big), axis=-1, keepdims=True)
    el2 = jnp.where(lane == i1, neg, el)
    v2 = jnp.max(el2, axis=-1, keepdims=True)
    i2 = jnp.min(jnp.where(el2 == v2, lane, big), axis=-1, keepdims=True)
    e2 = jnp.exp(v2 - v1)
    w1 = p_g / (1.0 + e2)
    w2 = p_g * e2 / (1.0 + e2)
    first = (i1 - MOE_GROUPS).astype(F32)
    second = (i2 - MOE_GROUPS).astype(F32)
    return jnp.where(lane == ROUTE_E1, first,
                     jnp.where(lane == ROUTE_E2, second,
                               jnp.where(lane == ROUTE_W1, w1,
                                         jnp.where(lane == ROUTE_W2, w2, 0.0))))


ROW_CHUNKS = D_MODEL // LANES


def _store_chunked(ref, val):
    rows = val.shape[0]
    for c in range(ROW_CHUNKS):
        ref[pl.ds(c, rows, stride=ROW_CHUNKS), :] = val[:, c * LANES:(c + 1) * LANES]


def _load_chunked(ref, c):
    return ref[pl.ds(c, ref.shape[0] // ROW_CHUNKS, stride=ROW_CHUNKS), :]


def _merge_kernel(n_first, ysa_ref, ysb_ref, yga_ref, ygb_ref, gs_ref, gg_ref, xa_ref, xb_ref, wo_ref,
                  g_ref, b_ref, wrh_ref, wrl_ref, br_ref, h1_ref, route_ref):
    def tile(ys_ref, yg_ref, x_ref):
        merged = (jax.nn.sigmoid(gs_ref[...]) * ys_ref[...]
                  + jax.nn.sigmoid(gg_ref[...]) * yg_ref[...])
        mix = _dot(merged.astype(BF16), wo_ref[...])
        h1 = _layernorm(DEEPNORM_ALPHA * x_ref[...] + mix, g_ref[...], b_ref[...])
        _store_chunked(h1_ref, h1)
        h_hi = h1.astype(BF16)
        h_lo = (h1 - h_hi.astype(F32)).astype(BF16)
        logits = (_dot(h_hi, wrh_ref[...]) + _dot(h_lo, wrh_ref[...]) + _dot(h_hi, wrl_ref[...])
                  + br_ref[...])
        route_ref[...] = _route(logits)

    first = pl.program_id(0) < n_first
    pl.when(first)(lambda: tile(ysa_ref, yga_ref, xa_ref))
    pl.when(jnp.logical_not(first))(lambda: tile(ysb_ref, ygb_ref, xb_ref))


def _merge(ys, yg, gates, xs, wo, g, b, wr_hi, wr_lo, br, tm):
    n_a, n_b = xs[0].shape[0] // tm, xs[1].shape[0] // tm
    m = (n_a + n_b) * tm

    def rows(i):
        return (i, 0)

    pair = [pl.BlockSpec((tm, D_MODEL), lambda i: (jnp.minimum(i, n_a - 1), 0)),
            pl.BlockSpec((tm, D_MODEL), lambda i: (jnp.maximum(i - n_a, 0), 0))]
    return pl.pallas_call(
        functools.partial(_merge_kernel, n_a),
        grid=(m // tm,),
        in_specs=[*pair, *pair,
                  pl.BlockSpec((tm, D_MODEL), lambda i: (i, COL_GS)),
                  pl.BlockSpec((tm, D_MODEL), lambda i: (i, COL_GG)),
                  *pair,
                  pl.BlockSpec(wo.shape, lambda i: (0, 0), pipeline_mode=pl.Buffered(1)),
                  _row_params(g), _row_params(b),
                  _row_params(wr_hi), _row_params(wr_lo), _row_params(br)],
        out_specs=[pl.BlockSpec((tm * ROW_CHUNKS, LANES), rows),
                   pl.BlockSpec((tm, ROUTER_LANES), rows)],
        out_shape=[jax.ShapeDtypeStruct((m * ROW_CHUNKS, LANES), F32),
                   jax.ShapeDtypeStruct((m, ROUTER_LANES), F32)],
        compiler_params=_cparams(1),
        name="merge_outproj_ln_router",
    )(*ys, *yg, gates, gates, *xs, wo, g, b, wr_hi, wr_lo, br)


def _zero_runs(pads_ref, zeros, x_hbm, sem, action):
    tm = zeros.shape[0] // ROW_CHUNKS

    def piece(pos, rows):
        dst = pl.multiple_of(pos * ROW_CHUNKS, ROW_CHUNKS)
        return pltpu.make_async_copy(zeros.at[pl.ds(0, rows * ROW_CHUNKS)],
                                     x_hbm.at[pl.ds(dst, rows * ROW_CHUNKS)], sem)

    def run(e, carry):
        pos, length = pads_ref[0, e], pads_ref[1, e]

        def whole(q, c):
            getattr(piece(pos + q * tm, tm), action)()
            return c

        lax.fori_loop(0, length // tm, whole, 0)
        pos = pos + (length // tm) * tm
        rows = tm // 2
        while rows >= 1:
            @pl.when((length & rows) != 0)
            def _(pos=pos, rows=rows):
                getattr(piece(pos, rows), action)()

            pos = pos + (length & rows)
            rows //= 2
        return carry

    lax.fori_loop(0, pads_ref.shape[1], run, 0)


def _dispatch_kernel(slot_ref, pads_ref, h_ref, x_hbm, zeros, sem, zsem):
    n = h_ref.shape[0] // ROW_CHUNKS

    def issue(r, carry):
        src = h_ref.at[pl.ds(pl.multiple_of(r * ROW_CHUNKS, ROW_CHUNKS), ROW_CHUNKS)]
        for k in range(2):
            dst = pl.multiple_of(slot_ref[k, r] * ROW_CHUNKS, ROW_CHUNKS)
            pltpu.make_async_copy(src, x_hbm.at[pl.ds(dst, ROW_CHUNKS)], sem).start(
                priority=k % DMA_QUEUES)
        return carry

    lax.fori_loop(0, n, issue, 0, unroll=4)

    @pl.when(pl.program_id(0) == pl.num_programs(0) - 1)
    def _():
        zeros[...] = jnp.zeros(zeros.shape, zeros.dtype)
        _zero_runs(pads_ref, zeros, x_hbm, zsem, "start")
        _zero_runs(pads_ref, zeros, x_hbm, zsem, "wait")

    for k in range(2):
        pltpu.make_async_copy(h_ref, x_hbm.at[pl.ds(0, n * ROW_CHUNKS)], sem).wait()


def _dispatch(h1, slot, pads, n_slots, tm, window):
    t = slot.shape[1]
    steps = t // window
    slot3 = slot.reshape(2, steps, window).transpose(1, 0, 2)
    return pl.pallas_call(
        _dispatch_kernel,
        grid=(steps,),
        in_specs=[pl.BlockSpec((None, 2, window), lambda i: (i, 0, 0), memory_space=pltpu.SMEM),
                  pl.BlockSpec(pads.shape, lambda i: (0, 0), memory_space=pltpu.SMEM),
                  pl.BlockSpec((window * ROW_CHUNKS, LANES), lambda i: (i, 0))],
        out_specs=pl.BlockSpec(memory_space=pl.ANY),
        out_shape=jax.ShapeDtypeStruct((n_slots * ROW_CHUNKS, LANES), h1.dtype),
        scratch_shapes=[pltpu.VMEM((tm * ROW_CHUNKS, LANES), h1.dtype), pltpu.SemaphoreType.DMA,
                        pltpu.SemaphoreType.DMA],
        compiler_params=_cparams(1),
        name="dispatch_rows",
    )(slot3, pads, h1)


def _expert_plan(route, tm):
    t = route.shape[0]
    ids = jnp.arange(MOE_EXPERTS, dtype=jnp.int32)
    e = jnp.concatenate([route[:, ROUTE_E1], route[:, ROUTE_E2]]).astype(jnp.int32)
    onehot = (e[:, None] == ids[None, :]).astype(jnp.int32)
    csum = jnp.cumsum(onehot, axis=0)
    rank = jnp.sum(onehot * csum, axis=1) - 1
    cnt = csum[-1]
    tiles_e = (cnt + tm - 1) // tm
    tile_end = jnp.cumsum(tiles_e)
    tile_start = tile_end - tiles_e
    slot = tile_start[e] * tm + rank
    n_tiles = (2 * t) // tm + MOE_EXPERTS
    tid = jnp.arange(n_tiles, dtype=jnp.int32)
    used = tid < tile_end[-1]
    tile_e = jnp.sum(tile_end[None, :] <= jnp.minimum(tid, tile_end[-1] - 1)[:, None], axis=1)
    nvalid = jnp.where(used, jnp.clip(cnt[tile_e] - (tid - tile_start[tile_e]) * tm, 0, tm), 0)
    first = used & (tid == tile_start[tile_e])
    has = tiles_e > 0
    ordinal = jnp.cumsum(has.astype(jnp.int32)) - 1
    later = jnp.where((ids[None, :] > ids[:, None]) & has[None, :], ids[None, :], MOE_EXPERTS)
    nxt_e = jnp.min(later, axis=1)
    nxt_e = jnp.where(nxt_e == MOE_EXPERTS, -1, nxt_e)
    i32 = lambda a: a.astype(jnp.int32)
    pads = jnp.stack([jnp.append(tile_start * tm + cnt, tile_end[-1] * tm),
                      jnp.append(tiles_e * tm - cnt, (n_tiles - tile_end[-1]) * tm)])
    return (slot.reshape(2, t), i32(pads), n_tiles * tm, i32(tile_e), i32(nvalid), i32(first),
            i32(nxt_e[tile_e]), i32(ordinal[tile_e] % 2))


def _experts_kernel(te_ref, nv_ref, first_ref, nxt_ref, par_ref, x_ref, wg_hbm, wu_hbm, wd_hbm, y_ref,
                    wg_f, wu_f, wd_f, wg_b, wu_b, wd_b, x_b, sem):
    i = pl.program_id(0)
    s = par_ref[i]

    def weight_copies(e, slot):
        return (pltpu.make_async_copy(wg_hbm.at[e], wg_f.at[slot], sem.at[slot]),
                pltpu.make_async_copy(wu_hbm.at[e], wu_f.at[slot], sem.at[slot]),
                pltpu.make_async_copy(wd_hbm.at[e], wd_f.at[slot], sem.at[slot]))

    @pl.when(i == 0)
    def _():
        for cp in weight_copies(te_ref[0], par_ref[0]):
            cp.start()

    @pl.when(first_ref[i] == 1)
    def _():
        @pl.when(nxt_ref[i] >= 0)
        def _():
            for cp in weight_copies(nxt_ref[i], 1 - s):
                cp.start()

        for cp in weight_copies(te_ref[i], s):
            cp.wait()
        wg_b[...] = wg_f[s].astype(BF16)
        wu_b[...] = wu_f[s].astype(BF16)
        wd_b[...] = wd_f[s].astype(BF16)

    nv = nv_ref[i]

    @pl.when(nv > 0)
    def _():
        for c in range(ROW_CHUNKS):
            x_b[:, c * LANES:(c + 1) * LANES] = _load_chunked(x_ref, c).astype(BF16)
        x = x_b[...]
        hid = (_silu(_dot(x, wg_b[...])) * _dot(x, wu_b[...])).astype(BF16)
        _store_chunked(y_ref, _dot(hid, wd_b[...]))

    @pl.when(nv == 0)
    def _():
        y_ref[...] = jnp.zeros(y_ref.shape, y_ref.dtype)


def _experts(xs, tile_e, nvalid, first, nxt, par, wg, wu, wd, tm):
    n_tiles = xs.shape[0] // (tm * ROW_CHUNKS)
    rows = pl.BlockSpec((tm * ROW_CHUNKS, LANES), lambda i, *_: (i, 0))
    hbm = pl.BlockSpec(memory_space=pl.ANY)
    grid_spec = pltpu.PrefetchScalarGridSpec(
        num_scalar_prefetch=5,
        grid=(n_tiles,),
        in_specs=[rows, hbm, hbm, hbm],
        out_specs=rows,
        scratch_shapes=[pltpu.VMEM((2, D_MODEL, MOE_FF), F32), pltpu.VMEM((2, D_MODEL, MOE_FF), F32),
                        pltpu.VMEM((2, MOE_FF, D_MODEL), F32),
                        pltpu.VMEM((D_MODEL, MOE_FF), BF16), pltpu.VMEM((D_MODEL, MOE_FF), BF16),
                        pltpu.VMEM((MOE_FF, D_MODEL), BF16), pltpu.VMEM((tm, D_MODEL), BF16),
                        pltpu.SemaphoreType.DMA((2,))])
    return pl.pallas_call(
        _experts_kernel,
        grid_spec=grid_spec,
        out_shape=jax.ShapeDtypeStruct(xs.shape, F32),
        compiler_params=_cparams(1),
        name="experts",
    )(tile_e, nvalid, first, nxt, par, xs, wg, wu, wd)


def _combine_kernel(n_steps, slot_ref, slot_next_ref, route_ref, h1_ref, g_ref, b_ref, y_hbm, o_ref,
                    gbuf, sem):
    i = pl.program_id(0)
    tm = route_ref.shape[0]

    def request(idx_ref, parity):
        def body(r, carry):
            dst = pl.ds(pl.multiple_of(r * ROW_CHUNKS, ROW_CHUNKS), ROW_CHUNKS)
            for k in range(2):
                src = pl.multiple_of(idx_ref[k, r] * ROW_CHUNKS, ROW_CHUNKS)
                pltpu.make_async_copy(y_hbm.at[pl.ds(src, ROW_CHUNKS)], gbuf.at[parity, k, dst],
                                      sem.at[parity]).start(priority=k % DMA_QUEUES)
            return carry

        lax.fori_loop(0, tm, body, 0, unroll=4)

    @pl.when(i == 0)
    def _():
        request(slot_ref, 0)

    @pl.when(i + 1 < n_steps)
    def _():
        request(slot_next_ref, (i + 1) % 2)

    parity = i % 2
    for k in range(2):
        pltpu.make_async_copy(y_hbm.at[pl.ds(0, tm * ROW_CHUNKS)], gbuf.at[parity, k],
                              sem.at[parity]).wait()
    route = route_ref[...]
    w1 = route[:, ROUTE_W1:ROUTE_W1 + 1]
    w2 = route[:, ROUTE_W2:ROUTE_W2 + 1]
    pre = jnp.concatenate(
        [DEEPNORM_ALPHA * _load_chunked(h1_ref, c)
         + w1 * _load_chunked(gbuf.at[parity, 0], c) + w2 * _load_chunked(gbuf.at[parity, 1], c)
         for c in range(ROW_CHUNKS)], axis=1)
    o_ref[...] = _layernorm(pre, g_ref[...], b_ref[...])


def _combine(y_sorted, slot, route, h1, row0, m, g, b, tm):
    t = slot.shape[1]
    blk0 = row0 // tm
    last = (row0 + m) // tm - 1
    slot3 = slot.reshape(2, t // tm, tm).transpose(1, 0, 2)
    return pl.pallas_call(
        functools.partial(_combine_kernel, m // tm),
        grid=(m // tm,),
        in_specs=[pl.BlockSpec((None, 2, tm), lambda i: (blk0 + i, 0, 0), memory_space=pltpu.SMEM),
                  pl.BlockSpec((None, 2, tm), lambda i: (jnp.minimum(blk0 + i + 1, last), 0, 0),
                               memory_space=pltpu.SMEM),
                  pl.BlockSpec((tm, ROUTER_LANES), lambda i: (blk0 + i, 0)),
                  pl.BlockSpec((tm * ROW_CHUNKS, LANES), lambda i: (blk0 + i, 0)),
                  _row_params(g), _row_params(b),
                  pl.BlockSpec(memory_space=pl.ANY)],
        out_specs=pl.BlockSpec((tm, D_MODEL), lambda i: (i, 0)),
        out_shape=jax.ShapeDtypeStruct((m, D_MODEL), F32),
        scratch_shapes=[pltpu.VMEM((2, 2, tm * ROW_CHUNKS, LANES), F32),
                        pltpu.SemaphoreType.DMA((2,))],
        compiler_params=_cparams(1),
        name="combine_ln",
    )(slot3, slot3, route, h1, g, b, y_sorted)


def _pad_lanes(v, width=LANES):
    v = v.reshape(1, -1)
    return jnp.pad(v, ((0, 0), (0, width - v.shape[1])))


def kernel(x_prompt, x_sample, state_ssd, state_conv, state_gla, w_in, conv_w, conv_b, dt_bias, a_log, d_skip, ssd_norm_w, gla_w_a2, gla_b_a, gla_norm_w, w_out, ln1_g, ln1_b, w_router_group, b_router_group, w_router_expert, b_router_expert, w_gate, w_up, w_down, ln2_g, ln2_b):
    assert w_in.shape[0] == DEPTH == 1
    n_p, len_p, _ = x_prompt.shape
    n_s, len_s, _ = x_sample.shape
    rows_p, rows_s = n_p * len_p, n_s * len_s
    xp = x_prompt.reshape(rows_p, D_MODEL)
    xs = x_sample.reshape(rows_s, D_MODEL)

    offs = [0]
    for s in IN_SPLIT_SIZES:
        offs.append(offs[-1] + s)
    w_t = w_in[0].T
    w_tail_t = jnp.concatenate(
        [jnp.pad(w_t[offs[2]:offs[3]], ((0, LANES - SSD_HEADS), (0, 0))),
         jnp.pad(w_t[offs[7]:offs[8]], ((0, LANES - GLA_RANK), (0, 0)))], axis=0)
    x_b = (xp.astype(BF16), xs.astype(BF16))

    proj = _in_proj(x_b, w_t, ((0, offs[2]), (offs[3], offs[7] - offs[3]),
                               (offs[8], offs[10] - offs[8])))
    tail = _matmul_nt(x_b, w_tail_t, IN_PROJ_ROWS)

    cw, cb = conv_w[0], conv_b[0].reshape(1, -1)
    dtb, alog = _pad_lanes(dt_bias[0]), _pad_lanes(a_log[0])
    dskip_e = jnp.repeat(d_skip[0], SSD_HEAD_DIM).reshape(1, -1)
    nw_ssd = ssd_norm_w[0].reshape(1, -1)
    wa, ba, nw_gla = gla_w_a2[0].astype(BF16), gla_b_a[0].reshape(1, -1), gla_norm_w[0].reshape(1, -1)

    ys_p, ssd_p, conv_p = _ssd_prompt(proj, tail, n_p, len_p, cw, cb, dtb, alog, dskip_e, nw_ssd)
    ys_s, ssd_s, conv_s = _ssd_sample(proj, tail, rows_p, n_s, len_s, state_ssd[0], state_conv[0],
                                      cw, cb, dtb, alog, dskip_e, nw_ssd)
    yg_p, gla_p = _gla_prompt(proj, tail, n_p, len_p, wa, ba, nw_gla)
    yg_s, gla_s = _gla_sample(proj, tail, rows_p, n_s, len_s, state_gla[0], wa, ba, nw_gla)

    wo = w_out[0].astype(BF16)
    w_r = jnp.pad(jnp.concatenate([w_router_group[0], w_router_expert[0]], axis=1),
                  ((0, 0), (0, ROUTER_LANES - MOE_GROUPS - MOE_EXPERTS)))
    wr_hi = w_r.astype(BF16)
    wr_lo = (w_r - wr_hi.astype(F32)).astype(BF16)
    b_r = _pad_lanes(jnp.concatenate([b_router_group[0], b_router_expert[0]]), ROUTER_LANES)
    g1, b1 = ln1_g[0].reshape(1, -1), ln1_b[0].reshape(1, -1)
    g2, b2 = ln2_g[0].reshape(1, -1), ln2_b[0].reshape(1, -1)

    h1, route = _merge((ys_p, ys_s), (yg_p, yg_s), proj, (xp, xs), wo, g1, b1, wr_hi, wr_lo, b_r,
                       MERGE_ROWS)
    slot, pads, n_slots, tile_e, nvalid, first, nxt, par = _expert_plan(route, EXPERT_ROWS)
    x_sorted = _dispatch(h1, slot, pads, n_slots, EXPERT_ROWS, DISPATCH_WINDOW)
    y_sorted = _experts(x_sorted, tile_e, nvalid, first, nxt, par, w_gate[0], w_up[0], w_down[0],
                        EXPERT_ROWS)
    outs = [_combine(y_sorted, slot, route, h1, 0, rows_p, g2, b2, MERGE_ROWS),
            _combine(y_sorted, slot, route, h1, rows_p, rows_s, g2, b2, MERGE_ROWS)]
    y_p = outs[0].reshape(x_prompt.shape)
    y_s = outs[1].reshape(x_sample.shape)
    return (y_p, y_s, ssd_p[None], conv_p[None], gla_p[None], ssd_s[None], conv_s[None], gla_s[None])
```

```python
import functools
import math

import jax
import jax.numpy as jnp
from jax import lax
from jax.experimental import pallas as pl
from jax.experimental.pallas import tpu as pltpu

F32 = jnp.float32
BF16 = jnp.bfloat16
HIGHEST = lax.Precision.HIGHEST

D_MODEL = 2048
SSD_HEADS = 32
SSD_HEAD_DIM = 64
SSD_GROUPS = 8
SSD_STATE = 128
SSD_CONV = 4
SSD_CHUNK = 128
SSD_BC = SSD_GROUPS * SSD_STATE
SSD_CONV_DIM = D_MODEL + 2 * SSD_BC
SSD_GROUP_COLS = D_MODEL // SSD_GROUPS
HEADS_PER_GROUP = SSD_HEADS // SSD_GROUPS
GLA_HEADS = 8
GLA_DK = 128
GLA_DV = 256
GLA_KEY_DIM = GLA_HEADS * GLA_DK
GLA_RANK = 16
GLA_TAU = 16.0
GLA_CHUNK = 64
MOE_GROUPS = 4
MOE_EPG = 8
MOE_EXPERTS = 32
MOE_FF = 512
DEPTH = 1
DEEPNORM_ALPHA = (2.0 * DEPTH) ** 0.25
EPS = 1e-5
IN_SPLIT_SIZES = (D_MODEL, SSD_CONV_DIM, SSD_HEADS, GLA_KEY_DIM, GLA_KEY_DIM, D_MODEL, D_MODEL,
                  GLA_RANK, D_MODEL, D_MODEL)

LANES = 128
HALO_ROWS = 8
COL_Z, COL_XS, COL_BC, COL_QK, COL_V, COL_R, COL_GS, COL_GG = range(8)
TAIL_DT, TAIL_ALO = 0, 1
ROUTER_LANES = 128
ROUTE_E1, ROUTE_E2, ROUTE_W1, ROUTE_W2 = 0, 1, 2, 3
MERGE_ROWS = 256
EXPERT_ROWS = 256
DISPATCH_WINDOW = 512
DMA_QUEUES = 2
VMEM_LIMIT = 56 * 1024 * 1024


def _cparams(n_axes):
    return pltpu.CompilerParams(dimension_semantics=("arbitrary",) * n_axes,
                                vmem_limit_bytes=VMEM_LIMIT)


def _silu(x):
    return x * jax.nn.sigmoid(x)


def _softplus(x):
    return jnp.maximum(x, 0.0) + jnp.log1p(jnp.exp(-jnp.abs(x)))


def _tril(n):
    r = lax.broadcasted_iota(jnp.int32, (n, n), 0)
    c = lax.broadcasted_iota(jnp.int32, (n, n), 1)
    return r >= c


def _expand_group(p, g, lane_head):
    h0 = g * HEADS_PER_GROUP
    out = p[:, h0 + HEADS_PER_GROUP - 1:h0 + HEADS_PER_GROUP]
    for hh in range(HEADS_PER_GROUP - 2, -1, -1):
        out = jnp.where(lane_head == hh, p[:, h0 + hh:h0 + hh + 1], out)
    return out


def _split3(x):
    hi = x.astype(BF16)
    r = x - hi.astype(F32)
    mid = r.astype(BF16)
    lo = (r - mid.astype(F32)).astype(BF16)
    return hi, mid, lo


def _dot_sel(sel, x):
    hi, mid, lo = _split3(x)
    return _dot(sel, lo) + _dot(sel, mid) + _dot(sel, hi)


def _col_bcast(row):
    return jnp.broadcast_to(row, (LANES, LANES)).T


def _dot(a, b, **kw):
    return jnp.dot(a, b, preferred_element_type=F32, **kw)


def _dot_nt(a, b, **kw):
    return lax.dot_general(a, b, (((1,), (1,)), ((), ())), preferred_element_type=F32, **kw)


def _dot_tn(a, b, **kw):
    return lax.dot_general(a, b, (((0,), (0,)), ((), ())), preferred_element_type=F32, **kw)


def _matmul_kernel(n_first, xa_ref, xb_ref, wt_ref, o_ref):
    x = jnp.where(pl.program_id(0) < n_first, xa_ref[...], xb_ref[...])
    o_ref[...] = _dot_nt(x, wt_ref[...].astype(BF16))


def _matmul_nt(xs, wt, tm):
    n, k = wt.shape
    n_a, n_b = xs[0].shape[0] // tm, xs[1].shape[0] // tm
    return pl.pallas_call(
        functools.partial(_matmul_kernel, n_a),
        grid=(n_a + n_b,),
        in_specs=[pl.BlockSpec((tm, k), lambda i: (jnp.minimum(i, n_a - 1), 0)),
                  pl.BlockSpec((tm, k), lambda i: (jnp.maximum(i - n_a, 0), 0)),
                  pl.BlockSpec((n, k), lambda i: (0, 0))],
        out_specs=pl.BlockSpec((tm, n), lambda i: (i, 0)),
        out_shape=jax.ShapeDtypeStruct(((n_a + n_b) * tm, n), F32),
        compiler_params=_cparams(1),
        name="in_proj_tail",
    )(*xs, wt)


IN_PROJ_COLS = 1024
IN_PROJ_ROWS = 1024
CAST_ROWS = 256
SUBLANES = 8


EPILOGUE_ROWS = 256
EPILOGUE_COLS = 256


def _conv_epilogue(i, col0, tiles_per_seq, cw_ref, cb_ref, o_ref, tail_hbm, cbuf, tsem):
    tm, tn = o_ref.shape

    @pl.when(i % tiles_per_seq == 0)
    def _():
        cbuf[0:HALO_ROWS, :] = jnp.zeros((HALO_ROWS, tn), F32)

    for r in range(0, tm, EPILOGUE_ROWS):
        cbuf[HALO_ROWS + r:HALO_ROWS + r + EPILOGUE_ROWS, :] = o_ref[r:r + EPILOGUE_ROWS, :]
    for c in range(0, tn, EPILOGUE_COLS):
        lanes = pl.ds(pl.multiple_of(col0 + c, EPILOGUE_COLS), EPILOGUE_COLS)
        w = cw_ref[:, lanes]
        bias = cb_ref[:, lanes]
        for r in range(0, tm, EPILOGUE_ROWS):
            acc = bias
            for t in range(SSD_CONV):
                r0 = HALO_ROWS - (SSD_CONV - 1) + t + r
                acc = acc + w[t:t + 1, :] * cbuf[r0:r0 + EPILOGUE_ROWS, c:c + EPILOGUE_COLS]
            o_ref[r:r + EPILOGUE_ROWS, c:c + EPILOGUE_COLS] = _silu(acc)

    @pl.when((i + 1) % tiles_per_seq == 0)
    def _():
        seq = i // tiles_per_seq
        cp = pltpu.make_async_copy(
            cbuf.at[pl.ds(tm, HALO_ROWS)],
            tail_hbm.at[pl.ds(pl.multiple_of(seq * HALO_ROWS, HALO_ROWS), HALO_ROWS),
                        pl.ds(pl.multiple_of(col0, tn), tn)], tsem)
        cp.start()
        cp.wait()

    cbuf[0:HALO_ROWS, :] = cbuf[tm:tm + HALO_ROWS, :]


def _in_proj_kernel(plan, acts, n_first, tiles_per_seq, xa_ref, xb_ref, cw_ref, cb_ref, wt_hbm, o_ref,
                    tail_hbm, wf, wb, cbuf, sem, tsem):
    j = pl.program_id(0)
    i = pl.program_id(1)
    tn = wb.shape[0]

    def block_copy(jj, slot):
        shift = 0
        for lo, hi, sh in plan:
            shift = jnp.where((jj >= lo) & (jj < hi), sh, shift)
        src = pl.multiple_of(jj * tn + shift, SUBLANES)
        return pltpu.make_async_copy(wt_hbm.at[pl.ds(src, tn)], wf.at[slot], sem.at[slot])

    @pl.when(i == 0)
    def _():
        slot = j % 2

        @pl.when(j == 0)
        def _():
            block_copy(j, slot).start()

        @pl.when(j + 1 < pl.num_programs(0))
        def _():
            block_copy(j + 1, 1 - slot).start()

        block_copy(j, slot).wait()
        for r in range(0, tn, CAST_ROWS):
            wb[r:r + CAST_ROWS, :] = wf[slot, r:r + CAST_ROWS, :].astype(BF16)

    x = jnp.where(i < n_first, xa_ref[...], xb_ref[...])
    o_ref[...] = _dot_nt(x, wb[...])

    for lo, hi, kind in acts:
        @pl.when((j >= lo) & (j < hi))
        def _(lo=lo, kind=kind):
            if kind == "conv":
                @pl.when(i < n_first)
                def _():
                    _conv_epilogue(i, (j - lo) * tn, tiles_per_seq, cw_ref, cb_ref, o_ref, tail_hbm,
                                   cbuf, tsem)
            else:
                fn = _silu if kind == "silu" else jax.nn.sigmoid
                for r in range(0, o_ref.shape[0], EPILOGUE_ROWS):
                    o_ref[r:r + EPILOGUE_ROWS, :] = fn(o_ref[r:r + EPILOGUE_ROWS, :])


def _in_proj(xs, wt, segments, acts, seq_len, cw, cb):
    tm, tn = IN_PROJ_ROWS, IN_PROJ_COLS
    k = wt.shape[1]
    n_a, n_b = xs[0].shape[0] // tm, xs[1].shape[0] // tm
    assert seq_len % tm == 0
    plan, dest = [], 0
    for src, count in segments:
        assert count % tn == 0 and dest % tn == 0 and (src - dest) % SUBLANES == 0
        plan.append((dest // tn, (dest + count) // tn, src - dest))
        dest += count
    n_seq = xs[0].shape[0] // seq_len
    return pl.pallas_call(
        functools.partial(_in_proj_kernel, tuple(plan), tuple(acts), n_a, seq_len // tm),
        grid=(dest // tn, n_a + n_b),
        in_specs=[pl.BlockSpec((tm, k), lambda j, i: (jnp.minimum(i, n_a - 1), 0)),
                  pl.BlockSpec((tm, k), lambda j, i: (jnp.maximum(i - n_a, 0), 0)),
                  pl.BlockSpec(cw.shape, lambda j, i: (0, 0)),
                  pl.BlockSpec(cb.shape, lambda j, i: (0, 0)),
                  pl.BlockSpec(memory_space=pl.ANY)],
        out_specs=[pl.BlockSpec((tm, tn), lambda j, i: (i, j)), pl.BlockSpec(memory_space=pl.ANY)],
        out_shape=[jax.ShapeDtypeStruct(((n_a + n_b) * tm, dest), F32),
                   jax.ShapeDtypeStruct((n_seq * HALO_ROWS, cw.shape[1]), F32)],
        scratch_shapes=[pltpu.VMEM((2, tn, k), F32), pltpu.VMEM((tn, k), BF16),
                        pltpu.VMEM((HALO_ROWS + tm, tn), F32),
                        pltpu.SemaphoreType.DMA((2,)), pltpu.SemaphoreType.DMA],
        compiler_params=_cparams(2),
        name="in_proj_main",
    )(*xs, cw, cb, wt)


def _conv_silu(ubuf, cw_ref, cb_ref, lo, width, rows):
    acc = cb_ref[:, lo:lo + width]
    for i in range(SSD_CONV):
        r0 = HALO_ROWS - (SSD_CONV - 1) + i
        acc = acc + cw_ref[i:i + 1, lo:lo + width] * ubuf[r0:r0 + rows, lo:lo + width]
    return _silu(acc)


def _gated_group_norm(y_g, gate_g, nw_g):
    yg = y_g * gate_g
    ms = jnp.mean(yg * yg, axis=-1, keepdims=True)
    return yg * lax.rsqrt(ms + EPS) * nw_g


def _ssd_prompt_kernel(z_ref, xs_ref, bc_ref, tail_ref, dtb_ref, alog_ref, dskip_ref, nw_ref, y_ref,
                       hout_ref, ht):
    q = SSD_CHUNK
    c = pl.program_id(1)

    @pl.when(c == 0)
    def _():
        ht[...] = jnp.zeros(ht.shape, F32)

    dt = _softplus(tail_ref[...] + dtb_ref[...])
    da = dt * (-jnp.exp(alog_ref[...]))
    causal = _tril(q)
    acum = _dot_sel(causal.astype(BF16), da)
    acum_t = acum.T
    dt_t = dt.T
    a_last = acum[q - 1:q, :]
    decay_in = jnp.exp(acum)
    w_end = dt * jnp.exp(a_last - acum)
    chunk_decay = jnp.exp(a_last)
    lane_head = lax.broadcasted_iota(jnp.int32, (q, SSD_GROUP_COLS), 1) >> 6
    assert SSD_HEAD_DIM == 1 << 6

    for g in range(SSD_GROUPS):
        lo = g * SSD_GROUP_COLS
        cols = slice(lo, lo + SSD_GROUP_COLS)
        xs_g = xs_ref[:, cols]
        b_g = bc_ref[:, g * SSD_STATE:(g + 1) * SSD_STATE].astype(BF16)
        c_g = bc_ref[:, SSD_BC + g * SSD_STATE:SSD_BC + (g + 1) * SSD_STATE].astype(BF16)
        scores = _dot_nt(c_g, b_g)
        h_g = ht[g]
        xs_b = xs_g.astype(BF16)
        m_heads, x_heads = [], []
        for hh in range(HEADS_PER_GROUP):
            h = g * HEADS_PER_GROUP + hh
            seg = acum[:, h:h + 1] - acum_t[h:h + 1, :]
            decay = jnp.exp(jnp.where(causal, seg, -jnp.inf))
            m_heads.append((scores * decay * dt_t[h:h + 1, :]).astype(BF16))
            x_heads.append(jnp.where(lane_head == hh, xs_b, jnp.zeros_like(xs_b)))
        y_g = (_dot(jnp.concatenate(m_heads, axis=1), jnp.concatenate(x_heads, axis=0))
               + _dot(c_g, h_g.astype(BF16)) * _expand_group(decay_in, g, lane_head)
               + dskip_ref[:, cols] * xs_g)
        y_ref[:, cols] = _gated_group_norm(y_g, z_ref[:, cols], nw_ref[:, cols])
        xw = (xs_g * _expand_group(w_end, g, lane_head)).astype(BF16)
        ht[g] = h_g * _expand_group(chunk_decay, g, lane_head[0:1]) + _dot_tn(b_g, xw)

    @pl.when(c == pl.num_programs(1) - 1)
    def _():
        for g in range(SSD_GROUPS):
            h_t = ht[g].T
            for hh in range(HEADS_PER_GROUP):
                hout_ref[0, g * HEADS_PER_GROUP + hh] = h_t[hh * SSD_HEAD_DIM:(hh + 1) * SSD_HEAD_DIM]


def _row_params(p):
    return pl.BlockSpec(p.shape, lambda *_: (0,) * p.ndim)


def _ssd_prompt(proj, tail, n_seq, seq_len, dtb, alog, dskip_e, nw):
    q = SSD_CHUNK
    nc = seq_len // q

    def col(blk):
        return pl.BlockSpec((q, D_MODEL), lambda b, c: (b * nc + c, blk))

    return pl.pallas_call(
        _ssd_prompt_kernel,
        grid=(n_seq, nc),
        in_specs=[col(COL_Z), col(COL_XS), col(COL_BC),
                  pl.BlockSpec((q, LANES), lambda b, c: (b * nc + c, TAIL_DT)),
                  _row_params(dtb), _row_params(alog), _row_params(dskip_e), _row_params(nw)],
        out_specs=[pl.BlockSpec((q, D_MODEL), lambda b, c: (b * nc + c, 0)),
                   pl.BlockSpec((1, SSD_HEADS, SSD_HEAD_DIM, SSD_STATE), lambda b, c: (b, 0, 0, 0))],
        out_shape=[jax.ShapeDtypeStruct((n_seq * seq_len, D_MODEL), F32),
                   jax.ShapeDtypeStruct((n_seq, SSD_HEADS, SSD_HEAD_DIM, SSD_STATE), F32)],
        scratch_shapes=[pltpu.VMEM((SSD_GROUPS, SSD_STATE, SSD_GROUP_COLS), F32)],
        compiler_params=_cparams(2),
        name="ssd_prompt",
    )(proj, proj, proj, tail, dtb, alog, dskip_e, nw)


SAMPLE_SEQS = 4


def _dot_tn_split(a, b):
    a_hi = a.astype(BF16)
    a_lo = (a - a_hi.astype(F32)).astype(BF16)
    b_hi = b.astype(BF16)
    b_lo = (b - b_hi.astype(F32)).astype(BF16)
    return _dot_tn(a_lo, b_hi) + _dot_tn(a_hi, b_lo) + _dot_tn(a_hi, b_hi)


def _ssd_sample_seq(z_ref, xs_ref, bc_ref, tail_ref, h_ref, conv_ref, cw_ref, cb_ref, dtb_ref,
                    alog_ref, dskip_ref, nw_ref, y_ref, hout_ref, convout_ref, ubuf, decay_rows):
    q = xs_ref.shape[0]
    ubuf[0:HALO_ROWS - (SSD_CONV - 1), :] = jnp.zeros((HALO_ROWS - (SSD_CONV - 1), SSD_CONV_DIM), F32)
    ubuf[HALO_ROWS - (SSD_CONV - 1):HALO_ROWS, :] = conv_ref[...]
    ubuf[HALO_ROWS:HALO_ROWS + q, 0:D_MODEL] = xs_ref[...]
    ubuf[HALO_ROWS:HALO_ROWS + q, D_MODEL:SSD_CONV_DIM] = bc_ref[...]
    convout_ref[...] = ubuf[HALO_ROWS + q - (SSD_CONV - 1):HALO_ROWS + q, :]

    dt = _softplus(tail_ref[...] + dtb_ref[...])
    da = dt * (-jnp.exp(alog_ref[...]))
    acum = _dot_sel(_tril(q).astype(BF16), da)
    a_last = acum[q - 1:q, :]
    decay_in = jnp.exp(acum)
    w_end = dt * jnp.exp(a_last - acum)
    decay_rows[...] = _col_bcast(jnp.exp(a_last))[:SSD_HEADS]
    row = lax.broadcasted_iota(jnp.int32, (q, SSD_GROUP_COLS), 0)
    lane_head = lax.broadcasted_iota(jnp.int32, (q, SSD_GROUP_COLS), 1) >> 6

    for g in range(SSD_GROUPS):
        lo = g * SSD_GROUP_COLS
        cols = slice(lo, lo + SSD_GROUP_COLS)
        xs_g = _conv_silu(ubuf, cw_ref, cb_ref, lo, SSD_GROUP_COLS, q)
        b_g = _conv_silu(ubuf, cw_ref, cb_ref, D_MODEL + g * SSD_STATE, SSD_STATE, q)
        c_g = _conv_silu(ubuf, cw_ref, cb_ref, D_MODEL + SSD_BC + g * SSD_STATE, SSD_STATE, q)
        h_g = h_ref[g * HEADS_PER_GROUP:(g + 1) * HEADS_PER_GROUP].reshape(
            SSD_GROUP_COLS, SSD_STATE)
        y_g = (_dot_nt(c_g.astype(BF16), h_g.astype(BF16)) * _expand_group(decay_in, g, lane_head)
               + dskip_ref[:, cols] * xs_g)
        acum_g = _expand_group(acum, g, lane_head)
        dt_g = _expand_group(dt, g, lane_head)
        for j in range(q):
            s_j = jnp.sum(c_g * b_g[j:j + 1, :], axis=-1, keepdims=True)
            coef = s_j * jnp.exp(acum_g - acum_g[j:j + 1, :]) * dt_g[j:j + 1, :]
            y_g = y_g + jnp.where(row >= j, coef, 0.0) * xs_g[j:j + 1, :]
        y_ref[:, cols] = _gated_group_norm(y_g, z_ref[:, cols], nw_ref[:, cols])
        upd = _dot_tn_split(xs_g * _expand_group(w_end, g, lane_head), b_g)
        for hh in range(HEADS_PER_GROUP):
            h = g * HEADS_PER_GROUP + hh
            rows = slice(hh * SSD_HEAD_DIM, (hh + 1) * SSD_HEAD_DIM)
            hout_ref[h] = h_g[rows] * decay_rows[h:h + 1, :] + upd[rows]


def _ssd_sample_kernel(z_ref, xs_ref, bc_ref, tail_ref, h_ref, conv_ref, cw_ref, cb_ref, dtb_ref,
                       alog_ref, dskip_ref, nw_ref, y_ref, hout_ref, convout_ref, ubuf, decay_rows):
    n = h_ref.shape[0]
    q = xs_ref.shape[0] // n
    for s in range(n):
        rows = pl.ds(s * q, q)
        _ssd_sample_seq(z_ref.at[rows], xs_ref.at[rows], bc_ref.at[rows], tail_ref.at[rows],
                        h_ref.at[s], conv_ref.at[s], cw_ref, cb_ref, dtb_ref, alog_ref, dskip_ref,
                        nw_ref, y_ref.at[rows], hout_ref.at[s], convout_ref.at[s], ubuf.at[s],
                        decay_rows.at[s])


def _ssd_sample(proj, tail, row0, n_seq, seq_len, h0, conv0, cw, cb, dtb, alog, dskip_e, nw):
    nb = SAMPLE_SEQS
    q = seq_len
    blk0 = row0 // (nb * q)

    def col(blk):
        return pl.BlockSpec((nb * q, D_MODEL), lambda b: (blk0 + b, blk))

    state_spec = pl.BlockSpec((nb, SSD_HEADS, SSD_HEAD_DIM, SSD_STATE), lambda b: (b, 0, 0, 0))
    conv_spec = pl.BlockSpec((nb, SSD_CONV - 1, SSD_CONV_DIM), lambda b: (b, 0, 0))
    return pl.pallas_call(
        _ssd_sample_kernel,
        grid=(n_seq // nb,),
        in_specs=[col(COL_Z), col(COL_XS), col(COL_BC),
                  pl.BlockSpec((nb * q, LANES), lambda b: (blk0 + b, TAIL_DT)),
                  state_spec, conv_spec,
                  _row_params(cw), _row_params(cb), _row_params(dtb), _row_params(alog),
                  _row_params(dskip_e), _row_params(nw)],
        out_specs=[pl.BlockSpec((nb * q, D_MODEL), lambda b: (b, 0)), state_spec, conv_spec],
        out_shape=[jax.ShapeDtypeStruct((n_seq * seq_len, D_MODEL), F32),
                   jax.ShapeDtypeStruct(h0.shape, F32),
                   jax.ShapeDtypeStruct(conv0.shape, F32)],
        scratch_shapes=[pltpu.VMEM((nb, HALO_ROWS + q, SSD_CONV_DIM), F32),
                        pltpu.VMEM((nb, SSD_HEADS, SSD_STATE), F32)],
        compiler_params=_cparams(1),
        name="ssd_sample",
    )(proj, proj, proj, tail, h0, conv0, cw, cb, dtb, alog, dskip_e, nw)


def _gla_chunk(qk_ref, v_ref, r_ref, alo_ref, wa_ref, ba_ref, nw_ref, y_ref, get_state, put_state):
    q = v_ref.shape[0]
    a_lo = alo_ref[...][:, :GLA_RANK].astype(BF16)
    gk = -_softplus(-(_dot(a_lo, wa_ref[...]) + ba_ref[...])) / GLA_TAU
    causal = _tril(q)
    bcum = _dot_sel(causal.astype(BF16), gk)
    for h in range(GLA_HEADS):
        kc = slice(h * GLA_DK, (h + 1) * GLA_DK)
        vc = slice(h * GLA_DV, (h + 1) * GLA_DV)
        b_h = bcum[:, kc]
        b_last = b_h[q - 1:q, :]
        q_h = qk_ref[:, kc] * (GLA_DK ** -0.5)
        k_h = qk_ref[:, GLA_KEY_DIM + h * GLA_DK:GLA_KEY_DIM + (h + 1) * GLA_DK]
        v_h = v_ref[:, vc].astype(BF16)
        q_in = (q_h * jnp.exp(b_h)).astype(BF16)
        k_in = (k_h * jnp.exp(-b_h)).astype(BF16)
        att = jnp.where(causal, _dot_nt(q_in, k_in), 0.0).astype(BF16)
        s_h = get_state(h)
        o = _dot(att, v_h) + _dot(q_in, s_h.astype(BF16))
        k_end = (k_h * jnp.exp(b_last - b_h)).astype(BF16)
        d_col = _col_bcast(jnp.exp(b_last))
        put_state(h, s_h * jnp.concatenate([d_col, d_col], axis=1) + _dot_tn(k_end, v_h))
        ms = jnp.mean(o * o, axis=-1, keepdims=True)
        y_ref[:, vc] = o * lax.rsqrt(ms + EPS) * nw_ref[...] * r_ref[:, vc]


def _gla_prompt_kernel(qk_ref, v_ref, r_ref, alo_ref, wa_ref, ba_ref, nw_ref, y_ref, sout_ref, st):
    c = pl.program_id(1)

    @pl.when(c == 0)
    def _():
        st[...] = jnp.zeros(st.shape, F32)

    def put(h, val):
        st[h] = val

    _gla_chunk(qk_ref, v_ref, r_ref, alo_ref, wa_ref, ba_ref, nw_ref, y_ref, lambda h: st[h], put)

    @pl.when(c == pl.num_programs(1) - 1)
    def _():
        sout_ref[0] = st[...]


def _gla_prompt(proj, tail, n_seq, seq_len, wa, ba, nw):
    q = GLA_CHUNK
    nc = seq_len // q

    def col(blk):
        return pl.BlockSpec((q, D_MODEL), lambda b, c: (b * nc + c, blk))

    return pl.pallas_call(
        _gla_prompt_kernel,
        grid=(n_seq, nc),
        in_specs=[col(COL_QK), col(COL_V), col(COL_R),
                  pl.BlockSpec((q, LANES), lambda b, c: (b * nc + c, TAIL_ALO)),
                  _row_params(wa), _row_params(ba), _row_params(nw)],
        out_specs=[pl.BlockSpec((q, D_MODEL), lambda b, c: (b * nc + c, 0)),
                   pl.BlockSpec((1, GLA_HEADS, GLA_DK, GLA_DV), lambda b, c: (b, 0, 0, 0))],
        out_shape=[jax.ShapeDtypeStruct((n_seq * seq_len, D_MODEL), F32),
                   jax.ShapeDtypeStruct((n_seq, GLA_HEADS, GLA_DK, GLA_DV), F32)],
        scratch_shapes=[pltpu.VMEM((GLA_HEADS, GLA_DK, GLA_DV), F32)],
        compiler_params=_cparams(2),
        name="gla_prompt",
    )(proj, proj, proj, tail, wa, ba, nw)


def _gla_sample_kernel(qk_ref, v_ref, r_ref, alo_ref, s_ref, wa_ref, ba_ref, nw_ref, y_ref, sout_ref):
    n = s_ref.shape[0]
    q = v_ref.shape[0] // n
    for s in range(n):
        rows = pl.ds(s * q, q)

        def put(h, val, s=s):
            sout_ref[s, h] = val

        _gla_chunk(qk_ref.at[rows], v_ref.at[rows], r_ref.at[rows], alo_ref.at[rows], wa_ref, ba_ref,
                   nw_ref, y_ref.at[rows], lambda h, s=s: s_ref[s, h], put)


def _gla_sample(proj, tail, row0, n_seq, seq_len, s0, wa, ba, nw):
    nb = SAMPLE_SEQS
    q = seq_len
    blk0 = row0 // (nb * q)

    def col(blk):
        return pl.BlockSpec((nb * q, D_MODEL), lambda b: (blk0 + b, blk))

    state_spec = pl.BlockSpec((nb, GLA_HEADS, GLA_DK, GLA_DV), lambda b: (b, 0, 0, 0))
    return pl.pallas_call(
        _gla_sample_kernel,
        grid=(n_seq // nb,),
        in_specs=[col(COL_QK), col(COL_V), col(COL_R),
                  pl.BlockSpec((nb * q, LANES), lambda b: (blk0 + b, TAIL_ALO)),
                  state_spec, _row_params(wa), _row_params(ba), _row_params(nw)],
        out_specs=[pl.BlockSpec((nb * q, D_MODEL), lambda b: (b, 0)), state_spec],
        out_shape=[jax.ShapeDtypeStruct((n_seq * seq_len, D_MODEL), F32),
                   jax.ShapeDtypeStruct(s0.shape, F32)],
        compiler_params=_cparams(1),
        name="gla_sample",
    )(proj, proj, proj, tail, s0, wa, ba, nw)


def _layernorm(x, g, b):
    mu = jnp.mean(x, axis=-1, keepdims=True)
    xc = x - mu
    var = jnp.mean(xc * xc, axis=-1, keepdims=True)
    return xc * lax.rsqrt(var + EPS) * g + b


def _route(logits):
    lane = lax.broadcasted_iota(jnp.int32, logits.shape, 1)
    neg = -jnp.inf
    big = ROUTER_LANES
    glog = jnp.where(lane < MOE_GROUPS, logits, neg)
    gmax = jnp.max(glog, axis=-1, keepdims=True)
    g_sel = jnp.min(jnp.where(glog == gmax, lane, big), axis=-1, keepdims=True)
    p_g = 1.0 / jnp.sum(jnp.exp(glog - gmax), axis=-1, keepdims=True)
    e_lane = lane - MOE_GROUPS
    in_group = (e_lane >= 0) & (e_lane < MOE_EXPERTS) & (e_lane // MOE_EPG == g_sel)
    el = jnp.where(in_group, logits, neg)
    v1 = jnp.max(el, axis=-1, keepdims=True)
    i1 = jnp.min(jnp.where(el == v1, lane, big), axis=-1, keepdims=True)
    el2 = jnp.where(lane == i1, neg, el)
    v2 = jnp.max(el2, axis=-1, keepdims=True)
    i2 = jnp.min(jnp.where(el2 == v2, lane, big), axis=-1, keepdims=True)
    e2 = jnp.exp(v2 - v1)
    w1 = p_g / (1.0 + e2)
    w2 = p_g * e2 / (1.0 + e2)
    first = (i1 - MOE_GROUPS).astype(F32)
    second = (i2 - MOE_GROUPS).astype(F32)
    return jnp.where(lane == ROUTE_E1, first,
                     jnp.where(lane == ROUTE_E2, second,
                               jnp.where(lane == ROUTE_W1, w1,
                                         jnp.where(lane == ROUTE_W2, w2, 0.0))))


ROW_CHUNKS = D_MODEL // LANES


def _store_chunked(ref, val):
    rows = val.shape[0]
    for c in range(ROW_CHUNKS):
        ref[pl.ds(c, rows, stride=ROW_CHUNKS), :] = val[:, c * LANES:(c + 1) * LANES]


def _load_chunked(ref, c):
    return ref[pl.ds(c, ref.shape[0] // ROW_CHUNKS, stride=ROW_CHUNKS), :]


def _merge_kernel(n_first, ysa_ref, ysb_ref, yga_ref, ygb_ref, gs_ref, gg_ref, xa_ref, xb_ref, wo_ref,
                  g_ref, b_ref, wrh_ref, wrl_ref, br_ref, h1_ref, route_ref):
    def tile(ys_ref, yg_ref, x_ref):
        merged = gs_ref[...] * ys_ref[...] + gg_ref[...] * yg_ref[...]
        mix = _dot(merged.astype(BF16), wo_ref[...])
        h1 = _layernorm(DEEPNORM_ALPHA * x_ref[...] + mix, g_ref[...], b_ref[...])
        _store_chunked(h1_ref, h1)
        h_hi = h1.astype(BF16)
        h_lo = (h1 - h_hi.astype(F32)).astype(BF16)
        logits = (_dot(h_hi, wrh_ref[...]) + _dot(h_lo, wrh_ref[...]) + _dot(h_hi, wrl_ref[...])
                  + br_ref[...])
        route_ref[...] = _route(logits)

    first = pl.program_id(0) < n_first
    pl.when(first)(lambda: tile(ysa_ref, yga_ref, xa_ref))
    pl.when(jnp.logical_not(first))(lambda: tile(ysb_ref, ygb_ref, xb_ref))


def _merge(ys, yg, gates, xs, wo, g, b, wr_hi, wr_lo, br, tm):
    n_a, n_b = xs[0].shape[0] // tm, xs[1].shape[0] // tm
    m = (n_a + n_b) * tm

    def rows(i):
        return (i, 0)

    pair = [pl.BlockSpec((tm, D_MODEL), lambda i: (jnp.minimum(i, n_a - 1), 0)),
            pl.BlockSpec((tm, D_MODEL), lambda i: (jnp.maximum(i - n_a, 0), 0))]
    return pl.pallas_call(
        functools.partial(_merge_kernel, n_a),
        grid=(m // tm,),
        in_specs=[*pair, *pair,
                  pl.BlockSpec((tm, D_MODEL), lambda i: (i, COL_GS)),
                  pl.BlockSpec((tm, D_MODEL), lambda i: (i, COL_GG)),
                  *pair,
                  pl.BlockSpec(wo.shape, lambda i: (0, 0), pipeline_mode=pl.Buffered(1)),
                  _row_params(g), _row_params(b),
                  _row_params(wr_hi), _row_params(wr_lo), _row_params(br)],
        out_specs=[pl.BlockSpec((tm * ROW_CHUNKS, LANES), rows),
                   pl.BlockSpec((tm, ROUTER_LANES), rows)],
        out_shape=[jax.ShapeDtypeStruct((m * ROW_CHUNKS, LANES), F32),
                   jax.ShapeDtypeStruct((m, ROUTER_LANES), F32)],
        compiler_params=_cparams(1),
        name="merge_outproj_ln_router",
    )(*ys, *yg, gates, gates, *xs, wo, g, b, wr_hi, wr_lo, br)


def _zero_runs(pads_ref, zeros, x_hbm, sem, action):
    tm = zeros.shape[0] // ROW_CHUNKS

    def piece(pos, rows):
        dst = pl.multiple_of(pos * ROW_CHUNKS, ROW_CHUNKS)
        return pltpu.make_async_copy(zeros.at[pl.ds(0, rows * ROW_CHUNKS)],
                                     x_hbm.at[pl.ds(dst, rows * ROW_CHUNKS)], sem)

    def run(e, carry):
        pos, length = pads_ref[0, e], pads_ref[1, e]

        def whole(q, c):
            getattr(piece(pos + q * tm, tm), action)()
            return c

        lax.fori_loop(0, length // tm, whole, 0)
        pos = pos + (length // tm) * tm
        rows = tm // 2
        while rows >= 1:
            @pl.when((length & rows) != 0)
            def _(pos=pos, rows=rows):
                getattr(piece(pos, rows), action)()

            pos = pos + (length & rows)
            rows //= 2
        return carry

    lax.fori_loop(0, pads_ref.shape[1], run, 0)


def _dispatch_kernel(slot_ref, pads_ref, h_ref, x_hbm, zeros, sem, zsem):
    n = h_ref.shape[0] // ROW_CHUNKS

    def issue(r, carry):
        src = h_ref.at[pl.ds(pl.multiple_of(r * ROW_CHUNKS, ROW_CHUNKS), ROW_CHUNKS)]
        for k in range(2):
            dst = pl.multiple_of(slot_ref[k, r] * ROW_CHUNKS, ROW_CHUNKS)
            pltpu.make_async_copy(src, x_hbm.at[pl.ds(dst, ROW_CHUNKS)], sem).start(
                priority=k % DMA_QUEUES)
        return carry

    lax.fori_loop(0, n, issue, 0, unroll=4)

    @pl.when(pl.program_id(0) == pl.num_programs(0) - 1)
    def _():
        zeros[...] = jnp.zeros(zeros.shape, zeros.dtype)
        _zero_runs(pads_ref, zeros, x_hbm, zsem, "start")
        _zero_runs(pads_ref, zeros, x_hbm, zsem, "wait")

    for k in range(2):
        pltpu.make_async_copy(h_ref, x_hbm.at[pl.ds(0, n * ROW_CHUNKS)], sem).wait()


def _dispatch(h1, slot, pads, n_slots, tm, window):
    t = slot.shape[1]
    steps = t // window
    slot3 = slot.reshape(2, steps, window).transpose(1, 0, 2)
    return pl.pallas_call(
        _dispatch_kernel,
        grid=(steps,),
        in_specs=[pl.BlockSpec((None, 2, window), lambda i: (i, 0, 0), memory_space=pltpu.SMEM),
                  pl.BlockSpec(pads.shape, lambda i: (0, 0), memory_space=pltpu.SMEM),
                  pl.BlockSpec((window * ROW_CHUNKS, LANES), lambda i: (i, 0))],
        out_specs=pl.BlockSpec(memory_space=pl.ANY),
        out_shape=jax.ShapeDtypeStruct((n_slots * ROW_CHUNKS, LANES), h1.dtype),
        scratch_shapes=[pltpu.VMEM((tm * ROW_CHUNKS, LANES), h1.dtype), pltpu.SemaphoreType.DMA,
                        pltpu.SemaphoreType.DMA],
        compiler_params=_cparams(1),
        name="dispatch_rows",
    )(slot3, pads, h1)


def _expert_plan(route, tm):
    t = route.shape[0]
    ids = jnp.arange(MOE_EXPERTS, dtype=jnp.int32)
    e = jnp.concatenate([route[:, ROUTE_E1], route[:, ROUTE_E2]]).astype(jnp.int32)
    onehot = (e[:, None] == ids[None, :]).astype(jnp.int32)
    csum = jnp.cumsum(onehot, axis=0)
    rank = jnp.sum(onehot * csum, axis=1) - 1
    cnt = csum[-1]
    tiles_e = (cnt + tm - 1) // tm
    tile_end = jnp.cumsum(tiles_e)
    tile_start = tile_end - tiles_e
    slot = tile_start[e] * tm + rank
    n_tiles = (2 * t) // tm + MOE_EXPERTS
    tid = jnp.arange(n_tiles, dtype=jnp.int32)
    used = tid < tile_end[-1]
    tile_e = jnp.sum(tile_end[None, :] <= jnp.minimum(tid, tile_end[-1] - 1)[:, None], axis=1)
    nvalid = jnp.where(used, jnp.clip(cnt[tile_e] - (tid - tile_start[tile_e]) * tm, 0, tm), 0)
    first = used & (tid == tile_start[tile_e])
    has = tiles_e > 0
    ordinal = jnp.cumsum(has.astype(jnp.int32)) - 1
    later = jnp.where((ids[None, :] > ids[:, None]) & has[None, :], ids[None, :], MOE_EXPERTS)
    nxt_e = jnp.min(later, axis=1)
    nxt_e = jnp.where(nxt_e == MOE_EXPERTS, -1, nxt_e)
    i32 = lambda a: a.astype(jnp.int32)
    pads = jnp.stack([jnp.append(tile_start * tm + cnt, tile_end[-1] * tm),
                      jnp.append(tiles_e * tm - cnt, (n_tiles - tile_end[-1]) * tm)])
    return (slot.reshape(2, t), i32(pads), n_tiles * tm, i32(tile_e), i32(nvalid), i32(first),
            i32(nxt_e[tile_e]), i32(ordinal[tile_e] % 2))


def _experts_kernel(te_ref, nv_ref, first_ref, nxt_ref, par_ref, x_ref, wg_hbm, wu_hbm, wd_hbm, y_ref,
                    wg_f, wu_f, wd_f, wg_b, wu_b, wd_b, x_b, sem):
    i = pl.program_id(0)
    s = par_ref[i]

    def weight_copies(e, slot):
        return (pltpu.make_async_copy(wg_hbm.at[e], wg_f.at[slot], sem.at[slot]),
                pltpu.make_async_copy(wu_hbm.at[e], wu_f.at[slot], sem.at[slot]),
                pltpu.make_async_copy(wd_hbm.at[e], wd_f.at[slot], sem.at[slot]))

    @pl.when(i == 0)
    def _():
        for cp in weight_copies(te_ref[0], par_ref[0]):
            cp.start()

    @pl.when(first_ref[i] == 1)
    def _():
        @pl.when(nxt_ref[i] >= 0)
        def _():
            for cp in weight_copies(nxt_ref[i], 1 - s):
                cp.start()

        for cp in weight_copies(te_ref[i], s):
            cp.wait()
        wg_b[...] = wg_f[s].astype(BF16)
        wu_b[...] = wu_f[s].astype(BF16)
        wd_b[...] = wd_f[s].astype(BF16)

    nv = nv_ref[i]

    @pl.when(nv > 0)
    def _():
        for c in range(ROW_CHUNKS):
            x_b[:, c * LANES:(c + 1) * LANES] = _load_chunked(x_ref, c).astype(BF16)
        x = x_b[...]
        hid = (_silu(_dot(x, wg_b[...])) * _dot(x, wu_b[...])).astype(BF16)
        _store_chunked(y_ref, _dot(hid, wd_b[...]))

    @pl.when(nv == 0)
    def _():
        y_ref[...] = jnp.zeros(y_ref.shape, y_ref.dtype)


def _experts(xs, tile_e, nvalid, first, nxt, par, wg, wu, wd, tm):
    n_tiles = xs.shape[0] // (tm * ROW_CHUNKS)
    rows = pl.BlockSpec((tm * ROW_CHUNKS, LANES), lambda i, *_: (i, 0))
    hbm = pl.BlockSpec(memory_space=pl.ANY)
    grid_spec = pltpu.PrefetchScalarGridSpec(
        num_scalar_prefetch=5,
        grid=(n_tiles,),
        in_specs=[rows, hbm, hbm, hbm],
        out_specs=rows,
        scratch_shapes=[pltpu.VMEM((2, D_MODEL, MOE_FF), F32), pltpu.VMEM((2, D_MODEL, MOE_FF), F32),
                        pltpu.VMEM((2, MOE_FF, D_MODEL), F32),
                        pltpu.VMEM((D_MODEL, MOE_FF), BF16), pltpu.VMEM((D_MODEL, MOE_FF), BF16),
                        pltpu.VMEM((MOE_FF, D_MODEL), BF16), pltpu.VMEM((tm, D_MODEL), BF16),
                        pltpu.SemaphoreType.DMA((2,))])
    return pl.pallas_call(
        _experts_kernel,
        grid_spec=grid_spec,
        out_shape=jax.ShapeDtypeStruct(xs.shape, F32),
        compiler_params=_cparams(1),
        name="experts",
    )(tile_e, nvalid, first, nxt, par, xs, wg, wu, wd)


def _combine_kernel(n_steps, slot_ref, slot_next_ref, route_ref, h1_ref, g_ref, b_ref, y_hbm, o_ref,
                    gbuf, sem):
    i = pl.program_id(0)
    tm = route_ref.shape[0]

    def request(idx_ref, parity):
        def body(r, carry):
            dst = pl.ds(pl.multiple_of(r * ROW_CHUNKS, ROW_CHUNKS), ROW_CHUNKS)
            for k in range(2):
                src = pl.multiple_of(idx_ref[k, r] * ROW_CHUNKS, ROW_CHUNKS)
                pltpu.make_async_copy(y_hbm.at[pl.ds(src, ROW_CHUNKS)], gbuf.at[parity, k, dst],
                                      sem.at[parity]).start(priority=k % DMA_QUEUES)
            return carry

        lax.fori_loop(0, tm, body, 0, unroll=4)

    @pl.when(i == 0)
    def _():
        request(slot_ref, 0)

    @pl.when(i + 1 < n_steps)
    def _():
        request(slot_next_ref, (i + 1) % 2)

    parity = i % 2
    for k in range(2):
        pltpu.make_async_copy(y_hbm.at[pl.ds(0, tm * ROW_CHUNKS)], gbuf.at[parity, k],
                              sem.at[parity]).wait()
    route = route_ref[...]
    w1 = route[:, ROUTE_W1:ROUTE_W1 + 1]
    w2 = route[:, ROUTE_W2:ROUTE_W2 + 1]
    pre = jnp.concatenate(
        [DEEPNORM_ALPHA * _load_chunked(h1_ref, c)
         + w1 * _load_chunked(gbuf.at[parity, 0], c) + w2 * _load_chunked(gbuf.at[parity, 1], c)
         for c in range(ROW_CHUNKS)], axis=1)
    o_ref[...] = _layernorm(pre, g_ref[...], b_ref[...])


def _combine(y_sorted, slot, route, h1, row0, m, g, b, tm):
    t = slot.shape[1]
    blk0 = row0 // tm
    last = (row0 + m) // tm - 1
    slot3 = slot.reshape(2, t // tm, tm).transpose(1, 0, 2)
    return pl.pallas_call(
        functools.partial(_combine_kernel, m // tm),
        grid=(m // tm,),
        in_specs=[pl.BlockSpec((None, 2, tm), lambda i: (blk0 + i, 0, 0), memory_space=pltpu.SMEM),
                  pl.BlockSpec((None, 2, tm), lambda i: (jnp.minimum(blk0 + i + 1, last), 0, 0),
                               memory_space=pltpu.SMEM),
                  pl.BlockSpec((tm, ROUTER_LANES), lambda i: (blk0 + i, 0)),
                  pl.BlockSpec((tm * ROW_CHUNKS, LANES), lambda i: (blk0 + i, 0)),
                  _row_params(g), _row_params(b),
                  pl.BlockSpec(memory_space=pl.ANY)],
        out_specs=pl.BlockSpec((tm, D_MODEL), lambda i: (i, 0)),
        out_shape=jax.ShapeDtypeStruct((m, D_MODEL), F32),
        scratch_shapes=[pltpu.VMEM((2, 2, tm * ROW_CHUNKS, LANES), F32),
                        pltpu.SemaphoreType.DMA((2,))],
        compiler_params=_cparams(1),
        name="combine_ln",
    )(slot3, slot3, route, h1, g, b, y_sorted)


def _pad_lanes(v, width=LANES):
    v = v.reshape(1, -1)
    return jnp.pad(v, ((0, 0), (0, width - v.shape[1])))


def kernel(x_prompt, x_sample, state_ssd, state_conv, state_gla, w_in, conv_w, conv_b, dt_bias, a_log, d_skip, ssd_norm_w, gla_w_a2, gla_b_a, gla_norm_w, w_out, ln1_g, ln1_b, w_router_group, b_router_group, w_router_expert, b_router_expert, w_gate, w_up, w_down, ln2_g, ln2_b):
    assert w_in.shape[0] == DEPTH == 1
    n_p, len_p, _ = x_prompt.shape
    n_s, len_s, _ = x_sample.shape
    rows_p, rows_s = n_p * len_p, n_s * len_s
    xp = x_prompt.reshape(rows_p, D_MODEL)
    xs = x_sample.reshape(rows_s, D_MODEL)

    offs = [0]
    for s in IN_SPLIT_SIZES:
        offs.append(offs[-1] + s)
    w_t = w_in[0].T
    w_tail_t = jnp.concatenate(
        [jnp.pad(w_t[offs[2]:offs[3]], ((0, LANES - SSD_HEADS), (0, 0))),
         jnp.pad(w_t[offs[7]:offs[8]], ((0, LANES - GLA_RANK), (0, 0)))], axis=0)
    x_b = (xp.astype(BF16), xs.astype(BF16))

    cw, cb = conv_w[0], conv_b[0].reshape(1, -1)
    per_col = D_MODEL // IN_PROJ_COLS
    acts = ((COL_Z * per_col, (COL_Z + 1) * per_col, "silu"),
            (COL_XS * per_col, (COL_BC + 1) * per_col, "conv"),
            (COL_R * per_col, (COL_R + 1) * per_col, "silu"),
            (COL_GS * per_col, (COL_GG + 1) * per_col, "sigmoid"))
    proj, conv_tail = _in_proj(x_b, w_t, ((0, offs[2]), (offs[3], offs[7] - offs[3]),
                                          (offs[8], offs[10] - offs[8])), acts, len_p, cw, cb)
    conv_p = conv_tail.reshape(n_p, HALO_ROWS, SSD_CONV_DIM)[:, HALO_ROWS - (SSD_CONV - 1):]
    tail = _matmul_nt(x_b, w_tail_t, IN_PROJ_ROWS)

    dtb, alog = _pad_lanes(dt_bias[0]), _pad_lanes(a_log[0])
    dskip_e = jnp.repeat(d_skip[0], SSD_HEAD_DIM).reshape(1, -1)
    nw_ssd = ssd_norm_w[0].reshape(1, -1)
    wa, ba, nw_gla = gla_w_a2[0].astype(BF16), gla_b_a[0].reshape(1, -1), gla_norm_w[0].reshape(1, -1)

    ys_p, ssd_p = _ssd_prompt(proj, tail, n_p, len_p, dtb, alog, dskip_e, nw_ssd)
    ys_s, ssd_s, conv_s = _ssd_sample(proj, tail, rows_p, n_s, len_s, state_ssd[0], state_conv[0],
                                      cw, cb, dtb, alog, dskip_e, nw_ssd)
    yg_p, gla_p = _gla_prompt(proj, tail, n_p, len_p, wa, ba, nw_gla)
    yg_s, gla_s = _gla_sample(proj, tail, rows_p, n_s, len_s, state_gla[0], wa, ba, nw_gla)

    wo = w_out[0].astype(BF16)
    w_r = jnp.pad(jnp.concatenate([w_router_group[0], w_router_expert[0]], axis=1),
                  ((0, 0), (0, ROUTER_LANES - MOE_GROUPS - MOE_EXPERTS)))
    wr_hi = w_r.astype(BF16)
    wr_lo = (w_r - wr_hi.astype(F32)).astype(BF16)
    b_r = _pad_lanes(jnp.concatenate([b_router_group[0], b_router_expert[0]]), ROUTER_LANES)
    g1, b1 = ln1_g[0].reshape(1, -1), ln1_b[0].reshape(1, -1)
    g2, b2 = ln2_g[0].reshape(1, -1), ln2_b[0].reshape(1, -1)

    h1, route = _merge((ys_p, ys_s), (yg_p, yg_s), proj, (xp, xs), wo, g1, b1, wr_hi, wr_lo, b_r,
                       MERGE_ROWS)
    slot, pads, n_slots, tile_e, nvalid, first, nxt, par = _expert_plan(route, EXPERT_ROWS)
    x_sorted = _dispatch(h1, slot, pads, n_slots, EXPERT_ROWS, DISPATCH_WINDOW)
    y_sorted = _experts(x_sorted, tile_e, nvalid, first, nxt, par, w_gate[0], w_up[0], w_down[0],
                        EXPERT_ROWS)
    outs = [_combine(y_sorted, slot, route, h1, 0, rows_p, g2, b2, MERGE_ROWS),
            _combine(y_sorted, slot, route, h1, rows_p, rows_s, g2, b2, MERGE_ROWS)]
    y_p = outs[0].reshape(x_prompt.shape)
    y_s = outs[1].reshape(x_sample.shape)
    return (y_p, y_s, ssd_p[None], conv_p[None], gla_p[None], ssd_s[None], conv_s[None], gla_s[None])
```

```python
import functools
import math

import jax
import jax.numpy as jnp
from jax import lax
from jax.experimental import pallas as pl
from jax.experimental.pallas import tpu as pltpu

F32 = jnp.float32
BF16 = jnp.bfloat16
HIGHEST = lax.Precision.HIGHEST

D_MODEL = 2048
SSD_HEADS = 32
SSD_HEAD_DIM = 64
SSD_GROUPS = 8
SSD_STATE = 128
SSD_CONV = 4
SSD_CHUNK = 128
SSD_BC = SSD_GROUPS * SSD_STATE
SSD_CONV_DIM = D_MODEL + 2 * SSD_BC
SSD_GROUP_COLS = D_MODEL // SSD_GROUPS
HEADS_PER_GROUP = SSD_HEADS // SSD_GROUPS
GLA_HEADS = 8
GLA_DK = 128
GLA_DV = 256
GLA_KEY_DIM = GLA_HEADS * GLA_DK
GLA_RANK = 16
GLA_TAU = 16.0
GLA_CHUNK = 64
MOE_GROUPS = 4
MOE_EPG = 8
MOE_EXPERTS = 32
MOE_FF = 512
DEPTH = 1
DEEPNORM_ALPHA = (2.0 * DEPTH) ** 0.25
EPS = 1e-5
IN_SPLIT_SIZES = (D_MODEL, SSD_CONV_DIM, SSD_HEADS, GLA_KEY_DIM, GLA_KEY_DIM, D_MODEL, D_MODEL,
                  GLA_RANK, D_MODEL, D_MODEL)

LANES = 128
HALO_ROWS = 8
COL_Z, COL_R = 0, 1
COL_XS, COL_BC = 0, 1
COL_QK, COL_V = 0, 1
COL_GS, COL_GG = 0, 1
TAIL_DT, TAIL_ALO = 0, 1
ROUTER_LANES = 128
ROUTE_E1, ROUTE_E2, ROUTE_W1, ROUTE_W2 = 0, 1, 2, 3
MERGE_ROWS = 256
EXPERT_ROWS = 256
DISPATCH_WINDOW = 512
DMA_QUEUES = 2
VMEM_LIMIT = 56 * 1024 * 1024


def _cparams(n_axes):
    return pltpu.CompilerParams(dimension_semantics=("arbitrary",) * n_axes,
                                vmem_limit_bytes=VMEM_LIMIT)


def _silu(x):
    return x * jax.nn.sigmoid(x)


def _softplus(x):
    return jnp.maximum(x, 0.0) + jnp.log1p(jnp.exp(-jnp.abs(x)))


def _tril(n):
    r = lax.broadcasted_iota(jnp.int32, (n, n), 0)
    c = lax.broadcasted_iota(jnp.int32, (n, n), 1)
    return r >= c


def _expand_group(p, g, lane_head):
    h0 = g * HEADS_PER_GROUP
    out = p[:, h0 + HEADS_PER_GROUP - 1:h0 + HEADS_PER_GROUP]
    for hh in range(HEADS_PER_GROUP - 2, -1, -1):
        out = jnp.where(lane_head == hh, p[:, h0 + hh:h0 + hh + 1], out)
    return out


def _split3(x):
    hi = x.astype(BF16)
    r = x - hi.astype(F32)
    mid = r.astype(BF16)
    lo = (r - mid.astype(F32)).astype(BF16)
    return hi, mid, lo


def _dot_sel(sel, x):
    hi, mid, lo = _split3(x)
    return _dot(sel, lo) + _dot(sel, mid) + _dot(sel, hi)


def _col_bcast(row):
    return jnp.broadcast_to(row, (LANES, LANES)).T


def _dot(a, b, **kw):
    return jnp.dot(a, b, preferred_element_type=F32, **kw)


def _dot_nt(a, b, **kw):
    return lax.dot_general(a, b, (((1,), (1,)), ((), ())), preferred_element_type=F32, **kw)


def _dot_tn(a, b, **kw):
    return lax.dot_general(a, b, (((0,), (0,)), ((), ())), preferred_element_type=F32, **kw)


def _matmul_kernel(n_first, xa_ref, xb_ref, wt_ref, o_ref):
    x = jnp.where(pl.program_id(0) < n_first, xa_ref[...], xb_ref[...])
    o_ref[...] = _dot_nt(x, wt_ref[...].astype(BF16))


def _matmul_nt(xs, wt, tm):
    n, k = wt.shape
    n_a, n_b = xs[0].shape[0] // tm, xs[1].shape[0] // tm
    return pl.pallas_call(
        functools.partial(_matmul_kernel, n_a),
        grid=(n_a + n_b,),
        in_specs=[pl.BlockSpec((tm, k), lambda i: (jnp.minimum(i, n_a - 1), 0)),
                  pl.BlockSpec((tm, k), lambda i: (jnp.maximum(i - n_a, 0), 0)),
                  pl.BlockSpec((n, k), lambda i: (0, 0))],
        out_specs=pl.BlockSpec((tm, n), lambda i: (i, 0)),
        out_shape=jax.ShapeDtypeStruct(((n_a + n_b) * tm, n), F32),
        compiler_params=_cparams(1),
        name="in_proj_tail",
    )(*xs, wt)


IN_PROJ_COLS = 1024
IN_PROJ_ROWS = 1024
CAST_ROWS = 256
SUBLANES = 8


EPILOGUE_ROWS = 256
EPILOGUE_COLS = 256


def _conv_rows(r, raw, col0, cw_ref, cb_ref, o_ref, cbuf):
    tn = o_ref.shape[1]
    cbuf[HALO_ROWS + r:HALO_ROWS + r + EPILOGUE_ROWS, :] = raw
    for c in range(0, tn, EPILOGUE_COLS):
        lanes = pl.ds(pl.multiple_of(col0 + c, EPILOGUE_COLS), EPILOGUE_COLS)
        w = cw_ref[:, lanes]
        acc = cb_ref[:, lanes]
        for t in range(SSD_CONV):
            r0 = HALO_ROWS - (SSD_CONV - 1) + t + r
            acc = acc + w[t:t + 1, :] * cbuf[r0:r0 + EPILOGUE_ROWS, c:c + EPILOGUE_COLS]
        o_ref[r:r + EPILOGUE_ROWS, c:c + EPILOGUE_COLS] = _silu(acc)


def _in_proj_kernel(plan, kind, n_first, tiles_per_seq, xa_ref, xb_ref, *refs):
    if kind == "conv":
        cw_ref, cb_ref, wt_hbm, o_ref, tail_hbm, wf, wb, sem, cbuf, tsem = refs
    else:
        wt_hbm, o_ref, wf, wb, sem = refs
    j = pl.program_id(0)
    i = pl.program_id(1)
    tm, tn = o_ref.shape

    def block_copy(jj, slot):
        shift = 0
        for lo, hi, sh in plan:
            shift = jnp.where((jj >= lo) & (jj < hi), sh, shift)
        src = pl.multiple_of(jj * tn + shift, SUBLANES)
        return pltpu.make_async_copy(wt_hbm.at[pl.ds(src, tn)], wf.at[slot], sem.at[slot])

    @pl.when(i == 0)
    def _():
        slot = j % 2

        @pl.when(j == 0)
        def _():
            block_copy(j, slot).start()

        @pl.when(j + 1 < plan[-1][1])
        def _():
            block_copy(j + 1, 1 - slot).start()

        block_copy(j, slot).wait()
        for r in range(0, tn, CAST_ROWS):
            wb[r:r + CAST_ROWS, :] = wf[slot, r:r + CAST_ROWS, :].astype(BF16)

    def chunks(x_ref, finish):
        for r in range(0, tm, EPILOGUE_ROWS):
            finish(r, _dot_nt(x_ref[r:r + EPILOGUE_ROWS, :], wb[...]))

    def store(fn):
        def finish(r, acc):
            o_ref[r:r + EPILOGUE_ROWS, :] = fn(acc)
        return finish

    first = i < n_first
    if kind != "conv":
        fn = {"none": lambda a: a, "silu": _silu, "sigmoid": jax.nn.sigmoid}[kind]
        pl.when(first)(lambda: chunks(xa_ref, store(fn)))
        pl.when(jnp.logical_not(first))(lambda: chunks(xb_ref, store(fn)))
        return

    pl.when(jnp.logical_not(first))(lambda: chunks(xb_ref, store(lambda a: a)))

    @pl.when(first)
    def _():
        col0 = j * tn

        @pl.when(i % tiles_per_seq == 0)
        def _():
            cbuf[0:HALO_ROWS, :] = jnp.zeros((HALO_ROWS, tn), F32)

        chunks(xa_ref, lambda r, acc: _conv_rows(r, acc, col0, cw_ref, cb_ref, o_ref, cbuf))

        @pl.when((i + 1) % tiles_per_seq == 0)
        def _():
            seq = i // tiles_per_seq
            cp = pltpu.make_async_copy(
                cbuf.at[pl.ds(tm, HALO_ROWS)],
                tail_hbm.at[pl.ds(pl.multiple_of(seq * HALO_ROWS, HALO_ROWS), HALO_ROWS),
                            pl.ds(pl.multiple_of(col0, tn), tn)], tsem)
            cp.start()
            cp.wait()

        cbuf[0:HALO_ROWS, :] = cbuf[tm:tm + HALO_ROWS, :]


def _in_proj(xs, wt, segments, kind, seq_len=None, cw=None, cb=None):
    tm, tn = IN_PROJ_ROWS, IN_PROJ_COLS
    k = wt.shape[1]
    n_a, n_b = xs[0].shape[0] // tm, xs[1].shape[0] // tm
    plan, dest = [], 0
    for src, count in segments:
        assert count % tn == 0 and dest % tn == 0 and (src - dest) % SUBLANES == 0
        plan.append((dest // tn, (dest + count) // tn, src - dest))
        dest += count
    x_specs = [pl.BlockSpec((tm, k), lambda j, i: (jnp.minimum(i, n_a - 1), 0)),
               pl.BlockSpec((tm, k), lambda j, i: (jnp.maximum(i - n_a, 0), 0))]
    hbm = pl.BlockSpec(memory_space=pl.ANY)
    out_spec = pl.BlockSpec((tm, tn), lambda j, i: (i, j))
    out_shape = jax.ShapeDtypeStruct(((n_a + n_b) * tm, dest), F32)
    scratch = [pltpu.VMEM((2, tn, k), F32), pltpu.VMEM((tn, k), BF16), pltpu.SemaphoreType.DMA((2,))]
    if kind == "conv":
        assert seq_len % tm == 0 and cw.shape[1] == dest
        n_seq = xs[0].shape[0] // seq_len
        return pl.pallas_call(
            functools.partial(_in_proj_kernel, tuple(plan), kind, n_a, seq_len // tm),
            grid=(dest // tn, n_a + n_b),
            in_specs=[*x_specs, pl.BlockSpec(cw.shape, lambda j, i: (0, 0)),
                      pl.BlockSpec(cb.shape, lambda j, i: (0, 0)), hbm],
            out_specs=[out_spec, hbm],
            out_shape=[out_shape, jax.ShapeDtypeStruct((n_seq * HALO_ROWS, dest), F32)],
            scratch_shapes=[*scratch, pltpu.VMEM((HALO_ROWS + tm, tn), F32), pltpu.SemaphoreType.DMA],
            compiler_params=_cparams(2),
            name="in_proj_conv",
        )(*xs, cw, cb, wt)
    return pl.pallas_call(
        functools.partial(_in_proj_kernel, tuple(plan), kind, n_a, None),
        grid=(dest // tn, n_a + n_b),
        in_specs=[*x_specs, hbm],
        out_specs=out_spec,
        out_shape=out_shape,
        scratch_shapes=scratch,
        compiler_params=_cparams(2),
        name="in_proj_" + kind,
    )(*xs, wt)


def _conv_silu(ubuf, cw_ref, cb_ref, lo, width, rows):
    acc = cb_ref[:, lo:lo + width]
    for i in range(SSD_CONV):
        r0 = HALO_ROWS - (SSD_CONV - 1) + i
        acc = acc + cw_ref[i:i + 1, lo:lo + width] * ubuf[r0:r0 + rows, lo:lo + width]
    return _silu(acc)


def _gated_group_norm(y_g, gate_g, nw_g):
    yg = y_g * gate_g
    ms = jnp.mean(yg * yg, axis=-1, keepdims=True)
    return yg * lax.rsqrt(ms + EPS) * nw_g


def _ssd_prompt_kernel(z_ref, xs_ref, bc_ref, tail_ref, dtb_ref, alog_ref, dskip_ref, nw_ref, y_ref,
                       hout_ref, ht):
    q = SSD_CHUNK
    c = pl.program_id(1)

    @pl.when(c == 0)
    def _():
        ht[...] = jnp.zeros(ht.shape, F32)

    dt = _softplus(tail_ref[...] + dtb_ref[...])
    da = dt * (-jnp.exp(alog_ref[...]))
    causal = _tril(q)
    acum = _dot_sel(causal.astype(BF16), da)
    acum_t = acum.T
    dt_t = dt.T
    a_last = acum[q - 1:q, :]
    decay_in = jnp.exp(acum)
    w_end = dt * jnp.exp(a_last - acum)
    chunk_decay = jnp.exp(a_last)
    lane_head = lax.broadcasted_iota(jnp.int32, (q, SSD_GROUP_COLS), 1) >> 6
    assert SSD_HEAD_DIM == 1 << 6

    for g in range(SSD_GROUPS):
        lo = g * SSD_GROUP_COLS
        cols = slice(lo, lo + SSD_GROUP_COLS)
        xs_g = xs_ref[:, cols]
        b_g = bc_ref[:, g * SSD_STATE:(g + 1) * SSD_STATE].astype(BF16)
        c_g = bc_ref[:, SSD_BC + g * SSD_STATE:SSD_BC + (g + 1) * SSD_STATE].astype(BF16)
        scores = _dot_nt(c_g, b_g)
        h_g = ht[g]
        xs_b = xs_g.astype(BF16)
        m_heads, x_heads = [], []
        for hh in range(HEADS_PER_GROUP):
            h = g * HEADS_PER_GROUP + hh
            seg = acum[:, h:h + 1] - acum_t[h:h + 1, :]
            decay = jnp.exp(jnp.where(causal, seg, -jnp.inf))
            m_heads.append((scores * decay * dt_t[h:h + 1, :]).astype(BF16))
            x_heads.append(jnp.where(lane_head == hh, xs_b, jnp.zeros_like(xs_b)))
        y_g = (_dot(jnp.concatenate(m_heads, axis=1), jnp.concatenate(x_heads, axis=0))
               + _dot(c_g, h_g.astype(BF16)) * _expand_group(decay_in, g, lane_head)
               + dskip_ref[:, cols] * xs_g)
        y_ref[:, cols] = _gated_group_norm(y_g, z_ref[:, cols], nw_ref[:, cols])
        xw = (xs_g * _expand_group(w_end, g, lane_head)).astype(BF16)
        ht[g] = h_g * _expand_group(chunk_decay, g, lane_head[0:1]) + _dot_tn(b_g, xw)

    @pl.when(c == pl.num_programs(1) - 1)
    def _():
        for g in range(SSD_GROUPS):
            h_t = ht[g].T
            for hh in range(HEADS_PER_GROUP):
                hout_ref[0, g * HEADS_PER_GROUP + hh] = h_t[hh * SSD_HEAD_DIM:(hh + 1) * SSD_HEAD_DIM]


def _row_params(p):
    return pl.BlockSpec(p.shape, lambda *_: (0,) * p.ndim)


def _ssd_prompt(zr, xbc, tail, n_seq, seq_len, dtb, alog, dskip_e, nw):
    q = SSD_CHUNK
    nc = seq_len // q

    def col(blk):
        return pl.BlockSpec((q, D_MODEL), lambda b, c: (b * nc + c, blk))

    return pl.pallas_call(
        _ssd_prompt_kernel,
        grid=(n_seq, nc),
        in_specs=[col(COL_Z), col(COL_XS), col(COL_BC),
                  pl.BlockSpec((q, LANES), lambda b, c: (b * nc + c, TAIL_DT)),
                  _row_params(dtb), _row_params(alog), _row_params(dskip_e), _row_params(nw)],
        out_specs=[pl.BlockSpec((q, D_MODEL), lambda b, c: (b * nc + c, 0)),
                   pl.BlockSpec((1, SSD_HEADS, SSD_HEAD_DIM, SSD_STATE), lambda b, c: (b, 0, 0, 0))],
        out_shape=[jax.ShapeDtypeStruct((n_seq * seq_len, D_MODEL), F32),
                   jax.ShapeDtypeStruct((n_seq, SSD_HEADS, SSD_HEAD_DIM, SSD_STATE), F32)],
        scratch_shapes=[pltpu.VMEM((SSD_GROUPS, SSD_STATE, SSD_GROUP_COLS), F32)],
        compiler_params=_cparams(2),
        name="ssd_prompt",
    )(zr, xbc, xbc, tail, dtb, alog, dskip_e, nw)


SAMPLE_SEQS = 4


def _dot_tn_split(a, b):
    a_hi = a.astype(BF16)
    a_lo = (a - a_hi.astype(F32)).astype(BF16)
    b_hi = b.astype(BF16)
    b_lo = (b - b_hi.astype(F32)).astype(BF16)
    return _dot_tn(a_lo, b_hi) + _dot_tn(a_hi, b_lo) + _dot_tn(a_hi, b_hi)


def _ssd_sample_seq(z_ref, xs_ref, bc_ref, tail_ref, h_ref, conv_ref, cw_ref, cb_ref, dtb_ref,
                    alog_ref, dskip_ref, nw_ref, y_ref, hout_ref, convout_ref, ubuf, decay_rows):
    q = xs_ref.shape[0]
    ubuf[0:HALO_ROWS - (SSD_CONV - 1), :] = jnp.zeros((HALO_ROWS - (SSD_CONV - 1), SSD_CONV_DIM), F32)
    ubuf[HALO_ROWS - (SSD_CONV - 1):HALO_ROWS, :] = conv_ref[...]
    ubuf[HALO_ROWS:HALO_ROWS + q, 0:D_MODEL] = xs_ref[...]
    ubuf[HALO_ROWS:HALO_ROWS + q, D_MODEL:SSD_CONV_DIM] = bc_ref[...]
    convout_ref[...] = ubuf[HALO_ROWS + q - (SSD_CONV - 1):HALO_ROWS + q, :]

    dt = _softplus(tail_ref[...] + dtb_ref[...])
    da = dt * (-jnp.exp(alog_ref[...]))
    acum = _dot_sel(_tril(q).astype(BF16), da)
    a_last = acum[q - 1:q, :]
    decay_in = jnp.exp(acum)
    w_end = dt * jnp.exp(a_last - acum)
    decay_rows[...] = _col_bcast(jnp.exp(a_last))[:SSD_HEADS]
    row = lax.broadcasted_iota(jnp.int32, (q, SSD_GROUP_COLS), 0)
    lane_head = lax.broadcasted_iota(jnp.int32, (q, SSD_GROUP_COLS), 1) >> 6

    for g in range(SSD_GROUPS):
        lo = g * SSD_GROUP_COLS
        cols = slice(lo, lo + SSD_GROUP_COLS)
        xs_g = _conv_silu(ubuf, cw_ref, cb_ref, lo, SSD_GROUP_COLS, q)
        b_g = _conv_silu(ubuf, cw_ref, cb_ref, D_MODEL + g * SSD_STATE, SSD_STATE, q)
        c_g = _conv_silu(ubuf, cw_ref, cb_ref, D_MODEL + SSD_BC + g * SSD_STATE, SSD_STATE, q)
        h_g = h_ref[g * HEADS_PER_GROUP:(g + 1) * HEADS_PER_GROUP].reshape(
            SSD_GROUP_COLS, SSD_STATE)
        y_g = (_dot_nt(c_g.astype(BF16), h_g.astype(BF16)) * _expand_group(decay_in, g, lane_head)
               + dskip_ref[:, cols] * xs_g)
        acum_g = _expand_group(acum, g, lane_head)
        dt_g = _expand_group(dt, g, lane_head)
        for j in range(q):
            s_j = jnp.sum(c_g * b_g[j:j + 1, :], axis=-1, keepdims=True)
            coef = s_j * jnp.exp(acum_g - acum_g[j:j + 1, :]) * dt_g[j:j + 1, :]
            y_g = y_g + jnp.where(row >= j, coef, 0.0) * xs_g[j:j + 1, :]
        y_ref[:, cols] = _gated_group_norm(y_g, z_ref[:, cols], nw_ref[:, cols])
        upd = _dot_tn_split(xs_g * _expand_group(w_end, g, lane_head), b_g)
        for hh in range(HEADS_PER_GROUP):
            h = g * HEADS_PER_GROUP + hh
            rows = slice(hh * SSD_HEAD_DIM, (hh + 1) * SSD_HEAD_DIM)
            hout_ref[h] = h_g[rows] * decay_rows[h:h + 1, :] + upd[rows]


def _ssd_sample_kernel(z_ref, xs_ref, bc_ref, tail_ref, h_ref, conv_ref, cw_ref, cb_ref, dtb_ref,
                       alog_ref, dskip_ref, nw_ref, y_ref, hout_ref, convout_ref, ubuf, decay_rows):
    n = h_ref.shape[0]
    q = xs_ref.shape[0] // n
    for s in range(n):
        rows = pl.ds(s * q, q)
        _ssd_sample_seq(z_ref.at[rows], xs_ref.at[rows], bc_ref.at[rows], tail_ref.at[rows],
                        h_ref.at[s], conv_ref.at[s], cw_ref, cb_ref, dtb_ref, alog_ref, dskip_ref,
                        nw_ref, y_ref.at[rows], hout_ref.at[s], convout_ref.at[s], ubuf.at[s],
                        decay_rows.at[s])


def _ssd_sample(zr, xbc, tail, row0, n_seq, seq_len, h0, conv0, cw, cb, dtb, alog, dskip_e, nw):
    nb = SAMPLE_SEQS
    q = seq_len
    blk0 = row0 // (nb * q)

    def col(blk):
        return pl.BlockSpec((nb * q, D_MODEL), lambda b: (blk0 + b, blk))

    state_spec = pl.BlockSpec((nb, SSD_HEADS, SSD_HEAD_DIM, SSD_STATE), lambda b: (b, 0, 0, 0))
    conv_spec = pl.BlockSpec((nb, SSD_CONV - 1, SSD_CONV_DIM), lambda b: (b, 0, 0))
    return pl.pallas_call(
        _ssd_sample_kernel,
        grid=(n_seq // nb,),
        in_specs=[col(COL_Z), col(COL_XS), col(COL_BC),
                  pl.BlockSpec((nb * q, LANES), lambda b: (blk0 + b, TAIL_DT)),
                  state_spec, conv_spec,
                  _row_params(cw), _row_params(cb), _row_params(dtb), _row_params(alog),
                  _row_params(dskip_e), _row_params(nw)],
        out_specs=[pl.BlockSpec((nb * q, D_MODEL), lambda b: (b, 0)), state_spec, conv_spec],
        out_shape=[jax.ShapeDtypeStruct((n_seq * seq_len, D_MODEL), F32),
                   jax.ShapeDtypeStruct(h0.shape, F32),
                   jax.ShapeDtypeStruct(conv0.shape, F32)],
        scratch_shapes=[pltpu.VMEM((nb, HALO_ROWS + q, SSD_CONV_DIM), F32),
                        pltpu.VMEM((nb, SSD_HEADS, SSD_STATE), F32)],
        compiler_params=_cparams(1),
        name="ssd_sample",
    )(zr, xbc, xbc, tail, h0, conv0, cw, cb, dtb, alog, dskip_e, nw)


def _gla_chunk(qk_ref, v_ref, r_ref, alo_ref, wa_ref, ba_ref, nw_ref, y_ref, get_state, put_state):
    q = v_ref.shape[0]
    a_lo = alo_ref[...][:, :GLA_RANK].astype(BF16)
    gk = -_softplus(-(_dot(a_lo, wa_ref[...]) + ba_ref[...])) / GLA_TAU
    causal = _tril(q)
    bcum = _dot_sel(causal.astype(BF16), gk)
    for h in range(GLA_HEADS):
        kc = slice(h * GLA_DK, (h + 1) * GLA_DK)
        vc = slice(h * GLA_DV, (h + 1) * GLA_DV)
        b_h = bcum[:, kc]
        b_last = b_h[q - 1:q, :]
        q_h = qk_ref[:, kc] * (GLA_DK ** -0.5)
        k_h = qk_ref[:, GLA_KEY_DIM + h * GLA_DK:GLA_KEY_DIM + (h + 1) * GLA_DK]
        v_h = v_ref[:, vc].astype(BF16)
        q_in = (q_h * jnp.exp(b_h)).astype(BF16)
        k_in = (k_h * jnp.exp(-b_h)).astype(BF16)
        att = jnp.where(causal, _dot_nt(q_in, k_in), 0.0).astype(BF16)
        s_h = get_state(h)
        o = _dot(att, v_h) + _dot(q_in, s_h.astype(BF16))
        k_end = (k_h * jnp.exp(b_last - b_h)).astype(BF16)
        d_col = _col_bcast(jnp.exp(b_last))
        put_state(h, s_h * jnp.concatenate([d_col, d_col], axis=1) + _dot_tn(k_end, v_h))
        ms = jnp.mean(o * o, axis=-1, keepdims=True)
        y_ref[:, vc] = o * lax.rsqrt(ms + EPS) * nw_ref[...] * r_ref[:, vc]


def _gla_prompt_kernel(qk_ref, v_ref, r_ref, alo_ref, wa_ref, ba_ref, nw_ref, y_ref, sout_ref, st):
    c = pl.program_id(1)

    @pl.when(c == 0)
    def _():
        st[...] = jnp.zeros(st.shape, F32)

    def put(h, val):
        st[h] = val

    _gla_chunk(qk_ref, v_ref, r_ref, alo_ref, wa_ref, ba_ref, nw_ref, y_ref, lambda h: st[h], put)

    @pl.when(c == pl.num_programs(1) - 1)
    def _():
        sout_ref[0] = st[...]


def _gla_prompt(qkv, zr, tail, n_seq, seq_len, wa, ba, nw):
    q = GLA_CHUNK
    nc = seq_len // q

    def col(blk):
        return pl.BlockSpec((q, D_MODEL), lambda b, c: (b * nc + c, blk))

    return pl.pallas_call(
        _gla_prompt_kernel,
        grid=(n_seq, nc),
        in_specs=[col(COL_QK), col(COL_V), col(COL_R),
                  pl.BlockSpec((q, LANES), lambda b, c: (b * nc + c, TAIL_ALO)),
                  _row_params(wa), _row_params(ba), _row_params(nw)],
        out_specs=[pl.BlockSpec((q, D_MODEL), lambda b, c: (b * nc + c, 0)),
                   pl.BlockSpec((1, GLA_HEADS, GLA_DK, GLA_DV), lambda b, c: (b, 0, 0, 0))],
        out_shape=[jax.ShapeDtypeStruct((n_seq * seq_len, D_MODEL), F32),
                   jax.ShapeDtypeStruct((n_seq, GLA_HEADS, GLA_DK, GLA_DV), F32)],
        scratch_shapes=[pltpu.VMEM((GLA_HEADS, GLA_DK, GLA_DV), F32)],
        compiler_params=_cparams(2),
        name="gla_prompt",
    )(qkv, qkv, zr, tail, wa, ba, nw)


def _gla_sample_kernel(qk_ref, v_ref, r_ref, alo_ref, s_ref, wa_ref, ba_ref, nw_ref, y_ref, sout_ref):
    n = s_ref.shape[0]
    q = v_ref.shape[0] // n
    for s in range(n):
        rows = pl.ds(s * q, q)

        def put(h, val, s=s):
            sout_ref[s, h] = val

        _gla_chunk(qk_ref.at[rows], v_ref.at[rows], r_ref.at[rows], alo_ref.at[rows], wa_ref, ba_ref,
                   nw_ref, y_ref.at[rows], lambda h, s=s: s_ref[s, h], put)


def _gla_sample(qkv, zr, tail, row0, n_seq, seq_len, s0, wa, ba, nw):
    nb = SAMPLE_SEQS
    q = seq_len
    blk0 = row0 // (nb * q)

    def col(blk):
        return pl.BlockSpec((nb * q, D_MODEL), lambda b: (blk0 + b, blk))

    state_spec = pl.BlockSpec((nb, GLA_HEADS, GLA_DK, GLA_DV), lambda b: (b, 0, 0, 0))
    return pl.pallas_call(
        _gla_sample_kernel,
        grid=(n_seq // nb,),
        in_specs=[col(COL_QK), col(COL_V), col(COL_R),
                  pl.BlockSpec((nb * q, LANES), lambda b: (blk0 + b, TAIL_ALO)),
                  state_spec, _row_params(wa), _row_params(ba), _row_params(nw)],
        out_specs=[pl.BlockSpec((nb * q, D_MODEL), lambda b: (b, 0)), state_spec],
        out_shape=[jax.ShapeDtypeStruct((n_seq * seq_len, D_MODEL), F32),
                   jax.ShapeDtypeStruct(s0.shape, F32)],
        compiler_params=_cparams(1),
        name="gla_sample",
    )(qkv, qkv, zr, tail, s0, wa, ba, nw)


def _layernorm(x, g, b):
    mu = jnp.mean(x, axis=-1, keepdims=True)
    xc = x - mu
    var = jnp.mean(xc * xc, axis=-1, keepdims=True)
    return xc * lax.rsqrt(var + EPS) * g + b


def _route(logits):
    lane = lax.broadcasted_iota(jnp.int32, logits.shape, 1)
    neg = -jnp.inf
    big = ROUTER_LANES
    glog = jnp.where(lane < MOE_GROUPS, logits, neg)
    gmax = jnp.max(glog, axis=-1, keepdims=True)
    g_sel = jnp.min(jnp.where(glog == gmax, lane, big), axis=-1, keepdims=True)
    p_g = 1.0 / jnp.sum(jnp.exp(glog - gmax), axis=-1, keepdims=True)
    e_lane = lane - MOE_GROUPS
    in_group = (e_lane >= 0) & (e_lane < MOE_EXPERTS) & (e_lane // MOE_EPG == g_sel)
    el = jnp.where(in_group, logits, neg)
    v1 = jnp.max(el, axis=-1, keepdims=True)
    i1 = jnp.min(jnp.where(el == v1, lane, big), axis=-1, keepdims=True)
    el2 = jnp.where(lane == i1, neg, el)
    v2 = jnp.max(el2, axis=-1, keepdims=True)
    i2 = jnp.min(jnp.where(el2 == v2, lane, big), axis=-1, keepdims=True)
    e2 = jnp.exp(v2 - v1)
    w1 = p_g / (1.0 + e2)
    w2 = p_g * e2 / (1.0 + e2)
    first = (i1 - MOE_GROUPS).astype(F32)
    second = (i2 - MOE_GROUPS).astype(F32)
    return jnp.where(lane == ROUTE_E1, first,
                     jnp.where(lane == ROUTE_E2, second,
                               jnp.where(lane == ROUTE_W1, w1,
                                         jnp.where(lane == ROUTE_W2, w2, 0.0))))


ROW_CHUNKS = D_MODEL // LANES


def _store_chunked(ref, val):
    rows = val.shape[0]
    for c in range(ROW_CHUNKS):
        ref[pl.ds(c, rows, stride=ROW_CHUNKS), :] = val[:, c * LANES:(c + 1) * LANES]


def _load_chunked(ref, c):
    return ref[pl.ds(c, ref.shape[0] // ROW_CHUNKS, stride=ROW_CHUNKS), :]


def _merge_kernel(n_first, ysa_ref, ysb_ref, yga_ref, ygb_ref, gs_ref, gg_ref, xa_ref, xb_ref, wo_ref,
                  g_ref, b_ref, wrh_ref, wrl_ref, br_ref, h1_ref, route_ref):
    def tile(ys_ref, yg_ref, x_ref):
        merged = gs_ref[...] * ys_ref[...] + gg_ref[...] * yg_ref[...]
        mix = _dot(merged.astype(BF16), wo_ref[...])
        h1 = _layernorm(DEEPNORM_ALPHA * x_ref[...] + mix, g_ref[...], b_ref[...])
        _store_chunked(h1_ref, h1)
        h_hi = h1.astype(BF16)
        h_lo = (h1 - h_hi.astype(F32)).astype(BF16)
        logits = (_dot(h_hi, wrh_ref[...]) + _dot(h_lo, wrh_ref[...]) + _dot(h_hi, wrl_ref[...])
                  + br_ref[...])
        route_ref[...] = _route(logits)

    first = pl.program_id(0) < n_first
    pl.when(first)(lambda: tile(ysa_ref, yga_ref, xa_ref))
    pl.when(jnp.logical_not(first))(lambda: tile(ysb_ref, ygb_ref, xb_ref))


def _merge(ys, yg, gates, xs, wo, g, b, wr_hi, wr_lo, br, tm):
    n_a, n_b = xs[0].shape[0] // tm, xs[1].shape[0] // tm
    m = (n_a + n_b) * tm

    def rows(i):
        return (i, 0)

    pair = [pl.BlockSpec((tm, D_MODEL), lambda i: (jnp.minimum(i, n_a - 1), 0)),
            pl.BlockSpec((tm, D_MODEL), lambda i: (jnp.maximum(i - n_a, 0), 0))]
    return pl.pallas_call(
        functools.partial(_merge_kernel, n_a),
        grid=(m // tm,),
        in_specs=[*pair, *pair,
                  pl.BlockSpec((tm, D_MODEL), lambda i: (i, COL_GS)),
                  pl.BlockSpec((tm, D_MODEL), lambda i: (i, COL_GG)),
                  *pair,
                  pl.BlockSpec(wo.shape, lambda i: (0, 0), pipeline_mode=pl.Buffered(1)),
                  _row_params(g), _row_params(b),
                  _row_params(wr_hi), _row_params(wr_lo), _row_params(br)],
        out_specs=[pl.BlockSpec((tm * ROW_CHUNKS, LANES), rows),
                   pl.BlockSpec((tm, ROUTER_LANES), rows)],
        out_shape=[jax.ShapeDtypeStruct((m * ROW_CHUNKS, LANES), F32),
                   jax.ShapeDtypeStruct((m, ROUTER_LANES), F32)],
        compiler_params=_cparams(1),
        name="merge_outproj_ln_router",
    )(*ys, *yg, gates, gates, *xs, wo, g, b, wr_hi, wr_lo, br)


def _zero_runs(pads_ref, zeros, x_hbm, sem, action):
    tm = zeros.shape[0] // ROW_CHUNKS

    def piece(pos, rows):
        dst = pl.multiple_of(pos * ROW_CHUNKS, ROW_CHUNKS)
        return pltpu.make_async_copy(zeros.at[pl.ds(0, rows * ROW_CHUNKS)],
                                     x_hbm.at[pl.ds(dst, rows * ROW_CHUNKS)], sem)

    def run(e, carry):
        pos, length = pads_ref[0, e], pads_ref[1, e]

        def whole(q, c):
            getattr(piece(pos + q * tm, tm), action)()
            return c

        lax.fori_loop(0, length // tm, whole, 0)
        pos = pos + (length // tm) * tm
        rows = tm // 2
        while rows >= 1:
            @pl.when((length & rows) != 0)
            def _(pos=pos, rows=rows):
                getattr(piece(pos, rows), action)()

            pos = pos + (length & rows)
            rows //= 2
        return carry

    lax.fori_loop(0, pads_ref.shape[1], run, 0)


def _dispatch_kernel(slot_ref, pads_ref, h_ref, x_hbm, zeros, sem, zsem):
    n = h_ref.shape[0] // ROW_CHUNKS

    def issue(r, carry):
        src = h_ref.at[pl.ds(pl.multiple_of(r * ROW_CHUNKS, ROW_CHUNKS), ROW_CHUNKS)]
        for k in range(2):
            dst = pl.multiple_of(slot_ref[k, r] * ROW_CHUNKS, ROW_CHUNKS)
            pltpu.make_async_copy(src, x_hbm.at[pl.ds(dst, ROW_CHUNKS)], sem).start(
                priority=k % DMA_QUEUES)
        return carry

    lax.fori_loop(0, n, issue, 0, unroll=4)

    @pl.when(pl.program_id(0) == pl.num_programs(0) - 1)
    def _():
        zeros[...] = jnp.zeros(zeros.shape, zeros.dtype)
        _zero_runs(pads_ref, zeros, x_hbm, zsem, "start")
        _zero_runs(pads_ref, zeros, x_hbm, zsem, "wait")

    for k in range(2):
        pltpu.make_async_copy(h_ref, x_hbm.at[pl.ds(0, n * ROW_CHUNKS)], sem).wait()


def _dispatch(h1, slot, pads, n_slots, tm, window):
    t = slot.shape[1]
    steps = t // window
    slot3 = slot.reshape(2, steps, window).transpose(1, 0, 2)
    return pl.pallas_call(
        _dispatch_kernel,
        grid=(steps,),
        in_specs=[pl.BlockSpec((None, 2, window), lambda i: (i, 0, 0), memory_space=pltpu.SMEM),
                  pl.BlockSpec(pads.shape, lambda i: (0, 0), memory_space=pltpu.SMEM),
                  pl.BlockSpec((window * ROW_CHUNKS, LANES), lambda i: (i, 0))],
        out_specs=pl.BlockSpec(memory_space=pl.ANY),
        out_shape=jax.ShapeDtypeStruct((n_slots * ROW_CHUNKS, LANES), h1.dtype),
        scratch_shapes=[pltpu.VMEM((tm * ROW_CHUNKS, LANES), h1.dtype), pltpu.SemaphoreType.DMA,
                        pltpu.SemaphoreType.DMA],
        compiler_params=_cparams(1),
        name="dispatch_rows",
    )(slot3, pads, h1)


def _expert_plan(route, tm):
    t = route.shape[0]
    ids = jnp.arange(MOE_EXPERTS, dtype=jnp.int32)
    e = jnp.concatenate([route[:, ROUTE_E1], route[:, ROUTE_E2]]).astype(jnp.int32)
    onehot = (e[:, None] == ids[None, :]).astype(jnp.int32)
    csum = jnp.cumsum(onehot, axis=0)
    rank = jnp.sum(onehot * csum, axis=1) - 1
    cnt = csum[-1]
    tiles_e = (cnt + tm - 1) // tm
    tile_end = jnp.cumsum(tiles_e)
    tile_start = tile_end - tiles_e
    slot = tile_start[e] * tm + rank
    n_tiles = (2 * t) // tm + MOE_EXPERTS
    tid = jnp.arange(n_tiles, dtype=jnp.int32)
    used = tid < tile_end[-1]
    tile_e = jnp.sum(tile_end[None, :] <= jnp.minimum(tid, tile_end[-1] - 1)[:, None], axis=1)
    nvalid = jnp.where(used, jnp.clip(cnt[tile_e] - (tid - tile_start[tile_e]) * tm, 0, tm), 0)
    first = used & (tid == tile_start[tile_e])
    has = tiles_e > 0
    ordinal = jnp.cumsum(has.astype(jnp.int32)) - 1
    later = jnp.where((ids[None, :] > ids[:, None]) & has[None, :], ids[None, :], MOE_EXPERTS)
    nxt_e = jnp.min(later, axis=1)
    nxt_e = jnp.where(nxt_e == MOE_EXPERTS, -1, nxt_e)
    i32 = lambda a: a.astype(jnp.int32)
    pads = jnp.stack([jnp.append(tile_start * tm + cnt, tile_end[-1] * tm),
                      jnp.append(tiles_e * tm - cnt, (n_tiles - tile_end[-1]) * tm)])
    return (slot.reshape(2, t), i32(pads), n_tiles * tm, i32(tile_e), i32(nvalid), i32(first),
            i32(nxt_e[tile_e]), i32(ordinal[tile_e] % 2))


def _experts_kernel(te_ref, nv_ref, first_ref, nxt_ref, par_ref, x_ref, wg_hbm, wu_hbm, wd_hbm, y_ref,
                    wg_f, wu_f, wd_f, wg_b, wu_b, wd_b, x_b, sem):
    i = pl.program_id(0)
    s = par_ref[i]

    def weight_copies(e, slot):
        return (pltpu.make_async_copy(wg_hbm.at[e], wg_f.at[slot], sem.at[slot]),
                pltpu.make_async_copy(wu_hbm.at[e], wu_f.at[slot], sem.at[slot]),
                pltpu.make_async_copy(wd_hbm.at[e], wd_f.at[slot], sem.at[slot]))

    @pl.when(i == 0)
    def _():
        for cp in weight_copies(te_ref[0], par_ref[0]):
            cp.start()

    @pl.when(first_ref[i] == 1)
    def _():
        @pl.when(nxt_ref[i] >= 0)
        def _():
            for cp in weight_copies(nxt_ref[i], 1 - s):
                cp.start()

        for cp in weight_copies(te_ref[i], s):
            cp.wait()
        wg_b[...] = wg_f[s].astype(BF16)
        wu_b[...] = wu_f[s].astype(BF16)
        wd_b[...] = wd_f[s].astype(BF16)

    nv = nv_ref[i]

    @pl.when(nv > 0)
    def _():
        for c in range(ROW_CHUNKS):
            x_b[:, c * LANES:(c + 1) * LANES] = _load_chunked(x_ref, c).astype(BF16)
        x = x_b[...]
        hid = (_silu(_dot(x, wg_b[...])) * _dot(x, wu_b[...])).astype(BF16)
        _store_chunked(y_ref, _dot(hid, wd_b[...]))

    @pl.when(nv == 0)
    def _():
        y_ref[...] = jnp.zeros(y_ref.shape, y_ref.dtype)


def _experts(xs, tile_e, nvalid, first, nxt, par, wg, wu, wd, tm):
    n_tiles = xs.shape[0] // (tm * ROW_CHUNKS)
    rows = pl.BlockSpec((tm * ROW_CHUNKS, LANES), lambda i, *_: (i, 0))
    hbm = pl.BlockSpec(memory_space=pl.ANY)
    grid_spec = pltpu.PrefetchScalarGridSpec(
        num_scalar_prefetch=5,
        grid=(n_tiles,),
        in_specs=[rows, hbm, hbm, hbm],
        out_specs=rows,
        scratch_shapes=[pltpu.VMEM((2, D_MODEL, MOE_FF), F32), pltpu.VMEM((2, D_MODEL, MOE_FF), F32),
                        pltpu.VMEM((2, MOE_FF, D_MODEL), F32),
                        pltpu.VMEM((D_MODEL, MOE_FF), BF16), pltpu.VMEM((D_MODEL, MOE_FF), BF16),
                        pltpu.VMEM((MOE_FF, D_MODEL), BF16), pltpu.VMEM((tm, D_MODEL), BF16),
                        pltpu.SemaphoreType.DMA((2,))])
    return pl.pallas_call(
        _experts_kernel,
        grid_spec=grid_spec,
        out_shape=jax.ShapeDtypeStruct(xs.shape, F32),
        compiler_params=_cparams(1),
        name="experts",
    )(tile_e, nvalid, first, nxt, par, xs, wg, wu, wd)


def _combine_kernel(n_steps, slot_ref, slot_next_ref, route_ref, h1_ref, g_ref, b_ref, y_hbm, o_ref,
                    gbuf, sem):
    i = pl.program_id(0)
    tm = route_ref.shape[0]

    def request(idx_ref, parity):
        def body(r, carry):
            dst = pl.ds(pl.multiple_of(r * ROW_CHUNKS, ROW_CHUNKS), ROW_CHUNKS)
            for k in range(2):
                src = pl.multiple_of(idx_ref[k, r] * ROW_CHUNKS, ROW_CHUNKS)
                pltpu.make_async_copy(y_hbm.at[pl.ds(src, ROW_CHUNKS)], gbuf.at[parity, k, dst],
                                      sem.at[parity]).start(priority=k % DMA_QUEUES)
            return carry

        lax.fori_loop(0, tm, body, 0, unroll=4)

    @pl.when(i == 0)
    def _():
        request(slot_ref, 0)

    @pl.when(i + 1 < n_steps)
    def _():
        request(slot_next_ref, (i + 1) % 2)

    parity = i % 2
    for k in range(2):
        pltpu.make_async_copy(y_hbm.at[pl.ds(0, tm * ROW_CHUNKS)], gbuf.at[parity, k],
                              sem.at[parity]).wait()
    route = route_ref[...]
    w1 = route[:, ROUTE_W1:ROUTE_W1 + 1]
    w2 = route[:, ROUTE_W2:ROUTE_W2 + 1]
    pre = jnp.concatenate(
        [DEEPNORM_ALPHA * _load_chunked(h1_ref, c)
         + w1 * _load_chunked(gbuf.at[parity, 0], c) + w2 * _load_chunked(gbuf.at[parity, 1], c)
         for c in range(ROW_CHUNKS)], axis=1)
    o_ref[...] = _layernorm(pre, g_ref[...], b_ref[...])


def _combine(y_sorted, slot, route, h1, row0, m, g, b, tm):
    t = slot.shape[1]
    blk0 = row0 // tm
    last = (row0 + m) // tm - 1
    slot3 = slot.reshape(2, t // tm, tm).transpose(1, 0, 2)
    return pl.pallas_call(
        functools.partial(_combine_kernel, m // tm),
        grid=(m // tm,),
        in_specs=[pl.BlockSpec((None, 2, tm), lambda i: (blk0 + i, 0, 0), memory_space=pltpu.SMEM),
                  pl.BlockSpec((None, 2, tm), lambda i: (jnp.minimum(blk0 + i + 1, last), 0, 0),
                               memory_space=pltpu.SMEM),
                  pl.BlockSpec((tm, ROUTER_LANES), lambda i: (blk0 + i, 0)),
                  pl.BlockSpec((tm * ROW_CHUNKS, LANES), lambda i: (blk0 + i, 0)),
                  _row_params(g), _row_params(b),
                  pl.BlockSpec(memory_space=pl.ANY)],
        out_specs=pl.BlockSpec((tm, D_MODEL), lambda i: (i, 0)),
        out_shape=jax.ShapeDtypeStruct((m, D_MODEL), F32),
        scratch_shapes=[pltpu.VMEM((2, 2, tm * ROW_CHUNKS, LANES), F32),
                        pltpu.SemaphoreType.DMA((2,))],
        compiler_params=_cparams(1),
        name="combine_ln",
    )(slot3, slot3, route, h1, g, b, y_sorted)


def _pad_lanes(v, width=LANES):
    v = v.reshape(1, -1)
    return jnp.pad(v, ((0, 0), (0, width - v.shape[1])))


def kernel(x_prompt, x_sample, state_ssd, state_conv, state_gla, w_in, conv_w, conv_b, dt_bias, a_log, d_skip, ssd_norm_w, gla_w_a2, gla_b_a, gla_norm_w, w_out, ln1_g, ln1_b, w_router_group, b_router_group, w_router_expert, b_router_expert, w_gate, w_up, w_down, ln2_g, ln2_b):
    assert w_in.shape[0] == DEPTH == 1
    n_p, len_p, _ = x_prompt.shape
    n_s, len_s, _ = x_sample.shape
    rows_p, rows_s = n_p * len_p, n_s * len_s
    xp = x_prompt.reshape(rows_p, D_MODEL)
    xs = x_sample.reshape(rows_s, D_MODEL)

    offs = [0]
    for s in IN_SPLIT_SIZES:
        offs.append(offs[-1] + s)
    w_t = w_in[0].T
    w_tail_t = jnp.concatenate(
        [jnp.pad(w_t[offs[2]:offs[3]], ((0, LANES - SSD_HEADS), (0, 0))),
         jnp.pad(w_t[offs[7]:offs[8]], ((0, LANES - GLA_RANK), (0, 0)))], axis=0)
    x_b = (xp.astype(BF16), xs.astype(BF16))

    cw, cb = conv_w[0], conv_b[0].reshape(1, -1)
    zr = _in_proj(x_b, w_t, ((offs[0], offs[1] - offs[0]), (offs[6], offs[7] - offs[6])), "silu")
    xbc, conv_tail = _in_proj(x_b, w_t, ((offs[1], offs[2] - offs[1]),), "conv", len_p, cw, cb)
    qkv = _in_proj(x_b, w_t, ((offs[3], offs[6] - offs[3]),), "none")
    gates = _in_proj(x_b, w_t, ((offs[8], offs[10] - offs[8]),), "sigmoid")
    conv_p = conv_tail.reshape(n_p, HALO_ROWS, SSD_CONV_DIM)[:, HALO_ROWS - (SSD_CONV - 1):]
    tail = _matmul_nt(x_b, w_tail_t, IN_PROJ_ROWS)

    dtb, alog = _pad_lanes(dt_bias[0]), _pad_lanes(a_log[0])
    dskip_e = jnp.repeat(d_skip[0], SSD_HEAD_DIM).reshape(1, -1)
    nw_ssd = ssd_norm_w[0].reshape(1, -1)
    wa, ba, nw_gla = gla_w_a2[0].astype(BF16), gla_b_a[0].reshape(1, -1), gla_norm_w[0].reshape(1, -1)

    ys_p, ssd_p = _ssd_prompt(zr, xbc, tail, n_p, len_p, dtb, alog, dskip_e, nw_ssd)
    ys_s, ssd_s, conv_s = _ssd_sample(zr, xbc, tail, rows_p, n_s, len_s, state_ssd[0], state_conv[0],
                                      cw, cb, dtb, alog, dskip_e, nw_ssd)
    yg_p, gla_p = _gla_prompt(qkv, zr, tail, n_p, len_p, wa, ba, nw_gla)
    yg_s, gla_s = _gla_sample(qkv, zr, tail, rows_p, n_s, len_s, state_gla[0], wa, ba, nw_gla)

    wo = w_out[0].astype(BF16)
    w_r = jnp.pad(jnp.concatenate([w_router_group[0], w_router_expert[0]], axis=1),
                  ((0, 0), (0, ROUTER_LANES - MOE_GROUPS - MOE_EXPERTS)))
    wr_hi = w_r.astype(BF16)
    wr_lo = (w_r - wr_hi.astype(F32)).astype(BF16)
    b_r = _pad_lanes(jnp.concatenate([b_router_group[0], b_router_expert[0]]), ROUTER_LANES)
    g1, b1 = ln1_g[0].reshape(1, -1), ln1_b[0].reshape(1, -1)
    g2, b2 = ln2_g[0].reshape(1, -1), ln2_b[0].reshape(1, -1)

    h1, route = _merge((ys_p, ys_s), (yg_p, yg_s), gates, (xp, xs), wo, g1, b1, wr_hi, wr_lo, b_r,
                       MERGE_ROWS)
    slot, pads, n_slots, tile_e, nvalid, first, nxt, par = _expert_plan(route, EXPERT_ROWS)
    x_sorted = _dispatch(h1, slot, pads, n_slots, EXPERT_ROWS, DISPATCH_WINDOW)
    y_sorted = _experts(x_sorted, tile_e, nvalid, first, nxt, par, w_gate[0], w_up[0], w_down[0],
                        EXPERT_ROWS)
    outs = [_combine(y_sorted, slot, route, h1, 0, rows_p, g2, b2, MERGE_ROWS),
            _combine(y_sorted, slot, route, h1, rows_p, rows_s, g2, b2, MERGE_ROWS)]
    y_p = outs[0].reshape(x_prompt.shape)
    y_s = outs[1].reshape(x_sample.shape)
    return (y_p, y_s, ssd_p[None], conv_p[None], gla_p[None], ssd_s[None], conv_s[None], gla_s[None])
```

```python
import functools

import jax
import jax.numpy as jnp
from jax import lax
from jax.experimental import pallas as pl
from jax.experimental.pallas import tpu as pltpu

F32 = jnp.float32
BF16 = jnp.bfloat16

D_MODEL = 2048
SSD_HEADS = 32
SSD_HEAD_DIM = 64
SSD_GROUPS = 8
SSD_STATE = 128
SSD_CONV = 4
SSD_CHUNK = 128
SSD_BC = SSD_GROUPS * SSD_STATE
SSD_CONV_DIM = D_MODEL + 2 * SSD_BC
SSD_GROUP_COLS = D_MODEL // SSD_GROUPS
HEADS_PER_GROUP = SSD_HEADS // SSD_GROUPS
GLA_HEADS = 8
GLA_DK = 128
GLA_DV = 256
GLA_KEY_DIM = GLA_HEADS * GLA_DK
GLA_RANK = 16
GLA_TAU = 16.0
GLA_CHUNK = 64
MOE_GROUPS = 4
MOE_EPG = 8
MOE_EXPERTS = 32
MOE_FF = 512
DEPTH = 1
DEEPNORM_ALPHA = (2.0 * DEPTH) ** 0.25
EPS = 1e-5
IN_SPLIT_SIZES = (D_MODEL, SSD_CONV_DIM, SSD_HEADS, GLA_KEY_DIM, GLA_KEY_DIM, D_MODEL, D_MODEL,
                  GLA_RANK, D_MODEL, D_MODEL)

LANES = 128
HALO_ROWS = 8
COL_Z, COL_XS, COL_BC, COL_QK, COL_V, COL_R, COL_GS, COL_GG = range(8)
TAIL_DT, TAIL_ALO = 0, 1
ROUTER_LANES = 128
ROUTE_E1, ROUTE_E2, ROUTE_W1, ROUTE_W2 = 0, 1, 2, 3
MERGE_ROWS = 256
EXPERT_ROWS = 256
DISPATCH_WINDOW = 512
DMA_QUEUES = 2
VMEM_LIMIT = 56 * 1024 * 1024


def _cparams(n_axes):
    return pltpu.CompilerParams(dimension_semantics=("arbitrary",) * n_axes,
                                vmem_limit_bytes=VMEM_LIMIT)


def _silu(x):
    return x * jax.nn.sigmoid(x)


def _softplus(x):
    return jnp.maximum(x, 0.0) + jnp.log1p(jnp.exp(-jnp.abs(x)))


def _tril(n):
    r = lax.broadcasted_iota(jnp.int32, (n, n), 0)
    c = lax.broadcasted_iota(jnp.int32, (n, n), 1)
    return r >= c


def _expand_group(p, g, lane_head):
    h0 = g * HEADS_PER_GROUP
    out = p[:, h0 + HEADS_PER_GROUP - 1:h0 + HEADS_PER_GROUP]
    for hh in range(HEADS_PER_GROUP - 2, -1, -1):
        out = jnp.where(lane_head == hh, p[:, h0 + hh:h0 + hh + 1], out)
    return out


def _split3(x):
    hi = x.astype(BF16)
    r = x - hi.astype(F32)
    mid = r.astype(BF16)
    lo = (r - mid.astype(F32)).astype(BF16)
    return hi, mid, lo


def _dot_sel(sel, x):
    hi, mid, lo = _split3(x)
    return _dot(sel, lo) + _dot(sel, mid) + _dot(sel, hi)


def _col_bcast(row):
    return jnp.broadcast_to(row, (LANES, LANES)).T


def _dot(a, b, **kw):
    return jnp.dot(a, b, preferred_element_type=F32, **kw)


def _dot_nt(a, b, **kw):
    return lax.dot_general(a, b, (((1,), (1,)), ((), ())), preferred_element_type=F32, **kw)


def _dot_tn(a, b, **kw):
    return lax.dot_general(a, b, (((0,), (0,)), ((), ())), preferred_element_type=F32, **kw)


def _matmul_kernel(n_first, xa_ref, xb_ref, wt_ref, o_ref):
    x = jnp.where(pl.program_id(0) < n_first, xa_ref[...], xb_ref[...])
    o_ref[...] = _dot_nt(x, wt_ref[...].astype(BF16))


def _matmul_nt(xs, wt, tm):
    n, k = wt.shape
    n_a, n_b = xs[0].shape[0] // tm, xs[1].shape[0] // tm
    return pl.pallas_call(
        functools.partial(_matmul_kernel, n_a),
        grid=(n_a + n_b,),
        in_specs=[pl.BlockSpec((tm, k), lambda i: (jnp.minimum(i, n_a - 1), 0)),
                  pl.BlockSpec((tm, k), lambda i: (jnp.maximum(i - n_a, 0), 0)),
                  pl.BlockSpec((n, k), lambda i: (0, 0))],
        out_specs=pl.BlockSpec((tm, n), lambda i: (i, 0)),
        out_shape=jax.ShapeDtypeStruct(((n_a + n_b) * tm, n), F32),
        compiler_params=_cparams(1),
        name="in_proj_tail",
    )(*xs, wt)


IN_PROJ_COLS = 1024
IN_PROJ_ROWS = 1024
CAST_ROWS = 256
SUBLANES = 8


def _in_proj_kernel(plan, n_first, xa_ref, xb_ref, wt_hbm, o_ref, wf, wb, sem):
    j = pl.program_id(0)
    i = pl.program_id(1)
    tn = wb.shape[1]
    n_blocks = plan[-1][1]

    def block_copy(jj, slot):
        shift = 0
        for lo, hi, sh in plan:
            shift = jnp.where((jj >= lo) & (jj < hi), sh, shift)
        src = pl.multiple_of(jj * tn + shift, SUBLANES)
        return pltpu.make_async_copy(wt_hbm.at[pl.ds(src, tn)], wf.at[slot], sem.at[slot])

    @pl.when(i == 0)
    def _():
        slot = j % 2

        @pl.when(j == 0)
        def _():
            block_copy(j, slot).start()

        @pl.when(j + 1 < n_blocks)
        def _():
            block_copy(j + 1, 1 - slot).start()

        block_copy(j, slot).wait()
        for r in range(0, tn, CAST_ROWS):
            wb[:, r:r + CAST_ROWS] = wf[slot, r:r + CAST_ROWS, :].T.astype(BF16)

    x = jnp.where(i < n_first, xa_ref[...], xb_ref[...])
    o_ref[...] = _dot(x, wb[...])


def _in_proj(xs, wt, segments):
    tm, tn = IN_PROJ_ROWS, IN_PROJ_COLS
    k = wt.shape[1]
    n_a, n_b = xs[0].shape[0] // tm, xs[1].shape[0] // tm
    plan, dest = [], 0
    for src, count in segments:
        assert count % tn == 0 and dest % tn == 0 and (src - dest) % SUBLANES == 0
        plan.append((dest // tn, (dest + count) // tn, src - dest))
        dest += count
    return pl.pallas_call(
        functools.partial(_in_proj_kernel, tuple(plan), n_a),
        grid=(dest // tn, n_a + n_b),
        in_specs=[pl.BlockSpec((tm, k), lambda j, i: (jnp.minimum(i, n_a - 1), 0)),
                  pl.BlockSpec((tm, k), lambda j, i: (jnp.maximum(i - n_a, 0), 0)),
                  pl.BlockSpec(memory_space=pl.ANY)],
        out_specs=pl.BlockSpec((tm, tn), lambda j, i: (i, j)),
        out_shape=jax.ShapeDtypeStruct(((n_a + n_b) * tm, dest), F32),
        scratch_shapes=[pltpu.VMEM((2, tn, k), F32), pltpu.VMEM((k, tn), BF16),
                        pltpu.SemaphoreType.DMA((2,))],
        compiler_params=_cparams(2),
        name="in_proj_main",
    )(*xs, wt)


def _conv_silu(ubuf, cw_ref, cb_ref, lo, width, rows):
    acc = cb_ref[:, lo:lo + width]
    for i in range(SSD_CONV):
        r0 = HALO_ROWS - (SSD_CONV - 1) + i
        acc = acc + cw_ref[i:i + 1, lo:lo + width] * ubuf[r0:r0 + rows, lo:lo + width]
    return _silu(acc)


def _gated_group_norm(y_g, z_g, nw_g):
    yg = y_g * _silu(z_g)
    ms = jnp.mean(yg * yg, axis=-1, keepdims=True)
    return yg * lax.rsqrt(ms + EPS) * nw_g


def _ssd_prompt_kernel(z_ref, xs_ref, bc_ref, tail_ref, cw_ref, cb_ref, dtb_ref, alog_ref,
                       dskip_ref, nw_ref, y_ref, hout_ref, convout_ref, ubuf, ht):
    q = SSD_CHUNK
    c = pl.program_id(1)

    @pl.when(c == 0)
    def _():
        ubuf[0:HALO_ROWS, :] = jnp.zeros((HALO_ROWS, SSD_CONV_DIM), F32)
        ht[...] = jnp.zeros(ht.shape, F32)

    ubuf[HALO_ROWS:HALO_ROWS + q, 0:D_MODEL] = xs_ref[...]
    ubuf[HALO_ROWS:HALO_ROWS + q, D_MODEL:SSD_CONV_DIM] = bc_ref[...]

    dt = _softplus(tail_ref[...] + dtb_ref[...])
    da = dt * (-jnp.exp(alog_ref[...]))
    causal = _tril(q)
    acum = _dot_sel(causal.astype(BF16), da)
    acum_t = acum.T
    dt_t = dt.T
    a_last = acum[q - 1:q, :]
    decay_in = jnp.exp(acum)
    w_end = dt * jnp.exp(a_last - acum)
    chunk_decay = jnp.exp(a_last)
    lane_head = lax.broadcasted_iota(jnp.int32, (q, SSD_GROUP_COLS), 1) >> 6
    assert SSD_HEAD_DIM == 1 << 6

    for g in range(SSD_GROUPS):
        lo = g * SSD_GROUP_COLS
        cols = slice(lo, lo + SSD_GROUP_COLS)
        xs_g = _conv_silu(ubuf, cw_ref, cb_ref, lo, SSD_GROUP_COLS, q)
        b_g = _conv_silu(ubuf, cw_ref, cb_ref, D_MODEL + g * SSD_STATE, SSD_STATE, q).astype(BF16)
        c_g = _conv_silu(ubuf, cw_ref, cb_ref, D_MODEL + SSD_BC + g * SSD_STATE, SSD_STATE,
                         q).astype(BF16)
        scores = _dot_nt(c_g, b_g)
        h_g = ht[g]
        xs_b = xs_g.astype(BF16)
        m_heads, x_heads = [], []
        for hh in range(HEADS_PER_GROUP):
            h = g * HEADS_PER_GROUP + hh
            seg = acum[:, h:h + 1] - acum_t[h:h + 1, :]
            decay = jnp.exp(jnp.where(causal, seg, -jnp.inf))
            m_heads.append((scores * decay * dt_t[h:h + 1, :]).astype(BF16))
            x_heads.append(jnp.where(lane_head == hh, xs_b, jnp.zeros_like(xs_b)))
        y_g = (_dot(jnp.concatenate(m_heads, axis=1), jnp.concatenate(x_heads, axis=0))
               + _dot(c_g, h_g.astype(BF16)) * _expand_group(decay_in, g, lane_head)
               + dskip_ref[:, cols] * xs_g)
        y_ref[:, cols] = _gated_group_norm(y_g, z_ref[:, cols], nw_ref[:, cols])
        xw = (xs_g * _expand_group(w_end, g, lane_head)).astype(BF16)
        ht[g] = h_g * _expand_group(chunk_decay, g, lane_head[0:1]) + _dot_tn(b_g, xw)

    last = HALO_ROWS + q - (SSD_CONV - 1)
    tail_rows = ubuf[last:last + SSD_CONV - 1, :]
    ubuf[HALO_ROWS - (SSD_CONV - 1):HALO_ROWS, :] = tail_rows

    @pl.when(c == pl.num_programs(1) - 1)
    def _():
        convout_ref[0] = tail_rows
        for g in range(SSD_GROUPS):
            h_t = ht[g].T
            for hh in range(HEADS_PER_GROUP):
                hout_ref[0, g * HEADS_PER_GROUP + hh] = h_t[hh * SSD_HEAD_DIM:(hh + 1) * SSD_HEAD_DIM]


def _row_params(p):
    return pl.BlockSpec(p.shape, lambda *_: (0,) * p.ndim)


def _ssd_prompt(proj, tail, n_seq, seq_len, cw, cb, dtb, alog, dskip_e, nw):
    q = SSD_CHUNK
    nc = seq_len // q

    def col(blk):
        return pl.BlockSpec((q, D_MODEL), lambda b, c: (b * nc + c, blk))

    return pl.pallas_call(
        _ssd_prompt_kernel,
        grid=(n_seq, nc),
        in_specs=[col(COL_Z), col(COL_XS), col(COL_BC),
                  pl.BlockSpec((q, LANES), lambda b, c: (b * nc + c, TAIL_DT)),
                  _row_params(cw), _row_params(cb), _row_params(dtb), _row_params(alog),
                  _row_params(dskip_e), _row_params(nw)],
        out_specs=[pl.BlockSpec((q, D_MODEL), lambda b, c: (b * nc + c, 0)),
                   pl.BlockSpec((1, SSD_HEADS, SSD_HEAD_DIM, SSD_STATE), lambda b, c: (b, 0, 0, 0)),
                   pl.BlockSpec((1, SSD_CONV - 1, SSD_CONV_DIM), lambda b, c: (b, 0, 0))],
        out_shape=[jax.ShapeDtypeStruct((n_seq * seq_len, D_MODEL), F32),
                   jax.ShapeDtypeStruct((n_seq, SSD_HEADS, SSD_HEAD_DIM, SSD_STATE), F32),
                   jax.ShapeDtypeStruct((n_seq, SSD_CONV - 1, SSD_CONV_DIM), F32)],
        scratch_shapes=[pltpu.VMEM((HALO_ROWS + q, SSD_CONV_DIM), F32),
                        pltpu.VMEM((SSD_GROUPS, SSD_STATE, SSD_GROUP_COLS), F32)],
        compiler_params=_cparams(2),
        name="ssd_prompt",
    )(proj, proj, proj, tail, cw, cb, dtb, alog, dskip_e, nw)


SAMPLE_SEQS = 4


def _dot_tn_split(a, b):
    a_hi = a.astype(BF16)
    a_lo = (a - a_hi.astype(F32)).astype(BF16)
    b_hi = b.astype(BF16)
    b_lo = (b - b_hi.astype(F32)).astype(BF16)
    return _dot_tn(a_lo, b_hi) + _dot_tn(a_hi, b_lo) + _dot_tn(a_hi, b_hi)


def _ssd_sample_seq(z_ref, xs_ref, bc_ref, tail_ref, h_ref, conv_ref, cw_ref, cb_ref, dtb_ref,
                    alog_ref, dskip_ref, nw_ref, y_ref, hout_ref, convout_ref, ubuf, decay_rows):
    q = xs_ref.shape[0]
    ubuf[0:HALO_ROWS - (SSD_CONV - 1), :] = jnp.zeros((HALO_ROWS - (SSD_CONV - 1), SSD_CONV_DIM), F32)
    ubuf[HALO_ROWS - (SSD_CONV - 1):HALO_ROWS, :] = conv_ref[...]
    ubuf[HALO_ROWS:HALO_ROWS + q, 0:D_MODEL] = xs_ref[...]
    ubuf[HALO_ROWS:HALO_ROWS + q, D_MODEL:SSD_CONV_DIM] = bc_ref[...]
    convout_ref[...] = ubuf[HALO_ROWS + q - (SSD_CONV - 1):HALO_ROWS + q, :]

    dt = _softplus(tail_ref[...] + dtb_ref[...])
    da = dt * (-jnp.exp(alog_ref[...]))
    acum = _dot_sel(_tril(q).astype(BF16), da)
    a_last = acum[q - 1:q, :]
    decay_in = jnp.exp(acum)
    w_end = dt * jnp.exp(a_last - acum)
    decay_rows[...] = _col_bcast(jnp.exp(a_last))[:SSD_HEADS]
    row = lax.broadcasted_iota(jnp.int32, (q, SSD_GROUP_COLS), 0)
    lane_head = lax.broadcasted_iota(jnp.int32, (q, SSD_GROUP_COLS), 1) >> 6

    for g in range(SSD_GROUPS):
        lo = g * SSD_GROUP_COLS
        cols = slice(lo, lo + SSD_GROUP_COLS)
        xs_g = _conv_silu(ubuf, cw_ref, cb_ref, lo, SSD_GROUP_COLS, q)
        b_g = _conv_silu(ubuf, cw_ref, cb_ref, D_MODEL + g * SSD_STATE, SSD_STATE, q)
        c_g = _conv_silu(ubuf, cw_ref, cb_ref, D_MODEL + SSD_BC + g * SSD_STATE, SSD_STATE, q)
        h_g = h_ref[g * HEADS_PER_GROUP:(g + 1) * HEADS_PER_GROUP].reshape(
            SSD_GROUP_COLS, SSD_STATE)
        y_g = (_dot_nt(c_g.astype(BF16), h_g.astype(BF16)) * _expand_group(decay_in, g, lane_head)
               + dskip_ref[:, cols] * xs_g)
        acum_g = _expand_group(acum, g, lane_head)
        dt_g = _expand_group(dt, g, lane_head)
        for j in range(q):
            s_j = jnp.sum(c_g * b_g[j:j + 1, :], axis=-1, keepdims=True)
            coef = s_j * jnp.exp(acum_g - acum_g[j:j + 1, :]) * dt_g[j:j + 1, :]
            y_g = y_g + jnp.where(row >= j, coef, 0.0) * xs_g[j:j + 1, :]
        y_ref[:, cols] = _gated_group_norm(y_g, z_ref[:, cols], nw_ref[:, cols])
        upd = _dot_tn_split(xs_g * _expand_group(w_end, g, lane_head), b_g)
        for hh in range(HEADS_PER_GROUP):
            h = g * HEADS_PER_GROUP + hh
            rows = slice(hh * SSD_HEAD_DIM, (hh + 1) * SSD_HEAD_DIM)
            hout_ref[h] = h_g[rows] * decay_rows[h:h + 1, :] + upd[rows]


def _ssd_sample_kernel(z_ref, xs_ref, bc_ref, tail_ref, h_ref, conv_ref, cw_ref, cb_ref, dtb_ref,
                       alog_ref, dskip_ref, nw_ref, y_ref, hout_ref, convout_ref, ubuf, decay_rows):
    n = h_ref.shape[0]
    q = xs_ref.shape[0] // n
    for s in range(n):
        rows = pl.ds(s * q, q)
        _ssd_sample_seq(z_ref.at[rows], xs_ref.at[rows], bc_ref.at[rows], tail_ref.at[rows],
                        h_ref.at[s], conv_ref.at[s], cw_ref, cb_ref, dtb_ref, alog_ref, dskip_ref,
                        nw_ref, y_ref.at[rows], hout_ref.at[s], convout_ref.at[s], ubuf.at[s],
                        decay_rows.at[s])


def _ssd_sample(proj, tail, row0, n_seq, seq_len, h0, conv0, cw, cb, dtb, alog, dskip_e, nw):
    nb = SAMPLE_SEQS
    q = seq_len
    blk0 = row0 // (nb * q)

    def col(blk):
        return pl.BlockSpec((nb * q, D_MODEL), lambda b: (blk0 + b, blk))

    state_spec = pl.BlockSpec((nb, SSD_HEADS, SSD_HEAD_DIM, SSD_STATE), lambda b: (b, 0, 0, 0))
    conv_spec = pl.BlockSpec((nb, SSD_CONV - 1, SSD_CONV_DIM), lambda b: (b, 0, 0))
    return pl.pallas_call(
        _ssd_sample_kernel,
        grid=(n_seq // nb,),
        in_specs=[col(COL_Z), col(COL_XS), col(COL_BC),
                  pl.BlockSpec((nb * q, LANES), lambda b: (blk0 + b, TAIL_DT)),
                  state_spec, conv_spec,
                  _row_params(cw), _row_params(cb), _row_params(dtb), _row_params(alog),
                  _row_params(dskip_e), _row_params(nw)],
        out_specs=[pl.BlockSpec((nb * q, D_MODEL), lambda b: (b, 0)), state_spec, conv_spec],
        out_shape=[jax.ShapeDtypeStruct((n_seq * seq_len, D_MODEL), F32),
                   jax.ShapeDtypeStruct(h0.shape, F32),
                   jax.ShapeDtypeStruct(conv0.shape, F32)],
        scratch_shapes=[pltpu.VMEM((nb, HALO_ROWS + q, SSD_CONV_DIM), F32),
                        pltpu.VMEM((nb, SSD_HEADS, SSD_STATE), F32)],
        compiler_params=_cparams(1),
        name="ssd_sample",
    )(proj, proj, proj, tail, h0, conv0, cw, cb, dtb, alog, dskip_e, nw)


def _gla_chunk(qk_ref, v_ref, r_ref, alo_ref, wa_ref, ba_ref, nw_ref, y_ref, get_state, put_state):
    q = v_ref.shape[0]
    a_lo = alo_ref[...][:, :GLA_RANK].astype(BF16)
    gk = -_softplus(-(_dot(a_lo, wa_ref[...]) + ba_ref[...])) / GLA_TAU
    causal = _tril(q)
    bcum = _dot_sel(causal.astype(BF16), gk)
    for h in range(GLA_HEADS):
        kc = slice(h * GLA_DK, (h + 1) * GLA_DK)
        vc = slice(h * GLA_DV, (h + 1) * GLA_DV)
        b_h = bcum[:, kc]
        b_last = b_h[q - 1:q, :]
        q_h = qk_ref[:, kc] * (GLA_DK ** -0.5)
        k_h = qk_ref[:, GLA_KEY_DIM + h * GLA_DK:GLA_KEY_DIM + (h + 1) * GLA_DK]
        v_h = v_ref[:, vc].astype(BF16)
        q_in = (q_h * jnp.exp(b_h)).astype(BF16)
        k_in = (k_h * jnp.exp(-b_h)).astype(BF16)
        att = jnp.where(causal, _dot_nt(q_in, k_in), 0.0).astype(BF16)
        s_h = get_state(h)
        o = _dot(att, v_h) + _dot(q_in, s_h.astype(BF16))
        k_end = (k_h * jnp.exp(b_last - b_h)).astype(BF16)
        d_col = _col_bcast(jnp.exp(b_last))
        put_state(h, s_h * jnp.concatenate([d_col, d_col], axis=1) + _dot_tn(k_end, v_h))
        ms = jnp.mean(o * o, axis=-1, keepdims=True)
        y_ref[:, vc] = o * lax.rsqrt(ms + EPS) * nw_ref[...] * _silu(r_ref[:, vc])


def _gla_prompt_kernel(qk_ref, v_ref, r_ref, alo_ref, wa_ref, ba_ref, nw_ref, y_ref, sout_ref, st):
    c = pl.program_id(1)

    @pl.when(c == 0)
    def _():
        st[...] = jnp.zeros(st.shape, F32)

    def put(h, val):
        st[h] = val

    _gla_chunk(qk_ref, v_ref, r_ref, alo_ref, wa_ref, ba_ref, nw_ref, y_ref, lambda h: st[h], put)

    @pl.when(c == pl.num_programs(1) - 1)
    def _():
        sout_ref[0] = st[...]


def _gla_prompt(proj, tail, n_seq, seq_len, wa, ba, nw):
    q = GLA_CHUNK
    nc = seq_len // q

    def col(blk):
        return pl.BlockSpec((q, D_MODEL), lambda b, c: (b * nc + c, blk))

    return pl.pallas_call(
        _gla_prompt_kernel,
        grid=(n_seq, nc),
        in_specs=[col(COL_QK), col(COL_V), col(COL_R),
                  pl.BlockSpec((q, LANES), lambda b, c: (b * nc + c, TAIL_ALO)),
                  _row_params(wa), _row_params(ba), _row_params(nw)],
        out_specs=[pl.BlockSpec((q, D_MODEL), lambda b, c: (b * nc + c, 0)),
                   pl.BlockSpec((1, GLA_HEADS, GLA_DK, GLA_DV), lambda b, c: (b, 0, 0, 0))],
        out_shape=[jax.ShapeDtypeStruct((n_seq * seq_len, D_MODEL), F32),
                   jax.ShapeDtypeStruct((n_seq, GLA_HEADS, GLA_DK, GLA_DV), F32)],
        scratch_shapes=[pltpu.VMEM((GLA_HEADS, GLA_DK, GLA_DV), F32)],
        compiler_params=_cparams(2),
        name="gla_prompt",
    )(proj, proj, proj, tail, wa, ba, nw)


def _gla_sample_kernel(qk_ref, v_ref, r_ref, alo_ref, s_ref, wa_ref, ba_ref, nw_ref, y_ref, sout_ref):
    n = s_ref.shape[0]
    q = v_ref.shape[0] // n
    for s in range(n):
        rows = pl.ds(s * q, q)

        def put(h, val, s=s):
            sout_ref[s, h] = val

        _gla_chunk(qk_ref.at[rows], v_ref.at[rows], r_ref.at[rows], alo_ref.at[rows], wa_ref, ba_ref,
                   nw_ref, y_ref.at[rows], lambda h, s=s: s_ref[s, h], put)


def _gla_sample(proj, tail, row0, n_seq, seq_len, s0, wa, ba, nw):
    nb = SAMPLE_SEQS
    q = seq_len
    blk0 = row0 // (nb * q)

    def col(blk):
        return pl.BlockSpec((nb * q, D_MODEL), lambda b: (blk0 + b, blk))

    state_spec = pl.BlockSpec((nb, GLA_HEADS, GLA_DK, GLA_DV), lambda b: (b, 0, 0, 0))
    return pl.pallas_call(
        _gla_sample_kernel,
        grid=(n_seq // nb,),
        in_specs=[col(COL_QK), col(COL_V), col(COL_R),
                  pl.BlockSpec((nb * q, LANES), lambda b: (blk0 + b, TAIL_ALO)),
                  state_spec, _row_params(wa), _row_params(ba), _row_params(nw)],
        out_specs=[pl.BlockSpec((nb * q, D_MODEL), lambda b: (b, 0)), state_spec],
        out_shape=[jax.ShapeDtypeStruct((n_seq * seq_len, D_MODEL), F32),
                   jax.ShapeDtypeStruct(s0.shape, F32)],
        compiler_params=_cparams(1),
        name="gla_sample",
    )(proj, proj, proj, tail, s0, wa, ba, nw)


def _layernorm(x, g, b):
    mu = jnp.mean(x, axis=-1, keepdims=True)
    xc = x - mu
    var = jnp.mean(xc * xc, axis=-1, keepdims=True)
    return xc * lax.rsqrt(var + EPS) * g + b


def _route(logits):
    lane = lax.broadcasted_iota(jnp.int32, logits.shape, 1)
    neg = -jnp.inf
    big = ROUTER_LANES
    glog = jnp.where(lane < MOE_GROUPS, logits, neg)
    gmax = jnp.max(glog, axis=-1, keepdims=True)
    g_sel = jnp.min(jnp.where(glog == gmax, lane, big), axis=-1, keepdims=True)
    p_g = 1.0 / jnp.sum(jnp.exp(glog - gmax), axis=-1, keepdims=True)
    e_lane = lane - MOE_GROUPS
    in_group = (e_lane >= 0) & (e_lane < MOE_EXPERTS) & (e_lane // MOE_EPG == g_sel)
    el = jnp.where(in_group, logits, neg)
    v1 = jnp.max(el, axis=-1, keepdims=True)
    i1 = jnp.min(jnp.where(el == v1, lane, big), axis=-1, keepdims=True)
    el2 = jnp.where(lane == i1, neg, el)
    v2 = jnp.max(el2, axis=-1, keepdims=True)
    i2 = jnp.min(jnp.where(el2 == v2, lane, big), axis=-1, keepdims=True)
    e2 = jnp.exp(v2 - v1)
    w1 = p_g / (1.0 + e2)
    w2 = p_g * e2 / (1.0 + e2)
    first = (i1 - MOE_GROUPS).astype(F32)
    second = (i2 - MOE_GROUPS).astype(F32)
    return jnp.where(lane == ROUTE_E1, first,
                     jnp.where(lane == ROUTE_E2, second,
                               jnp.where(lane == ROUTE_W1, w1,
                                         jnp.where(lane == ROUTE_W2, w2, 0.0))))


ROW_CHUNKS = D_MODEL // LANES


def _store_chunked(ref, val):
    rows = val.shape[0]
    for c in range(ROW_CHUNKS):
        ref[pl.ds(c, rows, stride=ROW_CHUNKS), :] = val[:, c * LANES:(c + 1) * LANES]


def _load_chunked(ref, c):
    return ref[pl.ds(c, ref.shape[0] // ROW_CHUNKS, stride=ROW_CHUNKS), :]


def _merge_kernel(n_first, ysa_ref, ysb_ref, yga_ref, ygb_ref, gs_ref, gg_ref, xa_ref, xb_ref, wo_ref,
                  g_ref, b_ref, wrh_ref, wrl_ref, br_ref, h1_ref, route_ref):
    def tile(ys_ref, yg_ref, x_ref):
        merged = (jax.nn.sigmoid(gs_ref[...]) * ys_ref[...]
                  + jax.nn.sigmoid(gg_ref[...]) * yg_ref[...])
        mix = _dot(merged.astype(BF16), wo_ref[...])
        h1 = _layernorm(DEEPNORM_ALPHA * x_ref[...] + mix, g_ref[...], b_ref[...])
        _store_chunked(h1_ref, h1)
        h_hi = h1.astype(BF16)
        h_lo = (h1 - h_hi.astype(F32)).astype(BF16)
        logits = (_dot(h_hi, wrh_ref[...]) + _dot(h_lo, wrh_ref[...]) + _dot(h_hi, wrl_ref[...])
                  + br_ref[...])
        route_ref[...] = _route(logits)

    first = pl.program_id(0) < n_first
    pl.when(first)(lambda: tile(ysa_ref, yga_ref, xa_ref))
    pl.when(jnp.logical_not(first))(lambda: tile(ysb_ref, ygb_ref, xb_ref))


def _merge(ys, yg, gates, xs, wo, g, b, wr_hi, wr_lo, br, tm):
    n_a, n_b = xs[0].shape[0] // tm, xs[1].shape[0] // tm
    m = (n_a + n_b) * tm

    def rows(i):
        return (i, 0)

    pair = [pl.BlockSpec((tm, D_MODEL), lambda i: (jnp.minimum(i, n_a - 1), 0)),
            pl.BlockSpec((tm, D_MODEL), lambda i: (jnp.maximum(i - n_a, 0), 0))]
    return pl.pallas_call(
        functools.partial(_merge_kernel, n_a),
        grid=(m // tm,),
        in_specs=[*pair, *pair,
                  pl.BlockSpec((tm, D_MODEL), lambda i: (i, COL_GS)),
                  pl.BlockSpec((tm, D_MODEL), lambda i: (i, COL_GG)),
                  *pair,
                  pl.BlockSpec(wo.shape, lambda i: (0, 0), pipeline_mode=pl.Buffered(1)),
                  _row_params(g), _row_params(b),
                  _row_params(wr_hi), _row_params(wr_lo), _row_params(br)],
        out_specs=[pl.BlockSpec((tm * ROW_CHUNKS, LANES), rows),
                   pl.BlockSpec((tm, ROUTER_LANES), rows)],
        out_shape=[jax.ShapeDtypeStruct((m * ROW_CHUNKS, LANES), F32),
                   jax.ShapeDtypeStruct((m, ROUTER_LANES), F32)],
        compiler_params=_cparams(1),
        name="merge_outproj_ln_router",
    )(*ys, *yg, gates, gates, *xs, wo, g, b, wr_hi, wr_lo, br)


def _zero_runs(pads_ref, zeros, x_hbm, sem, action):
    tm = zeros.shape[0] // ROW_CHUNKS

    def piece(pos, rows):
        dst = pl.multiple_of(pos * ROW_CHUNKS, ROW_CHUNKS)
        return pltpu.make_async_copy(zeros.at[pl.ds(0, rows * ROW_CHUNKS)],
                                     x_hbm.at[pl.ds(dst, rows * ROW_CHUNKS)], sem)

    def run(e, carry):
        pos, length = pads_ref[0, e], pads_ref[1, e]

        def whole(q, c):
            getattr(piece(pos + q * tm, tm), action)()
            return c

        lax.fori_loop(0, length // tm, whole, 0)
        pos = pos + (length // tm) * tm
        rows = tm // 2
        while rows >= 1:
            @pl.when((length & rows) != 0)
            def _(pos=pos, rows=rows):
                getattr(piece(pos, rows), action)()

            pos = pos + (length & rows)
            rows //= 2
        return carry

    lax.fori_loop(0, pads_ref.shape[1], run, 0)


def _dispatch_kernel(slot_ref, pads_ref, h_ref, x_hbm, zeros, sem, zsem):
    n = h_ref.shape[0] // ROW_CHUNKS

    def issue(r, carry):
        src = h_ref.at[pl.ds(pl.multiple_of(r * ROW_CHUNKS, ROW_CHUNKS), ROW_CHUNKS)]
        for k in range(2):
            dst = pl.multiple_of(slot_ref[k, r] * ROW_CHUNKS, ROW_CHUNKS)
            pltpu.make_async_copy(src, x_hbm.at[pl.ds(dst, ROW_CHUNKS)], sem).start(
                priority=k % DMA_QUEUES)
        return carry

    lax.fori_loop(0, n, issue, 0, unroll=4)

    @pl.when(pl.program_id(0) == pl.num_programs(0) - 1)
    def _():
        zeros[...] = jnp.zeros(zeros.shape, zeros.dtype)
        _zero_runs(pads_ref, zeros, x_hbm, zsem, "start")
        _zero_runs(pads_ref, zeros, x_hbm, zsem, "wait")

    for k in range(2):
        pltpu.make_async_copy(h_ref, x_hbm.at[pl.ds(0, n * ROW_CHUNKS)], sem).wait()


def _dispatch(h1, slot, pads, n_slots, tm, window):
    t = slot.shape[1]
    steps = t // window
    slot3 = slot.reshape(2, steps, window).transpose(1, 0, 2)
    return pl.pallas_call(
        _dispatch_kernel,
        grid=(steps,),
        in_specs=[pl.BlockSpec((None, 2, window), lambda i: (i, 0, 0), memory_space=pltpu.SMEM),
                  pl.BlockSpec(pads.shape, lambda i: (0, 0), memory_space=pltpu.SMEM),
                  pl.BlockSpec((window * ROW_CHUNKS, LANES), lambda i: (i, 0))],
        out_specs=pl.BlockSpec(memory_space=pl.ANY),
        out_shape=jax.ShapeDtypeStruct((n_slots * ROW_CHUNKS, LANES), h1.dtype),
        scratch_shapes=[pltpu.VMEM((tm * ROW_CHUNKS, LANES), h1.dtype), pltpu.SemaphoreType.DMA,
                        pltpu.SemaphoreType.DMA],
        compiler_params=_cparams(1),
        name="dispatch_rows",
    )(slot3, pads, h1)


def _expert_plan(route, tm):
    t = route.shape[0]
    ids = jnp.arange(MOE_EXPERTS, dtype=jnp.int32)
    e = jnp.concatenate([route[:, ROUTE_E1], route[:, ROUTE_E2]]).astype(jnp.int32)
    onehot = (e[:, None] == ids[None, :]).astype(jnp.int32)
    csum = jnp.cumsum(onehot, axis=0)
    rank = jnp.sum(onehot * csum, axis=1) - 1
    cnt = csum[-1]
    tiles_e = (cnt + tm - 1) // tm
    tile_end = jnp.cumsum(tiles_e)
    tile_start = tile_end - tiles_e
    slot = tile_start[e] * tm + rank
    n_tiles = (2 * t) // tm + MOE_EXPERTS
    tid = jnp.arange(n_tiles, dtype=jnp.int32)
    used = tid < tile_end[-1]
    tile_e = jnp.sum(tile_end[None, :] <= jnp.minimum(tid, tile_end[-1] - 1)[:, None], axis=1)
    nvalid = jnp.where(used, jnp.clip(cnt[tile_e] - (tid - tile_start[tile_e]) * tm, 0, tm), 0)
    first = used & (tid == tile_start[tile_e])
    has = tiles_e > 0
    ordinal = jnp.cumsum(has.astype(jnp.int32)) - 1
    later = jnp.where((ids[None, :] > ids[:, None]) & has[None, :], ids[None, :], MOE_EXPERTS)
    nxt_e = jnp.min(later, axis=1)
    nxt_e = jnp.where(nxt_e == MOE_EXPERTS, -1, nxt_e)
    i32 = lambda a: a.astype(jnp.int32)
    pads = jnp.stack([jnp.append(tile_start * tm + cnt, tile_end[-1] * tm),
                      jnp.append(tiles_e * tm - cnt, (n_tiles - tile_end[-1]) * tm)])
    return (slot.reshape(2, t), i32(pads), n_tiles * tm, i32(tile_e), i32(nvalid), i32(first),
            i32(nxt_e[tile_e]), i32(ordinal[tile_e] % 2))


def _experts_kernel(te_ref, nv_ref, first_ref, nxt_ref, par_ref, x_ref, wg_hbm, wu_hbm, wd_hbm, y_ref,
                    wg_f, wu_f, wd_f, wg_b, wu_b, wd_b, x_b, sem):
    i = pl.program_id(0)
    s = par_ref[i]

    def weight_copies(e, slot):
        return (pltpu.make_async_copy(wg_hbm.at[e], wg_f.at[slot], sem.at[slot]),
                pltpu.make_async_copy(wu_hbm.at[e], wu_f.at[slot], sem.at[slot]),
                pltpu.make_async_copy(wd_hbm.at[e], wd_f.at[slot], sem.at[slot]))

    @pl.when(i == 0)
    def _():
        for cp in weight_copies(te_ref[0], par_ref[0]):
            cp.start()

    @pl.when(first_ref[i] == 1)
    def _():
        @pl.when(nxt_ref[i] >= 0)
        def _():
            for cp in weight_copies(nxt_ref[i], 1 - s):
                cp.start()

        for cp in weight_copies(te_ref[i], s):
            cp.wait()
        wg_b[...] = wg_f[s].astype(BF16)
        wu_b[...] = wu_f[s].astype(BF16)
        wd_b[...] = wd_f[s].astype(BF16)

    nv = nv_ref[i]

    @pl.when(nv > 0)
    def _():
        for c in range(ROW_CHUNKS):
            x_b[:, c * LANES:(c + 1) * LANES] = _load_chunked(x_ref, c).astype(BF16)
        x = x_b[...]
        hid = (_silu(_dot(x, wg_b[...])) * _dot(x, wu_b[...])).astype(BF16)
        _store_chunked(y_ref, _dot(hid, wd_b[...]))

    @pl.when(nv == 0)
    def _():
        y_ref[...] = jnp.zeros(y_ref.shape, y_ref.dtype)


def _experts(xs, tile_e, nvalid, first, nxt, par, wg, wu, wd, tm):
    n_tiles = xs.shape[0] // (tm * ROW_CHUNKS)
    rows = pl.BlockSpec((tm * ROW_CHUNKS, LANES), lambda i, *_: (i, 0))
    hbm = pl.BlockSpec(memory_space=pl.ANY)
    grid_spec = pltpu.PrefetchScalarGridSpec(
        num_scalar_prefetch=5,
        grid=(n_tiles,),
        in_specs=[rows, hbm, hbm, hbm],
        out_specs=rows,
        scratch_shapes=[pltpu.VMEM((2, D_MODEL, MOE_FF), F32), pltpu.VMEM((2, D_MODEL, MOE_FF), F32),
                        pltpu.VMEM((2, MOE_FF, D_MODEL), F32),
                        pltpu.VMEM((D_MODEL, MOE_FF), BF16), pltpu.VMEM((D_MODEL, MOE_FF), BF16),
                        pltpu.VMEM((MOE_FF, D_MODEL), BF16), pltpu.VMEM((tm, D_MODEL), BF16),
                        pltpu.SemaphoreType.DMA((2,))])
    return pl.pallas_call(
        _experts_kernel,
        grid_spec=grid_spec,
        out_shape=jax.ShapeDtypeStruct(xs.shape, F32),
        compiler_params=_cparams(1),
        name="experts",
    )(tile_e, nvalid, first, nxt, par, xs, wg, wu, wd)


def _combine_kernel(n_steps, slot_ref, slot_next_ref, route_ref, h1_ref, g_ref, b_ref, y_hbm, o_ref,
                    gbuf, sem):
    i = pl.program_id(0)
    tm = route_ref.shape[0]

    def request(idx_ref, parity):
        def body(r, carry):
            dst = pl.ds(pl.multiple_of(r * ROW_CHUNKS, ROW_CHUNKS), ROW_CHUNKS)
            for k in range(2):
                src = pl.multiple_of(idx_ref[k, r] * ROW_CHUNKS, ROW_CHUNKS)
                pltpu.make_async_copy(y_hbm.at[pl.ds(src, ROW_CHUNKS)], gbuf.at[parity, k, dst],
                                      sem.at[parity]).start(priority=k % DMA_QUEUES)
            return carry

        lax.fori_loop(0, tm, body, 0, unroll=4)

    @pl.when(i == 0)
    def _():
        request(slot_ref, 0)

    @pl.when(i + 1 < n_steps)
    def _():
        request(slot_next_ref, (i + 1) % 2)

    parity = i % 2
    for k in range(2):
        pltpu.make_async_copy(y_hbm.at[pl.ds(0, tm * ROW_CHUNKS)], gbuf.at[parity, k],
                              sem.at[parity]).wait()
    route = route_ref[...]
    w1 = route[:, ROUTE_W1:ROUTE_W1 + 1]
    w2 = route[:, ROUTE_W2:ROUTE_W2 + 1]
    pre = jnp.concatenate(
        [DEEPNORM_ALPHA * _load_chunked(h1_ref, c)
         + w1 * _load_chunked(gbuf.at[parity, 0], c) + w2 * _load_chunked(gbuf.at[parity, 1], c)
         for c in range(ROW_CHUNKS)], axis=1)
    o_ref[...] = _layernorm(pre, g_ref[...], b_ref[...])


def _combine(y_sorted, slot, route, h1, row0, m, g, b, tm):
    t = slot.shape[1]
    blk0 = row0 // tm
    last = (row0 + m) // tm - 1
    slot3 = slot.reshape(2, t // tm, tm).transpose(1, 0, 2)
    return pl.pallas_call(
        functools.partial(_combine_kernel, m // tm),
        grid=(m // tm,),
        in_specs=[pl.BlockSpec((None, 2, tm), lambda i: (blk0 + i, 0, 0), memory_space=pltpu.SMEM),
                  pl.BlockSpec((None, 2, tm), lambda i: (jnp.minimum(blk0 + i + 1, last), 0, 0),
                               memory_space=pltpu.SMEM),
                  pl.BlockSpec((tm, ROUTER_LANES), lambda i: (blk0 + i, 0)),
                  pl.BlockSpec((tm * ROW_CHUNKS, LANES), lambda i: (blk0 + i, 0)),
                  _row_params(g), _row_params(b),
                  pl.BlockSpec(memory_space=pl.ANY)],
        out_specs=pl.BlockSpec((tm, D_MODEL), lambda i: (i, 0)),
        out_shape=jax.ShapeDtypeStruct((m, D_MODEL), F32),
        scratch_shapes=[pltpu.VMEM((2, 2, tm * ROW_CHUNKS, LANES), F32),
                        pltpu.SemaphoreType.DMA((2,))],
        compiler_params=_cparams(1),
        name="combine_ln",
    )(slot3, slot3, route, h1, g, b, y_sorted)


def _pad_lanes(v, width=LANES):
    v = v.reshape(1, -1)
    return jnp.pad(v, ((0, 0), (0, width - v.shape[1])))


def kernel(x_prompt, x_sample, state_ssd, state_conv, state_gla, w_in, conv_w, conv_b, dt_bias, a_log, d_skip, ssd_norm_w, gla_w_a2, gla_b_a, gla_norm_w, w_out, ln1_g, ln1_b, w_router_group, b_router_group, w_router_expert, b_router_expert, w_gate, w_up, w_down, ln2_g, ln2_b):
    assert w_in.shape[0] == DEPTH == 1
    n_p, len_p, _ = x_prompt.shape
    n_s, len_s, _ = x_sample.shape
    rows_p, rows_s = n_p * len_p, n_s * len_s
    xp = x_prompt.reshape(rows_p, D_MODEL)
    xs = x_sample.reshape(rows_s, D_MODEL)

    offs = [0]
    for s in IN_SPLIT_SIZES:
        offs.append(offs[-1] + s)
    w_t = w_in[0].T
    w_tail_t = jnp.concatenate(
        [jnp.pad(w_t[offs[2]:offs[3]], ((0, LANES - SSD_HEADS), (0, 0))),
         jnp.pad(w_t[offs[7]:offs[8]], ((0, LANES - GLA_RANK), (0, 0)))], axis=0)
    x_b = (xp.astype(BF16), xs.astype(BF16))

    proj = _in_proj(x_b, w_t, ((0, offs[2]), (offs[3], offs[7] - offs[3]),
                               (offs[8], offs[10] - offs[8])))
    tail = _matmul_nt(x_b, w_tail_t, IN_PROJ_ROWS)

    cw, cb = conv_w[0], conv_b[0].reshape(1, -1)
    dtb, alog = _pad_lanes(dt_bias[0]), _pad_lanes(a_log[0])
    dskip_e = jnp.repeat(d_skip[0], SSD_HEAD_DIM).reshape(1, -1)
    nw_ssd = ssd_norm_w[0].reshape(1, -1)
    wa, ba, nw_gla = gla_w_a2[0].astype(BF16), gla_b_a[0].reshape(1, -1), gla_norm_w[0].reshape(1, -1)

    ys_p, ssd_p, conv_p = _ssd_prompt(proj, tail, n_p, len_p, cw, cb, dtb, alog, dskip_e, nw_ssd)
    ys_s, ssd_s, conv_s = _ssd_sample(proj, tail, rows_p, n_s, len_s, state_ssd[0], state_conv[0],
                                      cw, cb, dtb, alog, dskip_e, nw_ssd)
    yg_p, gla_p = _gla_prompt(proj, tail, n_p, len_p, wa, ba, nw_gla)
    yg_s, gla_s = _gla_sample(proj, tail, rows_p, n_s, len_s, state_gla[0], wa, ba, nw_gla)

    wo = w_out[0].astype(BF16)
    w_r = jnp.pad(jnp.concatenate([w_router_group[0], w_router_expert[0]], axis=1),
                  ((0, 0), (0, ROUTER_LANES - MOE_GROUPS - MOE_EXPERTS)))
    wr_hi = w_r.astype(BF16)
    wr_lo = (w_r - wr_hi.astype(F32)).astype(BF16)
    b_r = _pad_lanes(jnp.concatenate([b_router_group[0], b_router_expert[0]]), ROUTER_LANES)
    g1, b1 = ln1_g[0].reshape(1, -1), ln1_b[0].reshape(1, -1)
    g2, b2 = ln2_g[0].reshape(1, -1), ln2_b[0].reshape(1, -1)

    h1, route = _merge((ys_p, ys_s), (yg_p, yg_s), proj, (xp, xs), wo, g1, b1, wr_hi, wr_lo, b_r,
                       MERGE_ROWS)
    slot, pads, n_slots, tile_e, nvalid, first, nxt, par = _expert_plan(route, EXPERT_ROWS)
    x_sorted = _dispatch(h1, slot, pads, n_slots, EXPERT_ROWS, DISPATCH_WINDOW)
    y_sorted = _experts(x_sorted, tile_e, nvalid, first, nxt, par, w_gate[0], w_up[0], w_down[0],
                        EXPERT_ROWS)
    outs = [_combine(y_sorted, slot, route, h1, 0, rows_p, g2, b2, MERGE_ROWS),
            _combine(y_sorted, slot, route, h1, rows_p, rows_s, g2, b2, MERGE_ROWS)]
    y_p = outs[0].reshape(x_prompt.shape)
    y_s = outs[1].reshape(x_sample.shape)
    return (y_p, y_s, ssd_p[None], conv_p[None], gla_p[None], ssd_s[None], conv_s[None], gla_s[None])
```

```python
import functools

import jax
import jax.numpy as jnp
from jax import lax
from jax.experimental import pallas as pl
from jax.experimental.pallas import tpu as pltpu

F32 = jnp.float32
BF16 = jnp.bfloat16

D_MODEL = 2048
SSD_HEADS = 32
SSD_HEAD_DIM = 64
SSD_GROUPS = 8
SSD_STATE = 128
SSD_CONV = 4
SSD_CHUNK = 128
SSD_BC = SSD_GROUPS * SSD_STATE
SSD_CONV_DIM = D_MODEL + 2 * SSD_BC
SSD_GROUP_COLS = D_MODEL // SSD_GROUPS
HEADS_PER_GROUP = SSD_HEADS // SSD_GROUPS
GLA_HEADS = 8
GLA_DK = 128
GLA_DV = 256
GLA_KEY_DIM = GLA_HEADS * GLA_DK
GLA_RANK = 16
GLA_TAU = 16.0
GLA_CHUNK = 64
MOE_GROUPS = 4
MOE_EPG = 8
MOE_EXPERTS = 32
MOE_FF = 512
DEPTH = 1
DEEPNORM_ALPHA = (2.0 * DEPTH) ** 0.25
EPS = 1e-5
IN_SPLIT_SIZES = (D_MODEL, SSD_CONV_DIM, SSD_HEADS, GLA_KEY_DIM, GLA_KEY_DIM, D_MODEL, D_MODEL,
                  GLA_RANK, D_MODEL, D_MODEL)

LANES = 128
HALO_ROWS = 8
COL_Z, COL_XS, COL_BC, COL_QK, COL_V, COL_R, COL_GS, COL_GG = range(8)
TAIL_DT, TAIL_ALO = 0, 1
ROUTER_LANES = 128
ROUTE_E1, ROUTE_E2, ROUTE_W1, ROUTE_W2 = 0, 1, 2, 3
MERGE_ROWS = 256
EXPERT_ROWS = 256
DISPATCH_WINDOW = 512
DMA_QUEUES = 2
VMEM_LIMIT = 56 * 1024 * 1024


def _cparams(n_axes):
    return pltpu.CompilerParams(dimension_semantics=("arbitrary",) * n_axes,
                                vmem_limit_bytes=VMEM_LIMIT)


def _silu(x):
    return x * jax.nn.sigmoid(x)


def _softplus(x):
    return jnp.maximum(x, 0.0) + jnp.log1p(jnp.exp(-jnp.abs(x)))


def _tril(n):
    r = lax.broadcasted_iota(jnp.int32, (n, n), 0)
    c = lax.broadcasted_iota(jnp.int32, (n, n), 1)
    return r >= c


def _expand_group(p, g, lane_head):
    h0 = g * HEADS_PER_GROUP
    out = p[:, h0 + HEADS_PER_GROUP - 1:h0 + HEADS_PER_GROUP]
    for hh in range(HEADS_PER_GROUP - 2, -1, -1):
        out = jnp.where(lane_head == hh, p[:, h0 + hh:h0 + hh + 1], out)
    return out


def _split3(x):
    hi = x.astype(BF16)
    r = x - hi.astype(F32)
    mid = r.astype(BF16)
    lo = (r - mid.astype(F32)).astype(BF16)
    return hi, mid, lo


def _dot_sel(sel, x):
    hi, mid, lo = _split3(x)
    return _dot(sel, lo) + _dot(sel, mid) + _dot(sel, hi)


def _col_bcast(row):
    return jnp.broadcast_to(row, (LANES, LANES)).T


def _dot(a, b, **kw):
    return jnp.dot(a, b, preferred_element_type=F32, **kw)


def _dot_nt(a, b, **kw):
    return lax.dot_general(a, b, (((1,), (1,)), ((), ())), preferred_element_type=F32, **kw)


def _dot_tn(a, b, **kw):
    return lax.dot_general(a, b, (((0,), (0,)), ((), ())), preferred_element_type=F32, **kw)


def _matmul_kernel(n_first, xa_ref, xb_ref, wt_ref, o_ref):
    x = jnp.where(pl.program_id(0) < n_first, xa_ref[...], xb_ref[...])
    o_ref[...] = _dot_nt(x, wt_ref[...].astype(BF16))


def _matmul_nt(xs, wt, tm):
    n, k = wt.shape
    n_a, n_b = xs[0].shape[0] // tm, xs[1].shape[0] // tm
    return pl.pallas_call(
        functools.partial(_matmul_kernel, n_a),
        grid=(n_a + n_b,),
        in_specs=[pl.BlockSpec((tm, k), lambda i: (jnp.minimum(i, n_a - 1), 0)),
                  pl.BlockSpec((tm, k), lambda i: (jnp.maximum(i - n_a, 0), 0)),
                  pl.BlockSpec((n, k), lambda i: (0, 0))],
        out_specs=pl.BlockSpec((tm, n), lambda i: (i, 0)),
        out_shape=jax.ShapeDtypeStruct(((n_a + n_b) * tm, n), F32),
        compiler_params=_cparams(1),
        name="in_proj_tail",
    )(*xs, wt)


IN_PROJ_COLS = 1024
IN_PROJ_ROWS = 1024
CAST_ROWS = 256
SUBLANES = 8


def _in_proj_kernel(plan, n_first, xa_ref, xb_ref, wt_hbm, o_ref, wf, wb, sem):
    j = pl.program_id(0)
    i = pl.program_id(1)
    tn = wb.shape[1]
    n_blocks = plan[-1][1]

    def block_copy(jj, slot):
        shift = 0
        for lo, hi, sh in plan:
            shift = jnp.where((jj >= lo) & (jj < hi), sh, shift)
        src = pl.multiple_of(jj * tn + shift, SUBLANES)
        return pltpu.make_async_copy(wt_hbm.at[pl.ds(src, tn)], wf.at[slot], sem.at[slot])

    @pl.when(i == 0)
    def _():
        slot = j % 2

        @pl.when(j == 0)
        def _():
            block_copy(j, slot).start()

        @pl.when(j + 1 < n_blocks)
        def _():
            block_copy(j + 1, 1 - slot).start()

        block_copy(j, slot).wait()
        for r in range(0, tn, CAST_ROWS):
            wb[:, r:r + CAST_ROWS] = wf[slot, r:r + CAST_ROWS, :].T.astype(BF16)

    x = jnp.where(i < n_first, xa_ref[...], xb_ref[...])
    o_ref[...] = _dot(x, wb[...])


def _in_proj(xs, wt, segments):
    tm, tn = IN_PROJ_ROWS, IN_PROJ_COLS
    k = wt.shape[1]
    n_a, n_b = xs[0].shape[0] // tm, xs[1].shape[0] // tm
    plan, dest = [], 0
    for src, count in segments:
        assert count % tn == 0 and dest % tn == 0 and (src - dest) % SUBLANES == 0
        plan.append((dest // tn, (dest + count) // tn, src - dest))
        dest += count
    return pl.pallas_call(
        functools.partial(_in_proj_kernel, tuple(plan), n_a),
        grid=(dest // tn, n_a + n_b),
        in_specs=[pl.BlockSpec((tm, k), lambda j, i: (jnp.minimum(i, n_a - 1), 0)),
                  pl.BlockSpec((tm, k), lambda j, i: (jnp.maximum(i - n_a, 0), 0)),
                  pl.BlockSpec(memory_space=pl.ANY)],
        out_specs=pl.BlockSpec((tm, tn), lambda j, i: (i, j)),
        out_shape=jax.ShapeDtypeStruct(((n_a + n_b) * tm, dest), F32),
        scratch_shapes=[pltpu.VMEM((2, tn, k), F32), pltpu.VMEM((k, tn), BF16),
                        pltpu.SemaphoreType.DMA((2,))],
        compiler_params=_cparams(2),
        name="in_proj_main",
    )(*xs, wt)


def _conv_silu(ubuf, cw_ref, cb_ref, lo, width, rows):
    acc = cb_ref[:, lo:lo + width]
    for i in range(SSD_CONV):
        r0 = HALO_ROWS - (SSD_CONV - 1) + i
        acc = acc + cw_ref[i:i + 1, lo:lo + width] * ubuf[r0:r0 + rows, lo:lo + width]
    return _silu(acc)


def _gated_group_norm(y_g, z_g, nw_g):
    yg = y_g * _silu(z_g)
    ms = jnp.mean(yg * yg, axis=-1, keepdims=True)
    return yg * lax.rsqrt(ms + EPS) * nw_g


SCAN_CHUNKS = 2


def _ssd_prompt_chunk(z_ref, xs_ref, bc_ref, tail_ref, cw_ref, cb_ref, dtb_ref, alog_ref,
                      dskip_ref, nw_ref, y_ref, ubuf, ht):
    q = SSD_CHUNK
    ubuf[HALO_ROWS:HALO_ROWS + q, 0:D_MODEL] = xs_ref[...]
    ubuf[HALO_ROWS:HALO_ROWS + q, D_MODEL:SSD_CONV_DIM] = bc_ref[...]

    dt = _softplus(tail_ref[...] + dtb_ref[...])
    da = dt * (-jnp.exp(alog_ref[...]))
    causal = _tril(q)
    acum = _dot_sel(causal.astype(BF16), da)
    acum_t = acum.T
    dt_t = dt.T
    a_last = acum[q - 1:q, :]
    decay_in = jnp.exp(acum)
    w_end = dt * jnp.exp(a_last - acum)
    chunk_decay = jnp.exp(a_last)
    lane_head = lax.broadcasted_iota(jnp.int32, (q, SSD_GROUP_COLS), 1) >> 6
    assert SSD_HEAD_DIM == 1 << 6

    for g in range(SSD_GROUPS):
        lo = g * SSD_GROUP_COLS
        cols = slice(lo, lo + SSD_GROUP_COLS)
        xs_g = _conv_silu(ubuf, cw_ref, cb_ref, lo, SSD_GROUP_COLS, q)
        b_g = _conv_silu(ubuf, cw_ref, cb_ref, D_MODEL + g * SSD_STATE, SSD_STATE, q).astype(BF16)
        c_g = _conv_silu(ubuf, cw_ref, cb_ref, D_MODEL + SSD_BC + g * SSD_STATE, SSD_STATE,
                         q).astype(BF16)
        scores = _dot_nt(c_g, b_g)
        h_g = ht[g]
        xs_b = xs_g.astype(BF16)
        m_heads, x_heads = [], []
        for hh in range(HEADS_PER_GROUP):
            h = g * HEADS_PER_GROUP + hh
            seg = acum[:, h:h + 1] - acum_t[h:h + 1, :]
            decay = jnp.exp(jnp.where(causal, seg, -jnp.inf))
            m_heads.append((scores * decay * dt_t[h:h + 1, :]).astype(BF16))
            x_heads.append(jnp.where(lane_head == hh, xs_b, jnp.zeros_like(xs_b)))
        y_g = (_dot(jnp.concatenate(m_heads, axis=1), jnp.concatenate(x_heads, axis=0))
               + _dot(c_g, h_g.astype(BF16)) * _expand_group(decay_in, g, lane_head)
               + dskip_ref[:, cols] * xs_g)
        y_ref[:, cols] = _gated_group_norm(y_g, z_ref[:, cols], nw_ref[:, cols]).astype(y_ref.dtype)
        xw = (xs_g * _expand_group(w_end, g, lane_head)).astype(BF16)
        ht[g] = h_g * _expand_group(chunk_decay, g, lane_head[0:1]) + _dot_tn(b_g, xw)

    last = HALO_ROWS + q - (SSD_CONV - 1)
    tail_rows = ubuf[last:last + SSD_CONV - 1, :]
    ubuf[HALO_ROWS - (SSD_CONV - 1):HALO_ROWS, :] = tail_rows
    return tail_rows


def _ssd_prompt_kernel(z_ref, xs_ref, bc_ref, tail_ref, cw_ref, cb_ref, dtb_ref, alog_ref,
                       dskip_ref, nw_ref, y_ref, hout_ref, convout_ref, ubuf, ht):
    q = SSD_CHUNK
    c = pl.program_id(1)

    @pl.when(c == 0)
    def _():
        ubuf[0:HALO_ROWS, :] = jnp.zeros((HALO_ROWS, SSD_CONV_DIM), F32)
        ht[...] = jnp.zeros(ht.shape, F32)

    for ci in range(xs_ref.shape[0] // q):
        rows = pl.ds(ci * q, q)
        tail_rows = _ssd_prompt_chunk(z_ref.at[rows], xs_ref.at[rows], bc_ref.at[rows],
                                      tail_ref.at[rows], cw_ref, cb_ref, dtb_ref, alog_ref,
                                      dskip_ref, nw_ref, y_ref.at[rows], ubuf, ht)

    @pl.when(c == pl.num_programs(1) - 1)
    def _():
        convout_ref[0] = tail_rows
        for g in range(SSD_GROUPS):
            h_t = ht[g].T
            for hh in range(HEADS_PER_GROUP):
                hout_ref[0, g * HEADS_PER_GROUP + hh] = h_t[hh * SSD_HEAD_DIM:(hh + 1) * SSD_HEAD_DIM]


def _row_params(p):
    return pl.BlockSpec(p.shape, lambda *_: (0,) * p.ndim)


def _ssd_prompt(proj, tail, n_seq, seq_len, cw, cb, dtb, alog, dskip_e, nw):
    q = SSD_CHUNK * SCAN_CHUNKS
    nc = seq_len // q

    def col(blk):
        return pl.BlockSpec((q, D_MODEL), lambda b, c: (b * nc + c, blk))

    return pl.pallas_call(
        _ssd_prompt_kernel,
        grid=(n_seq, nc),
        in_specs=[col(COL_Z), col(COL_XS), col(COL_BC),
                  pl.BlockSpec((q, LANES), lambda b, c: (b * nc + c, TAIL_DT)),
                  _row_params(cw), _row_params(cb), _row_params(dtb), _row_params(alog),
                  _row_params(dskip_e), _row_params(nw)],
        out_specs=[pl.BlockSpec((q, D_MODEL), lambda b, c: (b * nc + c, 0)),
                   pl.BlockSpec((1, SSD_HEADS, SSD_HEAD_DIM, SSD_STATE), lambda b, c: (b, 0, 0, 0)),
                   pl.BlockSpec((1, SSD_CONV - 1, SSD_CONV_DIM), lambda b, c: (b, 0, 0))],
        out_shape=[jax.ShapeDtypeStruct((n_seq * seq_len, D_MODEL), BF16),
                   jax.ShapeDtypeStruct((n_seq, SSD_HEADS, SSD_HEAD_DIM, SSD_STATE), F32),
                   jax.ShapeDtypeStruct((n_seq, SSD_CONV - 1, SSD_CONV_DIM), F32)],
        scratch_shapes=[pltpu.VMEM((HALO_ROWS + SSD_CHUNK, SSD_CONV_DIM), F32),
                        pltpu.VMEM((SSD_GROUPS, SSD_STATE, SSD_GROUP_COLS), F32)],
        compiler_params=_cparams(2),
        name="ssd_prompt",
    )(proj, proj, proj, tail, cw, cb, dtb, alog, dskip_e, nw)


SAMPLE_SEQS = 4


def _dot_tn_split(a, b):
    a_hi = a.astype(BF16)
    a_lo = (a - a_hi.astype(F32)).astype(BF16)
    b_hi = b.astype(BF16)
    b_lo = (b - b_hi.astype(F32)).astype(BF16)
    return _dot_tn(a_lo, b_hi) + _dot_tn(a_hi, b_lo) + _dot_tn(a_hi, b_hi)


def _ssd_sample_seq(z_ref, xs_ref, bc_ref, tail_ref, h_ref, conv_ref, cw_ref, cb_ref, dtb_ref,
                    alog_ref, dskip_ref, nw_ref, y_ref, hout_ref, convout_ref, ubuf, decay_rows):
    q = xs_ref.shape[0]
    ubuf[0:HALO_ROWS - (SSD_CONV - 1), :] = jnp.zeros((HALO_ROWS - (SSD_CONV - 1), SSD_CONV_DIM), F32)
    ubuf[HALO_ROWS - (SSD_CONV - 1):HALO_ROWS, :] = conv_ref[...]
    ubuf[HALO_ROWS:HALO_ROWS + q, 0:D_MODEL] = xs_ref[...]
    ubuf[HALO_ROWS:HALO_ROWS + q, D_MODEL:SSD_CONV_DIM] = bc_ref[...]
    convout_ref[...] = ubuf[HALO_ROWS + q - (SSD_CONV - 1):HALO_ROWS + q, :]

    dt = _softplus(tail_ref[...] + dtb_ref[...])
    da = dt * (-jnp.exp(alog_ref[...]))
    acum = _dot_sel(_tril(q).astype(BF16), da)
    a_last = acum[q - 1:q, :]
    decay_in = jnp.exp(acum)
    w_end = dt * jnp.exp(a_last - acum)
    decay_rows[...] = _col_bcast(jnp.exp(a_last))[:SSD_HEADS]
    row = lax.broadcasted_iota(jnp.int32, (q, SSD_GROUP_COLS), 0)
    lane_head = lax.broadcasted_iota(jnp.int32, (q, SSD_GROUP_COLS), 1) >> 6

    for g in range(SSD_GROUPS):
        lo = g * SSD_GROUP_COLS
        cols = slice(lo, lo + SSD_GROUP_COLS)
        xs_g = _conv_silu(ubuf, cw_ref, cb_ref, lo, SSD_GROUP_COLS, q)
        b_g = _conv_silu(ubuf, cw_ref, cb_ref, D_MODEL + g * SSD_STATE, SSD_STATE, q)
        c_g = _conv_silu(ubuf, cw_ref, cb_ref, D_MODEL + SSD_BC + g * SSD_STATE, SSD_STATE, q)
        h_g = h_ref[g * HEADS_PER_GROUP:(g + 1) * HEADS_PER_GROUP].reshape(
            SSD_GROUP_COLS, SSD_STATE)
        y_g = (_dot_nt(c_g.astype(BF16), h_g.astype(BF16)) * _expand_group(decay_in, g, lane_head)
               + dskip_ref[:, cols] * xs_g)
        acum_g = _expand_group(acum, g, lane_head)
        dt_g = _expand_group(dt, g, lane_head)
        for j in range(q):
            s_j = jnp.sum(c_g * b_g[j:j + 1, :], axis=-1, keepdims=True)
            coef = s_j * jnp.exp(acum_g - acum_g[j:j + 1, :]) * dt_g[j:j + 1, :]
            y_g = y_g + jnp.where(row >= j, coef, 0.0) * xs_g[j:j + 1, :]
        y_ref[:, cols] = _gated_group_norm(y_g, z_ref[:, cols], nw_ref[:, cols])
        upd = _dot_tn_split(xs_g * _expand_group(w_end, g, lane_head), b_g)
        for hh in range(HEADS_PER_GROUP):
            h = g * HEADS_PER_GROUP + hh
            rows = slice(hh * SSD_HEAD_DIM, (hh + 1) * SSD_HEAD_DIM)
            hout_ref[h] = h_g[rows] * decay_rows[h:h + 1, :] + upd[rows]


def _ssd_sample_kernel(z_ref, xs_ref, bc_ref, tail_ref, h_ref, conv_ref, cw_ref, cb_ref, dtb_ref,
                       alog_ref, dskip_ref, nw_ref, y_ref, hout_ref, convout_ref, ubuf, decay_rows):
    n = h_ref.shape[0]
    q = xs_ref.shape[0] // n
    for s in range(n):
        rows = pl.ds(s * q, q)
        _ssd_sample_seq(z_ref.at[rows], xs_ref.at[rows], bc_ref.at[rows], tail_ref.at[rows],
                        h_ref.at[s], conv_ref.at[s], cw_ref, cb_ref, dtb_ref, alog_ref, dskip_ref,
                        nw_ref, y_ref.at[rows], hout_ref.at[s], convout_ref.at[s], ubuf.at[s],
                        decay_rows.at[s])


def _ssd_sample(proj, tail, row0, n_seq, seq_len, h0, conv0, cw, cb, dtb, alog, dskip_e, nw):
    nb = SAMPLE_SEQS
    q = seq_len
    blk0 = row0 // (nb * q)

    def col(blk):
        return pl.BlockSpec((nb * q, D_MODEL), lambda b: (blk0 + b, blk))

    state_spec = pl.BlockSpec((nb, SSD_HEADS, SSD_HEAD_DIM, SSD_STATE), lambda b: (b, 0, 0, 0))
    conv_spec = pl.BlockSpec((nb, SSD_CONV - 1, SSD_CONV_DIM), lambda b: (b, 0, 0))
    return pl.pallas_call(
        _ssd_sample_kernel,
        grid=(n_seq // nb,),
        in_specs=[col(COL_Z), col(COL_XS), col(COL_BC),
                  pl.BlockSpec((nb * q, LANES), lambda b: (blk0 + b, TAIL_DT)),
                  state_spec, conv_spec,
                  _row_params(cw), _row_params(cb), _row_params(dtb), _row_params(alog),
                  _row_params(dskip_e), _row_params(nw)],
        out_specs=[pl.BlockSpec((nb * q, D_MODEL), lambda b: (b, 0)), state_spec, conv_spec],
        out_shape=[jax.ShapeDtypeStruct((n_seq * seq_len, D_MODEL), F32),
                   jax.ShapeDtypeStruct(h0.shape, F32),
                   jax.ShapeDtypeStruct(conv0.shape, F32)],
        scratch_shapes=[pltpu.VMEM((nb, HALO_ROWS + q, SSD_CONV_DIM), F32),
                        pltpu.VMEM((nb, SSD_HEADS, SSD_STATE), F32)],
        compiler_params=_cparams(1),
        name="ssd_sample",
    )(proj, proj, proj, tail, h0, conv0, cw, cb, dtb, alog, dskip_e, nw)


def _gla_chunk(qk_ref, v_ref, r_ref, alo_ref, wa_ref, ba_ref, nw_ref, y_ref, get_state, put_state):
    q = v_ref.shape[0]
    a_lo = alo_ref[...][:, :GLA_RANK].astype(BF16)
    gk = -_softplus(-(_dot(a_lo, wa_ref[...]) + ba_ref[...])) / GLA_TAU
    causal = _tril(q)
    bcum = _dot_sel(causal.astype(BF16), gk)
    for h in range(GLA_HEADS):
        kc = slice(h * GLA_DK, (h + 1) * GLA_DK)
        vc = slice(h * GLA_DV, (h + 1) * GLA_DV)
        b_h = bcum[:, kc]
        b_last = b_h[q - 1:q, :]
        q_h = qk_ref[:, kc] * (GLA_DK ** -0.5)
        k_h = qk_ref[:, GLA_KEY_DIM + h * GLA_DK:GLA_KEY_DIM + (h + 1) * GLA_DK]
        v_h = v_ref[:, vc].astype(BF16)
        q_in = (q_h * jnp.exp(b_h)).astype(BF16)
        k_in = (k_h * jnp.exp(-b_h)).astype(BF16)
        att = jnp.where(causal, _dot_nt(q_in, k_in), 0.0).astype(BF16)
        s_h = get_state(h)
        o = _dot(att, v_h) + _dot(q_in, s_h.astype(BF16))
        k_end = (k_h * jnp.exp(b_last - b_h)).astype(BF16)
        d_col = _col_bcast(jnp.exp(b_last))
        put_state(h, s_h * jnp.concatenate([d_col, d_col], axis=1) + _dot_tn(k_end, v_h))
        ms = jnp.mean(o * o, axis=-1, keepdims=True)
        y_ref[:, vc] = (o * lax.rsqrt(ms + EPS) * nw_ref[...] * _silu(r_ref[:, vc])).astype(y_ref.dtype)


def _gla_prompt_kernel(qk_ref, v_ref, r_ref, alo_ref, wa_ref, ba_ref, nw_ref, y_ref, sout_ref, st):
    c = pl.program_id(1)

    @pl.when(c == 0)
    def _():
        st[...] = jnp.zeros(st.shape, F32)

    def put(h, val):
        st[h] = val

    q = GLA_CHUNK
    for ci in range(v_ref.shape[0] // q):
        rows = pl.ds(ci * q, q)
        _gla_chunk(qk_ref.at[rows], v_ref.at[rows], r_ref.at[rows], alo_ref.at[rows], wa_ref, ba_ref,
                   nw_ref, y_ref.at[rows], lambda h: st[h], put)

    @pl.when(c == pl.num_programs(1) - 1)
    def _():
        sout_ref[0] = st[...]


def _gla_prompt(proj, tail, n_seq, seq_len, wa, ba, nw):
    q = GLA_CHUNK * SCAN_CHUNKS
    nc = seq_len // q

    def col(blk):
        return pl.BlockSpec((q, D_MODEL), lambda b, c: (b * nc + c, blk))

    return pl.pallas_call(
        _gla_prompt_kernel,
        grid=(n_seq, nc),
        in_specs=[col(COL_QK), col(COL_V), col(COL_R),
                  pl.BlockSpec((q, LANES), lambda b, c: (b * nc + c, TAIL_ALO)),
                  _row_params(wa), _row_params(ba), _row_params(nw)],
        out_specs=[pl.BlockSpec((q, D_MODEL), lambda b, c: (b * nc + c, 0)),
                   pl.BlockSpec((1, GLA_HEADS, GLA_DK, GLA_DV), lambda b, c: (b, 0, 0, 0))],
        out_shape=[jax.ShapeDtypeStruct((n_seq * seq_len, D_MODEL), BF16),
                   jax.ShapeDtypeStruct((n_seq, GLA_HEADS, GLA_DK, GLA_DV), F32)],
        scratch_shapes=[pltpu.VMEM((GLA_HEADS, GLA_DK, GLA_DV), F32)],
        compiler_params=_cparams(2),
        name="gla_prompt",
    )(proj, proj, proj, tail, wa, ba, nw)


def _gla_sample_kernel(qk_ref, v_ref, r_ref, alo_ref, s_ref, wa_ref, ba_ref, nw_ref, y_ref, sout_ref):
    n = s_ref.shape[0]
    q = v_ref.shape[0] // n
    for s in range(n):
        rows = pl.ds(s * q, q)

        def put(h, val, s=s):
            sout_ref[s, h] = val

        _gla_chunk(qk_ref.at[rows], v_ref.at[rows], r_ref.at[rows], alo_ref.at[rows], wa_ref, ba_ref,
                   nw_ref, y_ref.at[rows], lambda h, s=s: s_ref[s, h], put)


def _gla_sample(proj, tail, row0, n_seq, seq_len, s0, wa, ba, nw):
    nb = SAMPLE_SEQS
    q = seq_len
    blk0 = row0 // (nb * q)

    def col(blk):
        return pl.BlockSpec((nb * q, D_MODEL), lambda b: (blk0 + b, blk))

    state_spec = pl.BlockSpec((nb, GLA_HEADS, GLA_DK, GLA_DV), lambda b: (b, 0, 0, 0))
    return pl.pallas_call(
        _gla_sample_kernel,
        grid=(n_seq // nb,),
        in_specs=[col(COL_QK), col(COL_V), col(COL_R),
                  pl.BlockSpec((nb * q, LANES), lambda b: (blk0 + b, TAIL_ALO)),
                  state_spec, _row_params(wa), _row_params(ba), _row_params(nw)],
        out_specs=[pl.BlockSpec((nb * q, D_MODEL), lambda b: (b, 0)), state_spec],
        out_shape=[jax.ShapeDtypeStruct((n_seq * seq_len, D_MODEL), F32),
                   jax.ShapeDtypeStruct(s0.shape, F32)],
        compiler_params=_cparams(1),
        name="gla_sample",
    )(proj, proj, proj, tail, s0, wa, ba, nw)


def _layernorm(x, g, b):
    mu = jnp.mean(x, axis=-1, keepdims=True)
    xc = x - mu
    var = jnp.mean(xc * xc, axis=-1, keepdims=True)
    return xc * lax.rsqrt(var + EPS) * g + b


def _route(logits):
    lane = lax.broadcasted_iota(jnp.int32, logits.shape, 1)
    neg = -jnp.inf
    big = ROUTER_LANES
    glog = jnp.where(lane < MOE_GROUPS, logits, neg)
    gmax = jnp.max(glog, axis=-1, keepdims=True)
    g_sel = jnp.min(jnp.where(glog == gmax, lane, big), axis=-1, keepdims=True)
    p_g = 1.0 / jnp.sum(jnp.exp(glog - gmax), axis=-1, keepdims=True)
    e_lane = lane - MOE_GROUPS
    in_group = (e_lane >= 0) & (e_lane < MOE_EXPERTS) & (e_lane // MOE_EPG == g_sel)
    el = jnp.where(in_group, logits, neg)
    v1 = jnp.max(el, axis=-1, keepdims=True)
    i1 = jnp.min(jnp.where(el == v1, lane, big), axis=-1, keepdims=True)
    el2 = jnp.where(lane == i1, neg, el)
    v2 = jnp.max(el2, axis=-1, keepdims=True)
    i2 = jnp.min(jnp.where(el2 == v2, lane, big), axis=-1, keepdims=True)
    e2 = jnp.exp(v2 - v1)
    w1 = p_g / (1.0 + e2)
    w2 = p_g * e2 / (1.0 + e2)
    first = (i1 - MOE_GROUPS).astype(F32)
    second = (i2 - MOE_GROUPS).astype(F32)
    return jnp.where(lane == ROUTE_E1, first,
                     jnp.where(lane == ROUTE_E2, second,
                               jnp.where(lane == ROUTE_W1, w1,
                                         jnp.where(lane == ROUTE_W2, w2, 0.0))))


ROW_CHUNKS = D_MODEL // LANES


def _store_chunked(ref, val):
    rows = val.shape[0]
    for c in range(ROW_CHUNKS):
        ref[pl.ds(c, rows, stride=ROW_CHUNKS), :] = val[:, c * LANES:(c + 1) * LANES]


def _load_chunked(ref, c):
    return ref[pl.ds(c, ref.shape[0] // ROW_CHUNKS, stride=ROW_CHUNKS), :]


def _merge_kernel(n_first, ysa_ref, ysb_ref, yga_ref, ygb_ref, gs_ref, gg_ref, xa_ref, xb_ref, wo_ref,
                  g_ref, b_ref, wrh_ref, wrl_ref, br_ref, h1_ref, route_ref):
    def tile(ys_ref, yg_ref, x_ref):
        merged = (jax.nn.sigmoid(gs_ref[...]) * ys_ref[...]
                  + jax.nn.sigmoid(gg_ref[...]) * yg_ref[...])
        mix = _dot(merged.astype(BF16), wo_ref[...])
        h1 = _layernorm(DEEPNORM_ALPHA * x_ref[...] + mix, g_ref[...], b_ref[...])
        _store_chunked(h1_ref, h1)
        h_hi = h1.astype(BF16)
        h_lo = (h1 - h_hi.astype(F32)).astype(BF16)
        logits = (_dot(h_hi, wrh_ref[...]) + _dot(h_lo, wrh_ref[...]) + _dot(h_hi, wrl_ref[...])
                  + br_ref[...])
        route_ref[...] = _route(logits)

    first = pl.program_id(0) < n_first
    pl.when(first)(lambda: tile(ysa_ref, yga_ref, xa_ref))
    pl.when(jnp.logical_not(first))(lambda: tile(ysb_ref, ygb_ref, xb_ref))


def _merge(ys, yg, gates, xs, wo, g, b, wr_hi, wr_lo, br, tm):
    n_a, n_b = xs[0].shape[0] // tm, xs[1].shape[0] // tm
    m = (n_a + n_b) * tm

    def rows(i):
        return (i, 0)

    pair = [pl.BlockSpec((tm, D_MODEL), lambda i: (jnp.minimum(i, n_a - 1), 0)),
            pl.BlockSpec((tm, D_MODEL), lambda i: (jnp.maximum(i - n_a, 0), 0))]
    return pl.pallas_call(
        functools.partial(_merge_kernel, n_a),
        grid=(m // tm,),
        in_specs=[*pair, *pair,
                  pl.BlockSpec((tm, D_MODEL), lambda i: (i, COL_GS)),
                  pl.BlockSpec((tm, D_MODEL), lambda i: (i, COL_GG)),
                  *pair,
                  pl.BlockSpec(wo.shape, lambda i: (0, 0), pipeline_mode=pl.Buffered(1)),
                  _row_params(g), _row_params(b),
                  _row_params(wr_hi), _row_params(wr_lo), _row_params(br)],
        out_specs=[pl.BlockSpec((tm * ROW_CHUNKS, LANES), rows),
                   pl.BlockSpec((tm, ROUTER_LANES), rows)],
        out_shape=[jax.ShapeDtypeStruct((m * ROW_CHUNKS, LANES), F32),
                   jax.ShapeDtypeStruct((m, ROUTER_LANES), F32)],
        compiler_params=_cparams(1),
        name="merge_outproj_ln_router",
    )(*ys, *yg, gates, gates, *xs, wo, g, b, wr_hi, wr_lo, br)


def _zero_runs(pads_ref, zeros, x_hbm, sem, action):
    tm = zeros.shape[0] // ROW_CHUNKS

    def piece(pos, rows):
        dst = pl.multiple_of(pos * ROW_CHUNKS, ROW_CHUNKS)
        return pltpu.make_async_copy(zeros.at[pl.ds(0, rows * ROW_CHUNKS)],
                                     x_hbm.at[pl.ds(dst, rows * ROW_CHUNKS)], sem)

    def run(e, carry):
        pos, length = pads_ref[0, e], pads_ref[1, e]

        def whole(q, c):
            getattr(piece(pos + q * tm, tm), action)()
            return c

        lax.fori_loop(0, length // tm, whole, 0)
        pos = pos + (length // tm) * tm
        rows = tm // 2
        while rows >= 1:
            @pl.when((length & rows) != 0)
            def _(pos=pos, rows=rows):
                getattr(piece(pos, rows), action)()

            pos = pos + (length & rows)
            rows //= 2
        return carry

    lax.fori_loop(0, pads_ref.shape[1], run, 0)


def _dispatch_kernel(slot_ref, pads_ref, h_ref, x_hbm, zeros, sem, zsem):
    n = h_ref.shape[0] // ROW_CHUNKS

    def issue(r, carry):
        src = h_ref.at[pl.ds(pl.multiple_of(r * ROW_CHUNKS, ROW_CHUNKS), ROW_CHUNKS)]
        for k in range(2):
            dst = pl.multiple_of(slot_ref[k, r] * ROW_CHUNKS, ROW_CHUNKS)
            pltpu.make_async_copy(src, x_hbm.at[pl.ds(dst, ROW_CHUNKS)], sem).start(
                priority=k % DMA_QUEUES)
        return carry

    lax.fori_loop(0, n, issue, 0, unroll=4)

    @pl.when(pl.program_id(0) == pl.num_programs(0) - 1)
    def _():
        zeros[...] = jnp.zeros(zeros.shape, zeros.dtype)
        _zero_runs(pads_ref, zeros, x_hbm, zsem, "start")
        _zero_runs(pads_ref, zeros, x_hbm, zsem, "wait")

    for k in range(2):
        pltpu.make_async_copy(h_ref, x_hbm.at[pl.ds(0, n * ROW_CHUNKS)], sem).wait()


def _dispatch(h1, slot, pads, n_slots, tm, window):
    t = slot.shape[1]
    steps = t // window
    slot3 = slot.reshape(2, steps, window).transpose(1, 0, 2)
    return pl.pallas_call(
        _dispatch_kernel,
        grid=(steps,),
        in_specs=[pl.BlockSpec((None, 2, window), lambda i: (i, 0, 0), memory_space=pltpu.SMEM),
                  pl.BlockSpec(pads.shape, lambda i: (0, 0), memory_space=pltpu.SMEM),
                  pl.BlockSpec((window * ROW_CHUNKS, LANES), lambda i: (i, 0))],
        out_specs=pl.BlockSpec(memory_space=pl.ANY),
        out_shape=jax.ShapeDtypeStruct((n_slots * ROW_CHUNKS, LANES), h1.dtype),
        scratch_shapes=[pltpu.VMEM((tm * ROW_CHUNKS, LANES), h1.dtype), pltpu.SemaphoreType.DMA,
                        pltpu.SemaphoreType.DMA],
        compiler_params=_cparams(1),
        name="dispatch_rows",
    )(slot3, pads, h1)


def _expert_plan(route, tm):
    t = route.shape[0]
    ids = jnp.arange(MOE_EXPERTS, dtype=jnp.int32)
    e = jnp.concatenate([route[:, ROUTE_E1], route[:, ROUTE_E2]]).astype(jnp.int32)
    onehot = (e[:, None] == ids[None, :]).astype(jnp.int32)
    csum = jnp.cumsum(onehot, axis=0)
    rank = jnp.sum(onehot * csum, axis=1) - 1
    cnt = csum[-1]
    tiles_e = (cnt + tm - 1) // tm
    tile_end = jnp.cumsum(tiles_e)
    tile_start = tile_end - tiles_e
    slot = tile_start[e] * tm + rank
    n_tiles = (2 * t) // tm + MOE_EXPERTS
    tid = jnp.arange(n_tiles, dtype=jnp.int32)
    used = tid < tile_end[-1]
    tile_e = jnp.sum(tile_end[None, :] <= jnp.minimum(tid, tile_end[-1] - 1)[:, None], axis=1)
    nvalid = jnp.where(used, jnp.clip(cnt[tile_e] - (tid - tile_start[tile_e]) * tm, 0, tm), 0)
    first = used & (tid == tile_start[tile_e])
    has = tiles_e > 0
    ordinal = jnp.cumsum(has.astype(jnp.int32)) - 1
    later = jnp.where((ids[None, :] > ids[:, None]) & has[None, :], ids[None, :], MOE_EXPERTS)
    nxt_e = jnp.min(later, axis=1)
    nxt_e = jnp.where(nxt_e == MOE_EXPERTS, -1, nxt_e)
    i32 = lambda a: a.astype(jnp.int32)
    pads = jnp.stack([jnp.append(tile_start * tm + cnt, tile_end[-1] * tm),
                      jnp.append(tiles_e * tm - cnt, (n_tiles - tile_end[-1]) * tm)])
    return (slot.reshape(2, t), i32(pads), n_tiles * tm, i32(tile_e), i32(nvalid), i32(first),
            i32(nxt_e[tile_e]), i32(ordinal[tile_e] % 2))


def _experts_kernel(te_ref, nv_ref, first_ref, nxt_ref, par_ref, x_ref, wg_hbm, wu_hbm, wd_hbm, y_ref,
                    wg_f, wu_f, wd_f, wg_b, wu_b, wd_b, x_b, sem):
    i = pl.program_id(0)
    s = par_ref[i]

    def weight_copies(e, slot):
        return (pltpu.make_async_copy(wg_hbm.at[e], wg_f.at[slot], sem.at[slot]),
                pltpu.make_async_copy(wu_hbm.at[e], wu_f.at[slot], sem.at[slot]),
                pltpu.make_async_copy(wd_hbm.at[e], wd_f.at[slot], sem.at[slot]))

    @pl.when(i == 0)
    def _():
        for cp in weight_copies(te_ref[0], par_ref[0]):
            cp.start()

    @pl.when(first_ref[i] == 1)
    def _():
        @pl.when(nxt_ref[i] >= 0)
        def _():
            for cp in weight_copies(nxt_ref[i], 1 - s):
                cp.start()

        for cp in weight_copies(te_ref[i], s):
            cp.wait()
        wg_b[...] = wg_f[s].astype(BF16)
        wu_b[...] = wu_f[s].astype(BF16)
        wd_b[...] = wd_f[s].astype(BF16)

    nv = nv_ref[i]

    @pl.when(nv > 0)
    def _():
        for c in range(ROW_CHUNKS):
            x_b[:, c * LANES:(c + 1) * LANES] = _load_chunked(x_ref, c).astype(BF16)
        x = x_b[...]
        hid = (_silu(_dot(x, wg_b[...])) * _dot(x, wu_b[...])).astype(BF16)
        _store_chunked(y_ref, _dot(hid, wd_b[...]))

    @pl.when(nv == 0)
    def _():
        y_ref[...] = jnp.zeros(y_ref.shape, y_ref.dtype)


def _experts(xs, tile_e, nvalid, first, nxt, par, wg, wu, wd, tm):
    n_tiles = xs.shape[0] // (tm * ROW_CHUNKS)
    rows = pl.BlockSpec((tm * ROW_CHUNKS, LANES), lambda i, *_: (i, 0))
    hbm = pl.BlockSpec(memory_space=pl.ANY)
    grid_spec = pltpu.PrefetchScalarGridSpec(
        num_scalar_prefetch=5,
        grid=(n_tiles,),
        in_specs=[rows, hbm, hbm, hbm],
        out_specs=rows,
        scratch_shapes=[pltpu.VMEM((2, D_MODEL, MOE_FF), F32), pltpu.VMEM((2, D_MODEL, MOE_FF), F32),
                        pltpu.VMEM((2, MOE_FF, D_MODEL), F32),
                        pltpu.VMEM((D_MODEL, MOE_FF), BF16), pltpu.VMEM((D_MODEL, MOE_FF), BF16),
                        pltpu.VMEM((MOE_FF, D_MODEL), BF16), pltpu.VMEM((tm, D_MODEL), BF16),
                        pltpu.SemaphoreType.DMA((2,))])
    return pl.pallas_call(
        _experts_kernel,
        grid_spec=grid_spec,
        out_shape=jax.ShapeDtypeStruct(xs.shape, F32),
        compiler_params=_cparams(1),
        name="experts",
    )(tile_e, nvalid, first, nxt, par, xs, wg, wu, wd)


def _combine_kernel(n_steps, slot_ref, slot_next_ref, route_ref, h1_ref, g_ref, b_ref, y_hbm, o_ref,
                    gbuf, sem):
    i = pl.program_id(0)
    tm = route_ref.shape[0]

    def request(idx_ref, parity):
        def body(r, carry):
            dst = pl.ds(pl.multiple_of(r * ROW_CHUNKS, ROW_CHUNKS), ROW_CHUNKS)
            for k in range(2):
                src = pl.multiple_of(idx_ref[k, r] * ROW_CHUNKS, ROW_CHUNKS)
                pltpu.make_async_copy(y_hbm.at[pl.ds(src, ROW_CHUNKS)], gbuf.at[parity, k, dst],
                                      sem.at[parity]).start(priority=k % DMA_QUEUES)
            return carry

        lax.fori_loop(0, tm, body, 0, unroll=4)

    @pl.when(i == 0)
    def _():
        request(slot_ref, 0)

    @pl.when(i + 1 < n_steps)
    def _():
        request(slot_next_ref, (i + 1) % 2)

    parity = i % 2
    for k in range(2):
        pltpu.make_async_copy(y_hbm.at[pl.ds(0, tm * ROW_CHUNKS)], gbuf.at[parity, k],
                              sem.at[parity]).wait()
    route = route_ref[...]
    w1 = route[:, ROUTE_W1:ROUTE_W1 + 1]
    w2 = route[:, ROUTE_W2:ROUTE_W2 + 1]
    pre = jnp.concatenate(
        [DEEPNORM_ALPHA * _load_chunked(h1_ref, c)
         + w1 * _load_chunked(gbuf.at[parity, 0], c) + w2 * _load_chunked(gbuf.at[parity, 1], c)
         for c in range(ROW_CHUNKS)], axis=1)
    o_ref[...] = _layernorm(pre, g_ref[...], b_ref[...])


def _combine(y_sorted, slot, route, h1, row0, m, g, b, tm):
    t = slot.shape[1]
    blk0 = row0 // tm
    last = (row0 + m) // tm - 1
    slot3 = slot.reshape(2, t // tm, tm).transpose(1, 0, 2)
    return pl.pallas_call(
        functools.partial(_combine_kernel, m // tm),
        grid=(m // tm,),
        in_specs=[pl.BlockSpec((None, 2, tm), lambda i: (blk0 + i, 0, 0), memory_space=pltpu.SMEM),
                  pl.BlockSpec((None, 2, tm), lambda i: (jnp.minimum(blk0 + i + 1, last), 0, 0),
                               memory_space=pltpu.SMEM),
                  pl.BlockSpec((tm, ROUTER_LANES), lambda i: (blk0 + i, 0)),
                  pl.BlockSpec((tm * ROW_CHUNKS, LANES), lambda i: (blk0 + i, 0)),
                  _row_params(g), _row_params(b),
                  pl.BlockSpec(memory_space=pl.ANY)],
        out_specs=pl.BlockSpec((tm, D_MODEL), lambda i: (i, 0)),
        out_shape=jax.ShapeDtypeStruct((m, D_MODEL), F32),
        scratch_shapes=[pltpu.VMEM((2, 2, tm * ROW_CHUNKS, LANES), F32),
                        pltpu.SemaphoreType.DMA((2,))],
        compiler_params=_cparams(1),
        name="combine_ln",
    )(slot3, slot3, route, h1, g, b, y_sorted)


def _pad_lanes(v, width=LANES):
    v = v.reshape(1, -1)
    return jnp.pad(v, ((0, 0), (0, width - v.shape[1])))


def kernel(x_prompt, x_sample, state_ssd, state_conv, state_gla, w_in, conv_w, conv_b, dt_bias, a_log, d_skip, ssd_norm_w, gla_w_a2, gla_b_a, gla_norm_w, w_out, ln1_g, ln1_b, w_router_group, b_router_group, w_router_expert, b_router_expert, w_gate, w_up, w_down, ln2_g, ln2_b):
    assert w_in.shape[0] == DEPTH == 1
    n_p, len_p, _ = x_prompt.shape
    n_s, len_s, _ = x_sample.shape
    rows_p, rows_s = n_p * len_p, n_s * len_s
    xp = x_prompt.reshape(rows_p, D_MODEL)
    xs = x_sample.reshape(rows_s, D_MODEL)

    offs = [0]
    for s in IN_SPLIT_SIZES:
        offs.append(offs[-1] + s)
    w_t = w_in[0].T
    w_tail_t = jnp.concatenate(
        [jnp.pad(w_t[offs[2]:offs[3]], ((0, LANES - SSD_HEADS), (0, 0))),
         jnp.pad(w_t[offs[7]:offs[8]], ((0, LANES - GLA_RANK), (0, 0)))], axis=0)
    x_b = (xp.astype(BF16), xs.astype(BF16))

    proj = _in_proj(x_b, w_t, ((0, offs[2]), (offs[3], offs[7] - offs[3]),
                               (offs[8], offs[10] - offs[8])))
    tail = _matmul_nt(x_b, w_tail_t, IN_PROJ_ROWS)

    cw, cb = conv_w[0], conv_b[0].reshape(1, -1)
    dtb, alog = _pad_lanes(dt_bias[0]), _pad_lanes(a_log[0])
    dskip_e = jnp.repeat(d_skip[0], SSD_HEAD_DIM).reshape(1, -1)
    nw_ssd = ssd_norm_w[0].reshape(1, -1)
    wa, ba, nw_gla = gla_w_a2[0].astype(BF16), gla_b_a[0].reshape(1, -1), gla_norm_w[0].reshape(1, -1)

    ys_p, ssd_p, conv_p = _ssd_prompt(proj, tail, n_p, len_p, cw, cb, dtb, alog, dskip_e, nw_ssd)
    ys_s, ssd_s, conv_s = _ssd_sample(proj, tail, rows_p, n_s, len_s, state_ssd[0], state_conv[0],
                                      cw, cb, dtb, alog, dskip_e, nw_ssd)
    yg_p, gla_p = _gla_prompt(proj, tail, n_p, len_p, wa, ba, nw_gla)
    yg_s, gla_s = _gla_sample(proj, tail, rows_p, n_s, len_s, state_gla[0], wa, ba, nw_gla)

    wo = w_out[0].astype(BF16)
    w_r = jnp.pad(jnp.concatenate([w_router_group[0], w_router_expert[0]], axis=1),
                  ((0, 0), (0, ROUTER_LANES - MOE_GROUPS - MOE_EXPERTS)))
    wr_hi = w_r.astype(BF16)
    wr_lo = (w_r - wr_hi.astype(F32)).astype(BF16)
    b_r = _pad_lanes(jnp.concatenate([b_router_group[0], b_router_expert[0]]), ROUTER_LANES)
    g1, b1 = ln1_g[0].reshape(1, -1), ln1_b[0].reshape(1, -1)
    g2, b2 = ln2_g[0].reshape(1, -1), ln2_b[0].reshape(1, -1)

    h1, route = _merge((ys_p, ys_s), (yg_p, yg_s), proj, (xp, xs), wo, g1, b1, wr_hi, wr_lo, b_r,
                       MERGE_ROWS)
    slot, pads, n_slots, tile_e, nvalid, first, nxt, par = _expert_plan(route, EXPERT_ROWS)
    x_sorted = _dispatch(h1, slot, pads, n_slots, EXPERT_ROWS, DISPATCH_WINDOW)
    y_sorted = _experts(x_sorted, tile_e, nvalid, first, nxt, par, w_gate[0], w_up[0], w_down[0],
                        EXPERT_ROWS)
    outs = [_combine(y_sorted, slot, route, h1, 0, rows_p, g2, b2, MERGE_ROWS),
            _combine(y_sorted, slot, route, h1, rows_p, rows_s, g2, b2, MERGE_ROWS)]
    y_p = outs[0].reshape(x_prompt.shape)
    y_s = outs[1].reshape(x_sample.shape)
    return (y_p, y_s, ssd_p[None], conv_p[None], gla_p[None], ssd_s[None], conv_s[None], gla_s[None])
```

```python
import functools

import jax
import jax.numpy as jnp
from jax import lax
from jax.experimental import pallas as pl
from jax.experimental.pallas import tpu as pltpu

F32 = jnp.float32
BF16 = jnp.bfloat16

D_MODEL = 2048
SSD_HEADS = 32
SSD_HEAD_DIM = 64
SSD_GROUPS = 8
SSD_STATE = 128
SSD_CONV = 4
SSD_CHUNK = 128
SSD_BC = SSD_GROUPS * SSD_STATE
SSD_CONV_DIM = D_MODEL + 2 * SSD_BC
SSD_GROUP_COLS = D_MODEL // SSD_GROUPS
HEADS_PER_GROUP = SSD_HEADS // SSD_GROUPS
GLA_HEADS = 8
GLA_DK = 128
GLA_DV = 256
GLA_KEY_DIM = GLA_HEADS * GLA_DK
GLA_RANK = 16
GLA_TAU = 16.0
GLA_CHUNK = 64
MOE_GROUPS = 4
MOE_EPG = 8
MOE_EXPERTS = 32
MOE_FF = 512
DEPTH = 1
DEEPNORM_ALPHA = (2.0 * DEPTH) ** 0.25
EPS = 1e-5
IN_SPLIT_SIZES = (D_MODEL, SSD_CONV_DIM, SSD_HEADS, GLA_KEY_DIM, GLA_KEY_DIM, D_MODEL, D_MODEL,
                  GLA_RANK, D_MODEL, D_MODEL)

LANES = 128
HALO_ROWS = 8
COL_Z, COL_XS, COL_BC, COL_QK, COL_V, COL_R, COL_GS, COL_GG = range(8)
TAIL_DT, TAIL_ALO = 0, 1
ROUTER_LANES = 128
ROUTE_E1, ROUTE_E2, ROUTE_W1, ROUTE_W2 = 0, 1, 2, 3
MERGE_ROWS = 256
EXPERT_ROWS = 256
DISPATCH_WINDOW = 512
DMA_QUEUES = 2
VMEM_LIMIT = 56 * 1024 * 1024


def _cparams(n_axes):
    return pltpu.CompilerParams(dimension_semantics=("arbitrary",) * n_axes,
                                vmem_limit_bytes=VMEM_LIMIT)


def _silu(x):
    h = 0.5 * x
    return h + h * jnp.tanh(h)


def _softplus(x):
    return jnp.maximum(x, 0.0) + jnp.log1p(jnp.exp(-jnp.abs(x)))


def _tril(n):
    r = lax.broadcasted_iota(jnp.int32, (n, n), 0)
    c = lax.broadcasted_iota(jnp.int32, (n, n), 1)
    return r >= c


def _expand_group(p, g, lane_head):
    h0 = g * HEADS_PER_GROUP
    out = p[:, h0 + HEADS_PER_GROUP - 1:h0 + HEADS_PER_GROUP]
    for hh in range(HEADS_PER_GROUP - 2, -1, -1):
        out = jnp.where(lane_head == hh, p[:, h0 + hh:h0 + hh + 1], out)
    return out


def _split3(x):
    hi = x.astype(BF16)
    r = x - hi.astype(F32)
    mid = r.astype(BF16)
    lo = (r - mid.astype(F32)).astype(BF16)
    return hi, mid, lo


def _dot_sel(sel, x):
    hi, mid, lo = _split3(x)
    return _dot(sel, lo) + _dot(sel, mid) + _dot(sel, hi)


def _col_bcast(row):
    return jnp.broadcast_to(row, (LANES, LANES)).T


def _dot(a, b, **kw):
    return jnp.dot(a, b, preferred_element_type=F32, **kw)


def _dot_nt(a, b, **kw):
    return lax.dot_general(a, b, (((1,), (1,)), ((), ())), preferred_element_type=F32, **kw)


def _dot_tn(a, b, **kw):
    return lax.dot_general(a, b, (((0,), (0,)), ((), ())), preferred_element_type=F32, **kw)


def _matmul_kernel(n_first, xa_ref, xb_ref, wt_ref, o_ref):
    x = jnp.where(pl.program_id(0) < n_first, xa_ref[...], xb_ref[...])
    o_ref[...] = _dot_nt(x, wt_ref[...].astype(BF16))


def _matmul_nt(xs, wt, tm):
    n, k = wt.shape
    n_a, n_b = xs[0].shape[0] // tm, xs[1].shape[0] // tm
    return pl.pallas_call(
        functools.partial(_matmul_kernel, n_a),
        grid=(n_a + n_b,),
        in_specs=[pl.BlockSpec((tm, k), lambda i: (jnp.minimum(i, n_a - 1), 0)),
                  pl.BlockSpec((tm, k), lambda i: (jnp.maximum(i - n_a, 0), 0)),
                  pl.BlockSpec((n, k), lambda i: (0, 0))],
        out_specs=pl.BlockSpec((tm, n), lambda i: (i, 0)),
        out_shape=jax.ShapeDtypeStruct(((n_a + n_b) * tm, n), F32),
        compiler_params=_cparams(1),
        name="in_proj_tail",
    )(*xs, wt)


IN_PROJ_COLS = 1024
IN_PROJ_ROWS = 1024
CAST_ROWS = 256
SUBLANES = 8


def _in_proj_kernel(plan, n_first, xa_ref, xb_ref, wt_hbm, o_ref, wf, wb, sem):
    j = pl.program_id(0)
    i = pl.program_id(1)
    tn = wb.shape[1]
    n_blocks = plan[-1][1]

    def block_copy(jj, slot):
        shift = 0
        for lo, hi, sh in plan:
            shift = jnp.where((jj >= lo) & (jj < hi), sh, shift)
        src = pl.multiple_of(jj * tn + shift, SUBLANES)
        return pltpu.make_async_copy(wt_hbm.at[pl.ds(src, tn)], wf.at[slot], sem.at[slot])

    @pl.when(i == 0)
    def _():
        slot = j % 2

        @pl.when(j == 0)
        def _():
            block_copy(j, slot).start()

        @pl.when(j + 1 < n_blocks)
        def _():
            block_copy(j + 1, 1 - slot).start()

        block_copy(j, slot).wait()
        for r in range(0, tn, CAST_ROWS):
            wb[:, r:r + CAST_ROWS] = wf[slot, r:r + CAST_ROWS, :].T.astype(BF16)

    x = jnp.where(i < n_first, xa_ref[...], xb_ref[...])
    o_ref[...] = _dot(x, wb[...])


def _in_proj(xs, wt, segments):
    tm, tn = IN_PROJ_ROWS, IN_PROJ_COLS
    k = wt.shape[1]
    n_a, n_b = xs[0].shape[0] // tm, xs[1].shape[0] // tm
    plan, dest = [], 0
    for src, count in segments:
        assert count % tn == 0 and dest % tn == 0 and (src - dest) % SUBLANES == 0
        plan.append((dest // tn, (dest + count) // tn, src - dest))
        dest += count
    return pl.pallas_call(
        functools.partial(_in_proj_kernel, tuple(plan), n_a),
        grid=(dest // tn, n_a + n_b),
        in_specs=[pl.BlockSpec((tm, k), lambda j, i: (jnp.minimum(i, n_a - 1), 0)),
                  pl.BlockSpec((tm, k), lambda j, i: (jnp.maximum(i - n_a, 0), 0)),
                  pl.BlockSpec(memory_space=pl.ANY)],
        out_specs=pl.BlockSpec((tm, tn), lambda j, i: (i, j)),
        out_shape=jax.ShapeDtypeStruct(((n_a + n_b) * tm, dest), F32),
        scratch_shapes=[pltpu.VMEM((2, tn, k), F32), pltpu.VMEM((k, tn), BF16),
                        pltpu.SemaphoreType.DMA((2,))],
        compiler_params=_cparams(2),
        name="in_proj_main",
    )(*xs, wt)


def _conv_silu(ubuf, cw_ref, cb_ref, lo, width, rows):
    acc = cb_ref[:, lo:lo + width]
    for i in range(SSD_CONV):
        r0 = HALO_ROWS - (SSD_CONV - 1) + i
        acc = acc + cw_ref[i:i + 1, lo:lo + width] * ubuf[r0:r0 + rows, lo:lo + width]
    return _silu(acc)


def _gated_group_norm(y_g, z_g, nw_g):
    yg = y_g * _silu(z_g)
    ms = jnp.mean(yg * yg, axis=-1, keepdims=True)
    return yg * lax.rsqrt(ms + EPS) * nw_g


SCAN_CHUNKS = 2
GLA_SCAN_CHUNKS = 4


def _ssd_prompt_chunk(z_ref, xs_ref, bc_ref, tail_ref, cw_ref, cb_ref, dtb_ref, alog_ref,
                      dskip_ref, nw_ref, y_ref, ubuf, ht):
    q = SSD_CHUNK
    ubuf[HALO_ROWS:HALO_ROWS + q, 0:D_MODEL] = xs_ref[...]
    ubuf[HALO_ROWS:HALO_ROWS + q, D_MODEL:SSD_CONV_DIM] = bc_ref[...]

    dt = _softplus(tail_ref[...] + dtb_ref[...])
    da = dt * (-jnp.exp(alog_ref[...]))
    causal = _tril(q)
    acum = _dot_sel(causal.astype(BF16), da)
    acum_t = acum.T
    dt_t = dt.T
    a_last = acum[q - 1:q, :]
    decay_in = jnp.exp(acum)
    w_end = dt * jnp.exp(a_last - acum)
    chunk_decay = jnp.exp(a_last)
    lane_head = lax.broadcasted_iota(jnp.int32, (q, SSD_GROUP_COLS), 1) >> 6
    assert SSD_HEAD_DIM == 1 << 6

    for g in range(SSD_GROUPS):
        lo = g * SSD_GROUP_COLS
        cols = slice(lo, lo + SSD_GROUP_COLS)
        xs_g = _conv_silu(ubuf, cw_ref, cb_ref, lo, SSD_GROUP_COLS, q)
        b_g = _conv_silu(ubuf, cw_ref, cb_ref, D_MODEL + g * SSD_STATE, SSD_STATE, q).astype(BF16)
        c_g = _conv_silu(ubuf, cw_ref, cb_ref, D_MODEL + SSD_BC + g * SSD_STATE, SSD_STATE,
                         q).astype(BF16)
        scores = _dot_nt(c_g, b_g)
        h_g = ht[g]
        xs_b = xs_g.astype(BF16)
        m_heads, x_heads = [], []
        for hh in range(HEADS_PER_GROUP):
            h = g * HEADS_PER_GROUP + hh
            seg = acum[:, h:h + 1] - acum_t[h:h + 1, :]
            decay = jnp.exp(jnp.where(causal, seg, -jnp.inf))
            m_heads.append((scores * decay * dt_t[h:h + 1, :]).astype(BF16))
            x_heads.append(jnp.where(lane_head == hh, xs_b, jnp.zeros_like(xs_b)))
        y_g = (_dot(jnp.concatenate(m_heads, axis=1), jnp.concatenate(x_heads, axis=0))
               + _dot(c_g, h_g.astype(BF16)) * _expand_group(decay_in, g, lane_head)
               + dskip_ref[:, cols] * xs_g)
        y_ref[:, cols] = _gated_group_norm(y_g, z_ref[:, cols], nw_ref[:, cols]).astype(y_ref.dtype)
        xw = (xs_g * _expand_group(w_end, g, lane_head)).astype(BF16)
        ht[g] = h_g * _expand_group(chunk_decay, g, lane_head[0:1]) + _dot_tn(b_g, xw)

    last = HALO_ROWS + q - (SSD_CONV - 1)
    tail_rows = ubuf[last:last + SSD_CONV - 1, :]
    ubuf[HALO_ROWS - (SSD_CONV - 1):HALO_ROWS, :] = tail_rows
    return tail_rows


def _ssd_prompt_kernel(z_ref, xs_ref, bc_ref, tail_ref, cw_ref, cb_ref, dtb_ref, alog_ref,
                       dskip_ref, nw_ref, y_ref, hout_ref, convout_ref, ubuf, ht):
    q = SSD_CHUNK
    c = pl.program_id(1)

    @pl.when(c == 0)
    def _():
        ubuf[0:HALO_ROWS, :] = jnp.zeros((HALO_ROWS, SSD_CONV_DIM), F32)
        ht[...] = jnp.zeros(ht.shape, F32)

    for ci in range(xs_ref.shape[0] // q):
        rows = pl.ds(ci * q, q)
        tail_rows = _ssd_prompt_chunk(z_ref.at[rows], xs_ref.at[rows], bc_ref.at[rows],
                                      tail_ref.at[rows], cw_ref, cb_ref, dtb_ref, alog_ref,
                                      dskip_ref, nw_ref, y_ref.at[rows], ubuf, ht)

    @pl.when(c == pl.num_programs(1) - 1)
    def _():
        convout_ref[0] = tail_rows
        for g in range(SSD_GROUPS):
            h_t = ht[g].T
            for hh in range(HEADS_PER_GROUP):
                hout_ref[0, g * HEADS_PER_GROUP + hh] = h_t[hh * SSD_HEAD_DIM:(hh + 1) * SSD_HEAD_DIM]


def _row_params(p):
    return pl.BlockSpec(p.shape, lambda *_: (0,) * p.ndim)


def _ssd_prompt(proj, tail, n_seq, seq_len, cw, cb, dtb, alog, dskip_e, nw):
    q = SSD_CHUNK * SCAN_CHUNKS
    nc = seq_len // q

    def col(blk):
        return pl.BlockSpec((q, D_MODEL), lambda b, c: (b * nc + c, blk))

    return pl.pallas_call(
        _ssd_prompt_kernel,
        grid=(n_seq, nc),
        in_specs=[col(COL_Z), col(COL_XS), col(COL_BC),
                  pl.BlockSpec((q, LANES), lambda b, c: (b * nc + c, TAIL_DT)),
                  _row_params(cw), _row_params(cb), _row_params(dtb), _row_params(alog),
                  _row_params(dskip_e), _row_params(nw)],
        out_specs=[pl.BlockSpec((q, D_MODEL), lambda b, c: (b * nc + c, 0)),
                   pl.BlockSpec((1, SSD_HEADS, SSD_HEAD_DIM, SSD_STATE), lambda b, c: (b, 0, 0, 0)),
                   pl.BlockSpec((1, SSD_CONV - 1, SSD_CONV_DIM), lambda b, c: (b, 0, 0))],
        out_shape=[jax.ShapeDtypeStruct((n_seq * seq_len, D_MODEL), BF16),
                   jax.ShapeDtypeStruct((n_seq, SSD_HEADS, SSD_HEAD_DIM, SSD_STATE), F32),
                   jax.ShapeDtypeStruct((n_seq, SSD_CONV - 1, SSD_CONV_DIM), F32)],
        scratch_shapes=[pltpu.VMEM((HALO_ROWS + SSD_CHUNK, SSD_CONV_DIM), F32),
                        pltpu.VMEM((SSD_GROUPS, SSD_STATE, SSD_GROUP_COLS), F32)],
        compiler_params=_cparams(2),
        name="ssd_prompt",
    )(proj, proj, proj, tail, cw, cb, dtb, alog, dskip_e, nw)


SAMPLE_SEQS = 4
GLA_SAMPLE_SEQS = 8


def _dot_tn_split(a, b):
    a_hi = a.astype(BF16)
    a_lo = (a - a_hi.astype(F32)).astype(BF16)
    b_hi = b.astype(BF16)
    b_lo = (b - b_hi.astype(F32)).astype(BF16)
    return _dot_tn(a_lo, b_hi) + _dot_tn(a_hi, b_lo) + _dot_tn(a_hi, b_hi)


def _ssd_sample_seq(z_ref, xs_ref, bc_ref, tail_ref, h_ref, conv_ref, cw_ref, cb_ref, dtb_ref,
                    alog_ref, dskip_ref, nw_ref, y_ref, hout_ref, convout_ref, ubuf, decay_rows):
    q = xs_ref.shape[0]
    ubuf[0:HALO_ROWS - (SSD_CONV - 1), :] = jnp.zeros((HALO_ROWS - (SSD_CONV - 1), SSD_CONV_DIM), F32)
    ubuf[HALO_ROWS - (SSD_CONV - 1):HALO_ROWS, :] = conv_ref[...]
    ubuf[HALO_ROWS:HALO_ROWS + q, 0:D_MODEL] = xs_ref[...]
    ubuf[HALO_ROWS:HALO_ROWS + q, D_MODEL:SSD_CONV_DIM] = bc_ref[...]
    convout_ref[...] = ubuf[HALO_ROWS + q - (SSD_CONV - 1):HALO_ROWS + q, :]

    dt = _softplus(tail_ref[...] + dtb_ref[...])
    da = dt * (-jnp.exp(alog_ref[...]))
    acum = _dot_sel(_tril(q).astype(BF16), da)
    a_last = acum[q - 1:q, :]
    decay_in = jnp.exp(acum)
    w_end = dt * jnp.exp(a_last - acum)
    decay_rows[...] = _col_bcast(jnp.exp(a_last))[:SSD_HEADS]
    row = lax.broadcasted_iota(jnp.int32, (q, SSD_GROUP_COLS), 0)
    lane_head = lax.broadcasted_iota(jnp.int32, (q, SSD_GROUP_COLS), 1) >> 6

    for g in range(SSD_GROUPS):
        lo = g * SSD_GROUP_COLS
        cols = slice(lo, lo + SSD_GROUP_COLS)
        xs_g = _conv_silu(ubuf, cw_ref, cb_ref, lo, SSD_GROUP_COLS, q)
        b_g = _conv_silu(ubuf, cw_ref, cb_ref, D_MODEL + g * SSD_STATE, SSD_STATE, q)
        c_g = _conv_silu(ubuf, cw_ref, cb_ref, D_MODEL + SSD_BC + g * SSD_STATE, SSD_STATE, q)
        h_g = h_ref[g * HEADS_PER_GROUP:(g + 1) * HEADS_PER_GROUP].reshape(
            SSD_GROUP_COLS, SSD_STATE)
        y_g = (_dot_nt(c_g.astype(BF16), h_g.astype(BF16)) * _expand_group(decay_in, g, lane_head)
               + dskip_ref[:, cols] * xs_g)
        acum_g = _expand_group(acum, g, lane_head)
        dt_g = _expand_group(dt, g, lane_head)
        for j in range(q):
            s_j = jnp.sum(c_g * b_g[j:j + 1, :], axis=-1, keepdims=True)
            coef = s_j * jnp.exp(acum_g - acum_g[j:j + 1, :]) * dt_g[j:j + 1, :]
            y_g = y_g + jnp.where(row >= j, coef, 0.0) * xs_g[j:j + 1, :]
        y_ref[:, cols] = _gated_group_norm(y_g, z_ref[:, cols], nw_ref[:, cols])
        upd = _dot_tn_split(xs_g * _expand_group(w_end, g, lane_head), b_g)
        for hh in range(HEADS_PER_GROUP):
            h = g * HEADS_PER_GROUP + hh
            rows = slice(hh * SSD_HEAD_DIM, (hh + 1) * SSD_HEAD_DIM)
            hout_ref[h] = h_g[rows] * decay_rows[h:h + 1, :] + upd[rows]


def _ssd_sample_kernel(z_ref, xs_ref, bc_ref, tail_ref, h_ref, conv_ref, cw_ref, cb_ref, dtb_ref,
                       alog_ref, dskip_ref, nw_ref, y_ref, hout_ref, convout_ref, ubuf, decay_rows):
    n = h_ref.shape[0]
    q = xs_ref.shape[0] // n
    for s in range(n):
        rows = pl.ds(s * q, q)
        _ssd_sample_seq(z_ref.at[rows], xs_ref.at[rows], bc_ref.at[rows], tail_ref.at[rows],
                        h_ref.at[s], conv_ref.at[s], cw_ref, cb_ref, dtb_ref, alog_ref, dskip_ref,
                        nw_ref, y_ref.at[rows], hout_ref.at[s], convout_ref.at[s], ubuf.at[s],
                        decay_rows.at[s])


def _ssd_sample(proj, tail, row0, n_seq, seq_len, h0, conv0, cw, cb, dtb, alog, dskip_e, nw):
    nb = SAMPLE_SEQS
    q = seq_len
    blk0 = row0 // (nb * q)

    def col(blk):
        return pl.BlockSpec((nb * q, D_MODEL), lambda b: (blk0 + b, blk))

    state_spec = pl.BlockSpec((nb, SSD_HEADS, SSD_HEAD_DIM, SSD_STATE), lambda b: (b, 0, 0, 0))
    conv_spec = pl.BlockSpec((nb, SSD_CONV - 1, SSD_CONV_DIM), lambda b: (b, 0, 0))
    return pl.pallas_call(
        _ssd_sample_kernel,
        grid=(n_seq // nb,),
        in_specs=[col(COL_Z), col(COL_XS), col(COL_BC),
                  pl.BlockSpec((nb * q, LANES), lambda b: (blk0 + b, TAIL_DT)),
                  state_spec, conv_spec,
                  _row_params(cw), _row_params(cb), _row_params(dtb), _row_params(alog),
                  _row_params(dskip_e), _row_params(nw)],
        out_specs=[pl.BlockSpec((nb * q, D_MODEL), lambda b: (b, 0)), state_spec, conv_spec],
        out_shape=[jax.ShapeDtypeStruct((n_seq * seq_len, D_MODEL), F32),
                   jax.ShapeDtypeStruct(h0.shape, F32),
                   jax.ShapeDtypeStruct(conv0.shape, F32)],
        scratch_shapes=[pltpu.VMEM((nb, HALO_ROWS + q, SSD_CONV_DIM), F32),
                        pltpu.VMEM((nb, SSD_HEADS, SSD_STATE), F32)],
        compiler_params=_cparams(1),
        name="ssd_sample",
    )(proj, proj, proj, tail, h0, conv0, cw, cb, dtb, alog, dskip_e, nw)


def _gla_chunk(qk_ref, v_ref, r_ref, alo_ref, wa_ref, ba_ref, nw_ref, y_ref, get_state, put_state):
    q = v_ref.shape[0]
    a_lo = alo_ref[...][:, :GLA_RANK].astype(BF16)
    gk = -_softplus(-(_dot(a_lo, wa_ref[...]) + ba_ref[...])) / GLA_TAU
    causal = _tril(q)
    bcum = _dot_sel(causal.astype(BF16), gk)
    for h in range(GLA_HEADS):
        kc = slice(h * GLA_DK, (h + 1) * GLA_DK)
        vc = slice(h * GLA_DV, (h + 1) * GLA_DV)
        b_h = bcum[:, kc]
        b_last = b_h[q - 1:q, :]
        q_h = qk_ref[:, kc] * (GLA_DK ** -0.5)
        k_h = qk_ref[:, GLA_KEY_DIM + h * GLA_DK:GLA_KEY_DIM + (h + 1) * GLA_DK]
        v_h = v_ref[:, vc].astype(BF16)
        q_in = (q_h * jnp.exp(b_h)).astype(BF16)
        k_in = (k_h * jnp.exp(-b_h)).astype(BF16)
        att = jnp.where(causal, _dot_nt(q_in, k_in), 0.0).astype(BF16)
        s_h = get_state(h)
        o = _dot(att, v_h) + _dot(q_in, s_h.astype(BF16))
        k_end = (k_h * jnp.exp(b_last - b_h)).astype(BF16)
        d_col = _col_bcast(jnp.exp(b_last))
        put_state(h, s_h * jnp.concatenate([d_col, d_col], axis=1) + _dot_tn(k_end, v_h))
        ms = jnp.mean(o * o, axis=-1, keepdims=True)
        y_ref[:, vc] = (o * lax.rsqrt(ms + EPS) * nw_ref[...] * _silu(r_ref[:, vc])).astype(y_ref.dtype)


def _gla_prompt_kernel(qk_ref, v_ref, r_ref, alo_ref, wa_ref, ba_ref, nw_ref, y_ref, sout_ref, st):
    c = pl.program_id(1)

    @pl.when(c == 0)
    def _():
        st[...] = jnp.zeros(st.shape, F32)

    def put(h, val):
        st[h] = val

    q = GLA_CHUNK
    for ci in range(v_ref.shape[0] // q):
        rows = pl.ds(ci * q, q)
        _gla_chunk(qk_ref.at[rows], v_ref.at[rows], r_ref.at[rows], alo_ref.at[rows], wa_ref, ba_ref,
                   nw_ref, y_ref.at[rows], lambda h: st[h], put)

    @pl.when(c == pl.num_programs(1) - 1)
    def _():
        sout_ref[0] = st[...]


def _gla_prompt(proj, tail, n_seq, seq_len, wa, ba, nw):
    q = GLA_CHUNK * GLA_SCAN_CHUNKS
    nc = seq_len // q

    def col(blk):
        return pl.BlockSpec((q, D_MODEL), lambda b, c: (b * nc + c, blk))

    return pl.pallas_call(
        _gla_prompt_kernel,
        grid=(n_seq, nc),
        in_specs=[col(COL_QK), col(COL_V), col(COL_R),
                  pl.BlockSpec((q, LANES), lambda b, c: (b * nc + c, TAIL_ALO)),
                  _row_params(wa), _row_params(ba), _row_params(nw)],
        out_specs=[pl.BlockSpec((q, D_MODEL), lambda b, c: (b * nc + c, 0)),
                   pl.BlockSpec((1, GLA_HEADS, GLA_DK, GLA_DV), lambda b, c: (b, 0, 0, 0))],
        out_shape=[jax.ShapeDtypeStruct((n_seq * seq_len, D_MODEL), BF16),
                   jax.ShapeDtypeStruct((n_seq, GLA_HEADS, GLA_DK, GLA_DV), F32)],
        scratch_shapes=[pltpu.VMEM((GLA_HEADS, GLA_DK, GLA_DV), F32)],
        compiler_params=_cparams(2),
        name="gla_prompt",
    )(proj, proj, proj, tail, wa, ba, nw)


def _gla_sample_kernel(qk_ref, v_ref, r_ref, alo_ref, s_ref, wa_ref, ba_ref, nw_ref, y_ref, sout_ref):
    n = s_ref.shape[0]
    q = v_ref.shape[0] // n
    for s in range(n):
        rows = pl.ds(s * q, q)

        def put(h, val, s=s):
            sout_ref[s, h] = val

        _gla_chunk(qk_ref.at[rows], v_ref.at[rows], r_ref.at[rows], alo_ref.at[rows], wa_ref, ba_ref,
                   nw_ref, y_ref.at[rows], lambda h, s=s: s_ref[s, h], put)


def _gla_sample(proj, tail, row0, n_seq, seq_len, s0, wa, ba, nw):
    nb = GLA_SAMPLE_SEQS
    q = seq_len
    blk0 = row0 // (nb * q)

    def col(blk):
        return pl.BlockSpec((nb * q, D_MODEL), lambda b: (blk0 + b, blk))

    state_spec = pl.BlockSpec((nb, GLA_HEADS, GLA_DK, GLA_DV), lambda b: (b, 0, 0, 0))
    return pl.pallas_call(
        _gla_sample_kernel,
        grid=(n_seq // nb,),
        in_specs=[col(COL_QK), col(COL_V), col(COL_R),
                  pl.BlockSpec((nb * q, LANES), lambda b: (blk0 + b, TAIL_ALO)),
                  state_spec, _row_params(wa), _row_params(ba), _row_params(nw)],
        out_specs=[pl.BlockSpec((nb * q, D_MODEL), lambda b: (b, 0)), state_spec],
        out_shape=[jax.ShapeDtypeStruct((n_seq * seq_len, D_MODEL), F32),
                   jax.ShapeDtypeStruct(s0.shape, F32)],
        compiler_params=_cparams(1),
        name="gla_sample",
    )(proj, proj, proj, tail, s0, wa, ba, nw)


def _layernorm(x, g, b):
    mu = jnp.mean(x, axis=-1, keepdims=True)
    xc = x - mu
    var = jnp.mean(xc * xc, axis=-1, keepdims=True)
    return xc * lax.rsqrt(var + EPS) * g + b


def _route(logits):
    lane = lax.broadcasted_iota(jnp.int32, logits.shape, 1)
    neg = -jnp.inf
    big = ROUTER_LANES
    glog = jnp.where(lane < MOE_GROUPS, logits, neg)
    gmax = jnp.max(glog, axis=-1, keepdims=True)
    g_sel = jnp.min(jnp.where(glog == gmax, lane, big), axis=-1, keepdims=True)
    p_g = 1.0 / jnp.sum(jnp.exp(glog - gmax), axis=-1, keepdims=True)
    e_lane = lane - MOE_GROUPS
    in_group = (e_lane >= 0) & (e_lane < MOE_EXPERTS) & (e_lane // MOE_EPG == g_sel)
    el = jnp.where(in_group, logits, neg)
    v1 = jnp.max(el, axis=-1, keepdims=True)
    i1 = jnp.min(jnp.where(el == v1, lane, big), axis=-1, keepdims=True)
    el2 = jnp.where(lane == i1, neg, el)
    v2 = jnp.max(el2, axis=-1, keepdims=True)
    i2 = jnp.min(jnp.where(el2 == v2, lane, big), axis=-1, keepdims=True)
    e2 = jnp.exp(v2 - v1)
    w1 = p_g / (1.0 + e2)
    w2 = p_g * e2 / (1.0 + e2)
    first = (i1 - MOE_GROUPS).astype(F32)
    second = (i2 - MOE_GROUPS).astype(F32)
    return jnp.where(lane == ROUTE_E1, first,
                     jnp.where(lane == ROUTE_E2, second,
                               jnp.where(lane == ROUTE_W1, w1,
                                         jnp.where(lane == ROUTE_W2, w2, 0.0))))


ROW_CHUNKS = D_MODEL // LANES


def _store_chunked(ref, val):
    rows = val.shape[0]
    for c in range(ROW_CHUNKS):
        ref[pl.ds(c, rows, stride=ROW_CHUNKS), :] = val[:, c * LANES:(c + 1) * LANES]


def _load_chunked(ref, c):
    return ref[pl.ds(c, ref.shape[0] // ROW_CHUNKS, stride=ROW_CHUNKS), :]


def _merge_kernel(n_first, ysa_ref, ysb_ref, yga_ref, ygb_ref, gs_ref, gg_ref, xa_ref, xb_ref, wo_ref,
                  g_ref, b_ref, wrh_ref, wrl_ref, br_ref, h1_ref, route_ref):
    def tile(ys_ref, yg_ref, x_ref):
        merged = (jax.nn.sigmoid(gs_ref[...]) * ys_ref[...]
                  + jax.nn.sigmoid(gg_ref[...]) * yg_ref[...])
        mix = _dot(merged.astype(BF16), wo_ref[...])
        h1 = _layernorm(DEEPNORM_ALPHA * x_ref[...] + mix, g_ref[...], b_ref[...])
        _store_chunked(h1_ref, h1)
        h_hi = h1.astype(BF16)
        h_lo = (h1 - h_hi.astype(F32)).astype(BF16)
        logits = (_dot(h_hi, wrh_ref[...]) + _dot(h_lo, wrh_ref[...]) + _dot(h_hi, wrl_ref[...])
                  + br_ref[...])
        route_ref[...] = _route(logits)

    first = pl.program_id(0) < n_first
    pl.when(first)(lambda: tile(ysa_ref, yga_ref, xa_ref))
    pl.when(jnp.logical_not(first))(lambda: tile(ysb_ref, ygb_ref, xb_ref))


def _merge(ys, yg, gates, xs, wo, g, b, wr_hi, wr_lo, br, tm):
    n_a, n_b = xs[0].shape[0] // tm, xs[1].shape[0] // tm
    m = (n_a + n_b) * tm

    def rows(i):
        return (i, 0)

    pair = [pl.BlockSpec((tm, D_MODEL), lambda i: (jnp.minimum(i, n_a - 1), 0)),
            pl.BlockSpec((tm, D_MODEL), lambda i: (jnp.maximum(i - n_a, 0), 0))]
    return pl.pallas_call(
        functools.partial(_merge_kernel, n_a),
        grid=(m // tm,),
        in_specs=[*pair, *pair,
                  pl.BlockSpec((tm, D_MODEL), lambda i: (i, COL_GS)),
                  pl.BlockSpec((tm, D_MODEL), lambda i: (i, COL_GG)),
                  *pair,
                  pl.BlockSpec(wo.shape, lambda i: (0, 0), pipeline_mode=pl.Buffered(1)),
                  _row_params(g), _row_params(b),
                  _row_params(wr_hi), _row_params(wr_lo), _row_params(br)],
        out_specs=[pl.BlockSpec((tm * ROW_CHUNKS, LANES), rows),
                   pl.BlockSpec((tm, ROUTER_LANES), rows)],
        out_shape=[jax.ShapeDtypeStruct((m * ROW_CHUNKS, LANES), F32),
                   jax.ShapeDtypeStruct((m, ROUTER_LANES), F32)],
        compiler_params=_cparams(1),
        name="merge_outproj_ln_router",
    )(*ys, *yg, gates, gates, *xs, wo, g, b, wr_hi, wr_lo, br)


def _zero_runs(pads_ref, zeros, x_hbm, sem, action):
    tm = zeros.shape[0] // ROW_CHUNKS

    def piece(pos, rows):
        dst = pl.multiple_of(pos * ROW_CHUNKS, ROW_CHUNKS)
        return pltpu.make_async_copy(zeros.at[pl.ds(0, rows * ROW_CHUNKS)],
                                     x_hbm.at[pl.ds(dst, rows * ROW_CHUNKS)], sem)

    def run(e, carry):
        pos, length = pads_ref[0, e], pads_ref[1, e]

        def whole(q, c):
            getattr(piece(pos + q * tm, tm), action)()
            return c

        lax.fori_loop(0, length // tm, whole, 0)
        pos = pos + (length // tm) * tm
        rows = tm // 2
        while rows >= 1:
            @pl.when((length & rows) != 0)
            def _(pos=pos, rows=rows):
                getattr(piece(pos, rows), action)()

            pos = pos + (length & rows)
            rows //= 2
        return carry

    lax.fori_loop(0, pads_ref.shape[1], run, 0)


def _dispatch_kernel(slot_ref, pads_ref, h_ref, x_hbm, zeros, sem, zsem):
    n = h_ref.shape[0] // ROW_CHUNKS

    def issue(r, carry):
        src = h_ref.at[pl.ds(pl.multiple_of(r * ROW_CHUNKS, ROW_CHUNKS), ROW_CHUNKS)]
        for k in range(2):
            dst = pl.multiple_of(slot_ref[k, r] * ROW_CHUNKS, ROW_CHUNKS)
            pltpu.make_async_copy(src, x_hbm.at[pl.ds(dst, ROW_CHUNKS)], sem).start(
                priority=k % DMA_QUEUES)
        return carry

    lax.fori_loop(0, n, issue, 0, unroll=4)

    @pl.when(pl.program_id(0) == pl.num_programs(0) - 1)
    def _():
        zeros[...] = jnp.zeros(zeros.shape, zeros.dtype)
        _zero_runs(pads_ref, zeros, x_hbm, zsem, "start")
        _zero_runs(pads_ref, zeros, x_hbm, zsem, "wait")

    for k in range(2):
        pltpu.make_async_copy(h_ref, x_hbm.at[pl.ds(0, n * ROW_CHUNKS)], sem).wait()


def _dispatch(h1, slot, pads, n_slots, tm, window):
    t = slot.shape[1]
    steps = t // window
    slot3 = slot.reshape(2, steps, window).transpose(1, 0, 2)
    return pl.pallas_call(
        _dispatch_kernel,
        grid=(steps,),
        in_specs=[pl.BlockSpec((None, 2, window), lambda i: (i, 0, 0), memory_space=pltpu.SMEM),
                  pl.BlockSpec(pads.shape, lambda i: (0, 0), memory_space=pltpu.SMEM),
                  pl.BlockSpec((window * ROW_CHUNKS, LANES), lambda i: (i, 0))],
        out_specs=pl.BlockSpec(memory_space=pl.ANY),
        out_shape=jax.ShapeDtypeStruct((n_slots * ROW_CHUNKS, LANES), h1.dtype),
        scratch_shapes=[pltpu.VMEM((tm * ROW_CHUNKS, LANES), h1.dtype), pltpu.SemaphoreType.DMA,
                        pltpu.SemaphoreType.DMA],
        compiler_params=_cparams(1),
        name="dispatch_rows",
    )(slot3, pads, h1)


def _expert_plan(route, tm):
    t = route.shape[0]
    ids = jnp.arange(MOE_EXPERTS, dtype=jnp.int32)
    e = jnp.concatenate([route[:, ROUTE_E1], route[:, ROUTE_E2]]).astype(jnp.int32)
    onehot = (e[:, None] == ids[None, :]).astype(jnp.int32)
    csum = jnp.cumsum(onehot, axis=0)
    rank = jnp.sum(onehot * csum, axis=1) - 1
    cnt = csum[-1]
    tiles_e = (cnt + tm - 1) // tm
    tile_end = jnp.cumsum(tiles_e)
    tile_start = tile_end - tiles_e
    slot = tile_start[e] * tm + rank
    n_tiles = (2 * t) // tm + MOE_EXPERTS
    tid = jnp.arange(n_tiles, dtype=jnp.int32)
    used = tid < tile_end[-1]
    tile_e = jnp.sum(tile_end[None, :] <= jnp.minimum(tid, tile_end[-1] - 1)[:, None], axis=1)
    nvalid = jnp.where(used, jnp.clip(cnt[tile_e] - (tid - tile_start[tile_e]) * tm, 0, tm), 0)
    first = used & (tid == tile_start[tile_e])
    has = tiles_e > 0
    ordinal = jnp.cumsum(has.astype(jnp.int32)) - 1
    later = jnp.where((ids[None, :] > ids[:, None]) & has[None, :], ids[None, :], MOE_EXPERTS)
    nxt_e = jnp.min(later, axis=1)
    nxt_e = jnp.where(nxt_e == MOE_EXPERTS, -1, nxt_e)
    i32 = lambda a: a.astype(jnp.int32)
    pads = jnp.stack([jnp.append(tile_start * tm + cnt, tile_end[-1] * tm),
                      jnp.append(tiles_e * tm - cnt, (n_tiles - tile_end[-1]) * tm)])
    return (slot.reshape(2, t), i32(pads), n_tiles * tm, i32(tile_e), i32(nvalid), i32(first),
            i32(nxt_e[tile_e]), i32(ordinal[tile_e] % 2))


def _experts_kernel(te_ref, nv_ref, first_ref, nxt_ref, par_ref, x_ref, wg_hbm, wu_hbm, wd_hbm, y_ref,
                    wg_f, wu_f, wd_f, wg_b, wu_b, wd_b, x_b, sem):
    i = pl.program_id(0)
    s = par_ref[i]

    def weight_copies(e, slot):
        return (pltpu.make_async_copy(wg_hbm.at[e], wg_f.at[slot], sem.at[slot]),
                pltpu.make_async_copy(wu_hbm.at[e], wu_f.at[slot], sem.at[slot]),
                pltpu.make_async_copy(wd_hbm.at[e], wd_f.at[slot], sem.at[slot]))

    @pl.when(i == 0)
    def _():
        for cp in weight_copies(te_ref[0], par_ref[0]):
            cp.start()

    @pl.when(first_ref[i] == 1)
    def _():
        @pl.when(nxt_ref[i] >= 0)
        def _():
            for cp in weight_copies(nxt_ref[i], 1 - s):
                cp.start()

        for cp in weight_copies(te_ref[i], s):
            cp.wait()
        wg_b[...] = wg_f[s].astype(BF16)
        wu_b[...] = wu_f[s].astype(BF16)
        wd_b[...] = wd_f[s].astype(BF16)

    nv = nv_ref[i]

    @pl.when(nv > 0)
    def _():
        for c in range(ROW_CHUNKS):
            x_b[:, c * LANES:(c + 1) * LANES] = _load_chunked(x_ref, c).astype(BF16)
        x = x_b[...]
        hid = (_silu(_dot(x, wg_b[...])) * _dot(x, wu_b[...])).astype(BF16)
        _store_chunked(y_ref, _dot(hid, wd_b[...]))

    @pl.when(nv == 0)
    def _():
        y_ref[...] = jnp.zeros(y_ref.shape, y_ref.dtype)


def _experts(xs, tile_e, nvalid, first, nxt, par, wg, wu, wd, tm):
    n_tiles = xs.shape[0] // (tm * ROW_CHUNKS)
    rows = pl.BlockSpec((tm * ROW_CHUNKS, LANES), lambda i, *_: (i, 0))
    hbm = pl.BlockSpec(memory_space=pl.ANY)
    grid_spec = pltpu.PrefetchScalarGridSpec(
        num_scalar_prefetch=5,
        grid=(n_tiles,),
        in_specs=[rows, hbm, hbm, hbm],
        out_specs=rows,
        scratch_shapes=[pltpu.VMEM((2, D_MODEL, MOE_FF), F32), pltpu.VMEM((2, D_MODEL, MOE_FF), F32),
                        pltpu.VMEM((2, MOE_FF, D_MODEL), F32),
                        pltpu.VMEM((D_MODEL, MOE_FF), BF16), pltpu.VMEM((D_MODEL, MOE_FF), BF16),
                        pltpu.VMEM((MOE_FF, D_MODEL), BF16), pltpu.VMEM((tm, D_MODEL), BF16),
                        pltpu.SemaphoreType.DMA((2,))])
    return pl.pallas_call(
        _experts_kernel,
        grid_spec=grid_spec,
        out_shape=jax.ShapeDtypeStruct(xs.shape, F32),
        compiler_params=_cparams(1),
        name="experts",
    )(tile_e, nvalid, first, nxt, par, xs, wg, wu, wd)


def _combine_kernel(n_steps, slot_ref, slot_next_ref, route_ref, h1_ref, g_ref, b_ref, y_hbm, o_ref,
                    gbuf, sem):
    i = pl.program_id(0)
    tm = route_ref.shape[0]

    def request(idx_ref, parity):
        def body(r, carry):
            dst = pl.ds(pl.multiple_of(r * ROW_CHUNKS, ROW_CHUNKS), ROW_CHUNKS)
            for k in range(2):
                src = pl.multiple_of(idx_ref[k, r] * ROW_CHUNKS, ROW_CHUNKS)
                pltpu.make_async_copy(y_hbm.at[pl.ds(src, ROW_CHUNKS)], gbuf.at[parity, k, dst],
                                      sem.at[parity]).start(priority=k % DMA_QUEUES)
            return carry

        lax.fori_loop(0, tm, body, 0, unroll=4)

    @pl.when(i == 0)
    def _():
        request(slot_ref, 0)

    @pl.when(i + 1 < n_steps)
    def _():
        request(slot_next_ref, (i + 1) % 2)

    parity = i % 2
    for k in range(2):
        pltpu.make_async_copy(y_hbm.at[pl.ds(0, tm * ROW_CHUNKS)], gbuf.at[parity, k],
                              sem.at[parity]).wait()
    route = route_ref[...]
    w1 = route[:, ROUTE_W1:ROUTE_W1 + 1]
    w2 = route[:, ROUTE_W2:ROUTE_W2 + 1]
    pre = jnp.concatenate(
        [DEEPNORM_ALPHA * _load_chunked(h1_ref, c)
         + w1 * _load_chunked(gbuf.at[parity, 0], c) + w2 * _load_chunked(gbuf.at[parity, 1], c)
         for c in range(ROW_CHUNKS)], axis=1)
    o_ref[...] = _layernorm(pre, g_ref[...], b_ref[...])


def _combine(y_sorted, slot, route, h1, row0, m, g, b, tm):
    t = slot.shape[1]
    blk0 = row0 // tm
    last = (row0 + m) // tm - 1
    slot3 = slot.reshape(2, t // tm, tm).transpose(1, 0, 2)
    return pl.pallas_call(
        functools.partial(_combine_kernel, m // tm),
        grid=(m // tm,),
        in_specs=[pl.BlockSpec((None, 2, tm), lambda i: (blk0 + i, 0, 0), memory_space=pltpu.SMEM),
                  pl.BlockSpec((None, 2, tm), lambda i: (jnp.minimum(blk0 + i + 1, last), 0, 0),
                               memory_space=pltpu.SMEM),
                  pl.BlockSpec((tm, ROUTER_LANES), lambda i: (blk0 + i, 0)),
                  pl.BlockSpec((tm * ROW_CHUNKS, LANES), lambda i: (blk0 + i, 0)),
                  _row_params(g), _row_params(b),
                  pl.BlockSpec(memory_space=pl.ANY)],
        out_specs=pl.BlockSpec((tm, D_MODEL), lambda i: (i, 0)),
        out_shape=jax.ShapeDtypeStruct((m, D_MODEL), F32),
        scratch_shapes=[pltpu.VMEM((2, 2, tm * ROW_CHUNKS, LANES), F32),
                        pltpu.SemaphoreType.DMA((2,))],
        compiler_params=_cparams(1),
        name="combine_ln",
    )(slot3, slot3, route, h1, g, b, y_sorted)


def _pad_lanes(v, width=LANES):
    v = v.reshape(1, -1)
    return jnp.pad(v, ((0, 0), (0, width - v.shape[1])))


def kernel(x_prompt, x_sample, state_ssd, state_conv, state_gla, w_in, conv_w, conv_b, dt_bias, a_log, d_skip, ssd_norm_w, gla_w_a2, gla_b_a, gla_norm_w, w_out, ln1_g, ln1_b, w_router_group, b_router_group, w_router_expert, b_router_expert, w_gate, w_up, w_down, ln2_g, ln2_b):
    assert w_in.shape[0] == DEPTH == 1
    n_p, len_p, _ = x_prompt.shape
    n_s, len_s, _ = x_sample.shape
    rows_p, rows_s = n_p * len_p, n_s * len_s
    xp = x_prompt.reshape(rows_p, D_MODEL)
    xs = x_sample.reshape(rows_s, D_MODEL)

    offs = [0]
    for s in IN_SPLIT_SIZES:
        offs.append(offs[-1] + s)
    w_t = w_in[0].T
    w_tail_t = jnp.concatenate(
        [jnp.pad(w_t[offs[2]:offs[3]], ((0, LANES - SSD_HEADS), (0, 0))),
         jnp.pad(w_t[offs[7]:offs[8]], ((0, LANES - GLA_RANK), (0, 0)))], axis=0)
    x_b = (xp.astype(BF16), xs.astype(BF16))

    proj = _in_proj(x_b, w_t, ((0, offs[2]), (offs[3], offs[7] - offs[3]),
                               (offs[8], offs[10] - offs[8])))
    tail = _matmul_nt(x_b, w_tail_t, IN_PROJ_ROWS)

    cw, cb = conv_w[0], conv_b[0].reshape(1, -1)
    dtb, alog = _pad_lanes(dt_bias[0]), _pad_lanes(a_log[0])
    dskip_e = jnp.repeat(d_skip[0], SSD_HEAD_DIM).reshape(1, -1)
    nw_ssd = ssd_norm_w[0].reshape(1, -1)
    wa, ba, nw_gla = gla_w_a2[0].astype(BF16), gla_b_a[0].reshape(1, -1), gla_norm_w[0].reshape(1, -1)

    ys_p, ssd_p, conv_p = _ssd_prompt(proj, tail, n_p, len_p, cw, cb, dtb, alog, dskip_e, nw_ssd)
    ys_s, ssd_s, conv_s = _ssd_sample(proj, tail, rows_p, n_s, len_s, state_ssd[0], state_conv[0],
                                      cw, cb, dtb, alog, dskip_e, nw_ssd)
    yg_p, gla_p = _gla_prompt(proj, tail, n_p, len_p, wa, ba, nw_gla)
    yg_s, gla_s = _gla_sample(proj, tail, rows_p, n_s, len_s, state_gla[0], wa, ba, nw_gla)

    wo = w_out[0].astype(BF16)
    w_r = jnp.pad(jnp.concatenate([w_router_group[0], w_router_expert[0]], axis=1),
                  ((0, 0), (0, ROUTER_LANES - MOE_GROUPS - MOE_EXPERTS)))
    wr_hi = w_r.astype(BF16)
    wr_lo = (w_r - wr_hi.astype(F32)).astype(BF16)
    b_r = _pad_lanes(jnp.concatenate([b_router_group[0], b_router_expert[0]]), ROUTER_LANES)
    g1, b1 = ln1_g[0].reshape(1, -1), ln1_b[0].reshape(1, -1)
    g2, b2 = ln2_g[0].reshape(1, -1), ln2_b[0].reshape(1, -1)

    h1, route = _merge((ys_p, ys_s), (yg_p, yg_s), proj, (xp, xs), wo, g1, b1, wr_hi, wr_lo, b_r,
                       MERGE_ROWS)
    slot, pads, n_slots, tile_e, nvalid, first, nxt, par = _expert_plan(route, EXPERT_ROWS)
    x_sorted = _dispatch(h1, slot, pads, n_slots, EXPERT_ROWS, DISPATCH_WINDOW)
    y_sorted = _experts(x_sorted, tile_e, nvalid, first, nxt, par, w_gate[0], w_up[0], w_down[0],
                        EXPERT_ROWS)
    outs = [_combine(y_sorted, slot, route, h1, 0, rows_p, g2, b2, MERGE_ROWS),
            _combine(y_sorted, slot, route, h1, rows_p, rows_s, g2, b2, MERGE_ROWS)]
    y_p = outs[0].reshape(x_prompt.shape)
    y_s = outs[1].reshape(x_sample.shape)
    return (y_p, y_s, ssd_p[None], conv_p[None], gla_p[None], ssd_s[None], conv_s[None], gla_s[None])
```

```python
import functools

import jax
import jax.numpy as jnp
from jax import lax
from jax.experimental import pallas as pl
from jax.experimental.pallas import tpu as pltpu

F32 = jnp.float32
BF16 = jnp.bfloat16

D_MODEL = 2048
SSD_HEADS = 32
SSD_HEAD_DIM = 64
SSD_GROUPS = 8
SSD_STATE = 128
SSD_CONV = 4
SSD_CHUNK = 128
SSD_BC = SSD_GROUPS * SSD_STATE
SSD_CONV_DIM = D_MODEL + 2 * SSD_BC
SSD_GROUP_COLS = D_MODEL // SSD_GROUPS
HEADS_PER_GROUP = SSD_HEADS // SSD_GROUPS
GLA_HEADS = 8
GLA_DK = 128
GLA_DV = 256
GLA_KEY_DIM = GLA_HEADS * GLA_DK
GLA_RANK = 16
GLA_TAU = 16.0
GLA_CHUNK = 64
MOE_GROUPS = 4
MOE_EPG = 8
MOE_EXPERTS = 32
MOE_FF = 512
DEPTH = 1
DEEPNORM_ALPHA = (2.0 * DEPTH) ** 0.25
EPS = 1e-5
IN_SPLIT_SIZES = (D_MODEL, SSD_CONV_DIM, SSD_HEADS, GLA_KEY_DIM, GLA_KEY_DIM, D_MODEL, D_MODEL,
                  GLA_RANK, D_MODEL, D_MODEL)

LANES = 128
HALO_ROWS = 8
COL_Z, COL_XS, COL_BC, COL_QK, COL_V, COL_R, COL_GS, COL_GG = range(8)
TAIL_DT, TAIL_ALO = 0, 1
ROUTER_LANES = 128
ROUTE_E1, ROUTE_E2, ROUTE_W1, ROUTE_W2 = 0, 1, 2, 3
MERGE_ROWS = 256
EXPERT_ROWS = 256
DISPATCH_WINDOW = 512
DMA_QUEUES = 2
VMEM_LIMIT = 56 * 1024 * 1024


def _cparams(n_axes):
    return pltpu.CompilerParams(dimension_semantics=("arbitrary",) * n_axes,
                                vmem_limit_bytes=VMEM_LIMIT)


def _silu(x):
    h = 0.5 * x
    return h + h * jnp.tanh(h)


def _softplus(x):
    return jnp.maximum(x, 0.0) + jnp.log1p(jnp.exp(-jnp.abs(x)))


def _tril(n):
    r = lax.broadcasted_iota(jnp.int32, (n, n), 0)
    c = lax.broadcasted_iota(jnp.int32, (n, n), 1)
    return r >= c


def _expand_group(p, g, lane_head):
    h0 = g * HEADS_PER_GROUP
    out = p[:, h0 + HEADS_PER_GROUP - 1:h0 + HEADS_PER_GROUP]
    for hh in range(HEADS_PER_GROUP - 2, -1, -1):
        out = jnp.where(lane_head == hh, p[:, h0 + hh:h0 + hh + 1], out)
    return out


def _split3(x):
    hi = x.astype(BF16)
    r = x - hi.astype(F32)
    mid = r.astype(BF16)
    lo = (r - mid.astype(F32)).astype(BF16)
    return hi, mid, lo


def _dot_sel(sel, x):
    hi, mid, lo = _split3(x)
    return _dot(sel, lo) + _dot(sel, mid) + _dot(sel, hi)


def _col_bcast(row):
    return jnp.broadcast_to(row, (LANES, LANES)).T


def _dot(a, b, **kw):
    return jnp.dot(a, b, preferred_element_type=F32, **kw)


def _dot_nt(a, b, **kw):
    return lax.dot_general(a, b, (((1,), (1,)), ((), ())), preferred_element_type=F32, **kw)


def _dot_tn(a, b, **kw):
    return lax.dot_general(a, b, (((0,), (0,)), ((), ())), preferred_element_type=F32, **kw)


def _matmul_kernel(n_first, xa_ref, xb_ref, wt_ref, o_ref):
    x = jnp.where(pl.program_id(0) < n_first, xa_ref[...], xb_ref[...])
    o_ref[...] = _dot_nt(x, wt_ref[...].astype(BF16))


def _matmul_nt(xs, wt, tm):
    n, k = wt.shape
    n_a, n_b = xs[0].shape[0] // tm, xs[1].shape[0] // tm
    return pl.pallas_call(
        functools.partial(_matmul_kernel, n_a),
        grid=(n_a + n_b,),
        in_specs=[pl.BlockSpec((tm, k), lambda i: (jnp.minimum(i, n_a - 1), 0)),
                  pl.BlockSpec((tm, k), lambda i: (jnp.maximum(i - n_a, 0), 0)),
                  pl.BlockSpec((n, k), lambda i: (0, 0))],
        out_specs=pl.BlockSpec((tm, n), lambda i: (i, 0)),
        out_shape=jax.ShapeDtypeStruct(((n_a + n_b) * tm, n), F32),
        compiler_params=_cparams(1),
        name="in_proj_tail",
    )(*xs, wt)


IN_PROJ_COLS = 1024
IN_PROJ_ROWS = 1024
CAST_ROWS = 256
SUBLANES = 8


def _in_proj_kernel(plan, n_first, xa_ref, xb_ref, wt_hbm, o_ref, wf, wb, sem):
    j = pl.program_id(0)
    i = pl.program_id(1)
    tn = wb.shape[1]
    n_blocks = plan[-1][1]

    def block_copy(jj, slot):
        shift = 0
        for lo, hi, sh in plan:
            shift = jnp.where((jj >= lo) & (jj < hi), sh, shift)
        src = pl.multiple_of(jj * tn + shift, SUBLANES)
        return pltpu.make_async_copy(wt_hbm.at[pl.ds(src, tn)], wf.at[slot], sem.at[slot])

    @pl.when(i == 0)
    def _():
        slot = j % 2

        @pl.when(j == 0)
        def _():
            block_copy(j, slot).start()

        @pl.when(j + 1 < n_blocks)
        def _():
            block_copy(j + 1, 1 - slot).start()

        block_copy(j, slot).wait()
        for r in range(0, tn, CAST_ROWS):
            wb[:, r:r + CAST_ROWS] = wf[slot, r:r + CAST_ROWS, :].T.astype(BF16)

    def tile(x_ref):
        o_ref[...] = _dot(x_ref[...], wb[...])

    first = i < n_first
    pl.when(first)(lambda: tile(xa_ref))
    pl.when(jnp.logical_not(first))(lambda: tile(xb_ref))


def _in_proj(xs, wt, segments):
    tm, tn = IN_PROJ_ROWS, IN_PROJ_COLS
    k = wt.shape[1]
    n_a, n_b = xs[0].shape[0] // tm, xs[1].shape[0] // tm
    plan, dest = [], 0
    for src, count in segments:
        assert count % tn == 0 and dest % tn == 0 and (src - dest) % SUBLANES == 0
        plan.append((dest // tn, (dest + count) // tn, src - dest))
        dest += count
    return pl.pallas_call(
        functools.partial(_in_proj_kernel, tuple(plan), n_a),
        grid=(dest // tn, n_a + n_b),
        in_specs=[pl.BlockSpec((tm, k), lambda j, i: (jnp.minimum(i, n_a - 1), 0)),
                  pl.BlockSpec((tm, k), lambda j, i: (jnp.maximum(i - n_a, 0), 0)),
                  pl.BlockSpec(memory_space=pl.ANY)],
        out_specs=pl.BlockSpec((tm, tn), lambda j, i: (i, j)),
        out_shape=jax.ShapeDtypeStruct(((n_a + n_b) * tm, dest), F32),
        scratch_shapes=[pltpu.VMEM((2, tn, k), F32), pltpu.VMEM((k, tn), BF16),
                        pltpu.SemaphoreType.DMA((2,))],
        compiler_params=_cparams(2),
        name="in_proj_main",
    )(*xs, wt)


def _conv_silu(ubuf, cw_ref, cb_ref, lo, width, rows):
    acc = cb_ref[:, lo:lo + width]
    for i in range(SSD_CONV):
        r0 = HALO_ROWS - (SSD_CONV - 1) + i
        acc = acc + cw_ref[i:i + 1, lo:lo + width] * ubuf[r0:r0 + rows, lo:lo + width]
    return _silu(acc)


def _gated_group_norm(y_g, z_g, nw_g):
    yg = y_g * _silu(z_g)
    ms = jnp.mean(yg * yg, axis=-1, keepdims=True)
    return yg * lax.rsqrt(ms + EPS) * nw_g


SCAN_CHUNKS = 2
GLA_SCAN_CHUNKS = 4


def _ssd_prompt_chunk(z_ref, xs_ref, bc_ref, tail_ref, cw_ref, cb_ref, dtb_ref, alog_ref,
                      dskip_ref, nw_ref, y_ref, ubuf, ht):
    q = SSD_CHUNK
    ubuf[HALO_ROWS:HALO_ROWS + q, 0:D_MODEL] = xs_ref[...]
    ubuf[HALO_ROWS:HALO_ROWS + q, D_MODEL:SSD_CONV_DIM] = bc_ref[...]

    dt = _softplus(tail_ref[...] + dtb_ref[...])
    da = dt * (-jnp.exp(alog_ref[...]))
    causal = _tril(q)
    acum = _dot_sel(causal.astype(BF16), da)
    acum_t = acum.T
    dt_t = dt.T
    a_last = acum[q - 1:q, :]
    decay_in = jnp.exp(acum)
    w_end = dt * jnp.exp(a_last - acum)
    chunk_decay = jnp.exp(a_last)
    lane_head = lax.broadcasted_iota(jnp.int32, (q, SSD_GROUP_COLS), 1) >> 6
    assert SSD_HEAD_DIM == 1 << 6

    for g in range(SSD_GROUPS):
        lo = g * SSD_GROUP_COLS
        cols = slice(lo, lo + SSD_GROUP_COLS)
        xs_g = _conv_silu(ubuf, cw_ref, cb_ref, lo, SSD_GROUP_COLS, q)
        b_g = _conv_silu(ubuf, cw_ref, cb_ref, D_MODEL + g * SSD_STATE, SSD_STATE, q).astype(BF16)
        c_g = _conv_silu(ubuf, cw_ref, cb_ref, D_MODEL + SSD_BC + g * SSD_STATE, SSD_STATE,
                         q).astype(BF16)
        scores = _dot_nt(c_g, b_g)
        h_g = ht[g]
        xs_b = xs_g.astype(BF16)
        m_heads, x_heads = [], []
        for hh in range(HEADS_PER_GROUP):
            h = g * HEADS_PER_GROUP + hh
            seg = acum[:, h:h + 1] - acum_t[h:h + 1, :]
            decay = jnp.exp(jnp.where(causal, seg, -jnp.inf))
            m_heads.append((scores * decay * dt_t[h:h + 1, :]).astype(BF16))
            x_heads.append(jnp.where(lane_head == hh, xs_b, jnp.zeros_like(xs_b)))
        y_g = (_dot(jnp.concatenate(m_heads, axis=1), jnp.concatenate(x_heads, axis=0))
               + _dot(c_g, h_g.astype(BF16)) * _expand_group(decay_in, g, lane_head)
               + dskip_ref[:, cols] * xs_g)
        y_ref[:, cols] = _gated_group_norm(y_g, z_ref[:, cols], nw_ref[:, cols]).astype(y_ref.dtype)
        xw = (xs_g * _expand_group(w_end, g, lane_head)).astype(BF16)
        ht[g] = h_g * _expand_group(chunk_decay, g, lane_head[0:1]) + _dot_tn(b_g, xw)

    last = HALO_ROWS + q - (SSD_CONV - 1)
    tail_rows = ubuf[last:last + SSD_CONV - 1, :]
    ubuf[HALO_ROWS - (SSD_CONV - 1):HALO_ROWS, :] = tail_rows
    return tail_rows


def _ssd_prompt_kernel(z_ref, xs_ref, bc_ref, tail_ref, cw_ref, cb_ref, dtb_ref, alog_ref,
                       dskip_ref, nw_ref, y_ref, hout_ref, convout_ref, ubuf, ht):
    q = SSD_CHUNK
    c = pl.program_id(1)

    @pl.when(c == 0)
    def _():
        ubuf[0:HALO_ROWS, :] = jnp.zeros((HALO_ROWS, SSD_CONV_DIM), F32)
        ht[...] = jnp.zeros(ht.shape, F32)

    for ci in range(xs_ref.shape[0] // q):
        rows = pl.ds(ci * q, q)
        tail_rows = _ssd_prompt_chunk(z_ref.at[rows], xs_ref.at[rows], bc_ref.at[rows],
                                      tail_ref.at[rows], cw_ref, cb_ref, dtb_ref, alog_ref,
                                      dskip_ref, nw_ref, y_ref.at[rows], ubuf, ht)

    @pl.when(c == pl.num_programs(1) - 1)
    def _():
        convout_ref[0] = tail_rows
        for g in range(SSD_GROUPS):
            h_t = ht[g].T
            for hh in range(HEADS_PER_GROUP):
                hout_ref[0, g * HEADS_PER_GROUP + hh] = h_t[hh * SSD_HEAD_DIM:(hh + 1) * SSD_HEAD_DIM]


def _row_params(p):
    return pl.BlockSpec(p.shape, lambda *_: (0,) * p.ndim)


def _ssd_prompt(proj, tail, n_seq, seq_len, cw, cb, dtb, alog, dskip_e, nw):
    q = SSD_CHUNK * SCAN_CHUNKS
    nc = seq_len // q

    def col(blk):
        return pl.BlockSpec((q, D_MODEL), lambda b, c: (b * nc + c, blk))

    return pl.pallas_call(
        _ssd_prompt_kernel,
        grid=(n_seq, nc),
        in_specs=[col(COL_Z), col(COL_XS), col(COL_BC),
                  pl.BlockSpec((q, LANES), lambda b, c: (b * nc + c, TAIL_DT)),
                  _row_params(cw), _row_params(cb), _row_params(dtb), _row_params(alog),
                  _row_params(dskip_e), _row_params(nw)],
        out_specs=[pl.BlockSpec((q, D_MODEL), lambda b, c: (b * nc + c, 0)),
                   pl.BlockSpec((1, SSD_HEADS, SSD_HEAD_DIM, SSD_STATE), lambda b, c: (b, 0, 0, 0)),
                   pl.BlockSpec((1, SSD_CONV - 1, SSD_CONV_DIM), lambda b, c: (b, 0, 0))],
        out_shape=[jax.ShapeDtypeStruct((n_seq * seq_len, D_MODEL), BF16),
                   jax.ShapeDtypeStruct((n_seq, SSD_HEADS, SSD_HEAD_DIM, SSD_STATE), F32),
                   jax.ShapeDtypeStruct((n_seq, SSD_CONV - 1, SSD_CONV_DIM), F32)],
        scratch_shapes=[pltpu.VMEM((HALO_ROWS + SSD_CHUNK, SSD_CONV_DIM), F32),
                        pltpu.VMEM((SSD_GROUPS, SSD_STATE, SSD_GROUP_COLS), F32)],
        compiler_params=_cparams(2),
        name="ssd_prompt",
    )(proj, proj, proj, tail, cw, cb, dtb, alog, dskip_e, nw)


SAMPLE_SEQS = 4
GLA_SAMPLE_SEQS = 8


def _dot_tn_split(a, b):
    a_hi = a.astype(BF16)
    a_lo = (a - a_hi.astype(F32)).astype(BF16)
    b_hi = b.astype(BF16)
    b_lo = (b - b_hi.astype(F32)).astype(BF16)
    return _dot_tn(a_lo, b_hi) + _dot_tn(a_hi, b_lo) + _dot_tn(a_hi, b_hi)


def _ssd_sample_seq(z_ref, xs_ref, bc_ref, tail_ref, h_ref, conv_ref, cw_ref, cb_ref, dtb_ref,
                    alog_ref, dskip_ref, nw_ref, y_ref, hout_ref, convout_ref, ubuf, decay_rows):
    q = xs_ref.shape[0]
    ubuf[0:HALO_ROWS - (SSD_CONV - 1), :] = jnp.zeros((HALO_ROWS - (SSD_CONV - 1), SSD_CONV_DIM), F32)
    ubuf[HALO_ROWS - (SSD_CONV - 1):HALO_ROWS, :] = conv_ref[...]
    ubuf[HALO_ROWS:HALO_ROWS + q, 0:D_MODEL] = xs_ref[...]
    ubuf[HALO_ROWS:HALO_ROWS + q, D_MODEL:SSD_CONV_DIM] = bc_ref[...]
    convout_ref[...] = ubuf[HALO_ROWS + q - (SSD_CONV - 1):HALO_ROWS + q, :]

    dt = _softplus(tail_ref[...] + dtb_ref[...])
    da = dt * (-jnp.exp(alog_ref[...]))
    acum = _dot_sel(_tril(q).astype(BF16), da)
    a_last = acum[q - 1:q, :]
    decay_in = jnp.exp(acum)
    w_end = dt * jnp.exp(a_last - acum)
    decay_rows[...] = _col_bcast(jnp.exp(a_last))[:SSD_HEADS]
    row = lax.broadcasted_iota(jnp.int32, (q, SSD_GROUP_COLS), 0)
    lane_head = lax.broadcasted_iota(jnp.int32, (q, SSD_GROUP_COLS), 1) >> 6

    for g in range(SSD_GROUPS):
        lo = g * SSD_GROUP_COLS
        cols = slice(lo, lo + SSD_GROUP_COLS)
        xs_g = _conv_silu(ubuf, cw_ref, cb_ref, lo, SSD_GROUP_COLS, q)
        b_g = _conv_silu(ubuf, cw_ref, cb_ref, D_MODEL + g * SSD_STATE, SSD_STATE, q)
        c_g = _conv_silu(ubuf, cw_ref, cb_ref, D_MODEL + SSD_BC + g * SSD_STATE, SSD_STATE, q)
        h_g = h_ref[g * HEADS_PER_GROUP:(g + 1) * HEADS_PER_GROUP].reshape(
            SSD_GROUP_COLS, SSD_STATE)
        y_g = (_dot_nt(c_g.astype(BF16), h_g.astype(BF16)) * _expand_group(decay_in, g, lane_head)
               + dskip_ref[:, cols] * xs_g)
        acum_g = _expand_group(acum, g, lane_head)
        dt_g = _expand_group(dt, g, lane_head)
        for j in range(q):
            s_j = jnp.sum(c_g * b_g[j:j + 1, :], axis=-1, keepdims=True)
            coef = s_j * jnp.exp(acum_g - acum_g[j:j + 1, :]) * dt_g[j:j + 1, :]
            y_g = y_g + jnp.where(row >= j, coef, 0.0) * xs_g[j:j + 1, :]
        y_ref[:, cols] = _gated_group_norm(y_g, z_ref[:, cols], nw_ref[:, cols])
        upd = _dot_tn_split(xs_g * _expand_group(w_end, g, lane_head), b_g)
        for hh in range(HEADS_PER_GROUP):
            h = g * HEADS_PER_GROUP + hh
            rows = slice(hh * SSD_HEAD_DIM, (hh + 1) * SSD_HEAD_DIM)
            hout_ref[h] = h_g[rows] * decay_rows[h:h + 1, :] + upd[rows]


def _ssd_sample_kernel(z_ref, xs_ref, bc_ref, tail_ref, h_ref, conv_ref, cw_ref, cb_ref, dtb_ref,
                       alog_ref, dskip_ref, nw_ref, y_ref, hout_ref, convout_ref, ubuf, decay_rows):
    n = h_ref.shape[0]
    q = xs_ref.shape[0] // n
    for s in range(n):
        rows = pl.ds(s * q, q)
        _ssd_sample_seq(z_ref.at[rows], xs_ref.at[rows], bc_ref.at[rows], tail_ref.at[rows],
                        h_ref.at[s], conv_ref.at[s], cw_ref, cb_ref, dtb_ref, alog_ref, dskip_ref,
                        nw_ref, y_ref.at[rows], hout_ref.at[s], convout_ref.at[s], ubuf.at[s],
                        decay_rows.at[s])


def _ssd_sample(proj, tail, row0, n_seq, seq_len, h0, conv0, cw, cb, dtb, alog, dskip_e, nw):
    nb = SAMPLE_SEQS
    q = seq_len
    blk0 = row0 // (nb * q)

    def col(blk):
        return pl.BlockSpec((nb * q, D_MODEL), lambda b: (blk0 + b, blk))

    state_spec = pl.BlockSpec((nb, SSD_HEADS, SSD_HEAD_DIM, SSD_STATE), lambda b: (b, 0, 0, 0))
    conv_spec = pl.BlockSpec((nb, SSD_CONV - 1, SSD_CONV_DIM), lambda b: (b, 0, 0))
    return pl.pallas_call(
        _ssd_sample_kernel,
        grid=(n_seq // nb,),
        in_specs=[col(COL_Z), col(COL_XS), col(COL_BC),
                  pl.BlockSpec((nb * q, LANES), lambda b: (blk0 + b, TAIL_DT)),
                  state_spec, conv_spec,
                  _row_params(cw), _row_params(cb), _row_params(dtb), _row_params(alog),
                  _row_params(dskip_e), _row_params(nw)],
        out_specs=[pl.BlockSpec((nb * q, D_MODEL), lambda b: (b, 0)), state_spec, conv_spec],
        out_shape=[jax.ShapeDtypeStruct((n_seq * seq_len, D_MODEL), F32),
                   jax.ShapeDtypeStruct(h0.shape, F32),
                   jax.ShapeDtypeStruct(conv0.shape, F32)],
        scratch_shapes=[pltpu.VMEM((nb, HALO_ROWS + q, SSD_CONV_DIM), F32),
                        pltpu.VMEM((nb, SSD_HEADS, SSD_STATE), F32)],
        compiler_params=_cparams(1),
        name="ssd_sample",
    )(proj, proj, proj, tail, h0, conv0, cw, cb, dtb, alog, dskip_e, nw)


def _gla_chunk(qk_ref, v_ref, r_ref, alo_ref, wa_ref, ba_ref, nw_ref, y_ref, get_state, put_state):
    q = v_ref.shape[0]
    a_lo = alo_ref[...][:, :GLA_RANK].astype(BF16)
    gk = -_softplus(-(_dot(a_lo, wa_ref[...]) + ba_ref[...])) / GLA_TAU
    causal = _tril(q)
    bcum = _dot_sel(causal.astype(BF16), gk)
    for h in range(GLA_HEADS):
        kc = slice(h * GLA_DK, (h + 1) * GLA_DK)
        vc = slice(h * GLA_DV, (h + 1) * GLA_DV)
        b_h = bcum[:, kc]
        b_last = b_h[q - 1:q, :]
        q_h = qk_ref[:, kc] * (GLA_DK ** -0.5)
        k_h = qk_ref[:, GLA_KEY_DIM + h * GLA_DK:GLA_KEY_DIM + (h + 1) * GLA_DK]
        v_h = v_ref[:, vc].astype(BF16)
        q_in = (q_h * jnp.exp(b_h)).astype(BF16)
        k_in = (k_h * jnp.exp(-b_h)).astype(BF16)
        att = jnp.where(causal, _dot_nt(q_in, k_in), 0.0).astype(BF16)
        s_h = get_state(h)
        o = _dot(att, v_h) + _dot(q_in, s_h.astype(BF16))
        k_end = (k_h * jnp.exp(b_last - b_h)).astype(BF16)
        d_col = _col_bcast(jnp.exp(b_last))
        put_state(h, s_h * jnp.concatenate([d_col, d_col], axis=1) + _dot_tn(k_end, v_h))
        ms = jnp.mean(o * o, axis=-1, keepdims=True)
        y_ref[:, vc] = (o * lax.rsqrt(ms + EPS) * nw_ref[...] * _silu(r_ref[:, vc])).astype(y_ref.dtype)


def _gla_prompt_kernel(qk_ref, v_ref, r_ref, alo_ref, wa_ref, ba_ref, nw_ref, y_ref, sout_ref, st):
    c = pl.program_id(1)

    @pl.when(c == 0)
    def _():
        st[...] = jnp.zeros(st.shape, F32)

    def put(h, val):
        st[h] = val

    q = GLA_CHUNK
    for ci in range(v_ref.shape[0] // q):
        rows = pl.ds(ci * q, q)
        _gla_chunk(qk_ref.at[rows], v_ref.at[rows], r_ref.at[rows], alo_ref.at[rows], wa_ref, ba_ref,
                   nw_ref, y_ref.at[rows], lambda h: st[h], put)

    @pl.when(c == pl.num_programs(1) - 1)
    def _():
        sout_ref[0] = st[...]


def _gla_prompt(proj, tail, n_seq, seq_len, wa, ba, nw):
    q = GLA_CHUNK * GLA_SCAN_CHUNKS
    nc = seq_len // q

    def col(blk):
        return pl.BlockSpec((q, D_MODEL), lambda b, c: (b * nc + c, blk))

    return pl.pallas_call(
        _gla_prompt_kernel,
        grid=(n_seq, nc),
        in_specs=[col(COL_QK), col(COL_V), col(COL_R),
                  pl.BlockSpec((q, LANES), lambda b, c: (b * nc + c, TAIL_ALO)),
                  _row_params(wa), _row_params(ba), _row_params(nw)],
        out_specs=[pl.BlockSpec((q, D_MODEL), lambda b, c: (b * nc + c, 0)),
                   pl.BlockSpec((1, GLA_HEADS, GLA_DK, GLA_DV), lambda b, c: (b, 0, 0, 0))],
        out_shape=[jax.ShapeDtypeStruct((n_seq * seq_len, D_MODEL), BF16),
                   jax.ShapeDtypeStruct((n_seq, GLA_HEADS, GLA_DK, GLA_DV), F32)],
        scratch_shapes=[pltpu.VMEM((GLA_HEADS, GLA_DK, GLA_DV), F32)],
        compiler_params=_cparams(2),
        name="gla_prompt",
    )(proj, proj, proj, tail, wa, ba, nw)


def _gla_sample_kernel(qk_ref, v_ref, r_ref, alo_ref, s_ref, wa_ref, ba_ref, nw_ref, y_ref, sout_ref):
    n = s_ref.shape[0]
    q = v_ref.shape[0] // n
    for s in range(n):
        rows = pl.ds(s * q, q)

        def put(h, val, s=s):
            sout_ref[s, h] = val

        _gla_chunk(qk_ref.at[rows], v_ref.at[rows], r_ref.at[rows], alo_ref.at[rows], wa_ref, ba_ref,
                   nw_ref, y_ref.at[rows], lambda h, s=s: s_ref[s, h], put)


def _gla_sample(proj, tail, row0, n_seq, seq_len, s0, wa, ba, nw):
    nb = GLA_SAMPLE_SEQS
    q = seq_len
    blk0 = row0 // (nb * q)

    def col(blk):
        return pl.BlockSpec((nb * q, D_MODEL), lambda b: (blk0 + b, blk))

    state_spec = pl.BlockSpec((nb, GLA_HEADS, GLA_DK, GLA_DV), lambda b: (b, 0, 0, 0))
    return pl.pallas_call(
        _gla_sample_kernel,
        grid=(n_seq // nb,),
        in_specs=[col(COL_QK), col(COL_V), col(COL_R),
                  pl.BlockSpec((nb * q, LANES), lambda b: (blk0 + b, TAIL_ALO)),
                  state_spec, _row_params(wa), _row_params(ba), _row_params(nw)],
        out_specs=[pl.BlockSpec((nb * q, D_MODEL), lambda b: (b, 0)), state_spec],
        out_shape=[jax.ShapeDtypeStruct((n_seq * seq_len, D_MODEL), F32),
                   jax.ShapeDtypeStruct(s0.shape, F32)],
        compiler_params=_cparams(1),
        name="gla_sample",
    )(proj, proj, proj, tail, s0, wa, ba, nw)


def _layernorm(x, g, b):
    mu = jnp.mean(x, axis=-1, keepdims=True)
    xc = x - mu
    var = jnp.mean(xc * xc, axis=-1, keepdims=True)
    return xc * lax.rsqrt(var + EPS) * g + b


def _route(logits):
    lane = lax.broadcasted_iota(jnp.int32, logits.shape, 1)
    neg = -jnp.inf
    big = ROUTER_LANES
    glog = jnp.where(lane < MOE_GROUPS, logits, neg)
    gmax = jnp.max(glog, axis=-1, keepdims=True)
    g_sel = jnp.min(jnp.where(glog == gmax, lane, big), axis=-1, keepdims=True)
    p_g = 1.0 / jnp.sum(jnp.exp(glog - gmax), axis=-1, keepdims=True)
    e_lane = lane - MOE_GROUPS
    in_group = (e_lane >= 0) & (e_lane < MOE_EXPERTS) & (e_lane // MOE_EPG == g_sel)
    el = jnp.where(in_group, logits, neg)
    v1 = jnp.max(el, axis=-1, keepdims=True)
    i1 = jnp.min(jnp.where(el == v1, lane, big), axis=-1, keepdims=True)
    el2 = jnp.where(lane == i1, neg, el)
    v2 = jnp.max(el2, axis=-1, keepdims=True)
    i2 = jnp.min(jnp.where(el2 == v2, lane, big), axis=-1, keepdims=True)
    e2 = jnp.exp(v2 - v1)
    w1 = p_g / (1.0 + e2)
    w2 = p_g * e2 / (1.0 + e2)
    first = (i1 - MOE_GROUPS).astype(F32)
    second = (i2 - MOE_GROUPS).astype(F32)
    return jnp.where(lane == ROUTE_E1, first,
                     jnp.where(lane == ROUTE_E2, second,
                               jnp.where(lane == ROUTE_W1, w1,
                                         jnp.where(lane == ROUTE_W2, w2, 0.0))))


ROW_CHUNKS = D_MODEL // LANES


def _store_chunked(ref, val):
    rows = val.shape[0]
    for c in range(ROW_CHUNKS):
        ref[pl.ds(c, rows, stride=ROW_CHUNKS), :] = val[:, c * LANES:(c + 1) * LANES]


def _load_chunked(ref, c):
    return ref[pl.ds(c, ref.shape[0] // ROW_CHUNKS, stride=ROW_CHUNKS), :]


def _merge_kernel(n_first, ysa_ref, ysb_ref, yga_ref, ygb_ref, gs_ref, gg_ref, xa_ref, xb_ref, wo_ref,
                  g_ref, b_ref, wrh_ref, wrl_ref, br_ref, h1_ref, route_ref):
    def tile(ys_ref, yg_ref, x_ref):
        merged = (jax.nn.sigmoid(gs_ref[...]) * ys_ref[...]
                  + jax.nn.sigmoid(gg_ref[...]) * yg_ref[...])
        mix = _dot(merged.astype(BF16), wo_ref[...])
        h1 = _layernorm(DEEPNORM_ALPHA * x_ref[...] + mix, g_ref[...], b_ref[...])
        _store_chunked(h1_ref, h1)
        h_hi = h1.astype(BF16)
        h_lo = (h1 - h_hi.astype(F32)).astype(BF16)
        logits = (_dot(h_hi, wrh_ref[...]) + _dot(h_lo, wrh_ref[...]) + _dot(h_hi, wrl_ref[...])
                  + br_ref[...])
        route_ref[...] = _route(logits)

    first = pl.program_id(0) < n_first
    pl.when(first)(lambda: tile(ysa_ref, yga_ref, xa_ref))
    pl.when(jnp.logical_not(first))(lambda: tile(ysb_ref, ygb_ref, xb_ref))


def _merge(ys, yg, gates, xs, wo, g, b, wr_hi, wr_lo, br, tm):
    n_a, n_b = xs[0].shape[0] // tm, xs[1].shape[0] // tm
    m = (n_a + n_b) * tm

    def rows(i):
        return (i, 0)

    pair = [pl.BlockSpec((tm, D_MODEL), lambda i: (jnp.minimum(i, n_a - 1), 0)),
            pl.BlockSpec((tm, D_MODEL), lambda i: (jnp.maximum(i - n_a, 0), 0))]
    return pl.pallas_call(
        functools.partial(_merge_kernel, n_a),
        grid=(m // tm,),
        in_specs=[*pair, *pair,
                  pl.BlockSpec((tm, D_MODEL), lambda i: (i, COL_GS)),
                  pl.BlockSpec((tm, D_MODEL), lambda i: (i, COL_GG)),
                  *pair,
                  pl.BlockSpec(wo.shape, lambda i: (0, 0), pipeline_mode=pl.Buffered(1)),
                  _row_params(g), _row_params(b),
                  _row_params(wr_hi), _row_params(wr_lo), _row_params(br)],
        out_specs=[pl.BlockSpec((tm * ROW_CHUNKS, LANES), rows),
                   pl.BlockSpec((tm, ROUTER_LANES), rows)],
        out_shape=[jax.ShapeDtypeStruct((m * ROW_CHUNKS, LANES), F32),
                   jax.ShapeDtypeStruct((m, ROUTER_LANES), F32)],
        compiler_params=_cparams(1),
        name="merge_outproj_ln_router",
    )(*ys, *yg, gates, gates, *xs, wo, g, b, wr_hi, wr_lo, br)


def _zero_runs(pads_ref, zeros, x_hbm, sem, action):
    tm = zeros.shape[0] // ROW_CHUNKS

    def piece(pos, rows):
        dst = pl.multiple_of(pos * ROW_CHUNKS, ROW_CHUNKS)
        return pltpu.make_async_copy(zeros.at[pl.ds(0, rows * ROW_CHUNKS)],
                                     x_hbm.at[pl.ds(dst, rows * ROW_CHUNKS)], sem)

    def run(e, carry):
        pos, length = pads_ref[0, e], pads_ref[1, e]

        def whole(q, c):
            getattr(piece(pos + q * tm, tm), action)()
            return c

        lax.fori_loop(0, length // tm, whole, 0)
        pos = pos + (length // tm) * tm
        rows = tm // 2
        while rows >= 1:
            @pl.when((length & rows) != 0)
            def _(pos=pos, rows=rows):
                getattr(piece(pos, rows), action)()

            pos = pos + (length & rows)
            rows //= 2
        return carry

    lax.fori_loop(0, pads_ref.shape[1], run, 0)


def _dispatch_kernel(slot_ref, pads_ref, h_ref, x_hbm, zeros, sem, zsem):
    n = h_ref.shape[0] // ROW_CHUNKS

    def issue(r, carry):
        src = h_ref.at[pl.ds(pl.multiple_of(r * ROW_CHUNKS, ROW_CHUNKS), ROW_CHUNKS)]
        for k in range(2):
            dst = pl.multiple_of(slot_ref[k, r] * ROW_CHUNKS, ROW_CHUNKS)
            pltpu.make_async_copy(src, x_hbm.at[pl.ds(dst, ROW_CHUNKS)], sem).start(
                priority=k % DMA_QUEUES)
        return carry

    lax.fori_loop(0, n, issue, 0, unroll=4)

    @pl.when(pl.program_id(0) == pl.num_programs(0) - 1)
    def _():
        zeros[...] = jnp.zeros(zeros.shape, zeros.dtype)
        _zero_runs(pads_ref, zeros, x_hbm, zsem, "start")
        _zero_runs(pads_ref, zeros, x_hbm, zsem, "wait")

    for k in range(2):
        pltpu.make_async_copy(h_ref, x_hbm.at[pl.ds(0, n * ROW_CHUNKS)], sem).wait()


def _dispatch(h1, slot, pads, n_slots, tm, window):
    t = slot.shape[1]
    steps = t // window
    slot3 = slot.reshape(2, steps, window).transpose(1, 0, 2)
    return pl.pallas_call(
        _dispatch_kernel,
        grid=(steps,),
        in_specs=[pl.BlockSpec((None, 2, window), lambda i: (i, 0, 0), memory_space=pltpu.SMEM),
                  pl.BlockSpec(pads.shape, lambda i: (0, 0), memory_space=pltpu.SMEM),
                  pl.BlockSpec((window * ROW_CHUNKS, LANES), lambda i: (i, 0))],
        out_specs=pl.BlockSpec(memory_space=pl.ANY),
        out_shape=jax.ShapeDtypeStruct((n_slots * ROW_CHUNKS, LANES), h1.dtype),
        scratch_shapes=[pltpu.VMEM((tm * ROW_CHUNKS, LANES), h1.dtype), pltpu.SemaphoreType.DMA,
                        pltpu.SemaphoreType.DMA],
        compiler_params=_cparams(1),
        name="dispatch_rows",
    )(slot3, pads, h1)


def _expert_plan(route, tm):
    t = route.shape[0]
    ids = jnp.arange(MOE_EXPERTS, dtype=jnp.int32)
    e = jnp.concatenate([route[:, ROUTE_E1], route[:, ROUTE_E2]]).astype(jnp.int32)
    onehot = (e[:, None] == ids[None, :]).astype(jnp.int32)
    csum = jnp.cumsum(onehot, axis=0)
    rank = jnp.sum(onehot * csum, axis=1) - 1
    cnt = csum[-1]
    tiles_e = (cnt + tm - 1) // tm
    tile_end = jnp.cumsum(tiles_e)
    tile_start = tile_end - tiles_e
    slot = tile_start[e] * tm + rank
    n_tiles = (2 * t) // tm + MOE_EXPERTS
    tid = jnp.arange(n_tiles, dtype=jnp.int32)
    used = tid < tile_end[-1]
    tile_e = jnp.sum(tile_end[None, :] <= jnp.minimum(tid, tile_end[-1] - 1)[:, None], axis=1)
    nvalid = jnp.where(used, jnp.clip(cnt[tile_e] - (tid - tile_start[tile_e]) * tm, 0, tm), 0)
    first = used & (tid == tile_start[tile_e])
    has = tiles_e > 0
    ordinal = jnp.cumsum(has.astype(jnp.int32)) - 1
    later = jnp.where((ids[None, :] > ids[:, None]) & has[None, :], ids[None, :], MOE_EXPERTS)
    nxt_e = jnp.min(later, axis=1)
    nxt_e = jnp.where(nxt_e == MOE_EXPERTS, -1, nxt_e)
    i32 = lambda a: a.astype(jnp.int32)
    pads = jnp.stack([jnp.append(tile_start * tm + cnt, tile_end[-1] * tm),
                      jnp.append(tiles_e * tm - cnt, (n_tiles - tile_end[-1]) * tm)])
    return (slot.reshape(2, t), i32(pads), n_tiles * tm, i32(tile_e), i32(nvalid), i32(first),
            i32(nxt_e[tile_e]), i32(ordinal[tile_e] % 2))


def _experts_kernel(te_ref, nv_ref, first_ref, nxt_ref, par_ref, x_ref, wg_hbm, wu_hbm, wd_hbm, y_ref,
                    wg_f, wu_f, wd_f, wg_b, wu_b, wd_b, x_b, sem):
    i = pl.program_id(0)
    s = par_ref[i]

    def weight_copies(e, slot):
        return (pltpu.make_async_copy(wg_hbm.at[e], wg_f.at[slot], sem.at[slot]),
                pltpu.make_async_copy(wu_hbm.at[e], wu_f.at[slot], sem.at[slot]),
                pltpu.make_async_copy(wd_hbm.at[e], wd_f.at[slot], sem.at[slot]))

    @pl.when(i == 0)
    def _():
        for cp in weight_copies(te_ref[0], par_ref[0]):
            cp.start()

    @pl.when(first_ref[i] == 1)
    def _():
        @pl.when(nxt_ref[i] >= 0)
        def _():
            for cp in weight_copies(nxt_ref[i], 1 - s):
                cp.start()

        for cp in weight_copies(te_ref[i], s):
            cp.wait()
        wg_b[...] = wg_f[s].astype(BF16)
        wu_b[...] = wu_f[s].astype(BF16)
        wd_b[...] = wd_f[s].astype(BF16)

    nv = nv_ref[i]

    @pl.when(nv > 0)
    def _():
        for c in range(ROW_CHUNKS):
            x_b[:, c * LANES:(c + 1) * LANES] = _load_chunked(x_ref, c).astype(BF16)
        x = x_b[...]
        hid = (_silu(_dot(x, wg_b[...])) * _dot(x, wu_b[...])).astype(BF16)
        _store_chunked(y_ref, _dot(hid, wd_b[...]))

    @pl.when(nv == 0)
    def _():
        y_ref[...] = jnp.zeros(y_ref.shape, y_ref.dtype)


def _experts(xs, tile_e, nvalid, first, nxt, par, wg, wu, wd, tm):
    n_tiles = xs.shape[0] // (tm * ROW_CHUNKS)
    rows = pl.BlockSpec((tm * ROW_CHUNKS, LANES), lambda i, *_: (i, 0))
    hbm = pl.BlockSpec(memory_space=pl.ANY)
    grid_spec = pltpu.PrefetchScalarGridSpec(
        num_scalar_prefetch=5,
        grid=(n_tiles,),
        in_specs=[rows, hbm, hbm, hbm],
        out_specs=rows,
        scratch_shapes=[pltpu.VMEM((2, D_MODEL, MOE_FF), F32), pltpu.VMEM((2, D_MODEL, MOE_FF), F32),
                        pltpu.VMEM((2, MOE_FF, D_MODEL), F32),
                        pltpu.VMEM((D_MODEL, MOE_FF), BF16), pltpu.VMEM((D_MODEL, MOE_FF), BF16),
                        pltpu.VMEM((MOE_FF, D_MODEL), BF16), pltpu.VMEM((tm, D_MODEL), BF16),
                        pltpu.SemaphoreType.DMA((2,))])
    return pl.pallas_call(
        _experts_kernel,
        grid_spec=grid_spec,
        out_shape=jax.ShapeDtypeStruct(xs.shape, F32),
        compiler_params=_cparams(1),
        name="experts",
    )(tile_e, nvalid, first, nxt, par, xs, wg, wu, wd)


def _combine_kernel(n_steps, slot_ref, slot_next_ref, route_ref, h1_ref, g_ref, b_ref, y_hbm, o_ref,
                    gbuf, sem):
    i = pl.program_id(0)
    tm = route_ref.shape[0]

    def request(idx_ref, parity):
        def body(r, carry):
            dst = pl.ds(pl.multiple_of(r * ROW_CHUNKS, ROW_CHUNKS), ROW_CHUNKS)
            for k in range(2):
                src = pl.multiple_of(idx_ref[k, r] * ROW_CHUNKS, ROW_CHUNKS)
                pltpu.make_async_copy(y_hbm.at[pl.ds(src, ROW_CHUNKS)], gbuf.at[parity, k, dst],
                                      sem.at[parity]).start(priority=k % DMA_QUEUES)
            return carry

        lax.fori_loop(0, tm, body, 0, unroll=4)

    @pl.when(i == 0)
    def _():
        request(slot_ref, 0)

    @pl.when(i + 1 < n_steps)
    def _():
        request(slot_next_ref, (i + 1) % 2)

    parity = i % 2
    for k in range(2):
        pltpu.make_async_copy(y_hbm.at[pl.ds(0, tm * ROW_CHUNKS)], gbuf.at[parity, k],
                              sem.at[parity]).wait()
    route = route_ref[...]
    w1 = route[:, ROUTE_W1:ROUTE_W1 + 1]
    w2 = route[:, ROUTE_W2:ROUTE_W2 + 1]
    pre = jnp.concatenate(
        [DEEPNORM_ALPHA * _load_chunked(h1_ref, c)
         + w1 * _load_chunked(gbuf.at[parity, 0], c) + w2 * _load_chunked(gbuf.at[parity, 1], c)
         for c in range(ROW_CHUNKS)], axis=1)
    o_ref[...] = _layernorm(pre, g_ref[...], b_ref[...])


def _combine(y_sorted, slot, route, h1, row0, m, g, b, tm):
    t = slot.shape[1]
    blk0 = row0 // tm
    last = (row0 + m) // tm - 1
    slot3 = slot.reshape(2, t // tm, tm).transpose(1, 0, 2)
    return pl.pallas_call(
        functools.partial(_combine_kernel, m // tm),
        grid=(m // tm,),
        in_specs=[pl.BlockSpec((None, 2, tm), lambda i: (blk0 + i, 0, 0), memory_space=pltpu.SMEM),
                  pl.BlockSpec((None, 2, tm), lambda i: (jnp.minimum(blk0 + i + 1, last), 0, 0),
                               memory_space=pltpu.SMEM),
                  pl.BlockSpec((tm, ROUTER_LANES), lambda i: (blk0 + i, 0)),
                  pl.BlockSpec((tm * ROW_CHUNKS, LANES), lambda i: (blk0 + i, 0)),
                  _row_params(g), _row_params(b),
                  pl.BlockSpec(memory_space=pl.ANY)],
        out_specs=pl.BlockSpec((tm, D_MODEL), lambda i: (i, 0)),
        out_shape=jax.ShapeDtypeStruct((m, D_MODEL), F32),
        scratch_shapes=[pltpu.VMEM((2, 2, tm * ROW_CHUNKS, LANES), F32),
                        pltpu.SemaphoreType.DMA((2,))],
        compiler_params=_cparams(1),
        name="combine_ln",
    )(slot3, slot3, route, h1, g, b, y_sorted)


def _pad_lanes(v, width=LANES):
    v = v.reshape(1, -1)
    return jnp.pad(v, ((0, 0), (0, width - v.shape[1])))


def kernel(x_prompt, x_sample, state_ssd, state_conv, state_gla, w_in, conv_w, conv_b, dt_bias, a_log, d_skip, ssd_norm_w, gla_w_a2, gla_b_a, gla_norm_w, w_out, ln1_g, ln1_b, w_router_group, b_router_group, w_router_expert, b_router_expert, w_gate, w_up, w_down, ln2_g, ln2_b):
    assert w_in.shape[0] == DEPTH == 1
    n_p, len_p, _ = x_prompt.shape
    n_s, len_s, _ = x_sample.shape
    rows_p, rows_s = n_p * len_p, n_s * len_s
    xp = x_prompt.reshape(rows_p, D_MODEL)
    xs = x_sample.reshape(rows_s, D_MODEL)

    offs = [0]
    for s in IN_SPLIT_SIZES:
        offs.append(offs[-1] + s)
    w_t = w_in[0].T
    w_tail_t = jnp.concatenate(
        [jnp.pad(w_t[offs[2]:offs[3]], ((0, LANES - SSD_HEADS), (0, 0))),
         jnp.pad(w_t[offs[7]:offs[8]], ((0, LANES - GLA_RANK), (0, 0)))], axis=0)
    x_b = (xp.astype(BF16), xs.astype(BF16))

    proj = _in_proj(x_b, w_t, ((0, offs[2]), (offs[3], offs[7] - offs[3]),
                               (offs[8], offs[10] - offs[8])))
    tail = _matmul_nt(x_b, w_tail_t, IN_PROJ_ROWS)

    cw, cb = conv_w[0], conv_b[0].reshape(1, -1)
    dtb, alog = _pad_lanes(dt_bias[0]), _pad_lanes(a_log[0])
    dskip_e = jnp.repeat(d_skip[0], SSD_HEAD_DIM).reshape(1, -1)
    nw_ssd = ssd_norm_w[0].reshape(1, -1)
    wa, ba, nw_gla = gla_w_a2[0].astype(BF16), gla_b_a[0].reshape(1, -1), gla_norm_w[0].reshape(1, -1)

    ys_p, ssd_p, conv_p = _ssd_prompt(proj, tail, n_p, len_p, cw, cb, dtb, alog, dskip_e, nw_ssd)
    ys_s, ssd_s, conv_s = _ssd_sample(proj, tail, rows_p, n_s, len_s, state_ssd[0], state_conv[0],
                                      cw, cb, dtb, alog, dskip_e, nw_ssd)
    yg_p, gla_p = _gla_prompt(proj, tail, n_p, len_p, wa, ba, nw_gla)
    yg_s, gla_s = _gla_sample(proj, tail, rows_p, n_s, len_s, state_gla[0], wa, ba, nw_gla)

    wo = w_out[0].astype(BF16)
    w_r = jnp.pad(jnp.concatenate([w_router_group[0], w_router_expert[0]], axis=1),
                  ((0, 0), (0, ROUTER_LANES - MOE_GROUPS - MOE_EXPERTS)))
    wr_hi = w_r.astype(BF16)
    wr_lo = (w_r - wr_hi.astype(F32)).astype(BF16)
    b_r = _pad_lanes(jnp.concatenate([b_router_group[0], b_router_expert[0]]), ROUTER_LANES)
    g1, b1 = ln1_g[0].reshape(1, -1), ln1_b[0].reshape(1, -1)
    g2, b2 = ln2_g[0].reshape(1, -1), ln2_b[0].reshape(1, -1)

    h1, route = _merge((ys_p, ys_s), (yg_p, yg_s), proj, (xp, xs), wo, g1, b1, wr_hi, wr_lo, b_r,
                       MERGE_ROWS)
    slot, pads, n_slots, tile_e, nvalid, first, nxt, par = _expert_plan(route, EXPERT_ROWS)
    x_sorted = _dispatch(h1, slot, pads, n_slots, EXPERT_ROWS, DISPATCH_WINDOW)
    y_sorted = _experts(x_sorted, tile_e, nvalid, first, nxt, par, w_gate[0], w_up[0], w_down[0],
                        EXPERT_ROWS)
    outs = [_combine(y_sorted, slot, route, h1, 0, rows_p, g2, b2, MERGE_ROWS),
            _combine(y_sorted, slot, route, h1, rows_p, rows_s, g2, b2, MERGE_ROWS)]
    y_p = outs[0].reshape(x_prompt.shape)
    y_s = outs[1].reshape(x_sample.shape)
    return (y_p, y_s, ssd_p[None], conv_p[None], gla_p[None], ssd_s[None], conv_s[None], gla_s[None])
```

```python
import functools

import jax
import jax.numpy as jnp
from jax import lax
from jax.experimental import pallas as pl
from jax.experimental.pallas import tpu as pltpu

F32 = jnp.float32
BF16 = jnp.bfloat16

D_MODEL = 2048
SSD_HEADS = 32
SSD_HEAD_DIM = 64
SSD_GROUPS = 8
SSD_STATE = 128
SSD_CONV = 4
SSD_CHUNK = 128
SSD_BC = SSD_GROUPS * SSD_STATE
SSD_CONV_DIM = D_MODEL + 2 * SSD_BC
SSD_GROUP_COLS = D_MODEL // SSD_GROUPS
HEADS_PER_GROUP = SSD_HEADS // SSD_GROUPS
GLA_HEADS = 8
GLA_DK = 128
GLA_DV = 256
GLA_KEY_DIM = GLA_HEADS * GLA_DK
GLA_RANK = 16
GLA_TAU = 16.0
GLA_CHUNK = 64
MOE_GROUPS = 4
MOE_EPG = 8
MOE_EXPERTS = 32
MOE_FF = 512
DEPTH = 1
DEEPNORM_ALPHA = (2.0 * DEPTH) ** 0.25
EPS = 1e-5
IN_SPLIT_SIZES = (D_MODEL, SSD_CONV_DIM, SSD_HEADS, GLA_KEY_DIM, GLA_KEY_DIM, D_MODEL, D_MODEL,
                  GLA_RANK, D_MODEL, D_MODEL)

LANES = 128
HALO_ROWS = 8
COL_Z, COL_XS, COL_BC, COL_QK, COL_V, COL_R, COL_GS, COL_GG = range(8)
TAIL_DT, TAIL_ALO = 0, 1
ROUTER_LANES = 128
ROUTE_E1, ROUTE_E2, ROUTE_W1, ROUTE_W2 = 0, 1, 2, 3
MERGE_ROWS = 256
EXPERT_ROWS = 256
DISPATCH_WINDOW = 512
DMA_QUEUES = 2
VMEM_LIMIT = 56 * 1024 * 1024


def _cparams(n_axes):
    return pltpu.CompilerParams(dimension_semantics=("arbitrary",) * n_axes,
                                vmem_limit_bytes=VMEM_LIMIT)


def _silu(x):
    h = 0.5 * x
    return h + h * jnp.tanh(h)


def _softplus(x):
    return jnp.maximum(x, 0.0) + jnp.log1p(jnp.exp(-jnp.abs(x)))


def _tril(n):
    r = lax.broadcasted_iota(jnp.int32, (n, n), 0)
    c = lax.broadcasted_iota(jnp.int32, (n, n), 1)
    return r >= c


def _expand_group(p, g, lane_head):
    h0 = g * HEADS_PER_GROUP
    out = p[:, h0 + HEADS_PER_GROUP - 1:h0 + HEADS_PER_GROUP]
    for hh in range(HEADS_PER_GROUP - 2, -1, -1):
        out = jnp.where(lane_head == hh, p[:, h0 + hh:h0 + hh + 1], out)
    return out


def _split3(x):
    hi = x.astype(BF16)
    r = x - hi.astype(F32)
    mid = r.astype(BF16)
    lo = (r - mid.astype(F32)).astype(BF16)
    return hi, mid, lo


def _dot_sel(sel, x):
    hi, mid, lo = _split3(x)
    return _dot(sel, lo) + _dot(sel, mid) + _dot(sel, hi)


def _col_bcast(row):
    return jnp.broadcast_to(row, (LANES, LANES)).T


def _dot(a, b, **kw):
    return jnp.dot(a, b, preferred_element_type=F32, **kw)


def _dot_nt(a, b, **kw):
    return lax.dot_general(a, b, (((1,), (1,)), ((), ())), preferred_element_type=F32, **kw)


def _dot_tn(a, b, **kw):
    return lax.dot_general(a, b, (((0,), (0,)), ((), ())), preferred_element_type=F32, **kw)


def _matmul_kernel(x_ref, wt_ref, o_ref):
    o_ref[...] = _dot_nt(x_ref[...], wt_ref[...].astype(BF16))


def _matmul_nt(x, wt, tm):
    n, k = wt.shape
    m = x.shape[0]
    return pl.pallas_call(
        _matmul_kernel,
        grid=(m // tm,),
        in_specs=[pl.BlockSpec((tm, k), lambda i: (i, 0)),
                  pl.BlockSpec((n, k), lambda i: (0, 0))],
        out_specs=pl.BlockSpec((tm, n), lambda i: (i, 0)),
        out_shape=jax.ShapeDtypeStruct((m, n), F32),
        compiler_params=_cparams(1),
        name="in_proj_tail",
    )(x, wt)


IN_PROJ_COLS = 1024
IN_PROJ_ROWS = 1536
CAST_ROWS = 256
SUBLANES = 8


def _in_proj_kernel(plan, x_ref, wt_hbm, o_ref, wf, wb, sem):
    j = pl.program_id(0)
    i = pl.program_id(1)
    tn = wb.shape[1]
    n_blocks = plan[-1][1]

    def block_copy(jj, slot):
        shift = 0
        for lo, hi, sh in plan:
            shift = jnp.where((jj >= lo) & (jj < hi), sh, shift)
        src = pl.multiple_of(jj * tn + shift, SUBLANES)
        return pltpu.make_async_copy(wt_hbm.at[pl.ds(src, tn)], wf.at[slot], sem.at[slot])

    @pl.when(i == 0)
    def _():
        slot = j % 2

        @pl.when(j == 0)
        def _():
            block_copy(j, slot).start()

        @pl.when(j + 1 < n_blocks)
        def _():
            block_copy(j + 1, 1 - slot).start()

        block_copy(j, slot).wait()
        for r in range(0, tn, CAST_ROWS):
            wb[:, r:r + CAST_ROWS] = wf[slot, r:r + CAST_ROWS, :].T.astype(BF16)

    o_ref[...] = _dot(x_ref[...], wb[...])


def _in_proj(x, wt, segments):
    tm, tn = IN_PROJ_ROWS, IN_PROJ_COLS
    m, k = x.shape
    assert m % tm == 0
    plan, dest = [], 0
    for src, count in segments:
        assert count % tn == 0 and dest % tn == 0 and (src - dest) % SUBLANES == 0
        plan.append((dest // tn, (dest + count) // tn, src - dest))
        dest += count
    return pl.pallas_call(
        functools.partial(_in_proj_kernel, tuple(plan)),
        grid=(dest // tn, m // tm),
        in_specs=[pl.BlockSpec((tm, k), lambda j, i: (i, 0)),
                  pl.BlockSpec(memory_space=pl.ANY)],
        out_specs=pl.BlockSpec((tm, tn), lambda j, i: (i, j)),
        out_shape=jax.ShapeDtypeStruct((m, dest), F32),
        scratch_shapes=[pltpu.VMEM((2, tn, k), F32), pltpu.VMEM((k, tn), BF16),
                        pltpu.SemaphoreType.DMA((2,))],
        compiler_params=_cparams(2),
        name="in_proj_main",
    )(x, wt)


def _conv_silu(ubuf, cw_ref, cb_ref, lo, width, rows):
    acc = cb_ref[:, lo:lo + width]
    for i in range(SSD_CONV):
        r0 = HALO_ROWS - (SSD_CONV - 1) + i
        acc = acc + cw_ref[i:i + 1, lo:lo + width] * ubuf[r0:r0 + rows, lo:lo + width]
    return _silu(acc)


def _gated_group_norm(y_g, z_g, nw_g):
    yg = y_g * _silu(z_g)
    ms = jnp.mean(yg * yg, axis=-1, keepdims=True)
    return yg * lax.rsqrt(ms + EPS) * nw_g


SCAN_CHUNKS = 2
GLA_SCAN_CHUNKS = 4


def _ssd_prompt_chunk(z_ref, xs_ref, bc_ref, tail_ref, cw_ref, cb_ref, dtb_ref, alog_ref,
                      dskip_ref, nw_ref, y_ref, ubuf, ht):
    q = SSD_CHUNK
    ubuf[HALO_ROWS:HALO_ROWS + q, 0:D_MODEL] = xs_ref[...]
    ubuf[HALO_ROWS:HALO_ROWS + q, D_MODEL:SSD_CONV_DIM] = bc_ref[...]

    dt = _softplus(tail_ref[...] + dtb_ref[...])
    da = dt * (-jnp.exp(alog_ref[...]))
    causal = _tril(q)
    acum = _dot_sel(causal.astype(BF16), da)
    acum_t = acum.T
    dt_t = dt.T
    a_last = acum[q - 1:q, :]
    decay_in = jnp.exp(acum)
    w_end = dt * jnp.exp(a_last - acum)
    chunk_decay = jnp.exp(a_last)
    lane_head = lax.broadcasted_iota(jnp.int32, (q, SSD_GROUP_COLS), 1) >> 6
    assert SSD_HEAD_DIM == 1 << 6

    for g in range(SSD_GROUPS):
        lo = g * SSD_GROUP_COLS
        cols = slice(lo, lo + SSD_GROUP_COLS)
        xs_g = _conv_silu(ubuf, cw_ref, cb_ref, lo, SSD_GROUP_COLS, q)
        b_g = _conv_silu(ubuf, cw_ref, cb_ref, D_MODEL + g * SSD_STATE, SSD_STATE, q).astype(BF16)
        c_g = _conv_silu(ubuf, cw_ref, cb_ref, D_MODEL + SSD_BC + g * SSD_STATE, SSD_STATE,
                         q).astype(BF16)
        scores = _dot_nt(c_g, b_g)
        h_g = ht[g]
        xs_b = xs_g.astype(BF16)
        m_heads, x_heads = [], []
        for hh in range(HEADS_PER_GROUP):
            h = g * HEADS_PER_GROUP + hh
            seg = acum[:, h:h + 1] - acum_t[h:h + 1, :]
            decay = jnp.exp(jnp.where(causal, seg, -jnp.inf))
            m_heads.append((scores * decay * dt_t[h:h + 1, :]).astype(BF16))
            x_heads.append(jnp.where(lane_head == hh, xs_b, jnp.zeros_like(xs_b)))
        y_g = (_dot(jnp.concatenate(m_heads, axis=1), jnp.concatenate(x_heads, axis=0))
               + _dot(c_g, h_g.astype(BF16)) * _expand_group(decay_in, g, lane_head)
               + dskip_ref[:, cols] * xs_g)
        y_ref[:, cols] = _gated_group_norm(y_g, z_ref[:, cols], nw_ref[:, cols]).astype(y_ref.dtype)
        xw = (xs_g * _expand_group(w_end, g, lane_head)).astype(BF16)
        ht[g] = h_g * _expand_group(chunk_decay, g, lane_head[0:1]) + _dot_tn(b_g, xw)

    last = HALO_ROWS + q - (SSD_CONV - 1)
    tail_rows = ubuf[last:last + SSD_CONV - 1, :]
    ubuf[HALO_ROWS - (SSD_CONV - 1):HALO_ROWS, :] = tail_rows
    return tail_rows


def _ssd_prompt_kernel(z_ref, xs_ref, bc_ref, tail_ref, cw_ref, cb_ref, dtb_ref, alog_ref,
                       dskip_ref, nw_ref, y_ref, hout_ref, convout_ref, ubuf, ht):
    q = SSD_CHUNK
    c = pl.program_id(1)

    @pl.when(c == 0)
    def _():
        ubuf[0:HALO_ROWS, :] = jnp.zeros((HALO_ROWS, SSD_CONV_DIM), F32)
        ht[...] = jnp.zeros(ht.shape, F32)

    for ci in range(xs_ref.shape[0] // q):
        rows = pl.ds(ci * q, q)
        tail_rows = _ssd_prompt_chunk(z_ref.at[rows], xs_ref.at[rows], bc_ref.at[rows],
                                      tail_ref.at[rows], cw_ref, cb_ref, dtb_ref, alog_ref,
                                      dskip_ref, nw_ref, y_ref.at[rows], ubuf, ht)

    @pl.when(c == pl.num_programs(1) - 1)
    def _():
        convout_ref[0] = tail_rows
        for g in range(SSD_GROUPS):
            h_t = ht[g].T
            for hh in range(HEADS_PER_GROUP):
                hout_ref[0, g * HEADS_PER_GROUP + hh] = h_t[hh * SSD_HEAD_DIM:(hh + 1) * SSD_HEAD_DIM]


def _row_params(p):
    return pl.BlockSpec(p.shape, lambda *_: (0,) * p.ndim)


def _ssd_prompt(proj, tail, n_seq, seq_len, cw, cb, dtb, alog, dskip_e, nw):
    q = SSD_CHUNK * SCAN_CHUNKS
    nc = seq_len // q

    def col(blk):
        return pl.BlockSpec((q, D_MODEL), lambda b, c: (b * nc + c, blk))

    return pl.pallas_call(
        _ssd_prompt_kernel,
        grid=(n_seq, nc),
        in_specs=[col(COL_Z), col(COL_XS), col(COL_BC),
                  pl.BlockSpec((q, LANES), lambda b, c: (b * nc + c, TAIL_DT)),
                  _row_params(cw), _row_params(cb), _row_params(dtb), _row_params(alog),
                  _row_params(dskip_e), _row_params(nw)],
        out_specs=[pl.BlockSpec((q, D_MODEL), lambda b, c: (b * nc + c, 0)),
                   pl.BlockSpec((1, SSD_HEADS, SSD_HEAD_DIM, SSD_STATE), lambda b, c: (b, 0, 0, 0)),
                   pl.BlockSpec((1, SSD_CONV - 1, SSD_CONV_DIM), lambda b, c: (b, 0, 0))],
        out_shape=[jax.ShapeDtypeStruct((n_seq * seq_len, D_MODEL), BF16),
                   jax.ShapeDtypeStruct((n_seq, SSD_HEADS, SSD_HEAD_DIM, SSD_STATE), F32),
                   jax.ShapeDtypeStruct((n_seq, SSD_CONV - 1, SSD_CONV_DIM), F32)],
        scratch_shapes=[pltpu.VMEM((HALO_ROWS + SSD_CHUNK, SSD_CONV_DIM), F32),
                        pltpu.VMEM((SSD_GROUPS, SSD_STATE, SSD_GROUP_COLS), F32)],
        compiler_params=_cparams(2),
        name="ssd_prompt",
    )(proj, proj, proj, tail, cw, cb, dtb, alog, dskip_e, nw)


SAMPLE_SEQS = 4
GLA_SAMPLE_SEQS = 8


def _dot_tn_split(a, b):
    a_hi = a.astype(BF16)
    a_lo = (a - a_hi.astype(F32)).astype(BF16)
    b_hi = b.astype(BF16)
    b_lo = (b - b_hi.astype(F32)).astype(BF16)
    return _dot_tn(a_lo, b_hi) + _dot_tn(a_hi, b_lo) + _dot_tn(a_hi, b_hi)


def _ssd_sample_seq(z_ref, xs_ref, bc_ref, tail_ref, h_ref, conv_ref, cw_ref, cb_ref, dtb_ref,
                    alog_ref, dskip_ref, nw_ref, y_ref, hout_ref, convout_ref, ubuf, decay_rows):
    q = xs_ref.shape[0]
    ubuf[0:HALO_ROWS - (SSD_CONV - 1), :] = jnp.zeros((HALO_ROWS - (SSD_CONV - 1), SSD_CONV_DIM), F32)
    ubuf[HALO_ROWS - (SSD_CONV - 1):HALO_ROWS, :] = conv_ref[...]
    ubuf[HALO_ROWS:HALO_ROWS + q, 0:D_MODEL] = xs_ref[...]
    ubuf[HALO_ROWS:HALO_ROWS + q, D_MODEL:SSD_CONV_DIM] = bc_ref[...]
    convout_ref[...] = ubuf[HALO_ROWS + q - (SSD_CONV - 1):HALO_ROWS + q, :]

    dt = _softplus(tail_ref[...] + dtb_ref[...])
    da = dt * (-jnp.exp(alog_ref[...]))
    acum = _dot_sel(_tril(q).astype(BF16), da)
    a_last = acum[q - 1:q, :]
    decay_in = jnp.exp(acum)
    w_end = dt * jnp.exp(a_last - acum)
    decay_rows[...] = _col_bcast(jnp.exp(a_last))[:SSD_HEADS]
    row = lax.broadcasted_iota(jnp.int32, (q, SSD_GROUP_COLS), 0)
    lane_head = lax.broadcasted_iota(jnp.int32, (q, SSD_GROUP_COLS), 1) >> 6

    for g in range(SSD_GROUPS):
        lo = g * SSD_GROUP_COLS
        cols = slice(lo, lo + SSD_GROUP_COLS)
        xs_g = _conv_silu(ubuf, cw_ref, cb_ref, lo, SSD_GROUP_COLS, q)
        b_g = _conv_silu(ubuf, cw_ref, cb_ref, D_MODEL + g * SSD_STATE, SSD_STATE, q)
        c_g = _conv_silu(ubuf, cw_ref, cb_ref, D_MODEL + SSD_BC + g * SSD_STATE, SSD_STATE, q)
        h_g = h_ref[g * HEADS_PER_GROUP:(g + 1) * HEADS_PER_GROUP].reshape(
            SSD_GROUP_COLS, SSD_STATE)
        y_g = (_dot_nt(c_g.astype(BF16), h_g.astype(BF16)) * _expand_group(decay_in, g, lane_head)
               + dskip_ref[:, cols] * xs_g)
        acum_g = _expand_group(acum, g, lane_head)
        dt_g = _expand_group(dt, g, lane_head)
        for j in range(q):
            s_j = jnp.sum(c_g * b_g[j:j + 1, :], axis=-1, keepdims=True)
            coef = s_j * jnp.exp(acum_g - acum_g[j:j + 1, :]) * dt_g[j:j + 1, :]
            y_g = y_g + jnp.where(row >= j, coef, 0.0) * xs_g[j:j + 1, :]
        y_ref[:, cols] = _gated_group_norm(y_g, z_ref[:, cols], nw_ref[:, cols])
        upd = _dot_tn_split(xs_g * _expand_group(w_end, g, lane_head), b_g)
        for hh in range(HEADS_PER_GROUP):
            h = g * HEADS_PER_GROUP + hh
            rows = slice(hh * SSD_HEAD_DIM, (hh + 1) * SSD_HEAD_DIM)
            hout_ref[h] = h_g[rows] * decay_rows[h:h + 1, :] + upd[rows]


def _ssd_sample_kernel(z_ref, xs_ref, bc_ref, tail_ref, h_ref, conv_ref, cw_ref, cb_ref, dtb_ref,
                       alog_ref, dskip_ref, nw_ref, y_ref, hout_ref, convout_ref, ubuf, decay_rows):
    n = h_ref.shape[0]
    q = xs_ref.shape[0] // n
    for s in range(n):
        rows = pl.ds(s * q, q)
        _ssd_sample_seq(z_ref.at[rows], xs_ref.at[rows], bc_ref.at[rows], tail_ref.at[rows],
                        h_ref.at[s], conv_ref.at[s], cw_ref, cb_ref, dtb_ref, alog_ref, dskip_ref,
                        nw_ref, y_ref.at[rows], hout_ref.at[s], convout_ref.at[s], ubuf.at[s],
                        decay_rows.at[s])


def _ssd_sample(proj, tail, row0, n_seq, seq_len, h0, conv0, cw, cb, dtb, alog, dskip_e, nw):
    nb = SAMPLE_SEQS
    q = seq_len
    blk0 = row0 // (nb * q)

    def col(blk):
        return pl.BlockSpec((nb * q, D_MODEL), lambda b: (blk0 + b, blk))

    state_spec = pl.BlockSpec((nb, SSD_HEADS, SSD_HEAD_DIM, SSD_STATE), lambda b: (b, 0, 0, 0))
    conv_spec = pl.BlockSpec((nb, SSD_CONV - 1, SSD_CONV_DIM), lambda b: (b, 0, 0))
    return pl.pallas_call(
        _ssd_sample_kernel,
        grid=(n_seq // nb,),
        in_specs=[col(COL_Z), col(COL_XS), col(COL_BC),
                  pl.BlockSpec((nb * q, LANES), lambda b: (blk0 + b, TAIL_DT)),
                  state_spec, conv_spec,
                  _row_params(cw), _row_params(cb), _row_params(dtb), _row_params(alog),
                  _row_params(dskip_e), _row_params(nw)],
        out_specs=[pl.BlockSpec((nb * q, D_MODEL), lambda b: (b, 0)), state_spec, conv_spec],
        out_shape=[jax.ShapeDtypeStruct((n_seq * seq_len, D_MODEL), F32),
                   jax.ShapeDtypeStruct(h0.shape, F32),
                   jax.ShapeDtypeStruct(conv0.shape, F32)],
        scratch_shapes=[pltpu.VMEM((nb, HALO_ROWS + q, SSD_CONV_DIM), F32),
                        pltpu.VMEM((nb, SSD_HEADS, SSD_STATE), F32)],
        compiler_params=_cparams(1),
        name="ssd_sample",
    )(proj, proj, proj, tail, h0, conv0, cw, cb, dtb, alog, dskip_e, nw)


def _gla_chunk(qk_ref, v_ref, r_ref, alo_ref, wa_ref, ba_ref, nw_ref, y_ref, get_state, put_state):
    q = v_ref.shape[0]
    a_lo = alo_ref[...][:, :GLA_RANK].astype(BF16)
    gk = -_softplus(-(_dot(a_lo, wa_ref[...]) + ba_ref[...])) / GLA_TAU
    causal = _tril(q)
    bcum = _dot_sel(causal.astype(BF16), gk)
    for h in range(GLA_HEADS):
        kc = slice(h * GLA_DK, (h + 1) * GLA_DK)
        vc = slice(h * GLA_DV, (h + 1) * GLA_DV)
        b_h = bcum[:, kc]
        b_last = b_h[q - 1:q, :]
        q_h = qk_ref[:, kc] * (GLA_DK ** -0.5)
        k_h = qk_ref[:, GLA_KEY_DIM + h * GLA_DK:GLA_KEY_DIM + (h + 1) * GLA_DK]
        v_h = v_ref[:, vc].astype(BF16)
        q_in = (q_h * jnp.exp(b_h)).astype(BF16)
        k_in = (k_h * jnp.exp(-b_h)).astype(BF16)
        att = jnp.where(causal, _dot_nt(q_in, k_in), 0.0).astype(BF16)
        s_h = get_state(h)
        o = _dot(att, v_h) + _dot(q_in, s_h.astype(BF16))
        k_end = (k_h * jnp.exp(b_last - b_h)).astype(BF16)
        d_col = _col_bcast(jnp.exp(b_last))
        put_state(h, s_h * jnp.concatenate([d_col, d_col], axis=1) + _dot_tn(k_end, v_h))
        ms = jnp.mean(o * o, axis=-1, keepdims=True)
        y_ref[:, vc] = (o * lax.rsqrt(ms + EPS) * nw_ref[...] * _silu(r_ref[:, vc])).astype(y_ref.dtype)


def _gla_prompt_kernel(qk_ref, v_ref, r_ref, alo_ref, wa_ref, ba_ref, nw_ref, y_ref, sout_ref, st):
    c = pl.program_id(1)

    @pl.when(c == 0)
    def _():
        st[...] = jnp.zeros(st.shape, F32)

    def put(h, val):
        st[h] = val

    q = GLA_CHUNK
    for ci in range(v_ref.shape[0] // q):
        rows = pl.ds(ci * q, q)
        _gla_chunk(qk_ref.at[rows], v_ref.at[rows], r_ref.at[rows], alo_ref.at[rows], wa_ref, ba_ref,
                   nw_ref, y_ref.at[rows], lambda h: st[h], put)

    @pl.when(c == pl.num_programs(1) - 1)
    def _():
        sout_ref[0] = st[...]


def _gla_prompt(proj, tail, n_seq, seq_len, wa, ba, nw):
    q = GLA_CHUNK * GLA_SCAN_CHUNKS
    nc = seq_len // q

    def col(blk):
        return pl.BlockSpec((q, D_MODEL), lambda b, c: (b * nc + c, blk))

    return pl.pallas_call(
        _gla_prompt_kernel,
        grid=(n_seq, nc),
        in_specs=[col(COL_QK), col(COL_V), col(COL_R),
                  pl.BlockSpec((q, LANES), lambda b, c: (b * nc + c, TAIL_ALO)),
                  _row_params(wa), _row_params(ba), _row_params(nw)],
        out_specs=[pl.BlockSpec((q, D_MODEL), lambda b, c: (b * nc + c, 0)),
                   pl.BlockSpec((1, GLA_HEADS, GLA_DK, GLA_DV), lambda b, c: (b, 0, 0, 0))],
        out_shape=[jax.ShapeDtypeStruct((n_seq * seq_len, D_MODEL), BF16),
                   jax.ShapeDtypeStruct((n_seq, GLA_HEADS, GLA_DK, GLA_DV), F32)],
        scratch_shapes=[pltpu.VMEM((GLA_HEADS, GLA_DK, GLA_DV), F32)],
        compiler_params=_cparams(2),
        name="gla_prompt",
    )(proj, proj, proj, tail, wa, ba, nw)


def _gla_sample_kernel(qk_ref, v_ref, r_ref, alo_ref, s_ref, wa_ref, ba_ref, nw_ref, y_ref, sout_ref):
    n = s_ref.shape[0]
    q = v_ref.shape[0] // n
    for s in range(n):
        rows = pl.ds(s * q, q)

        def put(h, val, s=s):
            sout_ref[s, h] = val

        _gla_chunk(qk_ref.at[rows], v_ref.at[rows], r_ref.at[rows], alo_ref.at[rows], wa_ref, ba_ref,
                   nw_ref, y_ref.at[rows], lambda h, s=s: s_ref[s, h], put)


def _gla_sample(proj, tail, row0, n_seq, seq_len, s0, wa, ba, nw):
    nb = GLA_SAMPLE_SEQS
    q = seq_len
    blk0 = row0 // (nb * q)

    def col(blk):
        return pl.BlockSpec((nb * q, D_MODEL), lambda b: (blk0 + b, blk))

    state_spec = pl.BlockSpec((nb, GLA_HEADS, GLA_DK, GLA_DV), lambda b: (b, 0, 0, 0))
    return pl.pallas_call(
        _gla_sample_kernel,
        grid=(n_seq // nb,),
        in_specs=[col(COL_QK), col(COL_V), col(COL_R),
                  pl.BlockSpec((nb * q, LANES), lambda b: (blk0 + b, TAIL_ALO)),
                  state_spec, _row_params(wa), _row_params(ba), _row_params(nw)],
        out_specs=[pl.BlockSpec((nb * q, D_MODEL), lambda b: (b, 0)), state_spec],
        out_shape=[jax.ShapeDtypeStruct((n_seq * seq_len, D_MODEL), F32),
                   jax.ShapeDtypeStruct(s0.shape, F32)],
        compiler_params=_cparams(1),
        name="gla_sample",
    )(proj, proj, proj, tail, s0, wa, ba, nw)


def _layernorm(x, g, b):
    mu = jnp.mean(x, axis=-1, keepdims=True)
    xc = x - mu
    var = jnp.mean(xc * xc, axis=-1, keepdims=True)
    return xc * lax.rsqrt(var + EPS) * g + b


def _route(logits):
    lane = lax.broadcasted_iota(jnp.int32, logits.shape, 1)
    neg = -jnp.inf
    big = ROUTER_LANES
    glog = jnp.where(lane < MOE_GROUPS, logits, neg)
    gmax = jnp.max(glog, axis=-1, keepdims=True)
    g_sel = jnp.min(jnp.where(glog == gmax, lane, big), axis=-1, keepdims=True)
    p_g = 1.0 / jnp.sum(jnp.exp(glog - gmax), axis=-1, keepdims=True)
    e_lane = lane - MOE_GROUPS
    in_group = (e_lane >= 0) & (e_lane < MOE_EXPERTS) & (e_lane // MOE_EPG == g_sel)
    el = jnp.where(in_group, logits, neg)
    v1 = jnp.max(el, axis=-1, keepdims=True)
    i1 = jnp.min(jnp.where(el == v1, lane, big), axis=-1, keepdims=True)
    el2 = jnp.where(lane == i1, neg, el)
    v2 = jnp.max(el2, axis=-1, keepdims=True)
    i2 = jnp.min(jnp.where(el2 == v2, lane, big), axis=-1, keepdims=True)
    e2 = jnp.exp(v2 - v1)
    w1 = p_g / (1.0 + e2)
    w2 = p_g * e2 / (1.0 + e2)
    first = (i1 - MOE_GROUPS).astype(F32)
    second = (i2 - MOE_GROUPS).astype(F32)
    return jnp.where(lane == ROUTE_E1, first,
                     jnp.where(lane == ROUTE_E2, second,
                               jnp.where(lane == ROUTE_W1, w1,
                                         jnp.where(lane == ROUTE_W2, w2, 0.0))))


ROW_CHUNKS = D_MODEL // LANES


def _store_chunked(ref, val):
    rows = val.shape[0]
    for c in range(ROW_CHUNKS):
        ref[pl.ds(c, rows, stride=ROW_CHUNKS), :] = val[:, c * LANES:(c + 1) * LANES]


def _load_chunked(ref, c):
    return ref[pl.ds(c, ref.shape[0] // ROW_CHUNKS, stride=ROW_CHUNKS), :]


def _merge_kernel(n_first, ysa_ref, ysb_ref, yga_ref, ygb_ref, gs_ref, gg_ref, xa_ref, xb_ref, wo_ref,
                  g_ref, b_ref, wrh_ref, wrl_ref, br_ref, h1_ref, route_ref):
    def tile(ys_ref, yg_ref, x_ref):
        merged = (jax.nn.sigmoid(gs_ref[...]) * ys_ref[...]
                  + jax.nn.sigmoid(gg_ref[...]) * yg_ref[...])
        mix = _dot(merged.astype(BF16), wo_ref[...])
        h1 = _layernorm(DEEPNORM_ALPHA * x_ref[...] + mix, g_ref[...], b_ref[...])
        _store_chunked(h1_ref, h1)
        h_hi = h1.astype(BF16)
        h_lo = (h1 - h_hi.astype(F32)).astype(BF16)
        logits = (_dot(h_hi, wrh_ref[...]) + _dot(h_lo, wrh_ref[...]) + _dot(h_hi, wrl_ref[...])
                  + br_ref[...])
        route_ref[...] = _route(logits)

    first = pl.program_id(0) < n_first
    pl.when(first)(lambda: tile(ysa_ref, yga_ref, xa_ref))
    pl.when(jnp.logical_not(first))(lambda: tile(ysb_ref, ygb_ref, xb_ref))


def _merge(ys, yg, gates, xs, wo, g, b, wr_hi, wr_lo, br, tm):
    n_a, n_b = xs[0].shape[0] // tm, xs[1].shape[0] // tm
    m = (n_a + n_b) * tm

    def rows(i):
        return (i, 0)

    pair = [pl.BlockSpec((tm, D_MODEL), lambda i: (jnp.minimum(i, n_a - 1), 0)),
            pl.BlockSpec((tm, D_MODEL), lambda i: (jnp.maximum(i - n_a, 0), 0))]
    return pl.pallas_call(
        functools.partial(_merge_kernel, n_a),
        grid=(m // tm,),
        in_specs=[*pair, *pair,
                  pl.BlockSpec((tm, D_MODEL), lambda i: (i, COL_GS)),
                  pl.BlockSpec((tm, D_MODEL), lambda i: (i, COL_GG)),
                  *pair,
                  pl.BlockSpec(wo.shape, lambda i: (0, 0), pipeline_mode=pl.Buffered(1)),
                  _row_params(g), _row_params(b),
                  _row_params(wr_hi), _row_params(wr_lo), _row_params(br)],
        out_specs=[pl.BlockSpec((tm * ROW_CHUNKS, LANES), rows),
                   pl.BlockSpec((tm, ROUTER_LANES), rows)],
        out_shape=[jax.ShapeDtypeStruct((m * ROW_CHUNKS, LANES), F32),
                   jax.ShapeDtypeStruct((m, ROUTER_LANES), F32)],
        compiler_params=_cparams(1),
        name="merge_outproj_ln_router",
    )(*ys, *yg, gates, gates, *xs, wo, g, b, wr_hi, wr_lo, br)


def _zero_runs(pads_ref, zeros, x_hbm, sem, action):
    tm = zeros.shape[0] // ROW_CHUNKS

    def piece(pos, rows):
        dst = pl.multiple_of(pos * ROW_CHUNKS, ROW_CHUNKS)
        return pltpu.make_async_copy(zeros.at[pl.ds(0, rows * ROW_CHUNKS)],
                                     x_hbm.at[pl.ds(dst, rows * ROW_CHUNKS)], sem)

    def run(e, carry):
        pos, length = pads_ref[0, e], pads_ref[1, e]

        def whole(q, c):
            getattr(piece(pos + q * tm, tm), action)()
            return c

        lax.fori_loop(0, length // tm, whole, 0)
        pos = pos + (length // tm) * tm
        rows = tm // 2
        while rows >= 1:
            @pl.when((length & rows) != 0)
            def _(pos=pos, rows=rows):
                getattr(piece(pos, rows), action)()

            pos = pos + (length & rows)
            rows //= 2
        return carry

    lax.fori_loop(0, pads_ref.shape[1], run, 0)


def _dispatch_kernel(slot_ref, pads_ref, h_ref, x_hbm, zeros, sem, zsem):
    n = h_ref.shape[0] // ROW_CHUNKS

    def issue(r, carry):
        src = h_ref.at[pl.ds(pl.multiple_of(r * ROW_CHUNKS, ROW_CHUNKS), ROW_CHUNKS)]
        for k in range(2):
            dst = pl.multiple_of(slot_ref[k, r] * ROW_CHUNKS, ROW_CHUNKS)
            pltpu.make_async_copy(src, x_hbm.at[pl.ds(dst, ROW_CHUNKS)], sem).start(
                priority=k % DMA_QUEUES)
        return carry

    lax.fori_loop(0, n, issue, 0, unroll=4)

    @pl.when(pl.program_id(0) == pl.num_programs(0) - 1)
    def _():
        zeros[...] = jnp.zeros(zeros.shape, zeros.dtype)
        _zero_runs(pads_ref, zeros, x_hbm, zsem, "start")
        _zero_runs(pads_ref, zeros, x_hbm, zsem, "wait")

    for k in range(2):
        pltpu.make_async_copy(h_ref, x_hbm.at[pl.ds(0, n * ROW_CHUNKS)], sem).wait()


def _dispatch(h1, slot, pads, n_slots, tm, window):
    t = slot.shape[1]
    steps = t // window
    slot3 = slot.reshape(2, steps, window).transpose(1, 0, 2)
    return pl.pallas_call(
        _dispatch_kernel,
        grid=(steps,),
        in_specs=[pl.BlockSpec((None, 2, window), lambda i: (i, 0, 0), memory_space=pltpu.SMEM),
                  pl.BlockSpec(pads.shape, lambda i: (0, 0), memory_space=pltpu.SMEM),
                  pl.BlockSpec((window * ROW_CHUNKS, LANES), lambda i: (i, 0))],
        out_specs=pl.BlockSpec(memory_space=pl.ANY),
        out_shape=jax.ShapeDtypeStruct((n_slots * ROW_CHUNKS, LANES), h1.dtype),
        scratch_shapes=[pltpu.VMEM((tm * ROW_CHUNKS, LANES), h1.dtype), pltpu.SemaphoreType.DMA,
                        pltpu.SemaphoreType.DMA],
        compiler_params=_cparams(1),
        name="dispatch_rows",
    )(slot3, pads, h1)


def _expert_plan(route, tm):
    t = route.shape[0]
    ids = jnp.arange(MOE_EXPERTS, dtype=jnp.int32)
    e = jnp.concatenate([route[:, ROUTE_E1], route[:, ROUTE_E2]]).astype(jnp.int32)
    onehot = (e[:, None] == ids[None, :]).astype(jnp.int32)
    csum = jnp.cumsum(onehot, axis=0)
    rank = jnp.sum(onehot * csum, axis=1) - 1
    cnt = csum[-1]
    tiles_e = (cnt + tm - 1) // tm
    tile_end = jnp.cumsum(tiles_e)
    tile_start = tile_end - tiles_e
    slot = tile_start[e] * tm + rank
    n_tiles = (2 * t) // tm + MOE_EXPERTS
    tid = jnp.arange(n_tiles, dtype=jnp.int32)
    used = tid < tile_end[-1]
    tile_e = jnp.sum(tile_end[None, :] <= jnp.minimum(tid, tile_end[-1] - 1)[:, None], axis=1)
    nvalid = jnp.where(used, jnp.clip(cnt[tile_e] - (tid - tile_start[tile_e]) * tm, 0, tm), 0)
    first = used & (tid == tile_start[tile_e])
    has = tiles_e > 0
    ordinal = jnp.cumsum(has.astype(jnp.int32)) - 1
    later = jnp.where((ids[None, :] > ids[:, None]) & has[None, :], ids[None, :], MOE_EXPERTS)
    nxt_e = jnp.min(later, axis=1)
    nxt_e = jnp.where(nxt_e == MOE_EXPERTS, -1, nxt_e)
    i32 = lambda a: a.astype(jnp.int32)
    pads = jnp.stack([jnp.append(tile_start * tm + cnt, tile_end[-1] * tm),
                      jnp.append(tiles_e * tm - cnt, (n_tiles - tile_end[-1]) * tm)])
    return (slot.reshape(2, t), i32(pads), n_tiles * tm, i32(tile_e), i32(nvalid), i32(first),
            i32(nxt_e[tile_e]), i32(ordinal[tile_e] % 2))


def _experts_kernel(te_ref, nv_ref, first_ref, nxt_ref, par_ref, x_ref, wg_hbm, wu_hbm, wd_hbm, y_ref,
                    wg_f, wu_f, wd_f, wg_b, wu_b, wd_b, x_b, sem):
    i = pl.program_id(0)
    s = par_ref[i]

    def weight_copies(e, slot):
        return (pltpu.make_async_copy(wg_hbm.at[e], wg_f.at[slot], sem.at[slot]),
                pltpu.make_async_copy(wu_hbm.at[e], wu_f.at[slot], sem.at[slot]),
                pltpu.make_async_copy(wd_hbm.at[e], wd_f.at[slot], sem.at[slot]))

    @pl.when(i == 0)
    def _():
        for cp in weight_copies(te_ref[0], par_ref[0]):
            cp.start()

    @pl.when(first_ref[i] == 1)
    def _():
        @pl.when(nxt_ref[i] >= 0)
        def _():
            for cp in weight_copies(nxt_ref[i], 1 - s):
                cp.start()

        for cp in weight_copies(te_ref[i], s):
            cp.wait()
        wg_b[...] = wg_f[s].astype(BF16)
        wu_b[...] = wu_f[s].astype(BF16)
        wd_b[...] = wd_f[s].astype(BF16)

    nv = nv_ref[i]

    @pl.when(nv > 0)
    def _():
        for c in range(ROW_CHUNKS):
            x_b[:, c * LANES:(c + 1) * LANES] = _load_chunked(x_ref, c).astype(BF16)
        x = x_b[...]
        hid = (_silu(_dot(x, wg_b[...])) * _dot(x, wu_b[...])).astype(BF16)
        _store_chunked(y_ref, _dot(hid, wd_b[...]))

    @pl.when(nv == 0)
    def _():
        y_ref[...] = jnp.zeros(y_ref.shape, y_ref.dtype)


def _experts(xs, tile_e, nvalid, first, nxt, par, wg, wu, wd, tm):
    n_tiles = xs.shape[0] // (tm * ROW_CHUNKS)
    rows = pl.BlockSpec((tm * ROW_CHUNKS, LANES), lambda i, *_: (i, 0))
    hbm = pl.BlockSpec(memory_space=pl.ANY)
    grid_spec = pltpu.PrefetchScalarGridSpec(
        num_scalar_prefetch=5,
        grid=(n_tiles,),
        in_specs=[rows, hbm, hbm, hbm],
        out_specs=rows,
        scratch_shapes=[pltpu.VMEM((2, D_MODEL, MOE_FF), F32), pltpu.VMEM((2, D_MODEL, MOE_FF), F32),
                        pltpu.VMEM((2, MOE_FF, D_MODEL), F32),
                        pltpu.VMEM((D_MODEL, MOE_FF), BF16), pltpu.VMEM((D_MODEL, MOE_FF), BF16),
                        pltpu.VMEM((MOE_FF, D_MODEL), BF16), pltpu.VMEM((tm, D_MODEL), BF16),
                        pltpu.SemaphoreType.DMA((2,))])
    return pl.pallas_call(
        _experts_kernel,
        grid_spec=grid_spec,
        out_shape=jax.ShapeDtypeStruct(xs.shape, F32),
        compiler_params=_cparams(1),
        name="experts",
    )(tile_e, nvalid, first, nxt, par, xs, wg, wu, wd)


def _combine_kernel(n_steps, slot_ref, slot_next_ref, route_ref, h1_ref, g_ref, b_ref, y_hbm, o_ref,
                    gbuf, sem):
    i = pl.program_id(0)
    tm = route_ref.shape[0]

    def request(idx_ref, parity):
        def body(r, carry):
            dst = pl.ds(pl.multiple_of(r * ROW_CHUNKS, ROW_CHUNKS), ROW_CHUNKS)
            for k in range(2):
                src = pl.multiple_of(idx_ref[k, r] * ROW_CHUNKS, ROW_CHUNKS)
                pltpu.make_async_copy(y_hbm.at[pl.ds(src, ROW_CHUNKS)], gbuf.at[parity, k, dst],
                                      sem.at[parity]).start(priority=k % DMA_QUEUES)
            return carry

        lax.fori_loop(0, tm, body, 0, unroll=4)

    @pl.when(i == 0)
    def _():
        request(slot_ref, 0)

    @pl.when(i + 1 < n_steps)
    def _():
        request(slot_next_ref, (i + 1) % 2)

    parity = i % 2
    for k in range(2):
        pltpu.make_async_copy(y_hbm.at[pl.ds(0, tm * ROW_CHUNKS)], gbuf.at[parity, k],
                              sem.at[parity]).wait()
    route = route_ref[...]
    w1 = route[:, ROUTE_W1:ROUTE_W1 + 1]
    w2 = route[:, ROUTE_W2:ROUTE_W2 + 1]
    pre = jnp.concatenate(
        [DEEPNORM_ALPHA * _load_chunked(h1_ref, c)
         + w1 * _load_chunked(gbuf.at[parity, 0], c) + w2 * _load_chunked(gbuf.at[parity, 1], c)
         for c in range(ROW_CHUNKS)], axis=1)
    o_ref[...] = _layernorm(pre, g_ref[...], b_ref[...])


def _combine(y_sorted, slot, route, h1, row0, m, g, b, tm):
    t = slot.shape[1]
    blk0 = row0 // tm
    last = (row0 + m) // tm - 1
    slot3 = slot.reshape(2, t // tm, tm).transpose(1, 0, 2)
    return pl.pallas_call(
        functools.partial(_combine_kernel, m // tm),
        grid=(m // tm,),
        in_specs=[pl.BlockSpec((None, 2, tm), lambda i: (blk0 + i, 0, 0), memory_space=pltpu.SMEM),
                  pl.BlockSpec((None, 2, tm), lambda i: (jnp.minimum(blk0 + i + 1, last), 0, 0),
                               memory_space=pltpu.SMEM),
                  pl.BlockSpec((tm, ROUTER_LANES), lambda i: (blk0 + i, 0)),
                  pl.BlockSpec((tm * ROW_CHUNKS, LANES), lambda i: (blk0 + i, 0)),
                  _row_params(g), _row_params(b),
                  pl.BlockSpec(memory_space=pl.ANY)],
        out_specs=pl.BlockSpec((tm, D_MODEL), lambda i: (i, 0)),
        out_shape=jax.ShapeDtypeStruct((m, D_MODEL), F32),
        scratch_shapes=[pltpu.VMEM((2, 2, tm * ROW_CHUNKS, LANES), F32),
                        pltpu.SemaphoreType.DMA((2,))],
        compiler_params=_cparams(1),
        name="combine_ln",
    )(slot3, slot3, route, h1, g, b, y_sorted)


def _pad_lanes(v, width=LANES):
    v = v.reshape(1, -1)
    return jnp.pad(v, ((0, 0), (0, width - v.shape[1])))


def kernel(x_prompt, x_sample, state_ssd, state_conv, state_gla, w_in, conv_w, conv_b, dt_bias, a_log, d_skip, ssd_norm_w, gla_w_a2, gla_b_a, gla_norm_w, w_out, ln1_g, ln1_b, w_router_group, b_router_group, w_router_expert, b_router_expert, w_gate, w_up, w_down, ln2_g, ln2_b):
    assert w_in.shape[0] == DEPTH == 1
    n_p, len_p, _ = x_prompt.shape
    n_s, len_s, _ = x_sample.shape
    rows_p, rows_s = n_p * len_p, n_s * len_s
    xp = x_prompt.reshape(rows_p, D_MODEL)
    xs = x_sample.reshape(rows_s, D_MODEL)

    offs = [0]
    for s in IN_SPLIT_SIZES:
        offs.append(offs[-1] + s)
    w_t = w_in[0].T
    w_tail_t = jnp.concatenate(
        [jnp.pad(w_t[offs[2]:offs[3]], ((0, LANES - SSD_HEADS), (0, 0))),
         jnp.pad(w_t[offs[7]:offs[8]], ((0, LANES - GLA_RANK), (0, 0)))], axis=0)
    x_b = jnp.concatenate([xp, xs], axis=0).astype(BF16)

    proj = _in_proj(x_b, w_t, ((0, offs[2]), (offs[3], offs[7] - offs[3]),
                               (offs[8], offs[10] - offs[8])))
    tail = _matmul_nt(x_b, w_tail_t, IN_PROJ_ROWS)

    cw, cb = conv_w[0], conv_b[0].reshape(1, -1)
    dtb, alog = _pad_lanes(dt_bias[0]), _pad_lanes(a_log[0])
    dskip_e = jnp.repeat(d_skip[0], SSD_HEAD_DIM).reshape(1, -1)
    nw_ssd = ssd_norm_w[0].reshape(1, -1)
    wa, ba, nw_gla = gla_w_a2[0].astype(BF16), gla_b_a[0].reshape(1, -1), gla_norm_w[0].reshape(1, -1)

    ys_p, ssd_p, conv_p = _ssd_prompt(proj, tail, n_p, len_p, cw, cb, dtb, alog, dskip_e, nw_ssd)
    ys_s, ssd_s, conv_s = _ssd_sample(proj, tail, rows_p, n_s, len_s, state_ssd[0], state_conv[0],
                                      cw, cb, dtb, alog, dskip_e, nw_ssd)
    yg_p, gla_p = _gla_prompt(proj, tail, n_p, len_p, wa, ba, nw_gla)
    yg_s, gla_s = _gla_sample(proj, tail, rows_p, n_s, len_s, state_gla[0], wa, ba, nw_gla)

    wo = w_out[0].astype(BF16)
    w_r = jnp.pad(jnp.concatenate([w_router_group[0], w_router_expert[0]], axis=1),
                  ((0, 0), (0, ROUTER_LANES - MOE_GROUPS - MOE_EXPERTS)))
    wr_hi = w_r.astype(BF16)
    wr_lo = (w_r - wr_hi.astype(F32)).astype(BF16)
    b_r = _pad_lanes(jnp.concatenate([b_router_group[0], b_router_expert[0]]), ROUTER_LANES)
    g1, b1 = ln1_g[0].reshape(1, -1), ln1_b[0].reshape(1, -1)
    g2, b2 = ln2_g[0].reshape(1, -1), ln2_b[0].reshape(1, -1)

    h1, route = _merge((ys_p, ys_s), (yg_p, yg_s), proj, (xp, xs), wo, g1, b1, wr_hi, wr_lo, b_r,
                       MERGE_ROWS)
    slot, pads, n_slots, tile_e, nvalid, first, nxt, par = _expert_plan(route, EXPERT_ROWS)
    x_sorted = _dispatch(h1, slot, pads, n_slots, EXPERT_ROWS, DISPATCH_WINDOW)
    y_sorted = _experts(x_sorted, tile_e, nvalid, first, nxt, par, w_gate[0], w_up[0], w_down[0],
                        EXPERT_ROWS)
    outs = [_combine(y_sorted, slot, route, h1, 0, rows_p, g2, b2, MERGE_ROWS),
            _combine(y_sorted, slot, route, h1, rows_p, rows_s, g2, b2, MERGE_ROWS)]
    y_p = outs[0].reshape(x_prompt.shape)
    y_s = outs[1].reshape(x_sample.shape)
    return (y_p, y_s, ssd_p[None], conv_p[None], gla_p[None], ssd_s[None], conv_s[None], gla_s[None])
```
